```python
import math
import jax, jax.numpy as jnp
from jax import lax
import numpy as np

D_MODEL = 1024
BATCH = 16
SEQ = 256
DEPTH = 4
DEC_BATCH = 4
DEC_SEQ = 4096
PAST_LEN = 256

GRID_W = 64
CONV_W = 512
N_HEADS = 8
N_KV = 2
HEAD_DIM = 64
GROUP = N_HEADS // N_KV
Q_W = N_HEADS * HEAD_DIM
KV_W = N_KV * HEAD_DIM
WINDOW = 128
BLOCK = 128
ROPE_BASE = 10000.0
NEG_INF = -1e30
SSM_W = 512
SSM_GC = 16
SSM_G = SSM_W // SSM_GC
SSM_P = 64
N_BRANCH = 3
N_IN = 3 * CONV_W + Q_W + 2 * KV_W + SSM_W + N_BRANCH * D_MODEL
N_EXPERTS = 16
EXPERT_FF = 1024
CAPACITY = 2
EPS = 1e-6

kernel_name = "hybrid_diffusion_prefix_gated_conv_swa_s5_ec"


def _rms(x, g):
    xf = x.astype(jnp.float32)
    y = xf * lax.rsqrt(jnp.mean(xf * xf, axis=-1, keepdims=True) + EPS)
    return (y * g.astype(jnp.float32)).astype(x.dtype)


def _split_cols(p):
    sizes = (CONV_W, CONV_W, CONV_W, Q_W, KV_W, KV_W, SSM_W, N_BRANCH * D_MODEL)
    cuts, acc = [], 0
    for s in sizes[:-1]:
        acc += s
        cuts.append(acc)
    return jnp.split(p, cuts, axis=-1)


def _short_conv(xin, bg, cg, w):
    z = cg * xin
    zp = jnp.pad(z, ((0, 0), (1, 1), (0, 0)))
    y = zp[:, :-2] * w[0] + zp[:, 1:-1] * w[1] + zp[:, 2:] * w[2]
    return bg * y


def _axial_rope(x):
    t = x.shape[1]
    pos = jnp.arange(t)
    row = (pos // GRID_W).astype(jnp.float32)
    col = (pos % GRID_W).astype(jnp.float32)
    nf = HEAD_DIM // 4
    half = HEAD_DIM // 2
    inv = ROPE_BASE ** (-jnp.arange(nf, dtype=jnp.float32) / nf)

    def rot(z, p):
        ang = p[:, None] * inv[None, :]
        cos = jnp.cos(ang)[None, :, None, :].astype(z.dtype)
        sin = jnp.sin(ang)[None, :, None, :].astype(z.dtype)
        z1, z2 = z[..., :nf], z[..., nf:]
        return jnp.concatenate([z1 * cos - z2 * sin, z2 * cos + z1 * sin], axis=-1)

    return jnp.concatenate([rot(x[..., :half], row), rot(x[..., half:], col)], axis=-1)


def _sink_softmax(parts, sink):
    lead = parts[0].shape[:-1]
    s0 = jnp.broadcast_to(sink.astype(jnp.float32).reshape(1, N_KV, GROUP, 1, 1), lead + (1,))
    logits = jnp.concatenate([p.astype(jnp.float32) for p in parts] + [s0], axis=-1)
    probs = jax.nn.softmax(logits, axis=-1)
    out, start = [], 0
    for p in parts:
        w = p.shape[-1]
        out.append(probs[..., start:start + w])
        start += w
    return out


def _context_attention(q, k, v, sink):
    b, t = q.shape[:2]
    nb = t // BLOCK
    qb = (q * HEAD_DIM ** -0.5).reshape(b, nb, BLOCK, N_KV, GROUP, HEAD_DIM).swapaxes(0, 1)

    def one(qx):
        s = jnp.einsum('bqkgd,bskd->bkgqs', qx, k)
        (p,) = _sink_softmax((s,), sink)
        return jnp.einsum('bkgqs,bskd->bqkgd', p.astype(v.dtype), v)

    o = lax.map(one, qb)
    return o.swapaxes(0, 1).reshape(b, t, N_HEADS, HEAD_DIM)


def _latent_attention(q, k, v, k_ctx, v_ctx, sink):
    b, t = q.shape[:2]
    nb = t // BLOCK
    qb = (q * HEAD_DIM ** -0.5).reshape(b, nb, BLOCK, N_KV, GROUP, HEAD_DIM).swapaxes(0, 1)

    def band(z):
        zp = jnp.pad(z, ((0, 0), (BLOCK, BLOCK), (0, 0), (0, 0))).reshape(b, nb + 2, BLOCK, N_KV, HEAD_DIM)
        return jnp.concatenate([zp[:, :-2], zp[:, 1:-1], zp[:, 2:]], axis=2).swapaxes(0, 1)

    kb, vb = band(k), band(v)
    blk = jnp.arange(nb)[:, None, None] * BLOCK
    qpos = blk + jnp.arange(BLOCK)[None, :, None]
    kpos = blk - BLOCK + jnp.arange(3 * BLOCK)[None, None, :]
    mask = (jnp.abs(qpos - kpos) <= WINDOW) & (kpos >= 0) & (kpos < t)

    def one(args):
        qx, kx, vx, mx = args
        s_ctx = jnp.einsum('bqkgd,bskd->bkgqs', qx, k_ctx)
        s_band = jnp.where(mx, jnp.einsum('bqkgd,bskd->bkgqs', qx, kx).astype(jnp.float32), NEG_INF)
        p_ctx, p_band = _sink_softmax((s_ctx, s_band), sink)
        return (jnp.einsum('bkgqs,bskd->bqkgd', p_ctx.astype(v.dtype), v_ctx)
                + jnp.einsum('bkgqs,bskd->bqkgd', p_band.astype(v.dtype), vx))

    o = lax.map(one, (qb, kb, vb, mask))
    return o.swapaxes(0, 1).reshape(b, t, N_HEADS, HEAD_DIM)


def _lin_combine(left, right):
    a1, b1 = left
    a2, b2 = right
    return a1 * a2, a2 * b1 + b2


def _ssm(u, lp, h0):
    b, t, _ = u.shape
    uf = u.astype(jnp.float32).reshape(b, t, SSM_G, SSM_GC)
    lam = lax.complex(lp['a_re'].astype(jnp.float32), lp['a_im'].astype(jnp.float32))
    dt = jnp.exp(lp['log_dt'].astype(jnp.float32))[..., None]
    abar = jnp.exp(lam * dt)
    bmat = lax.complex(lp['b_re'].astype(jnp.float32), lp['b_im'].astype(jnp.float32))
    bbar = ((abar - 1.0) / lam)[..., None] * bmat
    cmat = lax.complex(lp['c_re'].astype(jnp.float32), lp['c_im'].astype(jnp.float32))
    ys, finals = [], []
    for dirn in range(2):
        ud = uf if dirn == 0 else uf[:, ::-1]
        bu = jnp.einsum('btgc,gpc->btgp', ud, bbar[dirn])
        if h0 is not None:
            bu = bu.at[:, 0].add(abar[dirn][None] * h0[:, dirn])
        a = jnp.broadcast_to(abar[dirn], bu.shape)
        _, hs = lax.associative_scan(_lin_combine, (a, bu), axis=1)
        finals.append(hs[:, -1])
        yd = jnp.einsum('btgp,gcp->btgc', hs, cmat[dirn]).real
        ys.append(yd if dirn == 0 else yd[:, ::-1])
    y = (ys[0] + ys[1]).reshape(b, t, SSM_W) + lp['ssm_d'].astype(jnp.float32) * u.astype(jnp.float32)
    return y.astype(u.dtype), jnp.stack(finals, axis=1)


def _expert_choice(h, lp):
    b, n, d = h.shape
    cap = CAPACITY * n // N_EXPERTS
    aff = jax.nn.softmax((h @ lp['router_w']).astype(jnp.float32), axis=-1)
    gate, idx = lax.top_k(jnp.swapaxes(aff, 1, 2), cap)
    xs = jax.vmap(lambda hb, ib: hb[ib])(h, idx)
    hg = jnp.einsum('becd,edf->becf', xs, lp['w_gate'])
    hu = jnp.einsum('becd,edf->becf', xs, lp['w_up'])
    y = jnp.einsum('becf,efd->becd', jax.nn.silu(hg) * hu, lp['w_down']) * gate[..., None].astype(h.dtype)
    return jax.vmap(lambda yb, ib: jnp.zeros((n, d), h.dtype).at[ib.reshape(-1)].add(yb.reshape(-1, d)))(y, idx)


def _block(x, mod, lp, cache=None):
    sh1, sc1, g1, sh2, sc2, g2 = jnp.split(mod, 6, axis=-1)
    b, t, _ = x.shape
    h = _rms(x, lp['norm1']) * (1.0 + sc1) + sh1
    xin, bg, cg, q, k, v, u, gates = _split_cols(h @ lp['w_in'])
    conv_y = _short_conv(xin, bg, cg, lp['conv_w']) @ lp['w_conv_out']
    q = q.reshape(b, t, N_HEADS, HEAD_DIM)
    k = k.reshape(b, t, N_KV, HEAD_DIM)
    v = v.reshape(b, t, N_KV, HEAD_DIM)
    if cache is None:
        attn = _context_attention(q, k, v, lp['sink'])
        h0 = None
    else:
        k_ctx, v_ctx, h0 = cache
        attn = _latent_attention(_axial_rope(q), _axial_rope(k), v, k_ctx, v_ctx, lp['sink'])
    attn_y = attn.reshape(b, t, Q_W) @ lp['w_attn_out']
    ssm, h_final = _ssm(u, lp, h0)
    za, zb = jnp.split(jax.nn.gelu(ssm) @ lp['w_glu'], 2, axis=-1)
    ssm_y = za * jax.nn.sigmoid(zb)
    ga, gb, gc = jnp.split(jax.nn.sigmoid(gates), N_BRANCH, axis=-1)
    mixed = (ga * conv_y + gb * attn_y + gc * ssm_y) @ lp['w_out']
    x = x + g1 * mixed
    h2 = _rms(x, lp['norm2']) * (1.0 + sc2) + sh2
    x = x + g2 * _expert_choice(h2, lp)
    if cache is None:
        return x, (k, v, h_final)
    return x


def setup_inputs(seed: int = 0) -> dict:
    key = jax.random.key(seed)
    ks = jax.random.split(key, 32)
    f32 = jnp.float32

    def nrm(k, shape, scale):
        return jax.random.normal(k, shape, f32) * scale

    a_im = jnp.broadcast_to(math.pi * jnp.arange(SSM_P, dtype=f32), (DEPTH, 2, SSM_G, SSM_P))
    return {
        'x_prompt': nrm(ks[0], (BATCH, SEQ, D_MODEL), 1.0),
        'x_sample': nrm(ks[1], (DEC_BATCH, DEC_SEQ, D_MODEL), 1.0),
        'cache_k': nrm(ks[2], (DEC_BATCH, DEPTH, PAST_LEN, N_KV, HEAD_DIM), 1.0),
        'cache_v': nrm(ks[3], (DEC_BATCH, DEPTH, PAST_LEN, N_KV, HEAD_DIM), 1.0),
        'state_ssm_re': nrm(ks[4], (DEC_BATCH, DEPTH, 2, SSM_G, SSM_P), 0.2),
        'state_ssm_im': nrm(ks[5], (DEC_BATCH, DEPTH, 2, SSM_G, SSM_P), 0.2),
        'c': nrm(ks[6], (DEC_BATCH, D_MODEL), 1.0),
        'c_ctx': nrm(ks[7], (D_MODEL,), 1.0),
        'ada_w': nrm(ks[8], (DEPTH, D_MODEL, 6 * D_MODEL), 0.5 * D_MODEL ** -0.5),
        'ada_b': nrm(ks[9], (DEPTH, 6 * D_MODEL), 0.02),
        'norm1': 1.0 + nrm(ks[10], (DEPTH, D_MODEL), 0.02),
        'norm2': 1.0 + nrm(ks[11], (DEPTH, D_MODEL), 0.02),
        'final_norm': 1.0 + nrm(ks[12], (D_MODEL,), 0.02),
        'w_in': nrm(ks[13], (DEPTH, D_MODEL, N_IN), D_MODEL ** -0.5),
        'conv_w': nrm(ks[14], (DEPTH, 3, CONV_W), 3 ** -0.5),
        'w_conv_out': nrm(ks[15], (DEPTH, CONV_W, D_MODEL), CONV_W ** -0.5),
        'attn_sink': nrm(ks[16], (DEPTH, N_HEADS), 0.5),
        'w_attn_out': nrm(ks[17], (DEPTH, Q_W, D_MODEL), Q_W ** -0.5),
        'ssm_a_re': -0.5 + nrm(ks[18], (DEPTH, 2, SSM_G, SSM_P), 0.01),
        'ssm_a_im': a_im + nrm(ks[19], (DEPTH, 2, SSM_G, SSM_P), 0.01),
        'ssm_log_dt': jax.random.uniform(ks[20], (DEPTH, 2, SSM_G), f32, math.log(1e-3), math.log(1e-1)),
        'ssm_b_re': nrm(ks[21], (DEPTH, 2, SSM_G, SSM_P, SSM_GC), (2 * SSM_GC) ** -0.5),
        'ssm_b_im': nrm(ks[22], (DEPTH, 2, SSM_G, SSM_P, SSM_GC), (2 * SSM_GC) ** -0.5),
        'ssm_c_re': nrm(ks[23], (DEPTH, 2, SSM_G, SSM_GC, SSM_P), SSM_P ** -0.5),
        'ssm_c_im': nrm(ks[24], (DEPTH, 2, SSM_G, SSM_GC, SSM_P), SSM_P ** -0.5),
        'ssm_d': nrm(ks[25], (DEPTH, SSM_W), 1.0),
        'w_glu': nrm(ks[26], (DEPTH, SSM_W, 2 * D_MODEL), SSM_W ** -0.5),
        'w_out': nrm(ks[27], (DEPTH, D_MODEL, D_MODEL), D_MODEL ** -0.5),
        'router_w': nrm(ks[28], (DEPTH, D_MODEL, N_EXPERTS), D_MODEL ** -0.5),
        'w_gate': nrm(ks[29], (DEPTH, N_EXPERTS, D_MODEL, EXPERT_FF), D_MODEL ** -0.5),
        'w_up': nrm(ks[30], (DEPTH, N_EXPERTS, D_MODEL, EXPERT_FF), D_MODEL ** -0.5),
        'w_down': nrm(ks[31], (DEPTH, N_EXPERTS, EXPERT_FF, D_MODEL), EXPERT_FF ** -0.5),
    }


def reference(x_prompt, x_sample, cache_k, cache_v, state_ssm_re, state_ssm_im, c, c_ctx,
              ada_w, ada_b, norm1, norm2, final_norm, w_in, conv_w, w_conv_out, attn_sink,
              w_attn_out, ssm_a_re, ssm_a_im, ssm_log_dt, ssm_b_re, ssm_b_im, ssm_c_re, ssm_c_im,
              ssm_d, w_glu, w_out, router_w, w_gate, w_up, w_down):
    layer_params = (('norm1', norm1), ('norm2', norm2), ('w_in', w_in), ('conv_w', conv_w),
                    ('w_conv_out', w_conv_out), ('sink', attn_sink), ('w_attn_out', w_attn_out),
                    ('a_re', ssm_a_re), ('a_im', ssm_a_im), ('log_dt', ssm_log_dt),
                    ('b_re', ssm_b_re), ('b_im', ssm_b_im), ('c_re', ssm_c_re), ('c_im', ssm_c_im),
                    ('ssm_d', ssm_d), ('w_glu', w_glu), ('w_out', w_out), ('router_w', router_w),
                    ('w_gate', w_gate), ('w_up', w_up), ('w_down', w_down))
    silu_ctx = jax.nn.silu(c_ctx)
    silu_c = jax.nn.silu(c)
    xp, xs = x_prompt, x_sample
    ks, vs, hres, hims = [], [], [], []
    for l in range(DEPTH):
        lp = {name: arr[l] for name, arr in layer_params}
        mod_ctx = (silu_ctx @ ada_w[l] + ada_b[l])[None, None, :]
        mod_lat = (silu_c @ ada_w[l] + ada_b[l])[:, None, :]
        xp, (k_l, v_l, hf) = _block(xp, mod_ctx, lp)
        ks.append(k_l)
        vs.append(v_l)
        hres.append(hf.real.astype(xp.dtype))
        hims.append(hf.imag.astype(xp.dtype))
        h0 = lax.complex(state_ssm_re[:, l].astype(jnp.float32), state_ssm_im[:, l].astype(jnp.float32))
        xs = _block(xs, mod_lat, lp, cache=(cache_k[:, l], cache_v[:, l], h0))
    y_prompt = _rms(xp, final_norm)
    y_sample = _rms(xs, final_norm)
    new_k = jnp.stack(ks, axis=1)
    new_v = jnp.stack(vs, axis=1)
    new_h_re = jnp.stack(hres, axis=1)
    new_h_im = jnp.stack(hims, axis=1)
    return (y_prompt, y_sample, new_k, new_v, new_h_re, new_h_im)
```

```python
import functools
import math

import jax
import jax.numpy as jnp
from jax import lax
from jax.experimental import pallas as pl
from jax.experimental.pallas import tpu as pltpu

D_MODEL = 1024
DEPTH = 4
GRID_W = 64
CONV_W = 512
N_HEADS = 8
N_KV = 2
HEAD_DIM = 64
GROUP = N_HEADS // N_KV
Q_W = N_HEADS * HEAD_DIM
KV_W = N_KV * HEAD_DIM
BLOCK = 128
ROPE_BASE = 10000.0
NEG_INF = -1e30
SSM_W = 512
SSM_GC = 16
SSM_G = SSM_W // SSM_GC
SSM_P = 64
N_BRANCH = 3
N_EXPERTS = 16
EXPERT_FF = 1024
CAPACITY = 2
EPS = 1e-6

F32 = jnp.float32
BF16 = jnp.bfloat16
HIGHEST = lax.Precision.HIGHEST

LANES = 128
SUBLANES = 8
SSM_CHUNK = 16
SSM_PAIRS = SSM_G // 2
SUPER = 4096
MIB = 1024 * 1024


def _params(sem, vmem_mib):
    return pltpu.CompilerParams(dimension_semantics=sem, vmem_limit_bytes=vmem_mib * MIB)


def _const_spec(shape):
    nd = len(shape)
    return pl.BlockSpec(shape, lambda *_: (0,) * nd, pipeline_mode=pl.Buffered(1))


def _sigmoid(x):
    return 1.0 / (1.0 + jnp.exp(-x))


def _rms_mod(x, g, sc, sh):
    y = x * lax.rsqrt(jnp.mean(x * x, axis=-1, keepdims=True) + EPS)
    return (y * g) * (1.0 + sc) + sh


def _mod_kernel(c_ref, w_ref, b_ref, o_ref):
    cv = c_ref[...]
    s = cv * _sigmoid(cv)
    o_ref[...] = jnp.dot(s, w_ref[...], precision=HIGHEST, preferred_element_type=F32) + b_ref[...]


def _modulation(cvecs, ada_w, ada_b):
    nt = 1536
    return pl.pallas_call(
        _mod_kernel,
        grid=(DEPTH, 6 * D_MODEL // nt),
        in_specs=[pl.BlockSpec((SUBLANES, D_MODEL), lambda l, j: (0, 0)),
                  pl.BlockSpec((None, D_MODEL, nt), lambda l, j: (l, 0, j)),
                  pl.BlockSpec((None, 1, nt), lambda l, j: (l, 0, j))],
        out_specs=pl.BlockSpec((None, SUBLANES, nt), lambda l, j: (l, 0, j)),
        out_shape=jax.ShapeDtypeStruct((DEPTH, SUBLANES, 6 * D_MODEL), F32),
        compiler_params=_params(("parallel", "parallel"), 32),
        name="modulation",
    )(cvecs, ada_w, ada_b.reshape(DEPTH, 1, 6 * D_MODEL))


def _qkvu_kernel(*refs, rope, fuse_res):
    it = iter(refs)
    x_ref = next(it)
    if fuse_res:
        moe_ref = next(it)
        modp_ref = next(it)
    mod_ref = next(it)
    n1_ref = next(it)
    w_ref = next(it)
    if rope:
        cos_ref = next(it)
        sin_ref = next(it)
    q_ref, k_ref, v_ref, u_ref = next(it), next(it), next(it), next(it)
    x = x_ref[...]
    if fuse_res:
        xo_ref = next(it)
        x = x + modp_ref[5:6, :] * moe_ref[...]
        xo_ref[...] = x
    h = _rms_mod(x, n1_ref[...], mod_ref[1:2, :], mod_ref[0:1, :]).astype(BF16)
    p = jnp.dot(h, w_ref[...], preferred_element_type=F32)
    q = p[:, 0:Q_W]
    k = p[:, Q_W:Q_W + KV_W]
    if rope:
        cos = cos_ref[...]
        sin = sin_ref[...]
        o = Q_W + 2 * KV_W + SSM_W
        q = jnp.concatenate(
            [q[:, m * LANES:(m + 1) * LANES] * cos + p[:, o + m * LANES:o + (m + 1) * LANES] * sin
             for m in range(Q_W // LANES)], axis=1)
        k = k * cos + p[:, o + Q_W:o + Q_W + KV_W] * sin
    q_ref[...] = (q * HEAD_DIM ** -0.5).astype(q_ref.dtype)
    k_ref[...] = k.astype(k_ref.dtype)
    v_ref[...] = p[:, Q_W + KV_W:Q_W + 2 * KV_W].astype(v_ref.dtype)
    u_ref[...] = p[:, Q_W + 2 * KV_W:Q_W + 2 * KV_W + SSM_W]


def _mod_spec(mod):
    if mod.shape[0] == 1:
        return pl.BlockSpec((None, 6, D_MODEL), lambda b, i: (0, 0, 0))
    return pl.BlockSpec((None, 6, D_MODEL), lambda b, i: (b, 0, 0))


def _qkvu(x, moe, modp, mod, n1, w, rope_tabs, tm, kv_dtype):
    bsz, t, _ = x.shape
    rope = rope_tabs is not None
    fuse_res = moe is not None
    tok = lambda wd: pl.BlockSpec((None, tm, wd), lambda b, i: (b, i, 0))
    in_specs, args = [tok(D_MODEL)], [x]
    if fuse_res:
        in_specs += [tok(D_MODEL), _mod_spec(modp)]
        args += [moe, modp]
    in_specs += [_mod_spec(mod), _const_spec((1, D_MODEL)), _const_spec(w.shape)]
    args += [mod, n1, w]
    if rope:
        in_specs += [pl.BlockSpec((tm, LANES), lambda b, i: (i, 0))] * 2
        args += list(rope_tabs)
    out_specs = [tok(Q_W), tok(KV_W), tok(KV_W), tok(SSM_W)]
    out_shape = [jax.ShapeDtypeStruct((bsz, t, Q_W), BF16),
                 jax.ShapeDtypeStruct((bsz, t, KV_W), kv_dtype),
                 jax.ShapeDtypeStruct((bsz, t, KV_W), kv_dtype),
                 jax.ShapeDtypeStruct((bsz, t, SSM_W), F32)]
    if fuse_res:
        out_specs.append(tok(D_MODEL))
        out_shape.append(jax.ShapeDtypeStruct((bsz, t, D_MODEL), F32))
    return pl.pallas_call(
        functools.partial(_qkvu_kernel, rope=rope, fuse_res=fuse_res),
        grid=(bsz, t // tm), in_specs=in_specs, out_specs=out_specs, out_shape=out_shape,
        compiler_params=_params(("parallel", "parallel"), 48),
        name="qkvu",
    )(*args)


def _stack_heads(q_ref):
    lo = lax.broadcasted_iota(jnp.int32, (1, LANES), 1) < HEAD_DIM
    keep_lo = jnp.where(lo, 1.0, 0.0).astype(BF16)
    keep_hi = jnp.where(lo, 0.0, 1.0).astype(BF16)
    parts = []
    for m in range(GROUP):
        qm = q_ref[:, m * LANES:(m + 1) * LANES]
        parts.append(qm * keep_lo)
        parts.append(qm * keep_hi)
    return jnp.concatenate(parts, axis=0), lo


def _attend(s, bias, sink_ref, v, tq, lo, o_ref):
    ps, dens = [], []
    for h in range(N_HEADS):
        sh = s[h * tq:(h + 1) * tq]
        if bias is not None:
            sh = sh + bias
        sink = sink_ref[h]
        mx = jnp.maximum(jnp.max(sh, axis=-1, keepdims=True), sink)
        p = jnp.exp(sh - mx)
        dens.append(jnp.sum(p, axis=-1, keepdims=True) + jnp.exp(sink - mx))
        ps.append(p.astype(BF16))
    o = jnp.dot(jnp.concatenate(ps, axis=0), v, preferred_element_type=F32)
    for m in range(GROUP):
        o0 = o[(2 * m) * tq:(2 * m + 1) * tq] / dens[2 * m]
        o1 = o[(2 * m + 1) * tq:(2 * m + 2) * tq] / dens[2 * m + 1]
        o_ref[:, m * LANES:(m + 1) * LANES] = jnp.where(lo, o0, o1).astype(o_ref.dtype)


def _ctx_attn_kernel(sink_ref, q_ref, k_ref, v_ref, o_ref):
    tq = q_ref.shape[0]
    qx, lo = _stack_heads(q_ref)
    s = lax.dot_general(qx, k_ref[...].astype(BF16), (((1,), (1,)), ((), ())), preferred_element_type=F32)
    _attend(s, None, sink_ref, v_ref[...].astype(BF16), tq, lo, o_ref)


def _ctx_attention(sink, q, k, v):
    bsz, t, _ = q.shape
    tok = lambda wd: pl.BlockSpec((None, t, wd), lambda b: (b, 0, 0))
    return pl.pallas_call(
        _ctx_attn_kernel, grid=(bsz,),
        in_specs=[pl.BlockSpec(memory_space=pltpu.SMEM), tok(Q_W), tok(KV_W), tok(KV_W)],
        out_specs=tok(Q_W), out_shape=jax.ShapeDtypeStruct((bsz, t, Q_W), BF16),
        compiler_params=_params(("parallel",), 32),
        name="ctx_attention",
    )(sink, q, k, v)


def _lat_attn_kernel(sink_ref, q_ref, kc_ref, vc_ref, kp_ref, k0_ref, kn_ref, vp_ref, v0_ref, vn_ref, o_ref):
    tq = q_ref.shape[0]
    i = pl.program_id(1)
    nb = pl.num_programs(1)
    qx, lo = _stack_heads(q_ref)
    kall = jnp.concatenate([kc_ref[...].astype(BF16), kp_ref[...], k0_ref[...], kn_ref[...]], axis=0)
    vall = jnp.concatenate([vc_ref[...].astype(BF16), vp_ref[...], v0_ref[...], vn_ref[...]], axis=0)
    s = lax.dot_general(qx, kall, (((1,), (1,)), ((), ())), preferred_element_type=F32)
    past = kc_ref.shape[0]
    r = lax.broadcasted_iota(jnp.int32, (tq, BLOCK), 0)
    c = lax.broadcasted_iota(jnp.int32, (tq, BLOCK), 1)
    m_prev = jnp.where((c >= r) & (i > 0), 0.0, NEG_INF)
    m_next = jnp.where((c <= r) & (i < nb - 1), 0.0, NEG_INF)
    bias = jnp.concatenate([jnp.zeros((tq, past), F32), m_prev, jnp.zeros((tq, BLOCK), F32), m_next], axis=1)
    _attend(s, bias, sink_ref, vall, tq, lo, o_ref)


def _lat_attention(sink, q, k, v, kc, vc):
    bsz, t, _ = q.shape
    nb = t // BLOCK
    past = kc.shape[1]
    tok = lambda wd: pl.BlockSpec((None, BLOCK, wd), lambda b, i: (b, i, 0))
    prev = pl.BlockSpec((None, BLOCK, KV_W), lambda b, i: (b, jnp.maximum(i - 1, 0), 0))
    nxt = pl.BlockSpec((None, BLOCK, KV_W), lambda b, i: (b, jnp.minimum(i + 1, nb - 1), 0))
    ctx = pl.BlockSpec((None, past, KV_W), lambda b, i: (b, 0, 0))
    return pl.pallas_call(
        _lat_attn_kernel, grid=(bsz, nb),
        in_specs=[pl.BlockSpec(memory_space=pltpu.SMEM), tok(Q_W), ctx, ctx,
                  prev, tok(KV_W), nxt, prev, tok(KV_W), nxt],
        out_specs=tok(Q_W), out_shape=jax.ShapeDtypeStruct((bsz, t, Q_W), BF16),
        compiler_params=_params(("parallel", "parallel"), 32),
        name="lat_attention",
    )(sink, q, kc, vc, k, k, k, v, v, v)


def _ssm_kernel(u_ref, w1_ref, w2_ref, a_ref, h0_ref, d_ref, y_ref, fin_ref, z_ref, ent_ref, *, bsz, nchunk):
    u = u_ref[...]
    z_ref[...] = jnp.dot(u.astype(BF16), w1_ref[...], preferred_element_type=F32)
    afr, afi, abr, abi = a_ref[0:1, :], a_ref[1:2, :], a_ref[2:3, :], a_ref[3:4, :]
    rows = max(bsz, SUBLANES)
    steps = rows // bsz
    niter = nchunk // steps
    ns = SSM_W

    def step(sr, si, dr, di, ar, ai):
        return ar * sr - ai * si + dr, ar * si + ai * sr + di

    def body(it, carry):
        sr, si, gr, gi = carry
        rf = pl.multiple_of(it * rows, SUBLANES)
        rb = pl.multiple_of((niter - 1 - it) * rows, SUBLANES)
        dfr, dfi = z_ref[pl.ds(rf, rows), ns:ns + LANES], z_ref[pl.ds(rf, rows), ns + LANES:ns + 2 * LANES]
        dbr = z_ref[pl.ds(rb, rows), ns + 2 * LANES:ns + 3 * LANES]
        dbi = z_ref[pl.ds(rb, rows), ns + 3 * LANES:ns + 4 * LANES]
        efr, efi, ebr, ebi = [], [], [None] * steps, [None] * steps
        for j in range(steps):
            efr.append(sr)
            efi.append(si)
            sr, si = step(sr, si, dfr[j * bsz:(j + 1) * bsz], dfi[j * bsz:(j + 1) * bsz], afr, afi)
        for j in reversed(range(steps)):
            ebr[j] = gr
            ebi[j] = gi
            gr, gi = step(gr, gi, dbr[j * bsz:(j + 1) * bsz], dbi[j * bsz:(j + 1) * bsz], abr, abi)
        cat = lambda xs: xs[0] if len(xs) == 1 else jnp.concatenate(xs, axis=0)
        ent_ref[pl.ds(rf, rows), 0:LANES] = cat(efr)
        ent_ref[pl.ds(rf, rows), LANES:2 * LANES] = cat(efi)
        ent_ref[pl.ds(rb, rows), 2 * LANES:3 * LANES] = cat(ebr)
        ent_ref[pl.ds(rb, rows), 3 * LANES:4 * LANES] = cat(ebi)
        return sr, si, gr, gi

    init = (h0_ref[0], h0_ref[1], h0_ref[2], h0_ref[3])
    sr, si, gr, gi = lax.fori_loop(0, niter, body, init)
    fin_ref[0] = sr
    fin_ref[1] = si
    fin_ref[2] = gr
    fin_ref[3] = gi
    y_ref[...] = (z_ref[:, 0:SSM_W]
                  + jnp.dot(ent_ref[...].astype(BF16), w2_ref[...], preferred_element_type=F32)
                  + d_ref[...] * u)


def _ssm(u, ssm_w, h0):
    bsz, t, _ = u.shape
    w1, w2, a16, dflat = ssm_w
    nchunk = t // SSM_CHUNK
    rows = nchunk * bsz
    uf = u.reshape(bsz, nchunk, SSM_CHUNK, SSM_PAIRS, 2, SSM_GC)
    uf = uf.transpose(3, 1, 0, 4, 2, 5).reshape(SSM_PAIRS, rows, 2 * SSM_CHUNK * SSM_GC)
    if h0 is None:
        h0 = jnp.zeros((SSM_PAIRS, 4, bsz, LANES), F32)
    wcols = w1.shape[-1]
    y, fin = pl.pallas_call(
        functools.partial(_ssm_kernel, bsz=bsz, nchunk=nchunk),
        grid=(SSM_PAIRS,),
        in_specs=[pl.BlockSpec((None, rows, SSM_W), lambda g: (g, 0, 0)),
                  pl.BlockSpec((None, SSM_W, wcols), lambda g: (g, 0, 0)),
                  pl.BlockSpec((None, SSM_W, SSM_W), lambda g: (g, 0, 0)),
                  pl.BlockSpec((None, 4, LANES), lambda g: (g, 0, 0)),
                  pl.BlockSpec((None, 4, bsz, LANES), lambda g: (g, 0, 0, 0)),
                  pl.BlockSpec((None, 1, SSM_W), lambda g: (g, 0, 0))],
        out_specs=[pl.BlockSpec((None, rows, SSM_W), lambda g: (g, 0, 0)),
                   pl.BlockSpec((None, 4, bsz, LANES), lambda g: (g, 0, 0, 0))],
        out_shape=[jax.ShapeDtypeStruct((SSM_PAIRS, rows, SSM_W), F32),
                   jax.ShapeDtypeStruct((SSM_PAIRS, 4, bsz, LANES), F32)],
        scratch_shapes=[pltpu.VMEM((rows, wcols), F32), pltpu.VMEM((rows, SSM_W), F32)],
        compiler_params=_params(("parallel",), 48),
        name="ssm",
    )(uf, w1, w2, a16, h0, dflat)
    y = y.reshape(SSM_PAIRS, nchunk, bsz, 2, SSM_CHUNK, SSM_GC).transpose(2, 1, 4, 0, 3, 5)
    return y.reshape(bsz, t, SSM_W), fin


def _ssm_weights(a_re, a_im, log_dt, b_re, b_im, c_re, c_im, ssm_d):
    L = SSM_CHUNK
    lam = lax.complex(a_re, a_im)
    dt = jnp.exp(log_dt)[..., None]
    steps = jnp.arange(L + 1, dtype=F32)[:, None, None, None]
    apow = jnp.exp(lam[None] * dt[None] * steps)
    abar = apow[1]
    bbar = ((abar - 1.0) / lam)[..., None] * lax.complex(b_re, b_im)
    cmat = lax.complex(c_re, c_im)
    kern = jnp.einsum('dgcp,tdgp,dgpk->tdgck', cmat, apow[:L], bbar, precision=HIGHEST).real
    s_in = jnp.arange(L)[:, None]
    s_out = jnp.arange(L)[None, :]
    lag = s_out - s_in
    tf = jnp.where((lag >= 0)[:, :, None, None, None], kern[jnp.clip(lag, 0, L - 1), 0], 0.0)
    tb = jnp.where((lag <= 0)[:, :, None, None, None], kern[jnp.clip(-lag, 0, L - 1), 1], 0.0)
    toep = (tf + tb).transpose(2, 0, 4, 1, 3)
    ws_f = apow[L - 1 - jnp.arange(L), 0][..., None] * bbar[0][None]
    ws_b = apow[jnp.arange(L), 1][..., None] * bbar[1][None]
    wo_f = cmat[0][None] * apow[1 + jnp.arange(L), 0][:, :, None, :]
    wo_b = cmat[1][None] * apow[L - jnp.arange(L), 1][:, :, None, :]
    eye = jnp.eye(2, dtype=F32)

    def pair_in(w):
        w = w.reshape(SSM_PAIRS, 2, L, SSM_GC, -1)
        return jnp.einsum('qgscn,gh->qgschn', w, eye).reshape(SSM_PAIRS, 2 * L * SSM_GC, -1)

    def pair_out(w):
        w = w.reshape(SSM_PAIRS, 2, SSM_P, L * SSM_GC)
        return jnp.einsum('qgpn,gh->qgphn', w, eye).reshape(SSM_PAIRS, 2 * SSM_P, 2 * L * SSM_GC)

    st = lambda w: w.transpose(1, 0, 3, 2)
    w1 = jnp.concatenate([pair_in(toep.reshape(SSM_G, L, SSM_GC, L * SSM_GC)),
                          pair_in(st(ws_f.real)), pair_in(st(ws_f.imag)),
                          pair_in(st(ws_b.real)), pair_in(st(ws_b.imag))], axis=-1)
    ot = lambda w: w.transpose(1, 3, 0, 2)
    w2 = jnp.concatenate([pair_out(ot(wo_f.real)), pair_out(ot(-wo_f.imag)),
                          pair_out(ot(wo_b.real)), pair_out(ot(-wo_b.imag))], axis=1)
    a16 = apow[L].reshape(2, SSM_PAIRS, 2 * SSM_P)
    a16 = jnp.stack([a16[0].real, a16[0].imag, a16[1].real, a16[1].imag], axis=1)
    dflat = jnp.broadcast_to(ssm_d.reshape(SSM_PAIRS, 2, 1, SSM_GC), (SSM_PAIRS, 2, L, SSM_GC))
    return w1.astype(BF16), w2.astype(BF16), a16, dflat.reshape(SSM_PAIRS, 1, 2 * L * SSM_GC)


def _merge_kernel(x_ref, xp_ref, xn_ref, mod_ref, n1_ref, n2_ref, wcg_ref, cw_ref, wco_ref, attn_ref, wao_ref,
                  ssm_ref, wglu_ref, wout_ref, rw_ref, x1_ref, h2_ref, aff_ref):
    tm = x_ref.shape[0]
    i = pl.program_id(1)
    nt = pl.num_programs(1)
    x = x_ref[...]
    n1 = n1_ref[...]
    sh1, sc1, g1 = mod_ref[0:1, :], mod_ref[1:2, :], mod_ref[2:3, :]
    sh2, sc2 = mod_ref[3:4, :], mod_ref[4:5, :]
    cw = CONV_W

    def conv_in(h):
        xin = jnp.dot(h, wcg_ref[:, 0:cw], preferred_element_type=F32)
        cg = jnp.dot(h, wcg_ref[:, 2 * cw:3 * cw], preferred_element_type=F32)
        return cg * xin

    h = _rms_mod(x, n1, sc1, sh1).astype(BF16)
    z = conv_in(h)
    zp8 = conv_in(_rms_mod(xp_ref[...], n1, sc1, sh1).astype(BF16))
    zn8 = conv_in(_rms_mod(xn_ref[...], n1, sc1, sh1).astype(BF16))
    z_before = jnp.where(i > 0, zp8[SUBLANES - 1:SUBLANES, :], 0.0)
    z_after = jnp.where(i < nt - 1, zn8[0:1, :], 0.0)
    rows = lax.broadcasted_iota(jnp.int32, (tm, 1), 0)
    zl = jnp.where(rows == 0, z_before, pltpu.roll(z, 1, axis=0))
    zr = jnp.where(rows == tm - 1, z_after, pltpu.roll(z, tm - 1, axis=0))
    y = zl * cw_ref[0:1, :] + z * cw_ref[1:2, :] + zr * cw_ref[2:3, :]
    bg = jnp.dot(h, wcg_ref[:, cw:2 * cw], preferred_element_type=F32)
    conv_y = jnp.dot((bg * y).astype(BF16), wco_ref[...], preferred_element_type=F32)
    o = 3 * cw
    ga = _sigmoid(jnp.dot(h, wcg_ref[:, o:o + D_MODEL], preferred_element_type=F32))
    mixed = ga * conv_y
    attn_y = jnp.dot(attn_ref[...], wao_ref[...], preferred_element_type=F32)
    gb = _sigmoid(jnp.dot(h, wcg_ref[:, o + D_MODEL:o + 2 * D_MODEL], preferred_element_type=F32))
    mixed = mixed + gb * attn_y
    s = ssm_ref[...]
    gelu = s * (0.5 * (1.0 + jnp.tanh(math.sqrt(2.0 / math.pi) * (s + 0.044715 * (s * s * s)))))
    zab = jnp.dot(gelu.astype(BF16), wglu_ref[...], preferred_element_type=F32)
    ssm_y = zab[:, 0:D_MODEL] * _sigmoid(zab[:, D_MODEL:2 * D_MODEL])
    gc = _sigmoid(jnp.dot(h, wcg_ref[:, o + 2 * D_MODEL:o + 3 * D_MODEL], preferred_element_type=F32))
    mixed = mixed + gc * ssm_y
    x1 = x + g1 * jnp.dot(mixed.astype(BF16), wout_ref[...], preferred_element_type=F32)
    x1_ref[...] = x1
    h2 = _rms_mod(x1, n2_ref[...], sc2, sh2)
    h2_ref[...] = h2.astype(BF16)
    logits = lax.dot_general(rw_ref[...], h2, (((1,), (1,)), ((), ())), precision=HIGHEST,
                             preferred_element_type=F32)
    e = jnp.exp(logits - jnp.max(logits, axis=0, keepdims=True))
    aff_ref[...] = e / jnp.sum(e, axis=0, keepdims=True)


def _merge(x, mod, n1, n2, wcg, cw, wco, attn, wao, ssm, wglu, wout, rwt, tm):
    bsz, t, _ = x.shape
    nt = t // tm
    per = tm // SUBLANES
    tok = lambda wd: pl.BlockSpec((None, tm, wd), lambda b, i: (b, i, 0))
    prev = pl.BlockSpec((None, SUBLANES, D_MODEL), lambda b, i: (b, jnp.maximum(i * per - 1, 0), 0))
    nxt = pl.BlockSpec((None, SUBLANES, D_MODEL), lambda b, i: (b, jnp.minimum((i + 1) * per, t // SUBLANES - 1), 0))
    return pl.pallas_call(
        _merge_kernel, grid=(bsz, nt),
        in_specs=[tok(D_MODEL), prev, nxt, _mod_spec(mod), _const_spec((1, D_MODEL)), _const_spec((1, D_MODEL)),
                  _const_spec(wcg.shape), _const_spec(cw.shape), _const_spec(wco.shape), tok(Q_W),
                  _const_spec(wao.shape), tok(SSM_W), _const_spec(wglu.shape), _const_spec(wout.shape),
                  _const_spec(rwt.shape)],
        out_specs=[tok(D_MODEL), tok(D_MODEL), pl.BlockSpec((None, N_EXPERTS, tm), lambda b, i: (b, 0, i))],
        out_shape=[jax.ShapeDtypeStruct((bsz, t, D_MODEL), F32), jax.ShapeDtypeStruct((bsz, t, D_MODEL), BF16),
                   jax.ShapeDtypeStruct((bsz, N_EXPERTS, t), F32)],
        compiler_params=_params(("parallel", "parallel"), 56),
        name="merge",
    )(x, x, x, mod, n1, n2, wcg, cw, wco, attn, wao, ssm, wglu, wout, rwt)


def _route_kernel(aff_ref, slot_ref, cnt_ref, *, cap, slot_stride):
    n = aff_ref.shape[1]
    bits = pltpu.bitcast(aff_ref[...], jnp.int32)
    thr = jnp.zeros((N_EXPERTS, 1), jnp.int32)
    for b in range(30, -1, -1):
        cand = thr | (1 << b)
        cnt = jnp.sum(jnp.where(bits >= cand, 1.0, 0.0), axis=1, keepdims=True)
        thr = jnp.where(cnt >= cap, cand, thr)
    need = cap - jnp.sum(jnp.where(bits > thr, 1.0, 0.0), axis=1, keepdims=True)
    ri = lax.broadcasted_iota(jnp.int32, (LANES, LANES), 0)
    ci = lax.broadcasted_iota(jnp.int32, (LANES, LANES), 1)
    tri = jnp.where(ri <= ci, 1.0, 0.0).astype(BF16)
    lane = lax.broadcasted_iota(jnp.int32, (N_EXPERTS, LANES), 1)
    base = pl.program_id(0) * slot_stride
    off_eq = jnp.zeros((N_EXPERTS, 1), F32)
    off = jnp.zeros((N_EXPERTS, 1), F32)
    cnts = jnp.zeros((N_EXPERTS, LANES), jnp.int32)
    for j in range(n // LANES):
        sl = slice(j * LANES, (j + 1) * LANES)
        bj = bits[:, sl]
        eqf = jnp.where(bj == thr, 1.0, 0.0)
        ceq = jnp.dot(eqf.astype(BF16), tri, preferred_element_type=F32) + off_eq
        off_eq = ceq[:, LANES - 1:LANES]
        self_ = jnp.where(bj > thr, 1.0, jnp.where(ceq <= need, eqf, 0.0))
        csel = jnp.dot(self_.astype(BF16), tri, preferred_element_type=F32) + off
        cnts = jnp.where(lane == j, off.astype(jnp.int32) + base, cnts)
        off = csel[:, LANES - 1:LANES]
        slot_ref[:, sl] = jnp.where(self_ > 0.0, csel.astype(jnp.int32) - 1 + base, -1)
    cnt_ref[...] = jnp.where(lane >= n // LANES, off.astype(jnp.int32) + base, cnts)


def _route(aff, cap, slot_stride):
    sets, _, n = aff.shape
    return pl.pallas_call(
        functools.partial(_route_kernel, cap=cap, slot_stride=slot_stride), grid=(sets,),
        in_specs=[pl.BlockSpec((None, N_EXPERTS, n), lambda s: (s, 0, 0))],
        out_specs=[pl.BlockSpec((None, N_EXPERTS, n), lambda s: (s, 0, 0)),
                   pl.BlockSpec((None, N_EXPERTS, LANES), lambda s: (s, 0, 0))],
        out_shape=[jax.ShapeDtypeStruct((sets, N_EXPERTS, n), jnp.int32),
                   jax.ShapeDtypeStruct((sets, N_EXPERTS, LANES), jnp.int32)],
        compiler_params=_params(("parallel",), 32),
        name="route",
    )(aff)


def _expert_kernel(cnt_ref, h2_ref, slot_ref, gate_ref, wg_ref, wu_ref, wd_ref, o_ref, xs_ref, ys_ref, gs_ref, *,
                   nslot):
    s = pl.program_id(0)
    e = pl.program_id(1)
    ntb = SUPER // BLOCK
    nsb = nslot // BLOCK
    cbase = (s * N_EXPERTS + e) * (ntb + 1)
    srow = lax.broadcasted_iota(jnp.int32, (BLOCK, BLOCK), 0)

    @pl.when(e == 0)
    def _():
        o_ref[...] = jnp.zeros_like(o_ref)

    xs_ref[...] = jnp.zeros_like(xs_ref)
    gs_ref[...] = jnp.zeros_like(gs_ref)

    def hits(i, j):
        return (cnt_ref[cbase + i] < (j + 1) * BLOCK) & (cnt_ref[cbase + i + 1] > j * BLOCK)

    def gather(i, carry):
        t0 = pl.multiple_of(i * BLOCK, BLOCK)
        slots = slot_ref[pl.ds(i, 1), :]
        gates = gate_ref[pl.ds(i, 1), :]
        for j in range(nsb):
            @pl.when(hits(i, j))
            def _():
                pick = slots == srow + j * BLOCK
                xs_ref[j * BLOCK:(j + 1) * BLOCK, :] += jnp.dot(
                    jnp.where(pick, 1.0, 0.0).astype(BF16), h2_ref[pl.ds(t0, BLOCK), :],
                    preferred_element_type=F32)
                gs_ref[j * BLOCK:(j + 1) * BLOCK, :] += jnp.sum(jnp.where(pick, gates, 0.0), axis=1, keepdims=True)
        return carry

    lax.fori_loop(0, ntb, gather, 0)
    xs = xs_ref[...].astype(BF16)
    hg = jnp.dot(xs, wg_ref[...], preferred_element_type=F32)
    hu = jnp.dot(xs, wu_ref[...], preferred_element_type=F32)
    act = (hg * _sigmoid(hg) * hu).astype(BF16)
    y = jnp.dot(act, wd_ref[...], preferred_element_type=F32) * gs_ref[...]
    ys_ref[...] = y.astype(BF16)

    def scatter(i, carry):
        t0 = pl.multiple_of(i * BLOCK, BLOCK)
        slots = slot_ref[pl.ds(i, 1), :]
        for j in range(nsb):
            @pl.when(hits(i, j))
            def _():
                pick = jnp.where(slots == srow + j * BLOCK, 1.0, 0.0).astype(BF16)
                o_ref[pl.ds(t0, BLOCK), :] += lax.dot_general(
                    pick, ys_ref[j * BLOCK:(j + 1) * BLOCK, :], (((0,), (0,)), ((), ())),
                    preferred_element_type=F32)
        return carry

    lax.fori_loop(0, ntb, scatter, 0)


def _experts(cnt, h2, slot, gate, wg, wu, wd):
    nsup = h2.shape[0]
    nslot = CAPACITY * SUPER // N_EXPERTS
    ntb = SUPER // BLOCK
    grid_spec = pltpu.PrefetchScalarGridSpec(
        num_scalar_prefetch=1, grid=(nsup, N_EXPERTS),
        in_specs=[pl.BlockSpec((None, SUPER, D_MODEL), lambda s, e, c: (s, 0, 0), pipeline_mode=pl.Buffered(1)),
                  pl.BlockSpec((None, None, ntb, BLOCK), lambda s, e, c: (s, e, 0, 0)),
                  pl.BlockSpec((None, None, ntb, BLOCK), lambda s, e, c: (s, e, 0, 0)),
                  pl.BlockSpec((None, D_MODEL, EXPERT_FF), lambda s, e, c: (e, 0, 0)),
                  pl.BlockSpec((None, D_MODEL, EXPERT_FF), lambda s, e, c: (e, 0, 0)),
                  pl.BlockSpec((None, EXPERT_FF, D_MODEL), lambda s, e, c: (e, 0, 0))],
        out_specs=pl.BlockSpec((None, SUPER, D_MODEL), lambda s, e, c: (s, 0, 0), pipeline_mode=pl.Buffered(1)),
        scratch_shapes=[pltpu.VMEM((nslot, D_MODEL), F32), pltpu.VMEM((nslot, D_MODEL), BF16),
                        pltpu.VMEM((nslot, 1), F32)])
    return pl.pallas_call(
        functools.partial(_expert_kernel, nslot=nslot), grid_spec=grid_spec,
        out_shape=jax.ShapeDtypeStruct((nsup, SUPER, D_MODEL), F32),
        compiler_params=_params(("arbitrary", "arbitrary"), 56),
        name="experts",
    )(cnt, h2, slot, gate, wg, wu, wd)


def _final_kernel(x_ref, moe_ref, mod_ref, g_ref, o_ref):
    x = x_ref[...] + mod_ref[5:6, :] * moe_ref[...]
    o_ref[...] = x * lax.rsqrt(jnp.mean(x * x, axis=-1, keepdims=True) + EPS) * g_ref[...]


def _final(x1, moe, mod, g, tm):
    bsz, t, _ = x1.shape
    tok = pl.BlockSpec((None, tm, D_MODEL), lambda b, i: (b, i, 0))
    return pl.pallas_call(
        _final_kernel, grid=(bsz, t // tm),
        in_specs=[tok, tok, _mod_spec(mod), _const_spec((1, D_MODEL))],
        out_specs=tok, out_shape=jax.ShapeDtypeStruct((bsz, t, D_MODEL), F32),
        compiler_params=_params(("parallel", "parallel"), 32),
        name="final_norm",
    )(x1, moe, mod, g)


def _head_perm():
    cols = []
    for m in range(GROUP):
        for kv in range(N_KV):
            head = kv * GROUP + m
            cols.extend(range(head * HEAD_DIM, (head + 1) * HEAD_DIM))
    return jnp.array(cols, jnp.int32)


def _rope_swap(width):
    idx = jnp.arange(width)
    nf = HEAD_DIM // 4
    return jnp.where((idx % (2 * nf)) < nf, idx + nf, idx - nf)


def _rope_tables(t):
    pos = jnp.arange(t)
    row = (pos // GRID_W).astype(F32)
    col = (pos % GRID_W).astype(F32)
    nf = HEAD_DIM // 4
    inv = ROPE_BASE ** (-jnp.arange(nf, dtype=F32) / nf)

    def tabs(p):
        ang = p[:, None] * inv[None, :]
        cos, sin = jnp.cos(ang), jnp.sin(ang)
        return jnp.concatenate([cos, cos], axis=1), jnp.concatenate([-sin, sin], axis=1)

    cr, sr = tabs(row)
    cc, sc = tabs(col)
    cos = jnp.concatenate([cr, cc], axis=1)
    sin = jnp.concatenate([sr, sc], axis=1)
    return jnp.tile(cos, (1, LANES // HEAD_DIM)), jnp.tile(sin, (1, LANES // HEAD_DIM))


def _route_and_experts(h2c, affc, h2l, affl, wg, wu, wd):
    bc, tc, _ = h2c.shape
    bl, tl, _ = h2l.shape
    ntb = SUPER // BLOCK
    capc = CAPACITY * tc // N_EXPERTS
    capl = CAPACITY * tl // N_EXPERTS
    slot_c, cnt_c = _route(affc, capc, capc)
    slot_l, cnt_l = _route(affl, capl, 0)
    tbc = tc // BLOCK
    slot_c = slot_c.transpose(1, 0, 2).reshape(1, N_EXPERTS, ntb, BLOCK)
    gate_c = affc.transpose(1, 0, 2).reshape(1, N_EXPERTS, ntb, BLOCK)
    cnt_c = jnp.concatenate([cnt_c[:, :, :tbc].transpose(1, 0, 2).reshape(N_EXPERTS, ntb),
                             cnt_c[bc - 1, :, tbc:tbc + 1]], axis=1)[None]
    slot = jnp.concatenate([slot_c, slot_l.reshape(bl, N_EXPERTS, ntb, BLOCK)], axis=0)
    gate = jnp.concatenate([gate_c, affl.reshape(bl, N_EXPERTS, ntb, BLOCK)], axis=0)
    cnt = jnp.concatenate([cnt_c, cnt_l[:, :, :ntb + 1]], axis=0).reshape(-1)
    h2 = jnp.concatenate([h2c.reshape(1, SUPER, D_MODEL), h2l], axis=0)
    moe = _experts(cnt, h2, slot, gate, wg, wu, wd)
    return moe[0].reshape(bc, tc, D_MODEL), moe[1:]


def kernel(x_prompt, x_sample, cache_k, cache_v, state_ssm_re, state_ssm_im, c, c_ctx, ada_w, ada_b, norm1, norm2,
           final_norm, w_in, conv_w, w_conv_out, attn_sink, w_attn_out, ssm_a_re, ssm_a_im, ssm_log_dt, ssm_b_re,
           ssm_b_im, ssm_c_re, ssm_c_im, ssm_d, w_glu, w_out, router_w, w_gate, w_up, w_down):
    bc, tc, _ = x_prompt.shape
    bl, tl, _ = x_sample.shape
    assert bc * tc == SUPER and tl == SUPER and bl + 1 <= SUBLANES
    hp = _head_perm()
    cvecs = jnp.zeros((SUBLANES, D_MODEL), F32).at[:bl].set(c).at[bl].set(c_ctx)
    mods = _modulation(cvecs, ada_w, ada_b).reshape(DEPTH, SUBLANES, 6, D_MODEL)

    o_q, o_k, o_v, o_u, o_g = 3 * CONV_W, 3 * CONV_W + Q_W, 3 * CONV_W + Q_W + KV_W, 3 * CONV_W + Q_W + 2 * KV_W, \
        3 * CONV_W + Q_W + 2 * KV_W + SSM_W
    wq = w_in[:, :, o_q:o_k][:, :, hp]
    wk = w_in[:, :, o_k:o_v]
    w_qkvu = jnp.concatenate([wq, wk, w_in[:, :, o_v:o_g]], axis=-1)
    w_qkvu_l = jnp.concatenate([w_qkvu, wq[:, :, _rope_swap(Q_W)], wk[:, :, _rope_swap(KV_W)]], axis=-1).astype(BF16)
    w_qkvu_c = w_qkvu.astype(BF16)
    w_cg = jnp.concatenate([w_in[:, :, :o_q], w_in[:, :, o_g:]], axis=-1).astype(BF16)
    w_co = w_conv_out.astype(BF16)
    w_ao = w_attn_out[:, hp, :].astype(BF16)
    w_gl = w_glu.astype(BF16)
    w_o = w_out.astype(BF16)
    rwt = router_w.transpose(0, 2, 1)
    wg, wu, wd = w_gate.astype(BF16), w_up.astype(BF16), w_down.astype(BF16)
    sink = attn_sink.reshape(DEPTH, N_KV, GROUP).transpose(0, 2, 1).reshape(DEPTH, N_HEADS)
    rope_tabs = _rope_tables(tl)
    h0_all = jnp.stack([state_ssm_re[:, :, 0], state_ssm_im[:, :, 0], state_ssm_re[:, :, 1], state_ssm_im[:, :, 1]],
                       axis=0)
    h0_all = h0_all.reshape(4, bl, DEPTH, SSM_PAIRS, 2 * SSM_P).transpose(2, 3, 0, 1, 4)

    xp, xs = x_prompt, x_sample
    moe_c = moe_l = None
    modp_c = modp_l = None
    ks, vs, fins = [], [], []
    for l in range(DEPTH):
        mod_l = mods[l, :bl]
        mod_c = mods[l, bl:bl + 1]
        n1 = norm1[l].reshape(1, D_MODEL)
        n2 = norm2[l].reshape(1, D_MODEL)
        ssm_w = _ssm_weights(ssm_a_re[l], ssm_a_im[l], ssm_log_dt[l], ssm_b_re[l], ssm_b_im[l], ssm_c_re[l],
                             ssm_c_im[l], ssm_d[l])
        outs = _qkvu(xp, moe_c, modp_c, mod_c, n1, w_qkvu_c[l], None, tc, F32)
        q_c, k_c, v_c, u_c = outs[:4]
        if moe_c is not None:
            xp = outs[4]
        ks.append(k_c)
        vs.append(v_c)
        attn_c = _ctx_attention(sink[l], q_c, k_c, v_c)
        ssm_c, fin = _ssm(u_c, ssm_w, None)
        fins.append(fin)
        x1c, h2c, affc = _merge(xp, mod_c, n1, n2, w_cg[l], conv_w[l], w_co[l], attn_c, w_ao[l], ssm_c, w_gl[l],
                                w_o[l], rwt[l], tc)
        outs = _qkvu(xs, moe_l, modp_l, mod_l, n1, w_qkvu_l[l], rope_tabs, 512, BF16)
        q_l, k_l, v_l, u_l = outs[:4]
        if moe_l is not None:
            xs = outs[4]
        attn_l = _lat_attention(sink[l], q_l, k_l, v_l, cache_k[:, l].reshape(bl, -1, KV_W),
                                cache_v[:, l].reshape(bl, -1, KV_W))
        ssm_l, _ = _ssm(u_l, ssm_w, h0_all[l])
        x1l, h2l, affl = _merge(xs, mod_l, n1, n2, w_cg[l], conv_w[l], w_co[l], attn_l, w_ao[l], ssm_l, w_gl[l],
                                w_o[l], rwt[l], 256)
        moe_c, moe_l = _route_and_experts(h2c, affc, h2l, affl, wg[l], wu[l], wd[l])
        xp, xs = x1c, x1l
        modp_c, modp_l = mod_c, mod_l

    fn = final_norm.reshape(1, D_MODEL)
    y_prompt = _final(xp, moe_c, modp_c, fn, tc)
    y_sample = _final(xs, moe_l, modp_l, fn, 512)
    new_k = jnp.stack(ks, axis=1).reshape(bc, DEPTH, tc, N_KV, HEAD_DIM)
    new_v = jnp.stack(vs, axis=1).reshape(bc, DEPTH, tc, N_KV, HEAD_DIM)
    fin = jnp.stack(fins, axis=0)
    fin = fin.reshape(DEPTH, SSM_PAIRS, 2, 2, bc, 2, SSM_P).transpose(3, 4, 0, 2, 1, 5, 6)
    fin = fin.reshape(2, bc, DEPTH, 2, SSM_G, SSM_P)
    return (y_prompt, y_sample, new_k, new_v, fin[0], fin[1])
```

```python
import functools
import math

import jax
import jax.numpy as jnp
from jax import lax
from jax.experimental import pallas as pl
from jax.experimental.pallas import tpu as pltpu

D_MODEL = 1024
DEPTH = 4
GRID_W = 64
CONV_W = 512
N_HEADS = 8
N_KV = 2
HEAD_DIM = 64
GROUP = N_HEADS // N_KV
Q_W = N_HEADS * HEAD_DIM
KV_W = N_KV * HEAD_DIM
BLOCK = 128
ROPE_BASE = 10000.0
NEG_INF = -1e30
SSM_W = 512
SSM_GC = 16
SSM_G = SSM_W // SSM_GC
SSM_P = 64
N_BRANCH = 3
N_EXPERTS = 16
EXPERT_FF = 1024
CAPACITY = 2
EPS = 1e-6

F32 = jnp.float32
BF16 = jnp.bfloat16
HIGHEST = lax.Precision.HIGHEST

LANES = 128
SUBLANES = 8
SSM_CHUNK = 16
SSM_PAIRS = SSM_G // 2
SUPER = 4096
MIB = 1024 * 1024


def _params(sem, vmem_mib):
    return pltpu.CompilerParams(dimension_semantics=sem, vmem_limit_bytes=vmem_mib * MIB)


def _const_spec(shape):
    nd = len(shape)
    return pl.BlockSpec(shape, lambda *_: (0,) * nd, pipeline_mode=pl.Buffered(1))


def _sigmoid(x):
    return 1.0 / (1.0 + jnp.exp(-x))


def _rms_mod(x, g, sc, sh):
    y = x * lax.rsqrt(jnp.mean(x * x, axis=-1, keepdims=True) + EPS)
    return (y * g) * (1.0 + sc) + sh


def _mod_kernel(c_ref, w_ref, b_ref, o_ref):
    cv = c_ref[...]
    s = cv * _sigmoid(cv)
    o_ref[...] = jnp.dot(s, w_ref[...], precision=HIGHEST, preferred_element_type=F32) + b_ref[...]


def _modulation(cvecs, ada_w, ada_b):
    nt = 1536
    return pl.pallas_call(
        _mod_kernel,
        grid=(DEPTH, 6 * D_MODEL // nt),
        in_specs=[pl.BlockSpec((SUBLANES, D_MODEL), lambda l, j: (0, 0)),
                  pl.BlockSpec((None, D_MODEL, nt), lambda l, j: (l, 0, j)),
                  pl.BlockSpec((None, 1, nt), lambda l, j: (l, 0, j))],
        out_specs=pl.BlockSpec((None, SUBLANES, nt), lambda l, j: (l, 0, j)),
        out_shape=jax.ShapeDtypeStruct((DEPTH, SUBLANES, 6 * D_MODEL), F32),
        compiler_params=_params(("parallel", "parallel"), 32),
        name="modulation",
    )(cvecs, ada_w, ada_b.reshape(DEPTH, 1, 6 * D_MODEL))


def _qkvu_kernel(*refs, rope, fuse_res):
    it = iter(refs)
    x_ref = next(it)
    if fuse_res:
        moe_ref = next(it)
        modp_ref = next(it)
    mod_ref = next(it)
    n1_ref = next(it)
    w_ref = next(it)
    if rope:
        cos_ref = next(it)
        sin_ref = next(it)
    q_ref, k_ref, v_ref, u_ref = next(it), next(it), next(it), next(it)
    x = x_ref[...]
    if fuse_res:
        xo_ref = next(it)
        x = x + modp_ref[5:6, :] * moe_ref[...]
        xo_ref[...] = x
    h = _rms_mod(x, n1_ref[...], mod_ref[1:2, :], mod_ref[0:1, :]).astype(BF16)
    p = jnp.dot(h, w_ref[...], preferred_element_type=F32)
    q = p[:, 0:Q_W]
    k = p[:, Q_W:Q_W + KV_W]
    if rope:
        cos = cos_ref[...]
        sin = sin_ref[...]
        o = Q_W + 2 * KV_W + SSM_W
        q = jnp.concatenate(
            [q[:, m * LANES:(m + 1) * LANES] * cos + p[:, o + m * LANES:o + (m + 1) * LANES] * sin
             for m in range(Q_W // LANES)], axis=1)
        k = k * cos + p[:, o + Q_W:o + Q_W + KV_W] * sin
    q_ref[...] = (q * HEAD_DIM ** -0.5).astype(q_ref.dtype)
    k_ref[...] = k.astype(k_ref.dtype)
    v_ref[...] = p[:, Q_W + KV_W:Q_W + 2 * KV_W].astype(v_ref.dtype)
    u_ref[...] = p[:, Q_W + 2 * KV_W:Q_W + 2 * KV_W + SSM_W]


def _mod_spec(mod):
    if mod.shape[0] == 1:
        return pl.BlockSpec((None, 6, D_MODEL), lambda b, i: (0, 0, 0))
    return pl.BlockSpec((None, 6, D_MODEL), lambda b, i: (b, 0, 0))


def _qkvu(x, moe, modp, mod, n1, w, rope_tabs, tm, kv_dtype):
    bsz, t, _ = x.shape
    rope = rope_tabs is not None
    fuse_res = moe is not None
    tok = lambda wd: pl.BlockSpec((None, tm, wd), lambda b, i: (b, i, 0))
    in_specs, args = [tok(D_MODEL)], [x]
    if fuse_res:
        in_specs += [tok(D_MODEL), _mod_spec(modp)]
        args += [moe, modp]
    in_specs += [_mod_spec(mod), _const_spec((1, D_MODEL)), _const_spec(w.shape)]
    args += [mod, n1, w]
    if rope:
        in_specs += [pl.BlockSpec((tm, LANES), lambda b, i: (i, 0))] * 2
        args += list(rope_tabs)
    out_specs = [tok(Q_W), tok(KV_W), tok(KV_W), tok(SSM_W)]
    out_shape = [jax.ShapeDtypeStruct((bsz, t, Q_W), BF16),
                 jax.ShapeDtypeStruct((bsz, t, KV_W), kv_dtype),
                 jax.ShapeDtypeStruct((bsz, t, KV_W), kv_dtype),
                 jax.ShapeDtypeStruct((bsz, t, SSM_W), F32)]
    if fuse_res:
        out_specs.append(tok(D_MODEL))
        out_shape.append(jax.ShapeDtypeStruct((bsz, t, D_MODEL), F32))
    return pl.pallas_call(
        functools.partial(_qkvu_kernel, rope=rope, fuse_res=fuse_res),
        grid=(bsz, t // tm), in_specs=in_specs, out_specs=out_specs, out_shape=out_shape,
        compiler_params=_params(("parallel", "parallel"), 48),
        name="qkvu",
    )(*args)


def _stack_heads(q_ref):
    lo = lax.broadcasted_iota(jnp.int32, (1, LANES), 1) < HEAD_DIM
    keep_lo = jnp.where(lo, 1.0, 0.0).astype(BF16)
    keep_hi = jnp.where(lo, 0.0, 1.0).astype(BF16)
    parts = []
    for m in range(GROUP):
        qm = q_ref[:, m * LANES:(m + 1) * LANES]
        parts.append(qm * keep_lo)
        parts.append(qm * keep_hi)
    return jnp.concatenate(parts, axis=0), lo


def _attend(s, bias, sink_ref, v, tq, lo, o_ref):
    ps, dens = [], []
    for h in range(N_HEADS):
        sh = s[h * tq:(h + 1) * tq]
        if bias is not None:
            sh = sh + bias
        sink = sink_ref[h]
        mx = jnp.maximum(jnp.max(sh, axis=-1, keepdims=True), sink)
        p = jnp.exp(sh - mx)
        dens.append(jnp.sum(p, axis=-1, keepdims=True) + jnp.exp(sink - mx))
        ps.append(p.astype(BF16))
    o = jnp.dot(jnp.concatenate(ps, axis=0), v, preferred_element_type=F32)
    for m in range(GROUP):
        o0 = o[(2 * m) * tq:(2 * m + 1) * tq] / dens[2 * m]
        o1 = o[(2 * m + 1) * tq:(2 * m + 2) * tq] / dens[2 * m + 1]
        o_ref[:, m * LANES:(m + 1) * LANES] = jnp.where(lo, o0, o1).astype(o_ref.dtype)


def _ctx_attn_kernel(sink_ref, q_ref, k_ref, v_ref, o_ref):
    tq = q_ref.shape[0]
    qx, lo = _stack_heads(q_ref)
    s = lax.dot_general(qx, k_ref[...].astype(BF16), (((1,), (1,)), ((), ())), preferred_element_type=F32)
    _attend(s, None, sink_ref, v_ref[...].astype(BF16), tq, lo, o_ref)


def _ctx_attention(sink, q, k, v):
    bsz, t, _ = q.shape
    tok = lambda wd: pl.BlockSpec((None, t, wd), lambda b: (b, 0, 0))
    return pl.pallas_call(
        _ctx_attn_kernel, grid=(bsz,),
        in_specs=[pl.BlockSpec(memory_space=pltpu.SMEM), tok(Q_W), tok(KV_W), tok(KV_W)],
        out_specs=tok(Q_W), out_shape=jax.ShapeDtypeStruct((bsz, t, Q_W), BF16),
        compiler_params=_params(("parallel",), 32),
        name="ctx_attention",
    )(sink, q, k, v)


def _lat_attn_kernel(sink_ref, q_ref, kc_ref, vc_ref, kp_ref, k0_ref, kn_ref, vp_ref, v0_ref, vn_ref, o_ref):
    tq = q_ref.shape[0]
    i = pl.program_id(1)
    nb = pl.num_programs(1)
    qx, lo = _stack_heads(q_ref)
    kall = jnp.concatenate([kc_ref[...].astype(BF16), kp_ref[...], k0_ref[...], kn_ref[...]], axis=0)
    vall = jnp.concatenate([vc_ref[...].astype(BF16), vp_ref[...], v0_ref[...], vn_ref[...]], axis=0)
    s = lax.dot_general(qx, kall, (((1,), (1,)), ((), ())), preferred_element_type=F32)
    past = kc_ref.shape[0]
    r = lax.broadcasted_iota(jnp.int32, (tq, BLOCK), 0)
    c = lax.broadcasted_iota(jnp.int32, (tq, BLOCK), 1)
    m_prev = jnp.where((c >= r) & (i > 0), 0.0, NEG_INF)
    m_next = jnp.where((c <= r) & (i < nb - 1), 0.0, NEG_INF)
    bias = jnp.concatenate([jnp.zeros((tq, past), F32), m_prev, jnp.zeros((tq, BLOCK), F32), m_next], axis=1)
    _attend(s, bias, sink_ref, vall, tq, lo, o_ref)


def _lat_attention(sink, q, k, v, kc, vc):
    bsz, t, _ = q.shape
    nb = t // BLOCK
    past = kc.shape[1]
    tok = lambda wd: pl.BlockSpec((None, BLOCK, wd), lambda b, i: (b, i, 0))
    prev = pl.BlockSpec((None, BLOCK, KV_W), lambda b, i: (b, jnp.maximum(i - 1, 0), 0))
    nxt = pl.BlockSpec((None, BLOCK, KV_W), lambda b, i: (b, jnp.minimum(i + 1, nb - 1), 0))
    ctx = pl.BlockSpec((None, past, KV_W), lambda b, i: (b, 0, 0))
    return pl.pallas_call(
        _lat_attn_kernel, grid=(bsz, nb),
        in_specs=[pl.BlockSpec(memory_space=pltpu.SMEM), tok(Q_W), ctx, ctx,
                  prev, tok(KV_W), nxt, prev, tok(KV_W), nxt],
        out_specs=tok(Q_W), out_shape=jax.ShapeDtypeStruct((bsz, t, Q_W), BF16),
        compiler_params=_params(("parallel", "parallel"), 32),
        name="lat_attention",
    )(sink, q, kc, vc, k, k, k, v, v, v)


SSM_QUAD = LANES // (2 * SSM_GC)
GRANULES = LANES // SSM_GC


def _ssm_kernel(u_ref, w1_ref, w2_ref, a_ref, h0_ref, d_ref, y_ref, fin_ref, pk_ref, z_ref, ent_ref, *, bsz, t):
    nk = t // SSM_CHUNK
    gran = lax.broadcasted_iota(jnp.int32, (1, LANES), 1) // SSM_GC
    half = SSM_CHUNK // GRANULES

    def pack(b, carry):
        for p in range(SSM_QUAD):
            for col in range(2 * half):
                gl, sh = col // half, col % half
                src = 2 * p + gl
                acc = None
                for s8 in range(GRANULES):
                    x = u_ref[pl.ds(b * t + sh * GRANULES + s8, nk, stride=SSM_CHUNK), :]
                    shift = (SSM_GC * (s8 - src)) % LANES
                    x = pltpu.roll(x, shift, axis=1) if shift else x
                    acc = x if acc is None else jnp.where(gran == s8, x, acc)
                pk_ref[p, col, pl.ds(b, nk, stride=bsz), :] = acc
        return carry

    lax.fori_loop(0, bsz, pack, 0)

    rows = max(bsz, SUBLANES)
    steps = rows // bsz
    niter = nk // steps
    ns = SSM_W

    def step(sr, si, dr, di, ar, ai):
        return ar * sr - ai * si + dr, ar * si + ai * sr + di

    for p in range(SSM_QUAD):
        u = jnp.concatenate([pk_ref[p, col] for col in range(2 * half)], axis=1)
        z_ref[...] = jnp.dot(u.astype(BF16), w1_ref[p], preferred_element_type=F32)
        afr, afi, abr, abi = a_ref[p, 0:1, :], a_ref[p, 1:2, :], a_ref[p, 2:3, :], a_ref[p, 3:4, :]

        def body(it, carry):
            sr, si, gr, gi = carry
            rf = pl.multiple_of(it * rows, SUBLANES)
            rb = pl.multiple_of((niter - 1 - it) * rows, SUBLANES)
            dfr, dfi = z_ref[pl.ds(rf, rows), ns:ns + LANES], z_ref[pl.ds(rf, rows), ns + LANES:ns + 2 * LANES]
            dbr = z_ref[pl.ds(rb, rows), ns + 2 * LANES:ns + 3 * LANES]
            dbi = z_ref[pl.ds(rb, rows), ns + 3 * LANES:ns + 4 * LANES]
            efr, efi, ebr, ebi = [], [], [None] * steps, [None] * steps
            for j in range(steps):
                efr.append(sr)
                efi.append(si)
                sr, si = step(sr, si, dfr[j * bsz:(j + 1) * bsz], dfi[j * bsz:(j + 1) * bsz], afr, afi)
            for j in reversed(range(steps)):
                ebr[j] = gr
                ebi[j] = gi
                gr, gi = step(gr, gi, dbr[j * bsz:(j + 1) * bsz], dbi[j * bsz:(j + 1) * bsz], abr, abi)
            cat = lambda xs: xs[0] if len(xs) == 1 else jnp.concatenate(xs, axis=0)
            ent_ref[pl.ds(rf, rows), 0:LANES] = cat(efr)
            ent_ref[pl.ds(rf, rows), LANES:2 * LANES] = cat(efi)
            ent_ref[pl.ds(rb, rows), 2 * LANES:3 * LANES] = cat(ebr)
            ent_ref[pl.ds(rb, rows), 3 * LANES:4 * LANES] = cat(ebi)
            return sr, si, gr, gi

        init = (h0_ref[p, 0], h0_ref[p, 1], h0_ref[p, 2], h0_ref[p, 3])
        sr, si, gr, gi = lax.fori_loop(0, niter, body, init)
        fin_ref[p, 0] = sr
        fin_ref[p, 1] = si
        fin_ref[p, 2] = gr
        fin_ref[p, 3] = gi
        y = (z_ref[:, 0:SSM_W] + jnp.dot(ent_ref[...].astype(BF16), w2_ref[p], preferred_element_type=F32)
             + d_ref[p] * u)
        for col in range(2 * half):
            pk_ref[p, col] = y[:, col * LANES:(col + 1) * LANES]

    def unpack(b, carry):
        for s in range(SSM_CHUNK):
            sh, s8 = s // GRANULES, s % GRANULES
            acc = None
            for dst in range(GRANULES):
                p, gl = dst // 2, dst % 2
                col = gl * half + sh
                x = pk_ref[p, col, pl.ds(b, nk, stride=bsz), :]
                shift = (SSM_GC * (dst - s8)) % LANES
                x = pltpu.roll(x, shift, axis=1) if shift else x
                acc = x if acc is None else jnp.where(gran == dst, x, acc)
            y_ref[pl.ds(b * t + s, nk, stride=SSM_CHUNK), :] = acc
        return carry

    lax.fori_loop(0, bsz, unpack, 0)


def _ssm(u, ssm_w, h0):
    bsz, t, _ = u.shape
    w1, w2, a16, dflat = ssm_w
    rows = (t // SSM_CHUNK) * bsz
    if h0 is None:
        h0 = jnp.zeros((SSM_PAIRS, 4, bsz, LANES), F32)
    wcols = w1.shape[-1]
    quad = lambda *tail: pl.BlockSpec((SSM_QUAD,) + tail, lambda q: (q,) + (0,) * len(tail))
    tok = pl.BlockSpec((bsz * t, LANES), lambda q: (0, q), pipeline_mode=pl.Buffered(1))
    y, fin = pl.pallas_call(
        functools.partial(_ssm_kernel, bsz=bsz, t=t),
        grid=(SSM_PAIRS // SSM_QUAD,),
        in_specs=[tok, quad(SSM_W, wcols), quad(SSM_W, SSM_W), quad(4, LANES), quad(4, bsz, LANES), quad(1, SSM_W)],
        out_specs=[tok, quad(4, bsz, LANES)],
        out_shape=[jax.ShapeDtypeStruct((bsz * t, SSM_W), F32),
                   jax.ShapeDtypeStruct((SSM_PAIRS, 4, bsz, LANES), F32)],
        scratch_shapes=[pltpu.VMEM((SSM_QUAD, SSM_W // LANES, rows, LANES), F32), pltpu.VMEM((rows, wcols), F32),
                        pltpu.VMEM((rows, SSM_W), F32)],
        compiler_params=_params(("parallel",), 52),
        name="ssm",
    )(u.reshape(bsz * t, SSM_W), w1, w2, a16, h0, dflat)
    return y.reshape(bsz, t, SSM_W), fin


def _ssm_weights(a_re, a_im, log_dt, b_re, b_im, c_re, c_im, ssm_d):
    L = SSM_CHUNK
    lam = lax.complex(a_re, a_im)
    dt = jnp.exp(log_dt)[..., None]
    steps = jnp.arange(L + 1, dtype=F32)[:, None, None, None]
    apow = jnp.exp(lam[None] * dt[None] * steps)
    abar = apow[1]
    bbar = ((abar - 1.0) / lam)[..., None] * lax.complex(b_re, b_im)
    cmat = lax.complex(c_re, c_im)
    kern = jnp.einsum('dgcp,tdgp,dgpk->tdgck', cmat, apow[:L], bbar, precision=HIGHEST).real
    s_in = jnp.arange(L)[:, None]
    s_out = jnp.arange(L)[None, :]
    lag = s_out - s_in
    tf = jnp.where((lag >= 0)[:, :, None, None, None], kern[jnp.clip(lag, 0, L - 1), 0], 0.0)
    tb = jnp.where((lag <= 0)[:, :, None, None, None], kern[jnp.clip(-lag, 0, L - 1), 1], 0.0)
    toep = (tf + tb).transpose(2, 0, 4, 1, 3)
    ws_f = apow[L - 1 - jnp.arange(L), 0][..., None] * bbar[0][None]
    ws_b = apow[jnp.arange(L), 1][..., None] * bbar[1][None]
    wo_f = cmat[0][None] * apow[1 + jnp.arange(L), 0][:, :, None, :]
    wo_b = cmat[1][None] * apow[L - jnp.arange(L), 1][:, :, None, :]
    eye = jnp.eye(2, dtype=F32)

    def pair_in(w):
        w = w.reshape(SSM_PAIRS, 2, L, SSM_GC, -1)
        return jnp.einsum('qgscn,gh->qgschn', w, eye).reshape(SSM_PAIRS, 2 * L * SSM_GC, -1)

    def pair_out(w):
        w = w.reshape(SSM_PAIRS, 2, SSM_P, L * SSM_GC)
        return jnp.einsum('qgpn,gh->qgphn', w, eye).reshape(SSM_PAIRS, 2 * SSM_P, 2 * L * SSM_GC)

    st = lambda w: w.transpose(1, 0, 3, 2)
    w1 = jnp.concatenate([pair_in(toep.reshape(SSM_G, L, SSM_GC, L * SSM_GC)),
                          pair_in(st(ws_f.real)), pair_in(st(ws_f.imag)),
                          pair_in(st(ws_b.real)), pair_in(st(ws_b.imag))], axis=-1)
    ot = lambda w: w.transpose(1, 3, 0, 2)
    w2 = jnp.concatenate([pair_out(ot(wo_f.real)), pair_out(ot(-wo_f.imag)),
                          pair_out(ot(wo_b.real)), pair_out(ot(-wo_b.imag))], axis=1)
    a16 = apow[L].reshape(2, SSM_PAIRS, 2 * SSM_P)
    a16 = jnp.stack([a16[0].real, a16[0].imag, a16[1].real, a16[1].imag], axis=1)
    dflat = jnp.broadcast_to(ssm_d.reshape(SSM_PAIRS, 2, 1, SSM_GC), (SSM_PAIRS, 2, L, SSM_GC))
    return w1.astype(BF16), w2.astype(BF16), a16, dflat.reshape(SSM_PAIRS, 1, 2 * L * SSM_GC)


def _merge_kernel(x_ref, xp_ref, xn_ref, mod_ref, n1_ref, n2_ref, wcg_ref, cw_ref, wco_ref, attn_ref, wao_ref,
                  ssm_ref, wglu_ref, wout_ref, rw_ref, x1_ref, h2_ref, aff_ref):
    tm = x_ref.shape[0]
    i = pl.program_id(1)
    nt = pl.num_programs(1)
    x = x_ref[...]
    n1 = n1_ref[...]
    sh1, sc1, g1 = mod_ref[0:1, :], mod_ref[1:2, :], mod_ref[2:3, :]
    sh2, sc2 = mod_ref[3:4, :], mod_ref[4:5, :]
    cw = CONV_W

    def conv_in(h):
        xin = jnp.dot(h, wcg_ref[:, 0:cw], preferred_element_type=F32)
        cg = jnp.dot(h, wcg_ref[:, 2 * cw:3 * cw], preferred_element_type=F32)
        return cg * xin

    h = _rms_mod(x, n1, sc1, sh1).astype(BF16)
    z = conv_in(h)
    zp8 = conv_in(_rms_mod(xp_ref[...], n1, sc1, sh1).astype(BF16))
    zn8 = conv_in(_rms_mod(xn_ref[...], n1, sc1, sh1).astype(BF16))
    z_before = jnp.where(i > 0, zp8[SUBLANES - 1:SUBLANES, :], 0.0)
    z_after = jnp.where(i < nt - 1, zn8[0:1, :], 0.0)
    rows = lax.broadcasted_iota(jnp.int32, (tm, 1), 0)
    zl = jnp.where(rows == 0, z_before, pltpu.roll(z, 1, axis=0))
    zr = jnp.where(rows == tm - 1, z_after, pltpu.roll(z, tm - 1, axis=0))
    y = zl * cw_ref[0:1, :] + z * cw_ref[1:2, :] + zr * cw_ref[2:3, :]
    bg = jnp.dot(h, wcg_ref[:, cw:2 * cw], preferred_element_type=F32)
    conv_y = jnp.dot((bg * y).astype(BF16), wco_ref[...], preferred_element_type=F32)
    o = 3 * cw
    ga = _sigmoid(jnp.dot(h, wcg_ref[:, o:o + D_MODEL], preferred_element_type=F32))
    mixed = ga * conv_y
    attn_y = jnp.dot(attn_ref[...], wao_ref[...], preferred_element_type=F32)
    gb = _sigmoid(jnp.dot(h, wcg_ref[:, o + D_MODEL:o + 2 * D_MODEL], preferred_element_type=F32))
    mixed = mixed + gb * attn_y
    s = ssm_ref[...]
    gelu = s * (0.5 * (1.0 + jnp.tanh(math.sqrt(2.0 / math.pi) * (s + 0.044715 * (s * s * s)))))
    zab = jnp.dot(gelu.astype(BF16), wglu_ref[...], preferred_element_type=F32)
    ssm_y = zab[:, 0:D_MODEL] * _sigmoid(zab[:, D_MODEL:2 * D_MODEL])
    gc = _sigmoid(jnp.dot(h, wcg_ref[:, o + 2 * D_MODEL:o + 3 * D_MODEL], preferred_element_type=F32))
    mixed = mixed + gc * ssm_y
    x1 = x + g1 * jnp.dot(mixed.astype(BF16), wout_ref[...], preferred_element_type=F32)
    x1_ref[...] = x1
    h2 = _rms_mod(x1, n2_ref[...], sc2, sh2)
    h2_ref[...] = h2.astype(BF16)
    logits = lax.dot_general(rw_ref[...], h2, (((1,), (1,)), ((), ())), precision=HIGHEST,
                             preferred_element_type=F32)
    e = jnp.exp(logits - jnp.max(logits, axis=0, keepdims=True))
    aff_ref[...] = e / jnp.sum(e, axis=0, keepdims=True)


def _merge(x, mod, n1, n2, wcg, cw, wco, attn, wao, ssm, wglu, wout, rwt, tm):
    bsz, t, _ = x.shape
    nt = t // tm
    per = tm // SUBLANES
    tok = lambda wd: pl.BlockSpec((None, tm, wd), lambda b, i: (b, i, 0))
    prev = pl.BlockSpec((None, SUBLANES, D_MODEL), lambda b, i: (b, jnp.maximum(i * per - 1, 0), 0))
    nxt = pl.BlockSpec((None, SUBLANES, D_MODEL), lambda b, i: (b, jnp.minimum((i + 1) * per, t // SUBLANES - 1), 0))
    return pl.pallas_call(
        _merge_kernel, grid=(bsz, nt),
        in_specs=[tok(D_MODEL), prev, nxt, _mod_spec(mod), _const_spec((1, D_MODEL)), _const_spec((1, D_MODEL)),
                  _const_spec(wcg.shape), _const_spec(cw.shape), _const_spec(wco.shape), tok(Q_W),
                  _const_spec(wao.shape), tok(SSM_W), _const_spec(wglu.shape), _const_spec(wout.shape),
                  _const_spec(rwt.shape)],
        out_specs=[tok(D_MODEL), tok(D_MODEL), pl.BlockSpec((None, N_EXPERTS, tm), lambda b, i: (b, 0, i))],
        out_shape=[jax.ShapeDtypeStruct((bsz, t, D_MODEL), F32), jax.ShapeDtypeStruct((bsz, t, D_MODEL), BF16),
                   jax.ShapeDtypeStruct((bsz, N_EXPERTS, t), F32)],
        compiler_params=_params(("parallel", "parallel"), 56),
        name="merge",
    )(x, x, x, mod, n1, n2, wcg, cw, wco, attn, wao, ssm, wglu, wout, rwt)


def _route_kernel(aff_ref, slot_ref, cnt_ref, *, cap, slot_stride):
    n = aff_ref.shape[1]
    bits = pltpu.bitcast(aff_ref[...], jnp.int32)
    thr = jnp.zeros((N_EXPERTS, 1), jnp.int32)
    for b in range(30, -1, -1):
        cand = thr | (1 << b)
        cnt = jnp.sum(jnp.where(bits >= cand, 1.0, 0.0), axis=1, keepdims=True)
        thr = jnp.where(cnt >= cap, cand, thr)
    need = cap - jnp.sum(jnp.where(bits > thr, 1.0, 0.0), axis=1, keepdims=True)
    ri = lax.broadcasted_iota(jnp.int32, (LANES, LANES), 0)
    ci = lax.broadcasted_iota(jnp.int32, (LANES, LANES), 1)
    tri = jnp.where(ri <= ci, 1.0, 0.0).astype(BF16)
    lane = lax.broadcasted_iota(jnp.int32, (N_EXPERTS, LANES), 1)
    base = pl.program_id(0) * slot_stride
    off_eq = jnp.zeros((N_EXPERTS, 1), F32)
    off = jnp.zeros((N_EXPERTS, 1), F32)
    cnts = jnp.zeros((N_EXPERTS, LANES), jnp.int32)
    for j in range(n // LANES):
        sl = slice(j * LANES, (j + 1) * LANES)
        bj = bits[:, sl]
        eqf = jnp.where(bj == thr, 1.0, 0.0)
        ceq = jnp.dot(eqf.astype(BF16), tri, preferred_element_type=F32) + off_eq
        off_eq = ceq[:, LANES - 1:LANES]
        self_ = jnp.where(bj > thr, 1.0, jnp.where(ceq <= need, eqf, 0.0))
        csel = jnp.dot(self_.astype(BF16), tri, preferred_element_type=F32) + off
        cnts = jnp.where(lane == j, off.astype(jnp.int32) + base, cnts)
        off = csel[:, LANES - 1:LANES]
        slot_ref[:, sl] = jnp.where(self_ > 0.0, csel.astype(jnp.int32) - 1 + base, -1)
    cnt_ref[...] = jnp.where(lane >= n // LANES, off.astype(jnp.int32) + base, cnts)


def _route(aff, cap, slot_stride):
    sets, _, n = aff.shape
    return pl.pallas_call(
        functools.partial(_route_kernel, cap=cap, slot_stride=slot_stride), grid=(sets,),
        in_specs=[pl.BlockSpec((None, N_EXPERTS, n), lambda s: (s, 0, 0))],
        out_specs=[pl.BlockSpec((None, N_EXPERTS, n), lambda s: (s, 0, 0)),
                   pl.BlockSpec((None, N_EXPERTS, LANES), lambda s: (s, 0, 0))],
        out_shape=[jax.ShapeDtypeStruct((sets, N_EXPERTS, n), jnp.int32),
                   jax.ShapeDtypeStruct((sets, N_EXPERTS, LANES), jnp.int32)],
        compiler_params=_params(("parallel",), 32),
        name="route",
    )(aff)


def _expert_kernel(cnt_ref, h2_ref, slot_ref, gate_ref, wg_ref, wu_ref, wd_ref, o_ref, xs_ref, ys_ref, gs_ref, *,
                   nslot):
    s = pl.program_id(0)
    e = pl.program_id(1)
    ntb = SUPER // BLOCK
    nsb = nslot // BLOCK
    cbase = (s * N_EXPERTS + e) * (ntb + 1)
    srow = lax.broadcasted_iota(jnp.int32, (BLOCK, BLOCK), 0)

    @pl.when(e == 0)
    def _():
        o_ref[...] = jnp.zeros_like(o_ref)

    xs_ref[...] = jnp.zeros_like(xs_ref)
    gs_ref[...] = jnp.zeros_like(gs_ref)

    def hits(i, j):
        return (cnt_ref[cbase + i] < (j + 1) * BLOCK) & (cnt_ref[cbase + i + 1] > j * BLOCK)

    def gather(i, carry):
        t0 = pl.multiple_of(i * BLOCK, BLOCK)
        slots = slot_ref[pl.ds(i, 1), :]
        gates = gate_ref[pl.ds(i, 1), :]
        for j in range(nsb):
            @pl.when(hits(i, j))
            def _():
                pick = slots == srow + j * BLOCK
                xs_ref[j * BLOCK:(j + 1) * BLOCK, :] += jnp.dot(
                    jnp.where(pick, 1.0, 0.0).astype(BF16), h2_ref[pl.ds(t0, BLOCK), :],
                    preferred_element_type=F32)
                gs_ref[j * BLOCK:(j + 1) * BLOCK, :] += jnp.sum(jnp.where(pick, gates, 0.0), axis=1, keepdims=True)
        return carry

    lax.fori_loop(0, ntb, gather, 0)
    xs = xs_ref[...].astype(BF16)
    hg = jnp.dot(xs, wg_ref[...], preferred_element_type=F32)
    hu = jnp.dot(xs, wu_ref[...], preferred_element_type=F32)
    act = (hg * _sigmoid(hg) * hu).astype(BF16)
    y = jnp.dot(act, wd_ref[...], preferred_element_type=F32) * gs_ref[...]
    ys_ref[...] = y.astype(BF16)

    def scatter(i, carry):
        t0 = pl.multiple_of(i * BLOCK, BLOCK)
        slots = slot_ref[pl.ds(i, 1), :]
        for j in range(nsb):
            @pl.when(hits(i, j))
            def _():
                pick = jnp.where(slots == srow + j * BLOCK, 1.0, 0.0).astype(BF16)
                o_ref[pl.ds(t0, BLOCK), :] += lax.dot_general(
                    pick, ys_ref[j * BLOCK:(j + 1) * BLOCK, :], (((0,), (0,)), ((), ())),
                    preferred_element_type=F32)
        return carry

    lax.fori_loop(0, ntb, scatter, 0)


def _experts(cnt, h2, slot, gate, wg, wu, wd):
    nsup = h2.shape[0]
    nslot = CAPACITY * SUPER // N_EXPERTS
    ntb = SUPER // BLOCK
    grid_spec = pltpu.PrefetchScalarGridSpec(
        num_scalar_prefetch=1, grid=(nsup, N_EXPERTS),
        in_specs=[pl.BlockSpec((None, SUPER, D_MODEL), lambda s, e, c: (s, 0, 0), pipeline_mode=pl.Buffered(1)),
                  pl.BlockSpec((None, None, ntb, BLOCK), lambda s, e, c: (s, e, 0, 0)),
                  pl.BlockSpec((None, None, ntb, BLOCK), lambda s, e, c: (s, e, 0, 0)),
                  pl.BlockSpec((None, D_MODEL, EXPERT_FF), lambda s, e, c: (e, 0, 0)),
                  pl.BlockSpec((None, D_MODEL, EXPERT_FF), lambda s, e, c: (e, 0, 0)),
                  pl.BlockSpec((None, EXPERT_FF, D_MODEL), lambda s, e, c: (e, 0, 0))],
        out_specs=pl.BlockSpec((None, SUPER, D_MODEL), lambda s, e, c: (s, 0, 0), pipeline_mode=pl.Buffered(1)),
        scratch_shapes=[pltpu.VMEM((nslot, D_MODEL), F32), pltpu.VMEM((nslot, D_MODEL), BF16),
                        pltpu.VMEM((nslot, 1), F32)])
    return pl.pallas_call(
        functools.partial(_expert_kernel, nslot=nslot), grid_spec=grid_spec,
        out_shape=jax.ShapeDtypeStruct((nsup, SUPER, D_MODEL), F32),
        compiler_params=_params(("arbitrary", "arbitrary"), 56),
        name="experts",
    )(cnt, h2, slot, gate, wg, wu, wd)


def _final_kernel(x_ref, moe_ref, mod_ref, g_ref, o_ref):
    x = x_ref[...] + mod_ref[5:6, :] * moe_ref[...]
    o_ref[...] = x * lax.rsqrt(jnp.mean(x * x, axis=-1, keepdims=True) + EPS) * g_ref[...]


def _final(x1, moe, mod, g, tm):
    bsz, t, _ = x1.shape
    tok = pl.BlockSpec((None, tm, D_MODEL), lambda b, i: (b, i, 0))
    return pl.pallas_call(
        _final_kernel, grid=(bsz, t // tm),
        in_specs=[tok, tok, _mod_spec(mod), _const_spec((1, D_MODEL))],
        out_specs=tok, out_shape=jax.ShapeDtypeStruct((bsz, t, D_MODEL), F32),
        compiler_params=_params(("parallel", "parallel"), 32),
        name="final_norm",
    )(x1, moe, mod, g)


def _head_perm():
    cols = []
    for m in range(GROUP):
        for kv in range(N_KV):
            head = kv * GROUP + m
            cols.extend(range(head * HEAD_DIM, (head + 1) * HEAD_DIM))
    return jnp.array(cols, jnp.int32)


def _rope_swap(width):
    idx = jnp.arange(width)
    nf = HEAD_DIM // 4
    return jnp.where((idx % (2 * nf)) < nf, idx + nf, idx - nf)


def _rope_tables(t):
    pos = jnp.arange(t)
    row = (pos // GRID_W).astype(F32)
    col = (pos % GRID_W).astype(F32)
    nf = HEAD_DIM // 4
    inv = ROPE_BASE ** (-jnp.arange(nf, dtype=F32) / nf)

    def tabs(p):
        ang = p[:, None] * inv[None, :]
        cos, sin = jnp.cos(ang), jnp.sin(ang)
        return jnp.concatenate([cos, cos], axis=1), jnp.concatenate([-sin, sin], axis=1)

    cr, sr = tabs(row)
    cc, sc = tabs(col)
    cos = jnp.concatenate([cr, cc], axis=1)
    sin = jnp.concatenate([sr, sc], axis=1)
    return jnp.tile(cos, (1, LANES // HEAD_DIM)), jnp.tile(sin, (1, LANES // HEAD_DIM))


def _route_and_experts(h2c, affc, h2l, affl, wg, wu, wd):
    bc, tc, _ = h2c.shape
    bl, tl, _ = h2l.shape
    ntb = SUPER // BLOCK
    capc = CAPACITY * tc // N_EXPERTS
    capl = CAPACITY * tl // N_EXPERTS
    slot_c, cnt_c = _route(affc, capc, capc)
    slot_l, cnt_l = _route(affl, capl, 0)
    tbc = tc // BLOCK
    slot_c = slot_c.transpose(1, 0, 2).reshape(1, N_EXPERTS, ntb, BLOCK)
    gate_c = affc.transpose(1, 0, 2).reshape(1, N_EXPERTS, ntb, BLOCK)
    cnt_c = jnp.concatenate([cnt_c[:, :, :tbc].transpose(1, 0, 2).reshape(N_EXPERTS, ntb),
                             cnt_c[bc - 1, :, tbc:tbc + 1]], axis=1)[None]
    slot = jnp.concatenate([slot_c, slot_l.reshape(bl, N_EXPERTS, ntb, BLOCK)], axis=0)
    gate = jnp.concatenate([gate_c, affl.reshape(bl, N_EXPERTS, ntb, BLOCK)], axis=0)
    cnt = jnp.concatenate([cnt_c, cnt_l[:, :, :ntb + 1]], axis=0).reshape(-1)
    h2 = jnp.concatenate([h2c.reshape(1, SUPER, D_MODEL), h2l], axis=0)
    moe = _experts(cnt, h2, slot, gate, wg, wu, wd)
    return moe[0].reshape(bc, tc, D_MODEL), moe[1:]


def kernel(x_prompt, x_sample, cache_k, cache_v, state_ssm_re, state_ssm_im, c, c_ctx, ada_w, ada_b, norm1, norm2,
           final_norm, w_in, conv_w, w_conv_out, attn_sink, w_attn_out, ssm_a_re, ssm_a_im, ssm_log_dt, ssm_b_re,
           ssm_b_im, ssm_c_re, ssm_c_im, ssm_d, w_glu, w_out, router_w, w_gate, w_up, w_down):
    bc, tc, _ = x_prompt.shape
    bl, tl, _ = x_sample.shape
    assert bc * tc == SUPER and tl == SUPER and bl + 1 <= SUBLANES
    hp = _head_perm()
    cvecs = jnp.zeros((SUBLANES, D_MODEL), F32).at[:bl].set(c).at[bl].set(c_ctx)
    mods = _modulation(cvecs, ada_w, ada_b).reshape(DEPTH, SUBLANES, 6, D_MODEL)

    o_q, o_k, o_v, o_u, o_g = 3 * CONV_W, 3 * CONV_W + Q_W, 3 * CONV_W + Q_W + KV_W, 3 * CONV_W + Q_W + 2 * KV_W, \
        3 * CONV_W + Q_W + 2 * KV_W + SSM_W
    wq = w_in[:, :, o_q:o_k][:, :, hp]
    wk = w_in[:, :, o_k:o_v]
    w_qkvu = jnp.concatenate([wq, wk, w_in[:, :, o_v:o_g]], axis=-1)
    w_qkvu_l = jnp.concatenate([w_qkvu, wq[:, :, _rope_swap(Q_W)], wk[:, :, _rope_swap(KV_W)]], axis=-1).astype(BF16)
    w_qkvu_c = w_qkvu.astype(BF16)
    w_cg = jnp.concatenate([w_in[:, :, :o_q], w_in[:, :, o_g:]], axis=-1).astype(BF16)
    w_co = w_conv_out.astype(BF16)
    w_ao = w_attn_out[:, hp, :].astype(BF16)
    w_gl = w_glu.astype(BF16)
    w_o = w_out.astype(BF16)
    rwt = router_w.transpose(0, 2, 1)
    wg, wu, wd = w_gate.astype(BF16), w_up.astype(BF16), w_down.astype(BF16)
    sink = attn_sink.reshape(DEPTH, N_KV, GROUP).transpose(0, 2, 1).reshape(DEPTH, N_HEADS)
    rope_tabs = _rope_tables(tl)
    h0_all = jnp.stack([state_ssm_re[:, :, 0], state_ssm_im[:, :, 0], state_ssm_re[:, :, 1], state_ssm_im[:, :, 1]],
                       axis=0)
    h0_all = h0_all.reshape(4, bl, DEPTH, SSM_PAIRS, 2 * SSM_P).transpose(2, 3, 0, 1, 4)

    xp, xs = x_prompt, x_sample
    moe_c = moe_l = None
    modp_c = modp_l = None
    ks, vs, fins = [], [], []
    for l in range(DEPTH):
        mod_l = mods[l, :bl]
        mod_c = mods[l, bl:bl + 1]
        n1 = norm1[l].reshape(1, D_MODEL)
        n2 = norm2[l].reshape(1, D_MODEL)
        ssm_w = _ssm_weights(ssm_a_re[l], ssm_a_im[l], ssm_log_dt[l], ssm_b_re[l], ssm_b_im[l], ssm_c_re[l],
                             ssm_c_im[l], ssm_d[l])
        outs = _qkvu(xp, moe_c, modp_c, mod_c, n1, w_qkvu_c[l], None, tc, F32)
        q_c, k_c, v_c, u_c = outs[:4]
        if moe_c is not None:
            xp = outs[4]
        ks.append(k_c)
        vs.append(v_c)
        attn_c = _ctx_attention(sink[l], q_c, k_c, v_c)
        ssm_c, fin = _ssm(u_c, ssm_w, None)
        fins.append(fin)
        x1c, h2c, affc = _merge(xp, mod_c, n1, n2, w_cg[l], conv_w[l], w_co[l], attn_c, w_ao[l], ssm_c, w_gl[l],
                                w_o[l], rwt[l], tc)
        outs = _qkvu(xs, moe_l, modp_l, mod_l, n1, w_qkvu_l[l], rope_tabs, 512, BF16)
        q_l, k_l, v_l, u_l = outs[:4]
        if moe_l is not None:
            xs = outs[4]
        attn_l = _lat_attention(sink[l], q_l, k_l, v_l, cache_k[:, l].reshape(bl, -1, KV_W),
                                cache_v[:, l].reshape(bl, -1, KV_W))
        ssm_l, _ = _ssm(u_l, ssm_w, h0_all[l])
        x1l, h2l, affl = _merge(xs, mod_l, n1, n2, w_cg[l], conv_w[l], w_co[l], attn_l, w_ao[l], ssm_l, w_gl[l],
                                w_o[l], rwt[l], 256)
        moe_c, moe_l = _route_and_experts(h2c, affc, h2l, affl, wg[l], wu[l], wd[l])
        xp, xs = x1c, x1l
        modp_c, modp_l = mod_c, mod_l

    fn = final_norm.reshape(1, D_MODEL)
    y_prompt = _final(xp, moe_c, modp_c, fn, tc)
    y_sample = _final(xs, moe_l, modp_l, fn, 512)
    new_k = jnp.stack(ks, axis=1).reshape(bc, DEPTH, tc, N_KV, HEAD_DIM)
    new_v = jnp.stack(vs, axis=1).reshape(bc, DEPTH, tc, N_KV, HEAD_DIM)
    fin = jnp.stack(fins, axis=0)
    fin = fin.reshape(DEPTH, SSM_PAIRS, 2, 2, bc, 2, SSM_P).transpose(3, 4, 0, 2, 1, 5, 6)
    fin = fin.reshape(2, bc, DEPTH, 2, SSM_G, SSM_P)
    return (y_prompt, y_sample, new_k, new_v, fin[0], fin[1])
```

```python
import functools
import math

import jax
import jax.numpy as jnp
from jax import lax
from jax.experimental import pallas as pl
from jax.experimental.pallas import tpu as pltpu

D_MODEL = 1024
DEPTH = 4
GRID_W = 64
CONV_W = 512
N_HEADS = 8
N_KV = 2
HEAD_DIM = 64
GROUP = N_HEADS // N_KV
Q_W = N_HEADS * HEAD_DIM
KV_W = N_KV * HEAD_DIM
BLOCK = 128
ROPE_BASE = 10000.0
NEG_INF = -1e30
SSM_W = 512
SSM_GC = 16
SSM_G = SSM_W // SSM_GC
SSM_P = 64
N_BRANCH = 3
N_EXPERTS = 16
EXPERT_FF = 1024
CAPACITY = 2
EPS = 1e-6

F32 = jnp.float32
BF16 = jnp.bfloat16
HIGHEST = lax.Precision.HIGHEST

LANES = 128
SUBLANES = 8
SSM_CHUNK = 16
SSM_PAIRS = SSM_G // 2
SUPER = 4096
GATHER_TOKENS = 512
MIB = 1024 * 1024


def _params(sem, vmem_mib):
    return pltpu.CompilerParams(dimension_semantics=sem, vmem_limit_bytes=vmem_mib * MIB)


def _const_spec(shape):
    nd = len(shape)
    return pl.BlockSpec(shape, lambda *_: (0,) * nd, pipeline_mode=pl.Buffered(1))


def _sigmoid(x):
    return 1.0 / (1.0 + jnp.exp(-x))


def _rms_mod(x, g, sc, sh):
    y = x * lax.rsqrt(jnp.mean(x * x, axis=-1, keepdims=True) + EPS)
    return (y * g) * (1.0 + sc) + sh


def _mod_kernel(c_ref, w_ref, b_ref, o_ref):
    cv = c_ref[...]
    s = cv * _sigmoid(cv)
    o_ref[...] = jnp.dot(s, w_ref[...], precision=HIGHEST, preferred_element_type=F32) + b_ref[...]


def _modulation(cvecs, ada_w, ada_b):
    nt = 1536
    return pl.pallas_call(
        _mod_kernel,
        grid=(DEPTH, 6 * D_MODEL // nt),
        in_specs=[pl.BlockSpec((SUBLANES, D_MODEL), lambda l, j: (0, 0)),
                  pl.BlockSpec((None, D_MODEL, nt), lambda l, j: (l, 0, j)),
                  pl.BlockSpec((None, 1, nt), lambda l, j: (l, 0, j))],
        out_specs=pl.BlockSpec((None, SUBLANES, nt), lambda l, j: (l, 0, j)),
        out_shape=jax.ShapeDtypeStruct((DEPTH, SUBLANES, 6 * D_MODEL), F32),
        compiler_params=_params(("parallel", "parallel"), 32),
        name="modulation",
    )(cvecs, ada_w, ada_b.reshape(DEPTH, 1, 6 * D_MODEL))


def _qkvu_kernel(*refs, rope, fuse_res):
    it = iter(refs)
    x_ref = next(it)
    if fuse_res:
        moe_ref = next(it)
        modp_ref = next(it)
    mod_ref = next(it)
    n1_ref = next(it)
    w_ref = next(it)
    if rope:
        cos_ref = next(it)
        sin_ref = next(it)
    q_ref, k_ref, v_ref, u_ref = next(it), next(it), next(it), next(it)
    x = x_ref[...]
    if fuse_res:
        xo_ref = next(it)
        x = x + modp_ref[5:6, :] * moe_ref[...]
        xo_ref[...] = x
    h = _rms_mod(x, n1_ref[...], mod_ref[1:2, :], mod_ref[0:1, :]).astype(BF16)
    p = jnp.dot(h, w_ref[...], preferred_element_type=F32)
    q = p[:, 0:Q_W]
    k = p[:, Q_W:Q_W + KV_W]
    if rope:
        cos = cos_ref[...]
        sin = sin_ref[...]
        o = Q_W + 2 * KV_W + SSM_W
        q = jnp.concatenate(
            [q[:, m * LANES:(m + 1) * LANES] * cos + p[:, o + m * LANES:o + (m + 1) * LANES] * sin
             for m in range(Q_W // LANES)], axis=1)
        k = k * cos + p[:, o + Q_W:o + Q_W + KV_W] * sin
    q_ref[...] = (q * HEAD_DIM ** -0.5).astype(q_ref.dtype)
    k_ref[...] = k.astype(k_ref.dtype)
    v_ref[...] = p[:, Q_W + KV_W:Q_W + 2 * KV_W].astype(v_ref.dtype)
    u_ref[...] = p[:, Q_W + 2 * KV_W:Q_W + 2 * KV_W + SSM_W]


def _mod_spec(mod):
    if mod.shape[0] == 1:
        return pl.BlockSpec((None, 6, D_MODEL), lambda b, i: (0, 0, 0))
    return pl.BlockSpec((None, 6, D_MODEL), lambda b, i: (b, 0, 0))


def _qkvu(x, moe, modp, mod, n1, w, rope_tabs, tm, kv_dtype):
    bsz, t, _ = x.shape
    rope = rope_tabs is not None
    fuse_res = moe is not None
    tok = lambda wd: pl.BlockSpec((None, tm, wd), lambda b, i: (b, i, 0))
    in_specs, args = [tok(D_MODEL)], [x]
    if fuse_res:
        in_specs += [tok(D_MODEL), _mod_spec(modp)]
        args += [moe, modp]
    in_specs += [_mod_spec(mod), _const_spec((1, D_MODEL)), _const_spec(w.shape)]
    args += [mod, n1, w]
    if rope:
        in_specs += [pl.BlockSpec((tm, LANES), lambda b, i: (i, 0))] * 2
        args += list(rope_tabs)
    out_specs = [tok(Q_W), tok(KV_W), tok(KV_W), tok(SSM_W)]
    out_shape = [jax.ShapeDtypeStruct((bsz, t, Q_W), BF16),
                 jax.ShapeDtypeStruct((bsz, t, KV_W), kv_dtype),
                 jax.ShapeDtypeStruct((bsz, t, KV_W), kv_dtype),
                 jax.ShapeDtypeStruct((bsz, t, SSM_W), F32)]
    if fuse_res:
        out_specs.append(tok(D_MODEL))
        out_shape.append(jax.ShapeDtypeStruct((bsz, t, D_MODEL), F32))
    return pl.pallas_call(
        functools.partial(_qkvu_kernel, rope=rope, fuse_res=fuse_res),
        grid=(bsz, t // tm), in_specs=in_specs, out_specs=out_specs, out_shape=out_shape,
        compiler_params=_params(("parallel", "parallel"), 48),
        name="qkvu",
    )(*args)


def _stack_heads(q_ref):
    lo = lax.broadcasted_iota(jnp.int32, (1, LANES), 1) < HEAD_DIM
    keep_lo = jnp.where(lo, 1.0, 0.0).astype(BF16)
    keep_hi = jnp.where(lo, 0.0, 1.0).astype(BF16)
    parts = []
    for m in range(GROUP):
        qm = q_ref[:, m * LANES:(m + 1) * LANES]
        parts.append(qm * keep_lo)
        parts.append(qm * keep_hi)
    return jnp.concatenate(parts, axis=0), lo


def _attend(s, bias, sink_ref, v, tq, lo, o_ref):
    ps, dens = [], []
    for h in range(N_HEADS):
        sh = s[h * tq:(h + 1) * tq]
        if bias is not None:
            sh = sh + bias
        sink = sink_ref[h]
        mx = jnp.maximum(jnp.max(sh, axis=-1, keepdims=True), sink)
        p = jnp.exp(sh - mx)
        dens.append(jnp.sum(p, axis=-1, keepdims=True) + jnp.exp(sink - mx))
        ps.append(p.astype(BF16))
    o = jnp.dot(jnp.concatenate(ps, axis=0), v, preferred_element_type=F32)
    for m in range(GROUP):
        o0 = o[(2 * m) * tq:(2 * m + 1) * tq] / dens[2 * m]
        o1 = o[(2 * m + 1) * tq:(2 * m + 2) * tq] / dens[2 * m + 1]
        o_ref[:, m * LANES:(m + 1) * LANES] = jnp.where(lo, o0, o1).astype(o_ref.dtype)


def _ctx_attn_kernel(sink_ref, q_ref, k_ref, v_ref, o_ref):
    tq = q_ref.shape[0]
    qx, lo = _stack_heads(q_ref)
    s = lax.dot_general(qx, k_ref[...].astype(BF16), (((1,), (1,)), ((), ())), preferred_element_type=F32)
    _attend(s, None, sink_ref, v_ref[...].astype(BF16), tq, lo, o_ref)


def _ctx_attention(sink, q, k, v):
    bsz, t, _ = q.shape
    tok = lambda wd: pl.BlockSpec((None, t, wd), lambda b: (b, 0, 0))
    return pl.pallas_call(
        _ctx_attn_kernel, grid=(bsz,),
        in_specs=[pl.BlockSpec(memory_space=pltpu.SMEM), tok(Q_W), tok(KV_W), tok(KV_W)],
        out_specs=tok(Q_W), out_shape=jax.ShapeDtypeStruct((bsz, t, Q_W), BF16),
        compiler_params=_params(("parallel",), 32),
        name="ctx_attention",
    )(sink, q, k, v)


def _lat_attn_kernel(sink_ref, q_ref, kc_ref, vc_ref, kp_ref, k0_ref, kn_ref, vp_ref, v0_ref, vn_ref, o_ref):
    tq = q_ref.shape[0]
    i = pl.program_id(1)
    nb = pl.num_programs(1)
    qx, lo = _stack_heads(q_ref)
    kall = jnp.concatenate([kc_ref[...].astype(BF16), kp_ref[...], k0_ref[...], kn_ref[...]], axis=0)
    vall = jnp.concatenate([vc_ref[...].astype(BF16), vp_ref[...], v0_ref[...], vn_ref[...]], axis=0)
    s = lax.dot_general(qx, kall, (((1,), (1,)), ((), ())), preferred_element_type=F32)
    past = kc_ref.shape[0]
    r = lax.broadcasted_iota(jnp.int32, (tq, BLOCK), 0)
    c = lax.broadcasted_iota(jnp.int32, (tq, BLOCK), 1)
    m_prev = jnp.where((c >= r) & (i > 0), 0.0, NEG_INF)
    m_next = jnp.where((c <= r) & (i < nb - 1), 0.0, NEG_INF)
    bias = jnp.concatenate([jnp.zeros((tq, past), F32), m_prev, jnp.zeros((tq, BLOCK), F32), m_next], axis=1)
    _attend(s, bias, sink_ref, vall, tq, lo, o_ref)


def _lat_attention(sink, q, k, v, kc, vc):
    bsz, t, _ = q.shape
    nb = t // BLOCK
    past = kc.shape[1]
    tok = lambda wd: pl.BlockSpec((None, BLOCK, wd), lambda b, i: (b, i, 0))
    prev = pl.BlockSpec((None, BLOCK, KV_W), lambda b, i: (b, jnp.maximum(i - 1, 0), 0))
    nxt = pl.BlockSpec((None, BLOCK, KV_W), lambda b, i: (b, jnp.minimum(i + 1, nb - 1), 0))
    ctx = pl.BlockSpec((None, past, KV_W), lambda b, i: (b, 0, 0))
    return pl.pallas_call(
        _lat_attn_kernel, grid=(bsz, nb),
        in_specs=[pl.BlockSpec(memory_space=pltpu.SMEM), tok(Q_W), ctx, ctx,
                  prev, tok(KV_W), nxt, prev, tok(KV_W), nxt],
        out_specs=tok(Q_W), out_shape=jax.ShapeDtypeStruct((bsz, t, Q_W), BF16),
        compiler_params=_params(("parallel", "parallel"), 32),
        name="lat_attention",
    )(sink, q, kc, vc, k, k, k, v, v, v)


SSM_QUAD = LANES // (2 * SSM_GC)
GRANULES = LANES // SSM_GC


def _ssm_kernel(u_ref, w1_ref, w2_ref, a_ref, h0_ref, d_ref, y_ref, fin_ref, pk_ref, z_ref, ent_ref, *, bsz, t):
    nk = t // SSM_CHUNK
    gran = lax.broadcasted_iota(jnp.int32, (1, LANES), 1) // SSM_GC
    half = SSM_CHUNK // GRANULES

    def pack(b, carry):
        for p in range(SSM_QUAD):
            for col in range(2 * half):
                gl, sh = col // half, col % half
                src = 2 * p + gl
                acc = None
                for s8 in range(GRANULES):
                    x = u_ref[pl.ds(b * t + sh * GRANULES + s8, nk, stride=SSM_CHUNK), :]
                    shift = (SSM_GC * (s8 - src)) % LANES
                    x = pltpu.roll(x, shift, axis=1) if shift else x
                    acc = x if acc is None else jnp.where(gran == s8, x, acc)
                pk_ref[p, col, pl.ds(b, nk, stride=bsz), :] = acc
        return carry

    lax.fori_loop(0, bsz, pack, 0)

    rows = max(bsz, SUBLANES)
    steps = rows // bsz
    niter = nk // steps
    ns = SSM_W

    def step(sr, si, dr, di, ar, ai):
        return ar * sr - ai * si + dr, ar * si + ai * sr + di

    for p in range(SSM_QUAD):
        u = jnp.concatenate([pk_ref[p, col] for col in range(2 * half)], axis=1)
        z_ref[...] = jnp.dot(u.astype(BF16), w1_ref[p], preferred_element_type=F32)
        afr, afi, abr, abi = a_ref[p, 0:1, :], a_ref[p, 1:2, :], a_ref[p, 2:3, :], a_ref[p, 3:4, :]

        def body(it, carry):
            sr, si, gr, gi = carry
            rf = pl.multiple_of(it * rows, SUBLANES)
            rb = pl.multiple_of((niter - 1 - it) * rows, SUBLANES)
            dfr, dfi = z_ref[pl.ds(rf, rows), ns:ns + LANES], z_ref[pl.ds(rf, rows), ns + LANES:ns + 2 * LANES]
            dbr = z_ref[pl.ds(rb, rows), ns + 2 * LANES:ns + 3 * LANES]
            dbi = z_ref[pl.ds(rb, rows), ns + 3 * LANES:ns + 4 * LANES]
            efr, efi, ebr, ebi = [], [], [None] * steps, [None] * steps
            for j in range(steps):
                efr.append(sr)
                efi.append(si)
                sr, si = step(sr, si, dfr[j * bsz:(j + 1) * bsz], dfi[j * bsz:(j + 1) * bsz], afr, afi)
            for j in reversed(range(steps)):
                ebr[j] = gr
                ebi[j] = gi
                gr, gi = step(gr, gi, dbr[j * bsz:(j + 1) * bsz], dbi[j * bsz:(j + 1) * bsz], abr, abi)
            cat = lambda xs: xs[0] if len(xs) == 1 else jnp.concatenate(xs, axis=0)
            ent_ref[pl.ds(rf, rows), 0:LANES] = cat(efr)
            ent_ref[pl.ds(rf, rows), LANES:2 * LANES] = cat(efi)
            ent_ref[pl.ds(rb, rows), 2 * LANES:3 * LANES] = cat(ebr)
            ent_ref[pl.ds(rb, rows), 3 * LANES:4 * LANES] = cat(ebi)
            return sr, si, gr, gi

        init = (h0_ref[p, 0], h0_ref[p, 1], h0_ref[p, 2], h0_ref[p, 3])
        sr, si, gr, gi = lax.fori_loop(0, niter, body, init)
        fin_ref[p, 0] = sr
        fin_ref[p, 1] = si
        fin_ref[p, 2] = gr
        fin_ref[p, 3] = gi
        y = (z_ref[:, 0:SSM_W] + jnp.dot(ent_ref[...].astype(BF16), w2_ref[p], preferred_element_type=F32)
             + d_ref[p] * u)
        for col in range(2 * half):
            pk_ref[p, col] = y[:, col * LANES:(col + 1) * LANES]

    def unpack(b, carry):
        for s in range(SSM_CHUNK):
            sh, s8 = s // GRANULES, s % GRANULES
            acc = None
            for dst in range(GRANULES):
                p, gl = dst // 2, dst % 2
                col = gl * half + sh
                x = pk_ref[p, col, pl.ds(b, nk, stride=bsz), :]
                shift = (SSM_GC * (dst - s8)) % LANES
                x = pltpu.roll(x, shift, axis=1) if shift else x
                acc = x if acc is None else jnp.where(gran == dst, x, acc)
            y_ref[pl.ds(b * t + s, nk, stride=SSM_CHUNK), :] = acc
        return carry

    lax.fori_loop(0, bsz, unpack, 0)


def _ssm(u, ssm_w, h0):
    bsz, t, _ = u.shape
    w1, w2, a16, dflat = ssm_w
    rows = (t // SSM_CHUNK) * bsz
    if h0 is None:
        h0 = jnp.zeros((SSM_PAIRS, 4, bsz, LANES), F32)
    wcols = w1.shape[-1]
    quad = lambda *tail: pl.BlockSpec((SSM_QUAD,) + tail, lambda q: (q,) + (0,) * len(tail))
    tok = pl.BlockSpec((bsz * t, LANES), lambda q: (0, q), pipeline_mode=pl.Buffered(1))
    y, fin = pl.pallas_call(
        functools.partial(_ssm_kernel, bsz=bsz, t=t),
        grid=(SSM_PAIRS // SSM_QUAD,),
        in_specs=[tok, quad(SSM_W, wcols), quad(SSM_W, SSM_W), quad(4, LANES), quad(4, bsz, LANES), quad(1, SSM_W)],
        out_specs=[tok, quad(4, bsz, LANES)],
        out_shape=[jax.ShapeDtypeStruct((bsz * t, SSM_W), F32),
                   jax.ShapeDtypeStruct((SSM_PAIRS, 4, bsz, LANES), F32)],
        scratch_shapes=[pltpu.VMEM((SSM_QUAD, SSM_W // LANES, rows, LANES), F32), pltpu.VMEM((rows, wcols), F32),
                        pltpu.VMEM((rows, SSM_W), F32)],
        compiler_params=_params(("parallel",), 52),
        name="ssm",
    )(u.reshape(bsz * t, SSM_W), w1, w2, a16, h0, dflat)
    return y.reshape(bsz, t, SSM_W), fin


def _ssm_weights(a_re, a_im, log_dt, b_re, b_im, c_re, c_im, ssm_d):
    L = SSM_CHUNK
    lam = lax.complex(a_re, a_im)
    dt = jnp.exp(log_dt)[..., None]
    steps = jnp.arange(L + 1, dtype=F32)[:, None, None, None]
    apow = jnp.exp(lam[None] * dt[None] * steps)
    abar = apow[1]
    bbar = ((abar - 1.0) / lam)[..., None] * lax.complex(b_re, b_im)
    cmat = lax.complex(c_re, c_im)
    kern = jnp.einsum('dgcp,tdgp,dgpk->tdgck', cmat, apow[:L], bbar, precision=HIGHEST).real
    s_in = jnp.arange(L)[:, None]
    s_out = jnp.arange(L)[None, :]
    lag = s_out - s_in
    tf = jnp.where((lag >= 0)[:, :, None, None, None], kern[jnp.clip(lag, 0, L - 1), 0], 0.0)
    tb = jnp.where((lag <= 0)[:, :, None, None, None], kern[jnp.clip(-lag, 0, L - 1), 1], 0.0)
    toep = (tf + tb).transpose(2, 0, 4, 1, 3)
    ws_f = apow[L - 1 - jnp.arange(L), 0][..., None] * bbar[0][None]
    ws_b = apow[jnp.arange(L), 1][..., None] * bbar[1][None]
    wo_f = cmat[0][None] * apow[1 + jnp.arange(L), 0][:, :, None, :]
    wo_b = cmat[1][None] * apow[L - jnp.arange(L), 1][:, :, None, :]
    eye = jnp.eye(2, dtype=F32)

    def pair_in(w):
        w = w.reshape(SSM_PAIRS, 2, L, SSM_GC, -1)
        return jnp.einsum('qgscn,gh->qgschn', w, eye).reshape(SSM_PAIRS, 2 * L * SSM_GC, -1)

    def pair_out(w):
        w = w.reshape(SSM_PAIRS, 2, SSM_P, L * SSM_GC)
        return jnp.einsum('qgpn,gh->qgphn', w, eye).reshape(SSM_PAIRS, 2 * SSM_P, 2 * L * SSM_GC)

    st = lambda w: w.transpose(1, 0, 3, 2)
    w1 = jnp.concatenate([pair_in(toep.reshape(SSM_G, L, SSM_GC, L * SSM_GC)),
                          pair_in(st(ws_f.real)), pair_in(st(ws_f.imag)),
                          pair_in(st(ws_b.real)), pair_in(st(ws_b.imag))], axis=-1)
    ot = lambda w: w.transpose(1, 3, 0, 2)
    w2 = jnp.concatenate([pair_out(ot(wo_f.real)), pair_out(ot(-wo_f.imag)),
                          pair_out(ot(wo_b.real)), pair_out(ot(-wo_b.imag))], axis=1)
    a16 = apow[L].reshape(2, SSM_PAIRS, 2 * SSM_P)
    a16 = jnp.stack([a16[0].real, a16[0].imag, a16[1].real, a16[1].imag], axis=1)
    dflat = jnp.broadcast_to(ssm_d.reshape(SSM_PAIRS, 2, 1, SSM_GC), (SSM_PAIRS, 2, L, SSM_GC))
    return w1.astype(BF16), w2.astype(BF16), a16, dflat.reshape(SSM_PAIRS, 1, 2 * L * SSM_GC)


def _merge_kernel(x_ref, xp_ref, xn_ref, mod_ref, n1_ref, n2_ref, wcg_ref, cw_ref, wco_ref, attn_ref, wao_ref,
                  ssm_ref, wglu_ref, wout_ref, rw_ref, x1_ref, h2_ref, aff_ref):
    tm = x_ref.shape[0]
    i = pl.program_id(1)
    nt = pl.num_programs(1)
    x = x_ref[...]
    n1 = n1_ref[...]
    sh1, sc1, g1 = mod_ref[0:1, :], mod_ref[1:2, :], mod_ref[2:3, :]
    sh2, sc2 = mod_ref[3:4, :], mod_ref[4:5, :]
    cw = CONV_W

    def conv_in(h):
        xin = jnp.dot(h, wcg_ref[:, 0:cw], preferred_element_type=F32)
        cg = jnp.dot(h, wcg_ref[:, 2 * cw:3 * cw], preferred_element_type=F32)
        return cg * xin

    h = _rms_mod(x, n1, sc1, sh1).astype(BF16)
    z = conv_in(h)
    zp8 = conv_in(_rms_mod(xp_ref[...], n1, sc1, sh1).astype(BF16))
    zn8 = conv_in(_rms_mod(xn_ref[...], n1, sc1, sh1).astype(BF16))
    z_before = jnp.where(i > 0, zp8[SUBLANES - 1:SUBLANES, :], 0.0)
    z_after = jnp.where(i < nt - 1, zn8[0:1, :], 0.0)
    rows = lax.broadcasted_iota(jnp.int32, (tm, 1), 0)
    zl = jnp.where(rows == 0, z_before, pltpu.roll(z, 1, axis=0))
    zr = jnp.where(rows == tm - 1, z_after, pltpu.roll(z, tm - 1, axis=0))
    y = zl * cw_ref[0:1, :] + z * cw_ref[1:2, :] + zr * cw_ref[2:3, :]
    bg = jnp.dot(h, wcg_ref[:, cw:2 * cw], preferred_element_type=F32)
    conv_y = jnp.dot((bg * y).astype(BF16), wco_ref[...], preferred_element_type=F32)
    o = 3 * cw
    ga = _sigmoid(jnp.dot(h, wcg_ref[:, o:o + D_MODEL], preferred_element_type=F32))
    mixed = ga * conv_y
    attn_y = jnp.dot(attn_ref[...], wao_ref[...], preferred_element_type=F32)
    gb = _sigmoid(jnp.dot(h, wcg_ref[:, o + D_MODEL:o + 2 * D_MODEL], preferred_element_type=F32))
    mixed = mixed + gb * attn_y
    s = ssm_ref[...]
    gelu = s * (0.5 * (1.0 + jnp.tanh(math.sqrt(2.0 / math.pi) * (s + 0.044715 * (s * s * s)))))
    zab = jnp.dot(gelu.astype(BF16), wglu_ref[...], preferred_element_type=F32)
    ssm_y = zab[:, 0:D_MODEL] * _sigmoid(zab[:, D_MODEL:2 * D_MODEL])
    gc = _sigmoid(jnp.dot(h, wcg_ref[:, o + 2 * D_MODEL:o + 3 * D_MODEL], preferred_element_type=F32))
    mixed = mixed + gc * ssm_y
    x1 = x + g1 * jnp.dot(mixed.astype(BF16), wout_ref[...], preferred_element_type=F32)
    x1_ref[...] = x1
    h2 = _rms_mod(x1, n2_ref[...], sc2, sh2)
    h2_ref[...] = h2.astype(BF16)
    logits = lax.dot_general(rw_ref[...], h2, (((1,), (1,)), ((), ())), precision=HIGHEST,
                             preferred_element_type=F32)
    e = jnp.exp(logits - jnp.max(logits, axis=0, keepdims=True))
    aff_ref[...] = e / jnp.sum(e, axis=0, keepdims=True)


def _merge(x, mod, n1, n2, wcg, cw, wco, attn, wao, ssm, wglu, wout, rwt, tm):
    bsz, t, _ = x.shape
    nt = t // tm
    per = tm // SUBLANES
    tok = lambda wd: pl.BlockSpec((None, tm, wd), lambda b, i: (b, i, 0))
    prev = pl.BlockSpec((None, SUBLANES, D_MODEL), lambda b, i: (b, jnp.maximum(i * per - 1, 0), 0))
    nxt = pl.BlockSpec((None, SUBLANES, D_MODEL), lambda b, i: (b, jnp.minimum((i + 1) * per, t // SUBLANES - 1), 0))
    return pl.pallas_call(
        _merge_kernel, grid=(bsz, nt),
        in_specs=[tok(D_MODEL), prev, nxt, _mod_spec(mod), _const_spec((1, D_MODEL)), _const_spec((1, D_MODEL)),
                  _const_spec(wcg.shape), _const_spec(cw.shape), _const_spec(wco.shape), tok(Q_W),
                  _const_spec(wao.shape), tok(SSM_W), _const_spec(wglu.shape), _const_spec(wout.shape),
                  _const_spec(rwt.shape)],
        out_specs=[tok(D_MODEL), tok(D_MODEL), pl.BlockSpec((None, N_EXPERTS, tm), lambda b, i: (b, 0, i))],
        out_shape=[jax.ShapeDtypeStruct((bsz, t, D_MODEL), F32), jax.ShapeDtypeStruct((bsz, t, D_MODEL), BF16),
                   jax.ShapeDtypeStruct((bsz, N_EXPERTS, t), F32)],
        compiler_params=_params(("parallel", "parallel"), 56),
        name="merge",
    )(x, x, x, mod, n1, n2, wcg, cw, wco, attn, wao, ssm, wglu, wout, rwt)


def _route_kernel(aff_ref, slot_ref, cnt_ref, *, cap, slot_stride):
    n = aff_ref.shape[1]
    bits = pltpu.bitcast(aff_ref[...], jnp.int32)
    thr = jnp.zeros((N_EXPERTS, 1), jnp.int32)
    for b in range(30, -1, -1):
        cand = thr | (1 << b)
        cnt = jnp.sum(jnp.where(bits >= cand, 1.0, 0.0), axis=1, keepdims=True)
        thr = jnp.where(cnt >= cap, cand, thr)
    need = cap - jnp.sum(jnp.where(bits > thr, 1.0, 0.0), axis=1, keepdims=True)
    ri = lax.broadcasted_iota(jnp.int32, (LANES, LANES), 0)
    ci = lax.broadcasted_iota(jnp.int32, (LANES, LANES), 1)
    tri = jnp.where(ri <= ci, 1.0, 0.0).astype(BF16)
    lane = lax.broadcasted_iota(jnp.int32, (N_EXPERTS, LANES), 1)
    base = pl.program_id(0) * slot_stride
    off_eq = jnp.zeros((N_EXPERTS, 1), F32)
    off = jnp.zeros((N_EXPERTS, 1), F32)
    cnts = jnp.zeros((N_EXPERTS, LANES), jnp.int32)
    for j in range(n // LANES):
        sl = slice(j * LANES, (j + 1) * LANES)
        bj = bits[:, sl]
        eqf = jnp.where(bj == thr, 1.0, 0.0)
        ceq = jnp.dot(eqf.astype(BF16), tri, preferred_element_type=F32) + off_eq
        off_eq = ceq[:, LANES - 1:LANES]
        self_ = jnp.where(bj > thr, 1.0, jnp.where(ceq <= need, eqf, 0.0))
        csel = jnp.dot(self_.astype(BF16), tri, preferred_element_type=F32) + off
        cnts = jnp.where(lane == j, off.astype(jnp.int32) + base, cnts)
        off = csel[:, LANES - 1:LANES]
        slot_ref[:, sl] = jnp.where(self_ > 0.0, csel.astype(jnp.int32) - 1 + base, -1)
    cnt_ref[...] = jnp.where(lane >= n // LANES, off.astype(jnp.int32) + base, cnts)


def _route(aff, cap, slot_stride):
    sets, _, n = aff.shape
    return pl.pallas_call(
        functools.partial(_route_kernel, cap=cap, slot_stride=slot_stride), grid=(sets,),
        in_specs=[pl.BlockSpec((None, N_EXPERTS, n), lambda s: (s, 0, 0))],
        out_specs=[pl.BlockSpec((None, N_EXPERTS, n), lambda s: (s, 0, 0)),
                   pl.BlockSpec((None, N_EXPERTS, LANES), lambda s: (s, 0, 0))],
        out_shape=[jax.ShapeDtypeStruct((sets, N_EXPERTS, n), jnp.int32),
                   jax.ShapeDtypeStruct((sets, N_EXPERTS, LANES), jnp.int32)],
        compiler_params=_params(("parallel",), 32),
        name="route",
    )(aff)


def _expert_kernel(cnt_ref, h2_ref, slot_ref, gate_ref, wg_ref, wu_ref, wd_ref, o_ref, xs_ref, ys_ref, gs_ref, *,
                   nslot):
    s = pl.program_id(0)
    e = pl.program_id(1)
    ntb = SUPER // BLOCK
    nsb = nslot // BLOCK
    per = GATHER_TOKENS // BLOCK
    cbase = (s * N_EXPERTS + e) * (ntb + 1)

    @pl.when(e == 0)
    def _():
        o_ref[...] = jnp.zeros_like(o_ref)

    xs_ref[...] = jnp.zeros_like(xs_ref)
    gs_ref[...] = jnp.zeros_like(gs_ref)
    ys_ref[nslot:nslot + BLOCK, :] = jnp.zeros((BLOCK, D_MODEL), BF16)

    srow = lax.broadcasted_iota(jnp.int32, (BLOCK, GATHER_TOKENS), 0)
    for g in range(SUPER // GATHER_TOKENS):
        slots = slot_ref[g:g + 1, :]
        gates = gate_ref[g:g + 1, :]
        for j in range(nsb):
            @pl.when((cnt_ref[cbase + g * per] < (j + 1) * BLOCK) & (cnt_ref[cbase + (g + 1) * per] > j * BLOCK))
            def _():
                pick = slots == srow + j * BLOCK
                x = jnp.dot(jnp.where(pick, 1.0, 0.0).astype(BF16), h2_ref[g * GATHER_TOKENS:(g + 1) * GATHER_TOKENS, :],
                            preferred_element_type=F32)
                xs_ref[j * BLOCK:(j + 1) * BLOCK, :] += x.astype(BF16)
                gs_ref[j * BLOCK:(j + 1) * BLOCK, :] += jnp.sum(jnp.where(pick, gates, 0.0), axis=1, keepdims=True)

    xs = xs_ref[...]
    hg = jnp.dot(xs, wg_ref[...], preferred_element_type=F32)
    hu = jnp.dot(xs, wu_ref[...], preferred_element_type=F32)
    act = (hg * _sigmoid(hg) * hu).astype(BF16)
    y = jnp.dot(act, wd_ref[...], preferred_element_type=F32) * gs_ref[...]
    ys_ref[0:nslot, :] = y.astype(BF16)

    wrow = lax.broadcasted_iota(jnp.int32, (2 * BLOCK, BLOCK), 0)
    for i in range(ntb):
        w0 = pl.multiple_of(jnp.minimum(cnt_ref[cbase + i] // BLOCK, nsb - 1) * BLOCK, BLOCK)
        slots = slot_ref[i // per:i // per + 1, (i % per) * BLOCK:(i % per + 1) * BLOCK]
        pick = jnp.where(slots - w0 == wrow, 1.0, 0.0).astype(BF16)
        o_ref[i * BLOCK:(i + 1) * BLOCK, :] += lax.dot_general(
            pick, ys_ref[pl.ds(w0, 2 * BLOCK), :], (((0,), (0,)), ((), ())), preferred_element_type=F32)


def _experts(cnt, h2, slot, gate, wg, wu, wd):
    nsup = h2.shape[0]
    nslot = CAPACITY * SUPER // N_EXPERTS
    ngt = SUPER // GATHER_TOKENS
    grid_spec = pltpu.PrefetchScalarGridSpec(
        num_scalar_prefetch=1, grid=(nsup, N_EXPERTS),
        in_specs=[pl.BlockSpec((None, SUPER, D_MODEL), lambda s, e, c: (s, 0, 0), pipeline_mode=pl.Buffered(1)),
                  pl.BlockSpec((None, None, ngt, GATHER_TOKENS), lambda s, e, c: (s, e, 0, 0)),
                  pl.BlockSpec((None, None, ngt, GATHER_TOKENS), lambda s, e, c: (s, e, 0, 0)),
                  pl.BlockSpec((None, D_MODEL, EXPERT_FF), lambda s, e, c: (e, 0, 0)),
                  pl.BlockSpec((None, D_MODEL, EXPERT_FF), lambda s, e, c: (e, 0, 0)),
                  pl.BlockSpec((None, EXPERT_FF, D_MODEL), lambda s, e, c: (e, 0, 0))],
        out_specs=pl.BlockSpec((None, SUPER, D_MODEL), lambda s, e, c: (s, 0, 0), pipeline_mode=pl.Buffered(1)),
        scratch_shapes=[pltpu.VMEM((nslot, D_MODEL), BF16), pltpu.VMEM((nslot + BLOCK, D_MODEL), BF16),
                        pltpu.VMEM((nslot, 1), F32)])
    return pl.pallas_call(
        functools.partial(_expert_kernel, nslot=nslot), grid_spec=grid_spec,
        out_shape=jax.ShapeDtypeStruct((nsup, SUPER, D_MODEL), F32),
        compiler_params=_params(("arbitrary", "arbitrary"), 56),
        name="experts",
    )(cnt, h2, slot, gate, wg, wu, wd)


def _final_kernel(x_ref, moe_ref, mod_ref, g_ref, o_ref):
    x = x_ref[...] + mod_ref[5:6, :] * moe_ref[...]
    o_ref[...] = x * lax.rsqrt(jnp.mean(x * x, axis=-1, keepdims=True) + EPS) * g_ref[...]


def _final(x1, moe, mod, g, tm):
    bsz, t, _ = x1.shape
    tok = pl.BlockSpec((None, tm, D_MODEL), lambda b, i: (b, i, 0))
    return pl.pallas_call(
        _final_kernel, grid=(bsz, t // tm),
        in_specs=[tok, tok, _mod_spec(mod), _const_spec((1, D_MODEL))],
        out_specs=tok, out_shape=jax.ShapeDtypeStruct((bsz, t, D_MODEL), F32),
        compiler_params=_params(("parallel", "parallel"), 32),
        name="final_norm",
    )(x1, moe, mod, g)


def _head_perm():
    cols = []
    for m in range(GROUP):
        for kv in range(N_KV):
            head = kv * GROUP + m
            cols.extend(range(head * HEAD_DIM, (head + 1) * HEAD_DIM))
    return jnp.array(cols, jnp.int32)


def _rope_swap(width):
    idx = jnp.arange(width)
    nf = HEAD_DIM // 4
    return jnp.where((idx % (2 * nf)) < nf, idx + nf, idx - nf)


def _rope_tables(t):
    pos = jnp.arange(t)
    row = (pos // GRID_W).astype(F32)
    col = (pos % GRID_W).astype(F32)
    nf = HEAD_DIM // 4
    inv = ROPE_BASE ** (-jnp.arange(nf, dtype=F32) / nf)

    def tabs(p):
        ang = p[:, None] * inv[None, :]
        cos, sin = jnp.cos(ang), jnp.sin(ang)
        return jnp.concatenate([cos, cos], axis=1), jnp.concatenate([-sin, sin], axis=1)

    cr, sr = tabs(row)
    cc, sc = tabs(col)
    cos = jnp.concatenate([cr, cc], axis=1)
    sin = jnp.concatenate([sr, sc], axis=1)
    return jnp.tile(cos, (1, LANES // HEAD_DIM)), jnp.tile(sin, (1, LANES // HEAD_DIM))


def _route_and_experts(h2c, affc, h2l, affl, wg, wu, wd):
    bc, tc, _ = h2c.shape
    bl, tl, _ = h2l.shape
    ntb = SUPER // BLOCK
    capc = CAPACITY * tc // N_EXPERTS
    capl = CAPACITY * tl // N_EXPERTS
    slot_c, cnt_c = _route(affc, capc, capc)
    slot_l, cnt_l = _route(affl, capl, 0)
    tbc = tc // BLOCK
    rows = (SUPER // GATHER_TOKENS, GATHER_TOKENS)
    slot_c = slot_c.transpose(1, 0, 2).reshape(1, N_EXPERTS, *rows)
    gate_c = affc.transpose(1, 0, 2).reshape(1, N_EXPERTS, *rows)
    cnt_c = jnp.concatenate([cnt_c[:, :, :tbc].transpose(1, 0, 2).reshape(N_EXPERTS, ntb),
                             cnt_c[bc - 1, :, tbc:tbc + 1]], axis=1)[None]
    slot = jnp.concatenate([slot_c, slot_l.reshape(bl, N_EXPERTS, *rows)], axis=0)
    gate = jnp.concatenate([gate_c, affl.reshape(bl, N_EXPERTS, *rows)], axis=0)
    cnt = jnp.concatenate([cnt_c, cnt_l[:, :, :ntb + 1]], axis=0).reshape(-1)
    h2 = jnp.concatenate([h2c.reshape(1, SUPER, D_MODEL), h2l], axis=0)
    moe = _experts(cnt, h2, slot, gate, wg, wu, wd)
    return moe[0].reshape(bc, tc, D_MODEL), moe[1:]


def kernel(x_prompt, x_sample, cache_k, cache_v, state_ssm_re, state_ssm_im, c, c_ctx, ada_w, ada_b, norm1, norm2,
           final_norm, w_in, conv_w, w_conv_out, attn_sink, w_attn_out, ssm_a_re, ssm_a_im, ssm_log_dt, ssm_b_re,
           ssm_b_im, ssm_c_re, ssm_c_im, ssm_d, w_glu, w_out, router_w, w_gate, w_up, w_down):
    bc, tc, _ = x_prompt.shape
    bl, tl, _ = x_sample.shape
    assert bc * tc == SUPER and tl == SUPER and bl + 1 <= SUBLANES
    hp = _head_perm()
    cvecs = jnp.zeros((SUBLANES, D_MODEL), F32).at[:bl].set(c).at[bl].set(c_ctx)
    mods = _modulation(cvecs, ada_w, ada_b).reshape(DEPTH, SUBLANES, 6, D_MODEL)

    o_q, o_k, o_v, o_u, o_g = 3 * CONV_W, 3 * CONV_W + Q_W, 3 * CONV_W + Q_W + KV_W, 3 * CONV_W + Q_W + 2 * KV_W, \
        3 * CONV_W + Q_W + 2 * KV_W + SSM_W
    wq = w_in[:, :, o_q:o_k][:, :, hp]
    wk = w_in[:, :, o_k:o_v]
    w_qkvu = jnp.concatenate([wq, wk, w_in[:, :, o_v:o_g]], axis=-1)
    w_qkvu_l = jnp.concatenate([w_qkvu, wq[:, :, _rope_swap(Q_W)], wk[:, :, _rope_swap(KV_W)]], axis=-1).astype(BF16)
    w_qkvu_c = w_qkvu.astype(BF16)
    w_cg = jnp.concatenate([w_in[:, :, :o_q], w_in[:, :, o_g:]], axis=-1).astype(BF16)
    w_co = w_conv_out.astype(BF16)
    w_ao = w_attn_out[:, hp, :].astype(BF16)
    w_gl = w_glu.astype(BF16)
    w_o = w_out.astype(BF16)
    rwt = router_w.transpose(0, 2, 1)
    wg, wu, wd = w_gate.astype(BF16), w_up.astype(BF16), w_down.astype(BF16)
    sink = attn_sink.reshape(DEPTH, N_KV, GROUP).transpose(0, 2, 1).reshape(DEPTH, N_HEADS)
    rope_tabs = _rope_tables(tl)
    h0_all = jnp.stack([state_ssm_re[:, :, 0], state_ssm_im[:, :, 0], state_ssm_re[:, :, 1], state_ssm_im[:, :, 1]],
                       axis=0)
    h0_all = h0_all.reshape(4, bl, DEPTH, SSM_PAIRS, 2 * SSM_P).transpose(2, 3, 0, 1, 4)

    xp, xs = x_prompt, x_sample
    moe_c = moe_l = None
    modp_c = modp_l = None
    ks, vs, fins = [], [], []
    for l in range(DEPTH):
        mod_l = mods[l, :bl]
        mod_c = mods[l, bl:bl + 1]
        n1 = norm1[l].reshape(1, D_MODEL)
        n2 = norm2[l].reshape(1, D_MODEL)
        ssm_w = _ssm_weights(ssm_a_re[l], ssm_a_im[l], ssm_log_dt[l], ssm_b_re[l], ssm_b_im[l], ssm_c_re[l],
                             ssm_c_im[l], ssm_d[l])
        outs = _qkvu(xp, moe_c, modp_c, mod_c, n1, w_qkvu_c[l], None, tc, F32)
        q_c, k_c, v_c, u_c = outs[:4]
        if moe_c is not None:
            xp = outs[4]
        ks.append(k_c)
        vs.append(v_c)
        attn_c = _ctx_attention(sink[l], q_c, k_c, v_c)
        ssm_c, fin = _ssm(u_c, ssm_w, None)
        fins.append(fin)
        x1c, h2c, affc = _merge(xp, mod_c, n1, n2, w_cg[l], conv_w[l], w_co[l], attn_c, w_ao[l], ssm_c, w_gl[l],
                                w_o[l], rwt[l], tc)
        outs = _qkvu(xs, moe_l, modp_l, mod_l, n1, w_qkvu_l[l], rope_tabs, 512, BF16)
        q_l, k_l, v_l, u_l = outs[:4]
        if moe_l is not None:
            xs = outs[4]
        attn_l = _lat_attention(sink[l], q_l, k_l, v_l, cache_k[:, l].reshape(bl, -1, KV_W),
                                cache_v[:, l].reshape(bl, -1, KV_W))
        ssm_l, _ = _ssm(u_l, ssm_w, h0_all[l])
        x1l, h2l, affl = _merge(xs, mod_l, n1, n2, w_cg[l], conv_w[l], w_co[l], attn_l, w_ao[l], ssm_l, w_gl[l],
                                w_o[l], rwt[l], 256)
        moe_c, moe_l = _route_and_experts(h2c, affc, h2l, affl, wg[l], wu[l], wd[l])
        xp, xs = x1c, x1l
        modp_c, modp_l = mod_c, mod_l

    fn = final_norm.reshape(1, D_MODEL)
    y_prompt = _final(xp, moe_c, modp_c, fn, tc)
    y_sample = _final(xs, moe_l, modp_l, fn, 512)
    new_k = jnp.stack(ks, axis=1).reshape(bc, DEPTH, tc, N_KV, HEAD_DIM)
    new_v = jnp.stack(vs, axis=1).reshape(bc, DEPTH, tc, N_KV, HEAD_DIM)
    fin = jnp.stack(fins, axis=0)
    fin = fin.reshape(DEPTH, SSM_PAIRS, 2, 2, bc, 2, SSM_P).transpose(3, 4, 0, 2, 1, 5, 6)
    fin = fin.reshape(2, bc, DEPTH, 2, SSM_G, SSM_P)
    return (y_prompt, y_sample, new_k, new_v, fin[0], fin[1])
```

```python
import functools
import math

import jax
import jax.numpy as jnp
from jax import lax
from jax.experimental import pallas as pl
from jax.experimental.pallas import tpu as pltpu

D_MODEL = 1024
DEPTH = 4
GRID_W = 64
CONV_W = 512
N_HEADS = 8
N_KV = 2
HEAD_DIM = 64
GROUP = N_HEADS // N_KV
Q_W = N_HEADS * HEAD_DIM
KV_W = N_KV * HEAD_DIM
BLOCK = 128
ROPE_BASE = 10000.0
NEG_INF = -1e30
SSM_W = 512
SSM_GC = 16
SSM_G = SSM_W // SSM_GC
SSM_P = 64
N_BRANCH = 3
N_EXPERTS = 16
EXPERT_FF = 1024
CAPACITY = 2
EPS = 1e-6

F32 = jnp.float32
BF16 = jnp.bfloat16
HIGHEST = lax.Precision.HIGHEST

LANES = 128
SUBLANES = 8
SSM_CHUNK = 16
SSM_PAIRS = SSM_G // 2
SUPER = 4096
GATHER_TOKENS = 512
MIB = 1024 * 1024


def _params(sem, vmem_mib):
    return pltpu.CompilerParams(dimension_semantics=sem, vmem_limit_bytes=vmem_mib * MIB)


def _const_spec(shape):
    nd = len(shape)
    return pl.BlockSpec(shape, lambda *_: (0,) * nd, pipeline_mode=pl.Buffered(1))


def _layer_spec(arr):
    nd = arr.ndim
    return pl.BlockSpec((None,) + arr.shape[1:], lambda *idx: (idx[-1][0],) + (0,) * (nd - 1),
                        pipeline_mode=pl.Buffered(1))


def _layer_index(layer):
    return jnp.full((1,), layer, jnp.int32)


def _sigmoid(x):
    return 1.0 / (1.0 + jnp.exp(-x))


def _rms_mod(x, g, sc, sh):
    y = x * lax.rsqrt(jnp.mean(x * x, axis=-1, keepdims=True) + EPS)
    return (y * g) * (1.0 + sc) + sh


def _mod_kernel(c_ref, w_ref, b_ref, o_ref):
    cv = c_ref[...]
    s = cv * _sigmoid(cv)
    o_ref[...] = jnp.dot(s, w_ref[...], precision=HIGHEST, preferred_element_type=F32) + b_ref[...]


def _modulation(cvecs, ada_w, ada_b):
    nt = 1536
    return pl.pallas_call(
        _mod_kernel,
        grid=(DEPTH, 6 * D_MODEL // nt),
        in_specs=[pl.BlockSpec((SUBLANES, D_MODEL), lambda l, j: (0, 0)),
                  pl.BlockSpec((None, D_MODEL, nt), lambda l, j: (l, 0, j)),
                  pl.BlockSpec((None, 1, nt), lambda l, j: (l, 0, j))],
        out_specs=pl.BlockSpec((None, SUBLANES, nt), lambda l, j: (l, 0, j)),
        out_shape=jax.ShapeDtypeStruct((DEPTH, SUBLANES, 6 * D_MODEL), F32),
        compiler_params=_params(("parallel", "parallel"), 32),
        name="modulation",
    )(cvecs, ada_w, ada_b.reshape(DEPTH, 1, 6 * D_MODEL))


def _qkvu_kernel(*refs, rope, fuse_res):
    it = iter(refs[1:])
    x_ref = next(it)
    if fuse_res:
        moe_ref = next(it)
        modp_ref = next(it)
    mod_ref = next(it)
    n1_ref = next(it)
    w_ref = next(it)
    if rope:
        cos_ref = next(it)
        sin_ref = next(it)
    q_ref, k_ref, v_ref, u_ref = next(it), next(it), next(it), next(it)
    x = x_ref[...]
    if fuse_res:
        xo_ref = next(it)
        x = x + modp_ref[5:6, :] * moe_ref[...]
        xo_ref[...] = x
    h = _rms_mod(x, n1_ref[...], mod_ref[1:2, :], mod_ref[0:1, :]).astype(BF16)
    ncol = w_ref.shape[1] if rope else Q_W + 2 * KV_W + SSM_W
    p = jnp.dot(h, w_ref[:, 0:ncol], preferred_element_type=F32)
    q = p[:, 0:Q_W]
    k = p[:, Q_W:Q_W + KV_W]
    if rope:
        cos = cos_ref[...]
        sin = sin_ref[...]
        o = Q_W + 2 * KV_W + SSM_W
        q = jnp.concatenate(
            [q[:, m * LANES:(m + 1) * LANES] * cos + p[:, o + m * LANES:o + (m + 1) * LANES] * sin
             for m in range(Q_W // LANES)], axis=1)
        k = k * cos + p[:, o + Q_W:o + Q_W + KV_W] * sin
    q_ref[...] = (q * HEAD_DIM ** -0.5).astype(q_ref.dtype)
    k_ref[...] = k.astype(k_ref.dtype)
    v_ref[...] = p[:, Q_W + KV_W:Q_W + 2 * KV_W].astype(v_ref.dtype)
    u_ref[...] = p[:, Q_W + 2 * KV_W:Q_W + 2 * KV_W + SSM_W]


def _mod_spec(mod):
    if mod.shape[0] == 1:
        return pl.BlockSpec((None, 6, D_MODEL), lambda b, i, *_: (0, 0, 0))
    return pl.BlockSpec((None, 6, D_MODEL), lambda b, i, *_: (b, 0, 0))


def _qkvu(x, moe, modp, mod, n1, w, layer, rope_tabs, tm, kv_dtype):
    bsz, t, _ = x.shape
    rope = rope_tabs is not None
    fuse_res = moe is not None
    tok = lambda wd: pl.BlockSpec((None, tm, wd), lambda b, i, *_: (b, i, 0))
    in_specs, args = [tok(D_MODEL)], [x]
    if fuse_res:
        in_specs += [tok(D_MODEL), _mod_spec(modp)]
        args += [moe, modp]
    in_specs += [_mod_spec(mod), _layer_spec(n1), _layer_spec(w)]
    args += [mod, n1, w]
    if rope:
        in_specs += [pl.BlockSpec((tm, LANES), lambda b, i, *_: (i, 0))] * 2
        args += list(rope_tabs)
    out_specs = [tok(Q_W), tok(KV_W), tok(KV_W), tok(SSM_W)]
    out_shape = [jax.ShapeDtypeStruct((bsz, t, Q_W), BF16),
                 jax.ShapeDtypeStruct((bsz, t, KV_W), kv_dtype),
                 jax.ShapeDtypeStruct((bsz, t, KV_W), kv_dtype),
                 jax.ShapeDtypeStruct((bsz, t, SSM_W), F32)]
    if fuse_res:
        out_specs.append(tok(D_MODEL))
        out_shape.append(jax.ShapeDtypeStruct((bsz, t, D_MODEL), F32))
    grid_spec = pltpu.PrefetchScalarGridSpec(num_scalar_prefetch=1, grid=(bsz, t // tm), in_specs=in_specs,
                                             out_specs=out_specs)
    return pl.pallas_call(
        functools.partial(_qkvu_kernel, rope=rope, fuse_res=fuse_res), grid_spec=grid_spec, out_shape=out_shape,
        compiler_params=_params(("parallel", "parallel"), 48),
        name="qkvu",
    )(_layer_index(layer), *args)


def _stack_heads(q_ref):
    lo = lax.broadcasted_iota(jnp.int32, (1, LANES), 1) < HEAD_DIM
    keep_lo = jnp.where(lo, 1.0, 0.0).astype(BF16)
    keep_hi = jnp.where(lo, 0.0, 1.0).astype(BF16)
    parts = []
    for m in range(GROUP):
        qm = q_ref[:, m * LANES:(m + 1) * LANES]
        parts.append(qm * keep_lo)
        parts.append(qm * keep_hi)
    return jnp.concatenate(parts, axis=0), lo


def _attend(s, bias, sink_ref, v, tq, lo, o_ref):
    ps, dens = [], []
    for h in range(N_HEADS):
        sh = s[h * tq:(h + 1) * tq]
        if bias is not None:
            sh = sh + bias
        sink = sink_ref[h]
        mx = jnp.maximum(jnp.max(sh, axis=-1, keepdims=True), sink)
        p = jnp.exp(sh - mx)
        dens.append(jnp.sum(p, axis=-1, keepdims=True) + jnp.exp(sink - mx))
        ps.append(p.astype(BF16))
    o = jnp.dot(jnp.concatenate(ps, axis=0), v, preferred_element_type=F32)
    for m in range(GROUP):
        o0 = o[(2 * m) * tq:(2 * m + 1) * tq] / dens[2 * m]
        o1 = o[(2 * m + 1) * tq:(2 * m + 2) * tq] / dens[2 * m + 1]
        o_ref[:, m * LANES:(m + 1) * LANES] = jnp.where(lo, o0, o1).astype(o_ref.dtype)


def _ctx_attn_kernel(sink_ref, q_ref, k_ref, v_ref, o_ref):
    tq = q_ref.shape[0]
    qx, lo = _stack_heads(q_ref)
    s = lax.dot_general(qx, k_ref[...].astype(BF16), (((1,), (1,)), ((), ())), preferred_element_type=F32)
    _attend(s, None, sink_ref, v_ref[...].astype(BF16), tq, lo, o_ref)


def _ctx_attention(sink, q, k, v):
    bsz, t, _ = q.shape
    tok = lambda wd: pl.BlockSpec((None, t, wd), lambda b: (b, 0, 0))
    return pl.pallas_call(
        _ctx_attn_kernel, grid=(bsz,),
        in_specs=[pl.BlockSpec(memory_space=pltpu.SMEM), tok(Q_W), tok(KV_W), tok(KV_W)],
        out_specs=tok(Q_W), out_shape=jax.ShapeDtypeStruct((bsz, t, Q_W), BF16),
        compiler_params=_params(("parallel",), 32),
        name="ctx_attention",
    )(sink, q, k, v)


def _lat_attn_kernel(sink_ref, q_ref, kc_ref, vc_ref, kp_ref, k0_ref, kn_ref, vp_ref, v0_ref, vn_ref, o_ref):
    tq = q_ref.shape[0]
    i = pl.program_id(1)
    nb = pl.num_programs(1)
    qx, lo = _stack_heads(q_ref)
    kall = jnp.concatenate([kc_ref[...].astype(BF16), kp_ref[...], k0_ref[...], kn_ref[...]], axis=0)
    vall = jnp.concatenate([vc_ref[...].astype(BF16), vp_ref[...], v0_ref[...], vn_ref[...]], axis=0)
    s = lax.dot_general(qx, kall, (((1,), (1,)), ((), ())), preferred_element_type=F32)
    past = kc_ref.shape[0]
    r = lax.broadcasted_iota(jnp.int32, (tq, BLOCK), 0)
    c = lax.broadcasted_iota(jnp.int32, (tq, BLOCK), 1)
    m_prev = jnp.where((c >= r) & (i > 0), 0.0, NEG_INF)
    m_next = jnp.where((c <= r) & (i < nb - 1), 0.0, NEG_INF)
    bias = jnp.concatenate([jnp.zeros((tq, past), F32), m_prev, jnp.zeros((tq, BLOCK), F32), m_next], axis=1)
    _attend(s, bias, sink_ref, vall, tq, lo, o_ref)


def _lat_attention(sink, q, k, v, kc, vc):
    bsz, t, _ = q.shape
    nb = t // BLOCK
    past = kc.shape[1]
    tok = lambda wd: pl.BlockSpec((None, BLOCK, wd), lambda b, i: (b, i, 0))
    prev = pl.BlockSpec((None, BLOCK, KV_W), lambda b, i: (b, jnp.maximum(i - 1, 0), 0))
    nxt = pl.BlockSpec((None, BLOCK, KV_W), lambda b, i: (b, jnp.minimum(i + 1, nb - 1), 0))
    ctx = pl.BlockSpec((None, past, KV_W), lambda b, i: (b, 0, 0))
    return pl.pallas_call(
        _lat_attn_kernel, grid=(bsz, nb),
        in_specs=[pl.BlockSpec(memory_space=pltpu.SMEM), tok(Q_W), ctx, ctx,
                  prev, tok(KV_W), nxt, prev, tok(KV_W), nxt],
        out_specs=tok(Q_W), out_shape=jax.ShapeDtypeStruct((bsz, t, Q_W), BF16),
        compiler_params=_params(("parallel", "parallel"), 32),
        name="lat_attention",
    )(sink, q, kc, vc, k, k, k, v, v, v)


SSM_QUAD = LANES // (2 * SSM_GC)
GRANULES = LANES // SSM_GC


def _ssm_kernel(l_ref, u_ref, strip_ref, wst_ref, w2_ref, a_ref, h0_ref, d_ref, y_ref, fin_ref, pk_ref, z_ref,
                ent_ref, toep_ref, *, bsz, t):
    del l_ref
    nk = t // SSM_CHUNK
    gran = lax.broadcasted_iota(jnp.int32, (1, LANES), 1) // SSM_GC
    half = SSM_CHUNK // GRANULES

    def pack(b, carry):
        for p in range(SSM_QUAD):
            for col in range(2 * half):
                gl, sh = col // half, col % half
                src = 2 * p + gl
                acc = None
                for s8 in range(GRANULES):
                    x = u_ref[pl.ds(b * t + sh * GRANULES + s8, nk, stride=SSM_CHUNK), :]
                    shift = (SSM_GC * (s8 - src)) % LANES
                    x = pltpu.roll(x, shift, axis=1) if shift else x
                    acc = x if acc is None else jnp.where(gran == s8, x, acc)
                pk_ref[p, col, pl.ds(b, nk, stride=bsz), :] = acc
        return carry

    lax.fori_loop(0, bsz, pack, 0)

    rows = max(bsz, SUBLANES)
    steps = rows // bsz
    niter = nk // steps
    ns = SSM_W

    def step(sr, si, dr, di, ar, ai):
        return ar * sr - ai * si + dr, ar * si + ai * sr + di

    for p in range(SSM_QUAD):
        u = jnp.concatenate([pk_ref[p, col] for col in range(2 * half)], axis=1)
        ub = u.astype(BF16)
        gw = SSM_CHUNK * SSM_GC
        for gl in range(2):
            strip = strip_ref[p, gl]
            for s in range(SSM_CHUNK):
                off = (SSM_CHUNK - 1 - s) * SSM_GC
                blk = pltpu.roll(strip, 2 * gw - off, axis=1) if off else strip
                toep_ref[s * SSM_GC:(s + 1) * SSM_GC, :] = blk[:, 0:gw].astype(BF16)
            z_ref[:, gl * gw:(gl + 1) * gw] = jnp.dot(ub[:, gl * gw:(gl + 1) * gw], toep_ref[...],
                                                      preferred_element_type=F32)
        z_ref[:, ns:ns + 4 * LANES] = jnp.dot(ub, wst_ref[p], preferred_element_type=F32)
        afr, afi, abr, abi = a_ref[p, 0:1, :], a_ref[p, 1:2, :], a_ref[p, 2:3, :], a_ref[p, 3:4, :]

        def body(it, carry):
            sr, si, gr, gi = carry
            rf = pl.multiple_of(it * rows, SUBLANES)
            rb = pl.multiple_of((niter - 1 - it) * rows, SUBLANES)
            dfr, dfi = z_ref[pl.ds(rf, rows), ns:ns + LANES], z_ref[pl.ds(rf, rows), ns + LANES:ns + 2 * LANES]
            dbr = z_ref[pl.ds(rb, rows), ns + 2 * LANES:ns + 3 * LANES]
            dbi = z_ref[pl.ds(rb, rows), ns + 3 * LANES:ns + 4 * LANES]
            efr, efi, ebr, ebi = [], [], [None] * steps, [None] * steps
            for j in range(steps):
                efr.append(sr)
                efi.append(si)
                sr, si = step(sr, si, dfr[j * bsz:(j + 1) * bsz], dfi[j * bsz:(j + 1) * bsz], afr, afi)
            for j in reversed(range(steps)):
                ebr[j] = gr
                ebi[j] = gi
                gr, gi = step(gr, gi, dbr[j * bsz:(j + 1) * bsz], dbi[j * bsz:(j + 1) * bsz], abr, abi)
            cat = lambda xs: xs[0] if len(xs) == 1 else jnp.concatenate(xs, axis=0)
            ent_ref[pl.ds(rf, rows), 0:LANES] = cat(efr)
            ent_ref[pl.ds(rf, rows), LANES:2 * LANES] = cat(efi)
            ent_ref[pl.ds(rb, rows), 2 * LANES:3 * LANES] = cat(ebr)
            ent_ref[pl.ds(rb, rows), 3 * LANES:4 * LANES] = cat(ebi)
            return sr, si, gr, gi

        init = (h0_ref[p, 0], h0_ref[p, 1], h0_ref[p, 2], h0_ref[p, 3])
        sr, si, gr, gi = lax.fori_loop(0, niter, body, init)
        fin_ref[p, 0] = sr
        fin_ref[p, 1] = si
        fin_ref[p, 2] = gr
        fin_ref[p, 3] = gi
        y = (z_ref[:, 0:SSM_W] + jnp.dot(ent_ref[...].astype(BF16), w2_ref[p], preferred_element_type=F32)
             + d_ref[p] * u)
        for col in range(2 * half):
            pk_ref[p, col] = y[:, col * LANES:(col + 1) * LANES]

    def unpack(b, carry):
        for s in range(SSM_CHUNK):
            sh, s8 = s // GRANULES, s % GRANULES
            acc = None
            for dst in range(GRANULES):
                p, gl = dst // 2, dst % 2
                col = gl * half + sh
                x = pk_ref[p, col, pl.ds(b, nk, stride=bsz), :]
                shift = (SSM_GC * (dst - s8)) % LANES
                x = pltpu.roll(x, shift, axis=1) if shift else x
                acc = x if acc is None else jnp.where(gran == dst, x, acc)
            y_ref[pl.ds(b * t + s, nk, stride=SSM_CHUNK), :] = acc
        return carry

    lax.fori_loop(0, bsz, unpack, 0)


def _ssm(u, ssm_w, layer, h0):
    bsz, t, _ = u.shape
    strip, wst, w2, a16, dflat = ssm_w
    rows = (t // SSM_CHUNK) * bsz
    if h0 is None:
        h0 = jnp.zeros((SSM_PAIRS, 4, bsz, LANES), F32)
    gw = SSM_CHUNK * SSM_GC
    quad = lambda *tail: pl.BlockSpec((SSM_QUAD,) + tail, lambda q, lref: (q,) + (0,) * len(tail))
    lquad = lambda *tail: pl.BlockSpec((None, SSM_QUAD) + tail, lambda q, lref: (lref[0], q) + (0,) * len(tail))
    tok = pl.BlockSpec((bsz * t, LANES), lambda q, lref: (0, q), pipeline_mode=pl.Buffered(1))
    grid_spec = pltpu.PrefetchScalarGridSpec(
        num_scalar_prefetch=1, grid=(SSM_PAIRS // SSM_QUAD,),
        in_specs=[tok, lquad(2, SSM_GC, 2 * gw), lquad(SSM_W, 4 * LANES), lquad(4 * LANES, SSM_W), lquad(4, LANES),
                  quad(4, bsz, LANES), lquad(1, SSM_W)],
        out_specs=[tok, quad(4, bsz, LANES)],
        scratch_shapes=[pltpu.VMEM((SSM_QUAD, SSM_W // LANES, rows, LANES), F32),
                        pltpu.VMEM((rows, SSM_W + 4 * LANES), F32), pltpu.VMEM((rows, 4 * LANES), F32),
                        pltpu.VMEM((gw, gw), BF16)])
    y, fin = pl.pallas_call(
        functools.partial(_ssm_kernel, bsz=bsz, t=t), grid_spec=grid_spec,
        out_shape=[jax.ShapeDtypeStruct((bsz * t, SSM_W), F32),
                   jax.ShapeDtypeStruct((SSM_PAIRS, 4, bsz, LANES), F32)],
        compiler_params=_params(("parallel",), 52),
        name="ssm",
    )(_layer_index(layer), u.reshape(bsz * t, SSM_W), strip, wst, w2, a16, h0, dflat)
    return y.reshape(bsz, t, SSM_W), fin


def _ssm_weights(a_re, a_im, log_dt, b_re, b_im, c_re, c_im, ssm_d):
    L = SSM_CHUNK
    nl = a_re.shape[0]
    dt = jnp.exp(log_dt)[..., None]
    steps = jnp.arange(L + 1, dtype=F32).reshape(L + 1, 1, 1, 1, 1)
    mag = jnp.exp(a_re * dt * steps)
    ang = a_im * dt * steps
    pr, pi = mag * jnp.cos(ang), mag * jnp.sin(ang)
    nr, ni = pr[1] - 1.0, pi[1]
    den = a_re * a_re + a_im * a_im
    qr, qi = (nr * a_re + ni * a_im) / den, (ni * a_re - nr * a_im) / den
    bbr = qr[..., None] * b_re - qi[..., None] * b_im
    bbi = qr[..., None] * b_im + qi[..., None] * b_re
    cpr = c_re * pr[:L, :, :, :, None, :] - c_im * pi[:L, :, :, :, None, :]
    cpi = c_re * pi[:L, :, :, :, None, :] + c_im * pr[:L, :, :, :, None, :]
    kern = (jnp.einsum('tldgcp,ldgpk->tldgck', cpr, bbr, precision=HIGHEST)
            - jnp.einsum('tldgcp,ldgpk->tldgck', cpi, bbi, precision=HIGHEST))
    kf, kb = kern[:, :, 0], kern[:, :, 1]
    lagk = jnp.concatenate([kb[:0:-1], (kf[0] + kb[0])[None], kf[1:]], axis=0)
    strip = lagk.transpose(1, 2, 4, 0, 3).reshape(nl, SSM_G, SSM_GC, (2 * L - 1) * SSM_GC)
    strip = jnp.pad(strip, ((0, 0), (0, 0), (0, 0), (0, SSM_GC))).reshape(nl, SSM_PAIRS, 2, SSM_GC, 2 * L * SSM_GC)
    idx = jnp.arange(L)
    eye = jnp.eye(2, dtype=F32)

    def inject(kpow, d):
        ar, ai = pr[kpow][:, :, d, :, :, None], pi[kpow][:, :, d, :, :, None]
        re = ar * bbr[None, :, d] - ai * bbi[None, :, d]
        im = ar * bbi[None, :, d] + ai * bbr[None, :, d]
        return [w.transpose(1, 2, 0, 4, 3) for w in (re, im)]

    def pair_in(w):
        w = w.reshape(nl, SSM_PAIRS, 2, L, SSM_GC, -1)
        return jnp.einsum('lqgscn,gh->lqgschn', w, eye).reshape(nl, SSM_PAIRS, 2 * L * SSM_GC, -1)

    wst = jnp.concatenate([pair_in(w) for w in inject(L - 1 - idx, 0) + inject(idx, 1)], axis=-1)

    def readout(kpow, d):
        ar, ai = pr[kpow][:, :, d, :, None, :], pi[kpow][:, :, d, :, None, :]
        re = c_re[None, :, d] * ar - c_im[None, :, d] * ai
        im = c_re[None, :, d] * ai + c_im[None, :, d] * ar
        return [w.transpose(1, 2, 4, 0, 3) for w in (re, -im)]

    def pair_out(w):
        w = w.reshape(nl, SSM_PAIRS, 2, SSM_P, L * SSM_GC)
        return jnp.einsum('lqgpn,gh->lqgphn', w, eye).reshape(nl, SSM_PAIRS, 2 * SSM_P, 2 * L * SSM_GC)

    w2 = jnp.concatenate([pair_out(w) for w in readout(1 + idx, 0) + readout(L - idx, 1)], axis=2)
    plane = lambda x: x.reshape(nl, SSM_PAIRS, 2 * SSM_P)
    a16 = jnp.stack([plane(pr[L][:, 0]), plane(pi[L][:, 0]), plane(pr[L][:, 1]), plane(pi[L][:, 1])], axis=2)
    dflat = jnp.broadcast_to(ssm_d.reshape(nl, SSM_PAIRS, 2, 1, SSM_GC), (nl, SSM_PAIRS, 2, L, SSM_GC))
    return strip, wst.astype(BF16), w2.astype(BF16), a16, dflat.reshape(nl, SSM_PAIRS, 1, 2 * L * SSM_GC)


def _merge_kernel(l_ref, x_ref, xp_ref, xn_ref, mod_ref, n1_ref, n2_ref, wcg_ref, cw_ref, wco_ref, attn_ref, wao_ref,
                  ssm_ref, wglu_ref, wout_ref, rw_ref, x1_ref, h2_ref, aff_ref):
    del l_ref
    tm = x_ref.shape[0]
    i = pl.program_id(1)
    nt = pl.num_programs(1)
    x = x_ref[...]
    n1 = n1_ref[...]
    sh1, sc1, g1 = mod_ref[0:1, :], mod_ref[1:2, :], mod_ref[2:3, :]
    sh2, sc2 = mod_ref[3:4, :], mod_ref[4:5, :]
    cw = CONV_W

    def conv_in(h):
        xin = jnp.dot(h, wcg_ref[:, 0:cw], preferred_element_type=F32)
        cg = jnp.dot(h, wcg_ref[:, 2 * cw:3 * cw], preferred_element_type=F32)
        return cg * xin

    h = _rms_mod(x, n1, sc1, sh1).astype(BF16)
    z = conv_in(h)
    zp8 = conv_in(_rms_mod(xp_ref[...], n1, sc1, sh1).astype(BF16))
    zn8 = conv_in(_rms_mod(xn_ref[...], n1, sc1, sh1).astype(BF16))
    z_before = jnp.where(i > 0, zp8[SUBLANES - 1:SUBLANES, :], 0.0)
    z_after = jnp.where(i < nt - 1, zn8[0:1, :], 0.0)
    rows = lax.broadcasted_iota(jnp.int32, (tm, 1), 0)
    zl = jnp.where(rows == 0, z_before, pltpu.roll(z, 1, axis=0))
    zr = jnp.where(rows == tm - 1, z_after, pltpu.roll(z, tm - 1, axis=0))
    y = zl * cw_ref[0:1, :] + z * cw_ref[1:2, :] + zr * cw_ref[2:3, :]
    bg = jnp.dot(h, wcg_ref[:, cw:2 * cw], preferred_element_type=F32)
    conv_y = jnp.dot((bg * y).astype(BF16), wco_ref[...], preferred_element_type=F32)
    o = wcg_ref.shape[1] - N_BRANCH * D_MODEL
    ga = _sigmoid(jnp.dot(h, wcg_ref[:, o:o + D_MODEL], preferred_element_type=F32))
    mixed = ga * conv_y
    attn_y = jnp.dot(attn_ref[...], wao_ref[...], preferred_element_type=F32)
    gb = _sigmoid(jnp.dot(h, wcg_ref[:, o + D_MODEL:o + 2 * D_MODEL], preferred_element_type=F32))
    mixed = mixed + gb * attn_y
    s = ssm_ref[...]
    gelu = s * (0.5 * (1.0 + jnp.tanh(math.sqrt(2.0 / math.pi) * (s + 0.044715 * (s * s * s)))))
    zab = jnp.dot(gelu.astype(BF16), wglu_ref[...], preferred_element_type=F32)
    ssm_y = zab[:, 0:D_MODEL] * _sigmoid(zab[:, D_MODEL:2 * D_MODEL])
    gc = _sigmoid(jnp.dot(h, wcg_ref[:, o + 2 * D_MODEL:o + 3 * D_MODEL], preferred_element_type=F32))
    mixed = mixed + gc * ssm_y
    x1 = x + g1 * jnp.dot(mixed.astype(BF16), wout_ref[...], preferred_element_type=F32)
    x1_ref[...] = x1
    h2 = _rms_mod(x1, n2_ref[...], sc2, sh2)
    h2_ref[...] = h2.astype(BF16)
    logits = lax.dot_general(rw_ref[...], h2, (((1,), (1,)), ((), ())), precision=HIGHEST,
                             preferred_element_type=F32)
    e = jnp.exp(logits - jnp.max(logits, axis=0, keepdims=True))
    aff_ref[...] = e / jnp.sum(e, axis=0, keepdims=True)


def _merge(x, mod, n1, n2, wcg, cw, wco, attn, wao, ssm, wglu, wout, rwt, layer, tm):
    bsz, t, _ = x.shape
    nt = t // tm
    per = tm // SUBLANES
    tok = lambda wd: pl.BlockSpec((None, tm, wd), lambda b, i, *_: (b, i, 0))
    prev = pl.BlockSpec((None, SUBLANES, D_MODEL), lambda b, i, *_: (b, jnp.maximum(i * per - 1, 0), 0))
    nxt = pl.BlockSpec((None, SUBLANES, D_MODEL),
                       lambda b, i, *_: (b, jnp.minimum((i + 1) * per, t // SUBLANES - 1), 0))
    lay = _layer_spec
    grid_spec = pltpu.PrefetchScalarGridSpec(
        num_scalar_prefetch=1, grid=(bsz, nt),
        in_specs=[tok(D_MODEL), prev, nxt, _mod_spec(mod), lay(n1), lay(n2), lay(wcg), lay(cw), lay(wco), tok(Q_W),
                  lay(wao), tok(SSM_W), lay(wglu), lay(wout), lay(rwt)],
        out_specs=[tok(D_MODEL), tok(D_MODEL), pl.BlockSpec((None, N_EXPERTS, tm), lambda b, i, *_: (b, 0, i))])
    return pl.pallas_call(
        _merge_kernel, grid_spec=grid_spec,
        out_shape=[jax.ShapeDtypeStruct((bsz, t, D_MODEL), F32), jax.ShapeDtypeStruct((bsz, t, D_MODEL), BF16),
                   jax.ShapeDtypeStruct((bsz, N_EXPERTS, t), F32)],
        compiler_params=_params(("parallel", "parallel"), 56),
        name="merge",
    )(_layer_index(layer), x, x, x, mod, n1, n2, wcg, cw, wco, attn, wao, ssm, wglu, wout, rwt)


def _route_kernel(aff_ref, slot_ref, cnt_ref, *, cap, slot_stride):
    n = aff_ref.shape[1]
    bits = pltpu.bitcast(aff_ref[...], jnp.int32)
    thr = jnp.zeros((N_EXPERTS, 1), jnp.int32)
    for b in range(30, -1, -1):
        cand = thr | (1 << b)
        cnt = jnp.sum(jnp.where(bits >= cand, 1.0, 0.0), axis=1, keepdims=True)
        thr = jnp.where(cnt >= cap, cand, thr)
    need = cap - jnp.sum(jnp.where(bits > thr, 1.0, 0.0), axis=1, keepdims=True)
    ri = lax.broadcasted_iota(jnp.int32, (LANES, LANES), 0)
    ci = lax.broadcasted_iota(jnp.int32, (LANES, LANES), 1)
    tri = jnp.where(ri <= ci, 1.0, 0.0).astype(BF16)
    lane = lax.broadcasted_iota(jnp.int32, (N_EXPERTS, LANES), 1)
    base = pl.program_id(0) * slot_stride
    off_eq = jnp.zeros((N_EXPERTS, 1), F32)
    off = jnp.zeros((N_EXPERTS, 1), F32)
    cnts = jnp.zeros((N_EXPERTS, LANES), jnp.int32)
    for j in range(n // LANES):
        sl = slice(j * LANES, (j + 1) * LANES)
        bj = bits[:, sl]
        eqf = jnp.where(bj == thr, 1.0, 0.0)
        ceq = jnp.dot(eqf.astype(BF16), tri, preferred_element_type=F32) + off_eq
        off_eq = ceq[:, LANES - 1:LANES]
        self_ = jnp.where(bj > thr, 1.0, jnp.where(ceq <= need, eqf, 0.0))
        csel = jnp.dot(self_.astype(BF16), tri, preferred_element_type=F32) + off
        cnts = jnp.where(lane == j, off.astype(jnp.int32) + base, cnts)
        off = csel[:, LANES - 1:LANES]
        slot_ref[:, sl] = jnp.where(self_ > 0.0, csel.astype(jnp.int32) - 1 + base, -1)
    cnt_ref[...] = jnp.where(lane >= n // LANES, off.astype(jnp.int32) + base, cnts)


def _route(aff, cap, slot_stride):
    sets, _, n = aff.shape
    return pl.pallas_call(
        functools.partial(_route_kernel, cap=cap, slot_stride=slot_stride), grid=(sets,),
        in_specs=[pl.BlockSpec((None, N_EXPERTS, n), lambda s: (s, 0, 0))],
        out_specs=[pl.BlockSpec((None, N_EXPERTS, n), lambda s: (s, 0, 0)),
                   pl.BlockSpec((None, N_EXPERTS, LANES), lambda s: (s, 0, 0))],
        out_shape=[jax.ShapeDtypeStruct((sets, N_EXPERTS, n), jnp.int32),
                   jax.ShapeDtypeStruct((sets, N_EXPERTS, LANES), jnp.int32)],
        compiler_params=_params(("parallel",), 32),
        name="route",
    )(aff)


def _expert_kernel(cnt_ref, l_ref, h2_ref, slot_ref, gate_ref, wg_ref, wu_ref, wd_ref, o_ref, xs_ref, ys_ref, gs_ref,
                   *, nslot):
    del l_ref
    s = pl.program_id(0)
    e = pl.program_id(1)
    ntb = SUPER // BLOCK
    nsb = nslot // BLOCK
    per = GATHER_TOKENS // BLOCK
    cbase = (s * N_EXPERTS + e) * (ntb + 1)

    @pl.when(e == 0)
    def _():
        o_ref[...] = jnp.zeros_like(o_ref)

    xs_ref[...] = jnp.zeros_like(xs_ref)
    gs_ref[...] = jnp.zeros_like(gs_ref)
    ys_ref[nslot:nslot + BLOCK, :] = jnp.zeros((BLOCK, D_MODEL), BF16)

    srow = lax.broadcasted_iota(jnp.int32, (BLOCK, GATHER_TOKENS), 0)
    for g in range(SUPER // GATHER_TOKENS):
        slots = slot_ref[g:g + 1, :]
        gates = gate_ref[g:g + 1, :]
        for j in range(nsb):
            @pl.when((cnt_ref[cbase + g * per] < (j + 1) * BLOCK) & (cnt_ref[cbase + (g + 1) * per] > j * BLOCK))
            def _():
                pick = slots == srow + j * BLOCK
                x = jnp.dot(jnp.where(pick, 1.0, 0.0).astype(BF16), h2_ref[g * GATHER_TOKENS:(g + 1) * GATHER_TOKENS, :],
                            preferred_element_type=F32)
                xs_ref[j * BLOCK:(j + 1) * BLOCK, :] += x.astype(BF16)
                gs_ref[j * BLOCK:(j + 1) * BLOCK, :] += jnp.sum(jnp.where(pick, gates, 0.0), axis=1, keepdims=True)

    xs = xs_ref[...]
    hg = jnp.dot(xs, wg_ref[...], preferred_element_type=F32)
    hu = jnp.dot(xs, wu_ref[...], preferred_element_type=F32)
    act = (hg * _sigmoid(hg) * hu).astype(BF16)
    y = jnp.dot(act, wd_ref[...], preferred_element_type=F32) * gs_ref[...]
    ys_ref[0:nslot, :] = y.astype(BF16)

    wrow = lax.broadcasted_iota(jnp.int32, (2 * BLOCK, BLOCK), 0)
    for i in range(ntb):
        w0 = pl.multiple_of(jnp.minimum(cnt_ref[cbase + i] // BLOCK, nsb - 1) * BLOCK, BLOCK)
        slots = slot_ref[i // per:i // per + 1, (i % per) * BLOCK:(i % per + 1) * BLOCK]
        pick = jnp.where(slots - w0 == wrow, 1.0, 0.0).astype(BF16)
        o_ref[i * BLOCK:(i + 1) * BLOCK, :] += lax.dot_general(
            pick, ys_ref[pl.ds(w0, 2 * BLOCK), :], (((0,), (0,)), ((), ())), preferred_element_type=F32)


def _experts(cnt, h2, slot, gate, wg, wu, wd, layer):
    nsup = h2.shape[0]
    nslot = CAPACITY * SUPER // N_EXPERTS
    ngt = SUPER // GATHER_TOKENS
    grid_spec = pltpu.PrefetchScalarGridSpec(
        num_scalar_prefetch=2, grid=(nsup, N_EXPERTS),
        in_specs=[pl.BlockSpec((None, SUPER, D_MODEL), lambda s, e, c, l: (s, 0, 0), pipeline_mode=pl.Buffered(1)),
                  pl.BlockSpec((None, None, ngt, GATHER_TOKENS), lambda s, e, c, l: (s, e, 0, 0)),
                  pl.BlockSpec((None, None, ngt, GATHER_TOKENS), lambda s, e, c, l: (s, e, 0, 0)),
                  pl.BlockSpec((None, None, D_MODEL, EXPERT_FF), lambda s, e, c, l: (l[0], e, 0, 0)),
                  pl.BlockSpec((None, None, D_MODEL, EXPERT_FF), lambda s, e, c, l: (l[0], e, 0, 0)),
                  pl.BlockSpec((None, None, EXPERT_FF, D_MODEL), lambda s, e, c, l: (l[0], e, 0, 0))],
        out_specs=pl.BlockSpec((None, SUPER, D_MODEL), lambda s, e, c, l: (s, 0, 0), pipeline_mode=pl.Buffered(1)),
        scratch_shapes=[pltpu.VMEM((nslot, D_MODEL), BF16), pltpu.VMEM((nslot + BLOCK, D_MODEL), BF16),
                        pltpu.VMEM((nslot, 1), F32)])
    return pl.pallas_call(
        functools.partial(_expert_kernel, nslot=nslot), grid_spec=grid_spec,
        out_shape=jax.ShapeDtypeStruct((nsup, SUPER, D_MODEL), F32),
        compiler_params=_params(("arbitrary", "arbitrary"), 56),
        name="experts",
    )(cnt, _layer_index(layer), h2, slot, gate, wg, wu, wd)


def _final_kernel(x_ref, moe_ref, mod_ref, g_ref, o_ref):
    x = x_ref[...] + mod_ref[5:6, :] * moe_ref[...]
    o_ref[...] = x * lax.rsqrt(jnp.mean(x * x, axis=-1, keepdims=True) + EPS) * g_ref[...]


def _final(x1, moe, mod, g, tm):
    bsz, t, _ = x1.shape
    tok = pl.BlockSpec((None, tm, D_MODEL), lambda b, i: (b, i, 0))
    return pl.pallas_call(
        _final_kernel, grid=(bsz, t // tm),
        in_specs=[tok, tok, _mod_spec(mod), _const_spec((1, D_MODEL))],
        out_specs=tok, out_shape=jax.ShapeDtypeStruct((bsz, t, D_MODEL), F32),
        compiler_params=_params(("parallel", "parallel"), 32),
        name="final_norm",
    )(x1, moe, mod, g)


def _head_perm():
    cols = []
    for m in range(GROUP):
        for kv in range(N_KV):
            head = kv * GROUP + m
            cols.extend(range(head * HEAD_DIM, (head + 1) * HEAD_DIM))
    return jnp.array(cols, jnp.int32)


def _rope_swap(width):
    idx = jnp.arange(width)
    nf = HEAD_DIM // 4
    return jnp.where((idx % (2 * nf)) < nf, idx + nf, idx - nf)


def _rope_tables(t):
    pos = jnp.arange(t)
    row = (pos // GRID_W).astype(F32)
    col = (pos % GRID_W).astype(F32)
    nf = HEAD_DIM // 4
    inv = ROPE_BASE ** (-jnp.arange(nf, dtype=F32) / nf)

    def tabs(p):
        ang = p[:, None] * inv[None, :]
        cos, sin = jnp.cos(ang), jnp.sin(ang)
        return jnp.concatenate([cos, cos], axis=1), jnp.concatenate([-sin, sin], axis=1)

    cr, sr = tabs(row)
    cc, sc = tabs(col)
    cos = jnp.concatenate([cr, cc], axis=1)
    sin = jnp.concatenate([sr, sc], axis=1)
    return jnp.tile(cos, (1, LANES // HEAD_DIM)), jnp.tile(sin, (1, LANES // HEAD_DIM))


def _route_and_experts(h2c, affc, h2l, affl, wg, wu, wd, layer):
    bc, tc, _ = h2c.shape
    bl, tl, _ = h2l.shape
    ntb = SUPER // BLOCK
    capc = CAPACITY * tc // N_EXPERTS
    capl = CAPACITY * tl // N_EXPERTS
    slot_c, cnt_c = _route(affc, capc, capc)
    slot_l, cnt_l = _route(affl, capl, 0)
    tbc = tc // BLOCK
    rows = (SUPER // GATHER_TOKENS, GATHER_TOKENS)
    slot_c = slot_c.transpose(1, 0, 2).reshape(1, N_EXPERTS, *rows)
    gate_c = affc.transpose(1, 0, 2).reshape(1, N_EXPERTS, *rows)
    cnt_c = jnp.concatenate([cnt_c[:, :, :tbc].transpose(1, 0, 2).reshape(N_EXPERTS, ntb),
                             cnt_c[bc - 1, :, tbc:tbc + 1]], axis=1)
    moe_c = _experts(cnt_c.reshape(-1), h2c.reshape(1, SUPER, D_MODEL), slot_c, gate_c, wg, wu, wd, layer)
    moe_l = _experts(cnt_l[:, :, :ntb + 1].reshape(-1), h2l, slot_l.reshape(bl, N_EXPERTS, *rows),
                     affl.reshape(bl, N_EXPERTS, *rows), wg, wu, wd, layer)
    return moe_c.reshape(bc, tc, D_MODEL), moe_l


def kernel(x_prompt, x_sample, cache_k, cache_v, state_ssm_re, state_ssm_im, c, c_ctx, ada_w, ada_b, norm1, norm2,
           final_norm, w_in, conv_w, w_conv_out, attn_sink, w_attn_out, ssm_a_re, ssm_a_im, ssm_log_dt, ssm_b_re,
           ssm_b_im, ssm_c_re, ssm_c_im, ssm_d, w_glu, w_out, router_w, w_gate, w_up, w_down):
    bc, tc, _ = x_prompt.shape
    bl, tl, _ = x_sample.shape
    assert bc * tc == SUPER and tl == SUPER and bl + 1 <= SUBLANES
    hp = _head_perm()
    cvecs = jnp.zeros((SUBLANES, D_MODEL), F32).at[:bl].set(c).at[bl].set(c_ctx)
    mods = _modulation(cvecs, ada_w, ada_b).reshape(DEPTH, SUBLANES, 6, D_MODEL)

    o_q, o_k, o_v, o_u, o_g = 3 * CONV_W, 3 * CONV_W + Q_W, 3 * CONV_W + Q_W + KV_W, 3 * CONV_W + Q_W + 2 * KV_W, \
        3 * CONV_W + Q_W + 2 * KV_W + SSM_W
    w_b = w_in.astype(BF16)
    wq = w_b[:, :, o_q:o_k][:, :, hp]
    wk = w_b[:, :, o_k:o_v]
    w_qkvu = jnp.concatenate([wq, w_b[:, :, o_k:o_g], wq[:, :, _rope_swap(Q_W)], wk[:, :, _rope_swap(KV_W)]], axis=-1)
    w_co = w_conv_out.astype(BF16)
    w_ao = w_attn_out[:, hp, :].astype(BF16)
    w_gl = w_glu.astype(BF16)
    w_o = w_out.astype(BF16)
    rwt = router_w.transpose(0, 2, 1)
    wg, wu, wd = w_gate.astype(BF16), w_up.astype(BF16), w_down.astype(BF16)
    n1 = norm1.reshape(DEPTH, 1, D_MODEL)
    n2 = norm2.reshape(DEPTH, 1, D_MODEL)
    sink = attn_sink.reshape(DEPTH, N_KV, GROUP).transpose(0, 2, 1).reshape(DEPTH, N_HEADS)
    rope_tabs = _rope_tables(tl)
    ssm_w = _ssm_weights(ssm_a_re, ssm_a_im, ssm_log_dt, ssm_b_re, ssm_b_im, ssm_c_re, ssm_c_im, ssm_d)
    h0_all = jnp.stack([state_ssm_re[:, :, 0], state_ssm_im[:, :, 0], state_ssm_re[:, :, 1], state_ssm_im[:, :, 1]],
                       axis=0)
    h0_all = h0_all.reshape(4, bl, DEPTH, SSM_PAIRS, 2 * SSM_P).transpose(2, 3, 0, 1, 4)

    xp, xs = x_prompt, x_sample
    moe_c = moe_l = None
    modp_c = modp_l = None
    ks, vs, fins = [], [], []
    for l in range(DEPTH):
        mod_l = mods[l, :bl]
        mod_c = mods[l, bl:bl + 1]
        outs = _qkvu(xp, moe_c, modp_c, mod_c, n1, w_qkvu, l, None, tc, F32)
        q_c, k_c, v_c, u_c = outs[:4]
        if moe_c is not None:
            xp = outs[4]
        ks.append(k_c)
        vs.append(v_c)
        attn_c = _ctx_attention(sink[l], q_c, k_c, v_c)
        ssm_c, fin = _ssm(u_c, ssm_w, l, None)
        fins.append(fin)
        x1c, h2c, affc = _merge(xp, mod_c, n1, n2, w_b, conv_w, w_co, attn_c, w_ao, ssm_c, w_gl, w_o, rwt, l, tc)
        outs = _qkvu(xs, moe_l, modp_l, mod_l, n1, w_qkvu, l, rope_tabs, 512, BF16)
        q_l, k_l, v_l, u_l = outs[:4]
        if moe_l is not None:
            xs = outs[4]
        attn_l = _lat_attention(sink[l], q_l, k_l, v_l, cache_k[:, l].reshape(bl, -1, KV_W),
                                cache_v[:, l].reshape(bl, -1, KV_W))
        ssm_l, _ = _ssm(u_l, ssm_w, l, h0_all[l])
        x1l, h2l, affl = _merge(xs, mod_l, n1, n2, w_b, conv_w, w_co, attn_l, w_ao, ssm_l, w_gl, w_o, rwt, l, 256)
        moe_c, moe_l = _route_and_experts(h2c, affc, h2l, affl, wg, wu, wd, l)
        xp, xs = x1c, x1l
        modp_c, modp_l = mod_c, mod_l

    fn = final_norm.reshape(1, D_MODEL)
    y_prompt = _final(xp, moe_c, modp_c, fn, tc)
    y_sample = _final(xs, moe_l, modp_l, fn, 512)
    new_k = jnp.stack(ks, axis=1).reshape(bc, DEPTH, tc, N_KV, HEAD_DIM)
    new_v = jnp.stack(vs, axis=1).reshape(bc, DEPTH, tc, N_KV, HEAD_DIM)
    fin = jnp.stack(fins, axis=0)
    fin = fin.reshape(DEPTH, SSM_PAIRS, 2, 2, bc, 2, SSM_P).transpose(3, 4, 0, 2, 1, 5, 6)
    fin = fin.reshape(2, bc, DEPTH, 2, SSM_G, SSM_P)
    return (y_prompt, y_sample, new_k, new_v, fin[0], fin[1])
```

```python
import functools
import math

import jax
import jax.numpy as jnp
from jax import lax
from jax.experimental import pallas as pl
from jax.experimental.pallas import tpu as pltpu

D_MODEL = 1024
DEPTH = 4
GRID_W = 64
CONV_W = 512
N_HEADS = 8
N_KV = 2
HEAD_DIM = 64
GROUP = N_HEADS // N_KV
Q_W = N_HEADS * HEAD_DIM
KV_W = N_KV * HEAD_DIM
BLOCK = 128
ROPE_BASE = 10000.0
NEG_INF = -1e30
SSM_W = 512
SSM_GC = 16
SSM_G = SSM_W // SSM_GC
SSM_P = 64
N_BRANCH = 3
N_EXPERTS = 16
EXPERT_FF = 1024
CAPACITY = 2
EPS = 1e-6

F32 = jnp.float32
BF16 = jnp.bfloat16
HIGHEST = lax.Precision.HIGHEST

LANES = 128
SUBLANES = 8
SSM_CHUNK = 16
SSM_PAIRS = SSM_G // 2
SUPER = 4096
GATHER_TOKENS = 512
MIB = 1024 * 1024


def _params(sem, vmem_mib):
    return pltpu.CompilerParams(dimension_semantics=sem, vmem_limit_bytes=vmem_mib * MIB)


def _const_spec(shape):
    nd = len(shape)
    return pl.BlockSpec(shape, lambda *_: (0,) * nd, pipeline_mode=pl.Buffered(1))


def _layer_spec(arr):
    nd = arr.ndim
    return pl.BlockSpec((None,) + arr.shape[1:], lambda *idx: (idx[-1][0],) + (0,) * (nd - 1),
                        pipeline_mode=pl.Buffered(1))


def _layer_index(layer):
    return jnp.full((1,), layer, jnp.int32)


def _sigmoid(x):
    return 1.0 / (1.0 + jnp.exp(-x))


def _rms_mod(x, g, sc, sh):
    y = x * lax.rsqrt(jnp.mean(x * x, axis=-1, keepdims=True) + EPS)
    return (y * g) * (1.0 + sc) + sh


def _mod_kernel(c_ref, w_ref, b_ref, o_ref):
    cv = c_ref[...]
    s = cv * _sigmoid(cv)
    o_ref[...] = jnp.dot(s, w_ref[...], precision=HIGHEST, preferred_element_type=F32) + b_ref[...]


def _modulation(cvecs, ada_w, ada_b):
    nt = 1536
    return pl.pallas_call(
        _mod_kernel,
        grid=(DEPTH, 6 * D_MODEL // nt),
        in_specs=[pl.BlockSpec((SUBLANES, D_MODEL), lambda l, j: (0, 0)),
                  pl.BlockSpec((None, D_MODEL, nt), lambda l, j: (l, 0, j)),
                  pl.BlockSpec((None, 1, nt), lambda l, j: (l, 0, j))],
        out_specs=pl.BlockSpec((None, SUBLANES, nt), lambda l, j: (l, 0, j)),
        out_shape=jax.ShapeDtypeStruct((DEPTH, SUBLANES, 6 * D_MODEL), F32),
        compiler_params=_params(("parallel", "parallel"), 32),
        name="modulation",
    )(cvecs, ada_w, ada_b.reshape(DEPTH, 1, 6 * D_MODEL))


def _qkvu_kernel(*refs, rope, fuse_res):
    it = iter(refs[1:])
    x_ref = next(it)
    if fuse_res:
        moe_ref = next(it)
        modp_ref = next(it)
    mod_ref = next(it)
    n1_ref = next(it)
    w_ref = next(it)
    if rope:
        cos_ref = next(it)
        sin_ref = next(it)
    q_ref, k_ref, v_ref, u_ref = next(it), next(it), next(it), next(it)
    x = x_ref[...]
    if fuse_res:
        xo_ref = next(it)
        x = x + modp_ref[5:6, :] * moe_ref[...]
        xo_ref[...] = x
    h = _rms_mod(x, n1_ref[...], mod_ref[1:2, :], mod_ref[0:1, :]).astype(BF16)
    ncol = w_ref.shape[1] if rope else Q_W + 2 * KV_W + SSM_W
    p = jnp.dot(h, w_ref[:, 0:ncol], preferred_element_type=F32)
    q = p[:, 0:Q_W]
    k = p[:, Q_W:Q_W + KV_W]
    if rope:
        cos = cos_ref[...]
        sin = sin_ref[...]
        o = Q_W + 2 * KV_W + SSM_W
        q = jnp.concatenate(
            [q[:, m * LANES:(m + 1) * LANES] * cos + p[:, o + m * LANES:o + (m + 1) * LANES] * sin
             for m in range(Q_W // LANES)], axis=1)
        k = k * cos + p[:, o + Q_W:o + Q_W + KV_W] * sin
    q_ref[...] = (q * HEAD_DIM ** -0.5).astype(q_ref.dtype)
    k_ref[...] = k.astype(k_ref.dtype)
    v_ref[...] = p[:, Q_W + KV_W:Q_W + 2 * KV_W].astype(v_ref.dtype)
    u_ref[...] = p[:, Q_W + 2 * KV_W:Q_W + 2 * KV_W + SSM_W]


def _mod_spec(mod):
    if mod.shape[0] == 1:
        return pl.BlockSpec((None, 6, D_MODEL), lambda b, i, *_: (0, 0, 0))
    return pl.BlockSpec((None, 6, D_MODEL), lambda b, i, *_: (b, 0, 0))


def _qkvu(x, moe, modp, mod, n1, w, layer, rope_tabs, tm, kv_dtype):
    bsz, t, _ = x.shape
    rope = rope_tabs is not None
    fuse_res = moe is not None
    tok = lambda wd: pl.BlockSpec((None, tm, wd), lambda b, i, *_: (b, i, 0))
    in_specs, args = [tok(D_MODEL)], [x]
    if fuse_res:
        in_specs += [tok(D_MODEL), _mod_spec(modp)]
        args += [moe, modp]
    in_specs += [_mod_spec(mod), _layer_spec(n1), _layer_spec(w)]
    args += [mod, n1, w]
    if rope:
        in_specs += [pl.BlockSpec((tm, LANES), lambda b, i, *_: (i, 0))] * 2
        args += list(rope_tabs)
    out_specs = [tok(Q_W), tok(KV_W), tok(KV_W), tok(SSM_W)]
    out_shape = [jax.ShapeDtypeStruct((bsz, t, Q_W), BF16),
                 jax.ShapeDtypeStruct((bsz, t, KV_W), kv_dtype),
                 jax.ShapeDtypeStruct((bsz, t, KV_W), kv_dtype),
                 jax.ShapeDtypeStruct((bsz, t, SSM_W), F32)]
    if fuse_res:
        out_specs.append(tok(D_MODEL))
        out_shape.append(jax.ShapeDtypeStruct((bsz, t, D_MODEL), F32))
    grid_spec = pltpu.PrefetchScalarGridSpec(num_scalar_prefetch=1, grid=(bsz, t // tm), in_specs=in_specs,
                                             out_specs=out_specs)
    return pl.pallas_call(
        functools.partial(_qkvu_kernel, rope=rope, fuse_res=fuse_res), grid_spec=grid_spec, out_shape=out_shape,
        compiler_params=_params(("parallel", "parallel"), 48),
        name="qkvu",
    )(_layer_index(layer), *args)


def _stack_heads(q_ref):
    lo = lax.broadcasted_iota(jnp.int32, (1, LANES), 1) < HEAD_DIM
    keep_lo = jnp.where(lo, 1.0, 0.0).astype(BF16)
    keep_hi = jnp.where(lo, 0.0, 1.0).astype(BF16)
    parts = []
    for m in range(GROUP):
        qm = q_ref[:, m * LANES:(m + 1) * LANES]
        parts.append(qm * keep_lo)
        parts.append(qm * keep_hi)
    return jnp.concatenate(parts, axis=0), lo


def _attend(s, bias, sink_ref, v, tq, lo, o_ref):
    ps, dens = [], []
    for h in range(N_HEADS):
        sh = s[h * tq:(h + 1) * tq]
        if bias is not None:
            sh = sh + bias
        sink = sink_ref[h]
        mx = jnp.maximum(jnp.max(sh, axis=-1, keepdims=True), sink)
        p = jnp.exp(sh - mx)
        dens.append(jnp.sum(p, axis=-1, keepdims=True) + jnp.exp(sink - mx))
        ps.append(p.astype(BF16))
    o = jnp.dot(jnp.concatenate(ps, axis=0), v, preferred_element_type=F32)
    for m in range(GROUP):
        o0 = o[(2 * m) * tq:(2 * m + 1) * tq] / dens[2 * m]
        o1 = o[(2 * m + 1) * tq:(2 * m + 2) * tq] / dens[2 * m + 1]
        o_ref[:, m * LANES:(m + 1) * LANES] = jnp.where(lo, o0, o1).astype(o_ref.dtype)


def _ctx_attn_kernel(sink_ref, q_ref, k_ref, v_ref, o_ref):
    tq = q_ref.shape[0]
    qx, lo = _stack_heads(q_ref)
    s = lax.dot_general(qx, k_ref[...].astype(BF16), (((1,), (1,)), ((), ())), preferred_element_type=F32)
    _attend(s, None, sink_ref, v_ref[...].astype(BF16), tq, lo, o_ref)


def _ctx_attention(sink, q, k, v):
    bsz, t, _ = q.shape
    tok = lambda wd: pl.BlockSpec((None, t, wd), lambda b: (b, 0, 0))
    return pl.pallas_call(
        _ctx_attn_kernel, grid=(bsz,),
        in_specs=[pl.BlockSpec(memory_space=pltpu.SMEM), tok(Q_W), tok(KV_W), tok(KV_W)],
        out_specs=tok(Q_W), out_shape=jax.ShapeDtypeStruct((bsz, t, Q_W), BF16),
        compiler_params=_params(("parallel",), 32),
        name="ctx_attention",
    )(sink, q, k, v)


def _lat_attn_kernel(sink_ref, q_ref, kc_ref, vc_ref, kp_ref, k0_ref, kn_ref, vp_ref, v0_ref, vn_ref, o_ref):
    tq = q_ref.shape[0]
    i = pl.program_id(1)
    nb = pl.num_programs(1)
    qx, lo = _stack_heads(q_ref)
    kall = jnp.concatenate([kc_ref[...].astype(BF16), kp_ref[...], k0_ref[...], kn_ref[...]], axis=0)
    vall = jnp.concatenate([vc_ref[...].astype(BF16), vp_ref[...], v0_ref[...], vn_ref[...]], axis=0)
    s = lax.dot_general(qx, kall, (((1,), (1,)), ((), ())), preferred_element_type=F32)
    past = kc_ref.shape[0]
    r = lax.broadcasted_iota(jnp.int32, (tq, BLOCK), 0)
    c = lax.broadcasted_iota(jnp.int32, (tq, BLOCK), 1)
    m_prev = jnp.where((c >= r) & (i > 0), 0.0, NEG_INF)
    m_next = jnp.where((c <= r) & (i < nb - 1), 0.0, NEG_INF)
    bias = jnp.concatenate([jnp.zeros((tq, past), F32), m_prev, jnp.zeros((tq, BLOCK), F32), m_next], axis=1)
    _attend(s, bias, sink_ref, vall, tq, lo, o_ref)


def _lat_attention(sink, q, k, v, kc, vc):
    bsz, t, _ = q.shape
    nb = t // BLOCK
    past = kc.shape[1]
    tok = lambda wd: pl.BlockSpec((None, BLOCK, wd), lambda b, i: (b, i, 0))
    prev = pl.BlockSpec((None, BLOCK, KV_W), lambda b, i: (b, jnp.maximum(i - 1, 0), 0))
    nxt = pl.BlockSpec((None, BLOCK, KV_W), lambda b, i: (b, jnp.minimum(i + 1, nb - 1), 0))
    ctx = pl.BlockSpec((None, past, KV_W), lambda b, i: (b, 0, 0))
    return pl.pallas_call(
        _lat_attn_kernel, grid=(bsz, nb),
        in_specs=[pl.BlockSpec(memory_space=pltpu.SMEM), tok(Q_W), ctx, ctx,
                  prev, tok(KV_W), nxt, prev, tok(KV_W), nxt],
        out_specs=tok(Q_W), out_shape=jax.ShapeDtypeStruct((bsz, t, Q_W), BF16),
        compiler_params=_params(("parallel", "parallel"), 32),
        name="lat_attention",
    )(sink, q, kc, vc, k, k, k, v, v, v)


SSM_QUAD = LANES // (2 * SSM_GC)
GRANULES = LANES // SSM_GC


def _ssm_kernel(l_ref, u_ref, strip_ref, wst_ref, w2_ref, a_ref, h0_ref, d_ref, y_ref, fin_ref, pk_ref, z_ref,
                ent_ref, toep_ref, *, bsz, t):
    del l_ref
    nk = t // SSM_CHUNK
    gran = lax.broadcasted_iota(jnp.int32, (1, LANES), 1) // SSM_GC
    half = SSM_CHUNK // GRANULES

    def pack(b, carry):
        for p in range(SSM_QUAD):
            for col in range(2 * half):
                gl, sh = col // half, col % half
                src = 2 * p + gl
                acc = None
                for s8 in range(GRANULES):
                    x = u_ref[pl.ds(b * t + sh * GRANULES + s8, nk, stride=SSM_CHUNK), :]
                    shift = (SSM_GC * (s8 - src)) % LANES
                    x = pltpu.roll(x, shift, axis=1) if shift else x
                    acc = x if acc is None else jnp.where(gran == s8, x, acc)
                pk_ref[p, col, pl.ds(b, nk, stride=bsz), :] = acc
        return carry

    lax.fori_loop(0, bsz, pack, 0)

    rows = max(bsz, SUBLANES)
    steps = rows // bsz
    niter = nk // steps
    ns = SSM_W

    def step(sr, si, dr, di, ar, ai):
        return ar * sr - ai * si + dr, ar * si + ai * sr + di

    for p in range(SSM_QUAD):
        u = jnp.concatenate([pk_ref[p, col] for col in range(2 * half)], axis=1)
        ub = u.astype(BF16)
        gw = SSM_CHUNK * SSM_GC
        for gl in range(2):
            strip = strip_ref[p, gl]
            for s in range(SSM_CHUNK):
                off = (SSM_CHUNK - 1 - s) * SSM_GC
                blk = pltpu.roll(strip, 2 * gw - off, axis=1) if off else strip
                toep_ref[s * SSM_GC:(s + 1) * SSM_GC, :] = blk[:, 0:gw].astype(BF16)
            z_ref[:, gl * gw:(gl + 1) * gw] = jnp.dot(ub[:, gl * gw:(gl + 1) * gw], toep_ref[...],
                                                      preferred_element_type=F32)
        z_ref[:, ns:ns + 4 * LANES] = jnp.dot(ub, wst_ref[p], preferred_element_type=F32)
        afr, afi, abr, abi = a_ref[p, 0:1, :], a_ref[p, 1:2, :], a_ref[p, 2:3, :], a_ref[p, 3:4, :]

        def body(it, carry):
            sr, si, gr, gi = carry
            rf = pl.multiple_of(it * rows, SUBLANES)
            rb = pl.multiple_of((niter - 1 - it) * rows, SUBLANES)
            dfr, dfi = z_ref[pl.ds(rf, rows), ns:ns + LANES], z_ref[pl.ds(rf, rows), ns + LANES:ns + 2 * LANES]
            dbr = z_ref[pl.ds(rb, rows), ns + 2 * LANES:ns + 3 * LANES]
            dbi = z_ref[pl.ds(rb, rows), ns + 3 * LANES:ns + 4 * LANES]
            efr, efi, ebr, ebi = [], [], [None] * steps, [None] * steps
            for j in range(steps):
                efr.append(sr)
                efi.append(si)
                sr, si = step(sr, si, dfr[j * bsz:(j + 1) * bsz], dfi[j * bsz:(j + 1) * bsz], afr, afi)
            for j in reversed(range(steps)):
                ebr[j] = gr
                ebi[j] = gi
                gr, gi = step(gr, gi, dbr[j * bsz:(j + 1) * bsz], dbi[j * bsz:(j + 1) * bsz], abr, abi)
            cat = lambda xs: xs[0] if len(xs) == 1 else jnp.concatenate(xs, axis=0)
            ent_ref[pl.ds(rf, rows), 0:LANES] = cat(efr)
            ent_ref[pl.ds(rf, rows), LANES:2 * LANES] = cat(efi)
            ent_ref[pl.ds(rb, rows), 2 * LANES:3 * LANES] = cat(ebr)
            ent_ref[pl.ds(rb, rows), 3 * LANES:4 * LANES] = cat(ebi)
            return sr, si, gr, gi

        init = (h0_ref[p, 0], h0_ref[p, 1], h0_ref[p, 2], h0_ref[p, 3])
        sr, si, gr, gi = lax.fori_loop(0, niter, body, init)
        fin_ref[p, 0] = sr
        fin_ref[p, 1] = si
        fin_ref[p, 2] = gr
        fin_ref[p, 3] = gi
        y = (z_ref[:, 0:SSM_W] + jnp.dot(ent_ref[...].astype(BF16), w2_ref[p], preferred_element_type=F32)
             + d_ref[p] * u)
        for col in range(2 * half):
            pk_ref[p, col] = y[:, col * LANES:(col + 1) * LANES]

    def unpack(b, carry):
        for s in range(SSM_CHUNK):
            sh, s8 = s // GRANULES, s % GRANULES
            acc = None
            for dst in range(GRANULES):
                p, gl = dst // 2, dst % 2
                col = gl * half + sh
                x = pk_ref[p, col, pl.ds(b, nk, stride=bsz), :]
                shift = (SSM_GC * (dst - s8)) % LANES
                x = pltpu.roll(x, shift, axis=1) if shift else x
                acc = x if acc is None else jnp.where(gran == dst, x, acc)
            y_ref[pl.ds(b * t + s, nk, stride=SSM_CHUNK), :] = acc
        return carry

    lax.fori_loop(0, bsz, unpack, 0)


def _ssm(u, ssm_w, layer, h0):
    bsz, t, _ = u.shape
    strip, wst, w2, a16, dflat = ssm_w
    rows = (t // SSM_CHUNK) * bsz
    if h0 is None:
        h0 = jnp.zeros((SSM_PAIRS, 4, bsz, LANES), F32)
    gw = SSM_CHUNK * SSM_GC
    quad = lambda *tail: pl.BlockSpec((SSM_QUAD,) + tail, lambda q, lref: (q,) + (0,) * len(tail))
    lquad = lambda *tail: pl.BlockSpec((None, SSM_QUAD) + tail, lambda q, lref: (lref[0], q) + (0,) * len(tail))
    tok = pl.BlockSpec((bsz * t, LANES), lambda q, lref: (0, q), pipeline_mode=pl.Buffered(1))
    grid_spec = pltpu.PrefetchScalarGridSpec(
        num_scalar_prefetch=1, grid=(SSM_PAIRS // SSM_QUAD,),
        in_specs=[tok, lquad(2, SSM_GC, 2 * gw), lquad(SSM_W, 4 * LANES), lquad(4 * LANES, SSM_W), lquad(4, LANES),
                  quad(4, bsz, LANES), lquad(1, SSM_W)],
        out_specs=[tok, quad(4, bsz, LANES)],
        scratch_shapes=[pltpu.VMEM((SSM_QUAD, SSM_W // LANES, rows, LANES), F32),
                        pltpu.VMEM((rows, SSM_W + 4 * LANES), F32), pltpu.VMEM((rows, 4 * LANES), F32),
                        pltpu.VMEM((gw, gw), BF16)])
    y, fin = pl.pallas_call(
        functools.partial(_ssm_kernel, bsz=bsz, t=t), grid_spec=grid_spec,
        out_shape=[jax.ShapeDtypeStruct((bsz * t, SSM_W), F32),
                   jax.ShapeDtypeStruct((SSM_PAIRS, 4, bsz, LANES), F32)],
        compiler_params=_params(("parallel",), 52),
        name="ssm",
    )(_layer_index(layer), u.reshape(bsz * t, SSM_W), strip, wst, w2, a16, h0, dflat)
    return y.reshape(bsz, t, SSM_W), fin


def _ssm_weights(a_re, a_im, log_dt, b_re, b_im, c_re, c_im, ssm_d):
    L = SSM_CHUNK
    nl = a_re.shape[0]
    dt = jnp.exp(log_dt)[..., None]
    steps = jnp.arange(L + 1, dtype=F32).reshape(L + 1, 1, 1, 1, 1)
    mag = jnp.exp(a_re * dt * steps)
    ang = a_im * dt * steps
    pr, pi = mag * jnp.cos(ang), mag * jnp.sin(ang)
    nr, ni = pr[1] - 1.0, pi[1]
    den = a_re * a_re + a_im * a_im
    qr, qi = (nr * a_re + ni * a_im) / den, (ni * a_re - nr * a_im) / den
    bbr = qr[..., None] * b_re - qi[..., None] * b_im
    bbi = qr[..., None] * b_im + qi[..., None] * b_re
    cpr = c_re * pr[:L, :, :, :, None, :] - c_im * pi[:L, :, :, :, None, :]
    cpi = c_re * pi[:L, :, :, :, None, :] + c_im * pr[:L, :, :, :, None, :]
    kern = jnp.sum(cpr[..., None] * bbr[None, :, :, :, None] - cpi[..., None] * bbi[None, :, :, :, None], axis=-2)
    kf, kb = kern[:, :, 0], kern[:, :, 1]
    lagk = jnp.concatenate([kb[:0:-1], (kf[0] + kb[0])[None], kf[1:]], axis=0)
    strip = lagk.transpose(1, 2, 4, 0, 3).reshape(nl, SSM_G, SSM_GC, (2 * L - 1) * SSM_GC)
    strip = jnp.pad(strip, ((0, 0), (0, 0), (0, 0), (0, SSM_GC))).reshape(nl, SSM_PAIRS, 2, SSM_GC, 2 * L * SSM_GC)
    idx = jnp.arange(L)
    eye = jnp.eye(2, dtype=F32)

    def inject(kpow, d):
        ar, ai = pr[kpow][:, :, d, :, :, None], pi[kpow][:, :, d, :, :, None]
        re = ar * bbr[None, :, d] - ai * bbi[None, :, d]
        im = ar * bbi[None, :, d] + ai * bbr[None, :, d]
        return [w.transpose(1, 2, 0, 4, 3) for w in (re, im)]

    def pair_in(w):
        w = w.reshape(nl, SSM_PAIRS, 2, L, SSM_GC, -1)
        return jnp.einsum('lqgscn,gh->lqgschn', w, eye).reshape(nl, SSM_PAIRS, 2 * L * SSM_GC, -1)

    wst = jnp.concatenate([pair_in(w) for w in inject(L - 1 - idx, 0) + inject(idx, 1)], axis=-1)

    def readout(kpow, d):
        ar, ai = pr[kpow][:, :, d, :, None, :], pi[kpow][:, :, d, :, None, :]
        re = c_re[None, :, d] * ar - c_im[None, :, d] * ai
        im = c_re[None, :, d] * ai + c_im[None, :, d] * ar
        return [w.transpose(1, 2, 4, 0, 3) for w in (re, -im)]

    def pair_out(w):
        w = w.reshape(nl, SSM_PAIRS, 2, SSM_P, L * SSM_GC)
        return jnp.einsum('lqgpn,gh->lqgphn', w, eye).reshape(nl, SSM_PAIRS, 2 * SSM_P, 2 * L * SSM_GC)

    w2 = jnp.concatenate([pair_out(w) for w in readout(1 + idx, 0) + readout(L - idx, 1)], axis=2)
    plane = lambda x: x.reshape(nl, SSM_PAIRS, 2 * SSM_P)
    a16 = jnp.stack([plane(pr[L][:, 0]), plane(pi[L][:, 0]), plane(pr[L][:, 1]), plane(pi[L][:, 1])], axis=2)
    dflat = jnp.broadcast_to(ssm_d.reshape(nl, SSM_PAIRS, 2, 1, SSM_GC), (nl, SSM_PAIRS, 2, L, SSM_GC))
    return strip, wst.astype(BF16), w2.astype(BF16), a16, dflat.reshape(nl, SSM_PAIRS, 1, 2 * L * SSM_GC)


def _merge_kernel(l_ref, x_ref, xp_ref, xn_ref, mod_ref, n1_ref, n2_ref, wcg_ref, cw_ref, wco_ref, attn_ref, wao_ref,
                  ssm_ref, wglu_ref, wout_ref, rw_ref, x1_ref, h2_ref, aff_ref):
    del l_ref
    tm = x_ref.shape[0]
    i = pl.program_id(1)
    nt = pl.num_programs(1)
    x = x_ref[...]
    n1 = n1_ref[...]
    sh1, sc1, g1 = mod_ref[0:1, :], mod_ref[1:2, :], mod_ref[2:3, :]
    sh2, sc2 = mod_ref[3:4, :], mod_ref[4:5, :]
    cw = CONV_W

    def conv_in(h):
        xin = jnp.dot(h, wcg_ref[:, 0:cw], preferred_element_type=F32)
        cg = jnp.dot(h, wcg_ref[:, 2 * cw:3 * cw], preferred_element_type=F32)
        return cg * xin

    x_ext = jnp.concatenate([x, xp_ref[...], xn_ref[...]], axis=0)
    h_ext = _rms_mod(x_ext, n1, sc1, sh1).astype(BF16)
    h = h_ext[0:tm]
    z_ext = conv_in(h_ext)
    z = z_ext[0:tm]
    z_before = jnp.where(i > 0, z_ext[tm + SUBLANES - 1:tm + SUBLANES, :], 0.0)
    z_after = jnp.where(i < nt - 1, z_ext[tm + SUBLANES:tm + SUBLANES + 1, :], 0.0)
    rows = lax.broadcasted_iota(jnp.int32, (tm, 1), 0)
    zl = jnp.where(rows == 0, z_before, pltpu.roll(z, 1, axis=0))
    zr = jnp.where(rows == tm - 1, z_after, pltpu.roll(z, tm - 1, axis=0))
    y = zl * cw_ref[0:1, :] + z * cw_ref[1:2, :] + zr * cw_ref[2:3, :]
    bg = jnp.dot(h, wcg_ref[:, cw:2 * cw], preferred_element_type=F32)
    conv_y = jnp.dot((bg * y).astype(BF16), wco_ref[...], preferred_element_type=F32)
    o = wcg_ref.shape[1] - N_BRANCH * D_MODEL
    ga = _sigmoid(jnp.dot(h, wcg_ref[:, o:o + D_MODEL], preferred_element_type=F32))
    mixed = ga * conv_y
    attn_y = jnp.dot(attn_ref[...], wao_ref[...], preferred_element_type=F32)
    gb = _sigmoid(jnp.dot(h, wcg_ref[:, o + D_MODEL:o + 2 * D_MODEL], preferred_element_type=F32))
    mixed = mixed + gb * attn_y
    s = ssm_ref[...]
    gelu = s * (0.5 * (1.0 + jnp.tanh(math.sqrt(2.0 / math.pi) * (s + 0.044715 * (s * s * s)))))
    zab = jnp.dot(gelu.astype(BF16), wglu_ref[...], preferred_element_type=F32)
    ssm_y = zab[:, 0:D_MODEL] * _sigmoid(zab[:, D_MODEL:2 * D_MODEL])
    gc = _sigmoid(jnp.dot(h, wcg_ref[:, o + 2 * D_MODEL:o + 3 * D_MODEL], preferred_element_type=F32))
    mixed = mixed + gc * ssm_y
    x1 = x + g1 * jnp.dot(mixed.astype(BF16), wout_ref[...], preferred_element_type=F32)
    x1_ref[...] = x1
    h2 = _rms_mod(x1, n2_ref[...], sc2, sh2)
    h2_ref[...] = h2.astype(BF16)
    h_hi = h2.astype(BF16)
    h_lo = (h2 - h_hi.astype(F32)).astype(BF16)
    rw = rw_ref[...]
    r_hi = rw.astype(BF16)
    r_lo = (rw - r_hi.astype(F32)).astype(BF16)
    nt_dims = (((1,), (1,)), ((), ()))
    a = lax.dot_general(jnp.concatenate([r_hi, r_lo], axis=0), h_hi, nt_dims, preferred_element_type=F32)
    b = lax.dot_general(r_hi, h_lo, nt_dims, preferred_element_type=F32)
    logits = a[0:N_EXPERTS] + a[N_EXPERTS:2 * N_EXPERTS] + b
    e = jnp.exp(logits - jnp.max(logits, axis=0, keepdims=True))
    aff_ref[...] = e / jnp.sum(e, axis=0, keepdims=True)


def _merge(x, mod, n1, n2, wcg, cw, wco, attn, wao, ssm, wglu, wout, rwt, layer, tm):
    bsz, t, _ = x.shape
    nt = t // tm
    per = tm // SUBLANES
    tok = lambda wd: pl.BlockSpec((None, tm, wd), lambda b, i, *_: (b, i, 0))
    prev = pl.BlockSpec((None, SUBLANES, D_MODEL), lambda b, i, *_: (b, jnp.maximum(i * per - 1, 0), 0))
    nxt = pl.BlockSpec((None, SUBLANES, D_MODEL),
                       lambda b, i, *_: (b, jnp.minimum((i + 1) * per, t // SUBLANES - 1), 0))
    lay = _layer_spec
    grid_spec = pltpu.PrefetchScalarGridSpec(
        num_scalar_prefetch=1, grid=(bsz, nt),
        in_specs=[tok(D_MODEL), prev, nxt, _mod_spec(mod), lay(n1), lay(n2), lay(wcg), lay(cw), lay(wco), tok(Q_W),
                  lay(wao), tok(SSM_W), lay(wglu), lay(wout), lay(rwt)],
        out_specs=[tok(D_MODEL), tok(D_MODEL), pl.BlockSpec((None, N_EXPERTS, tm), lambda b, i, *_: (b, 0, i))])
    return pl.pallas_call(
        _merge_kernel, grid_spec=grid_spec,
        out_shape=[jax.ShapeDtypeStruct((bsz, t, D_MODEL), F32), jax.ShapeDtypeStruct((bsz, t, D_MODEL), BF16),
                   jax.ShapeDtypeStruct((bsz, N_EXPERTS, t), F32)],
        compiler_params=_params(("parallel", "parallel"), 56),
        name="merge",
    )(_layer_index(layer), x, x, x, mod, n1, n2, wcg, cw, wco, attn, wao, ssm, wglu, wout, rwt)


def _route_kernel(aff_ref, slot_ref, cnt_ref, *, cap, slot_stride):
    n = aff_ref.shape[1]
    bits = pltpu.bitcast(aff_ref[...], jnp.int32)
    thr = jnp.zeros((N_EXPERTS, 1), jnp.int32)
    for b in range(30, -1, -1):
        cand = thr | (1 << b)
        cnt = jnp.sum(jnp.where(bits >= cand, 1.0, 0.0), axis=1, keepdims=True)
        thr = jnp.where(cnt >= cap, cand, thr)
    need = cap - jnp.sum(jnp.where(bits > thr, 1.0, 0.0), axis=1, keepdims=True)
    ri = lax.broadcasted_iota(jnp.int32, (LANES, LANES), 0)
    ci = lax.broadcasted_iota(jnp.int32, (LANES, LANES), 1)
    tri = jnp.where(ri <= ci, 1.0, 0.0).astype(BF16)
    lane = lax.broadcasted_iota(jnp.int32, (N_EXPERTS, LANES), 1)
    base = pl.program_id(0) * slot_stride
    off_eq = jnp.zeros((N_EXPERTS, 1), F32)
    off = jnp.zeros((N_EXPERTS, 1), F32)
    cnts = jnp.zeros((N_EXPERTS, LANES), jnp.int32)
    for j in range(n // LANES):
        sl = slice(j * LANES, (j + 1) * LANES)
        bj = bits[:, sl]
        eqf = jnp.where(bj == thr, 1.0, 0.0)
        ceq = jnp.dot(eqf.astype(BF16), tri, preferred_element_type=F32) + off_eq
        off_eq = ceq[:, LANES - 1:LANES]
        self_ = jnp.where(bj > thr, 1.0, jnp.where(ceq <= need, eqf, 0.0))
        csel = jnp.dot(self_.astype(BF16), tri, preferred_element_type=F32) + off
        cnts = jnp.where(lane == j, off.astype(jnp.int32) + base, cnts)
        off = csel[:, LANES - 1:LANES]
        slot_ref[:, sl] = jnp.where(self_ > 0.0, csel.astype(jnp.int32) - 1 + base, -1)
    cnt_ref[...] = jnp.where(lane >= n // LANES, off.astype(jnp.int32) + base, cnts)


def _route(aff, cap, slot_stride):
    sets, _, n = aff.shape
    return pl.pallas_call(
        functools.partial(_route_kernel, cap=cap, slot_stride=slot_stride), grid=(sets,),
        in_specs=[pl.BlockSpec((None, N_EXPERTS, n), lambda s: (s, 0, 0))],
        out_specs=[pl.BlockSpec((None, N_EXPERTS, n), lambda s: (s, 0, 0)),
                   pl.BlockSpec((None, N_EXPERTS, LANES), lambda s: (s, 0, 0))],
        out_shape=[jax.ShapeDtypeStruct((sets, N_EXPERTS, n), jnp.int32),
                   jax.ShapeDtypeStruct((sets, N_EXPERTS, LANES), jnp.int32)],
        compiler_params=_params(("parallel",), 32),
        name="route",
    )(aff)


def _expert_kernel(cnt_ref, l_ref, h2_ref, slot_ref, gate_ref, wg_ref, wu_ref, wd_ref, o_ref, xs_ref, ys_ref, gs_ref,
                   *, nslot):
    del l_ref
    s = pl.program_id(0)
    e = pl.program_id(1)
    ntb = SUPER // BLOCK
    nsb = nslot // BLOCK
    per = GATHER_TOKENS // BLOCK
    cbase = (s * N_EXPERTS + e) * (ntb + 1)

    @pl.when(e == 0)
    def _():
        o_ref[...] = jnp.zeros_like(o_ref)

    xs_ref[...] = jnp.zeros_like(xs_ref)
    gs_ref[...] = jnp.zeros_like(gs_ref)
    ys_ref[nslot:nslot + BLOCK, :] = jnp.zeros((BLOCK, D_MODEL), BF16)

    srow = lax.broadcasted_iota(jnp.int32, (BLOCK, GATHER_TOKENS), 0)
    for g in range(SUPER // GATHER_TOKENS):
        slots = slot_ref[g:g + 1, :]
        gates = gate_ref[g:g + 1, :]
        for j in range(nsb):
            @pl.when((cnt_ref[cbase + g * per] < (j + 1) * BLOCK) & (cnt_ref[cbase + (g + 1) * per] > j * BLOCK))
            def _():
                pick = slots == srow + j * BLOCK
                x = jnp.dot(jnp.where(pick, 1.0, 0.0).astype(BF16), h2_ref[g * GATHER_TOKENS:(g + 1) * GATHER_TOKENS, :],
                            preferred_element_type=F32)
                xs_ref[j * BLOCK:(j + 1) * BLOCK, :] += x.astype(BF16)
                gs_ref[j * BLOCK:(j + 1) * BLOCK, :] += jnp.sum(jnp.where(pick, gates, 0.0), axis=1, keepdims=True)

    xs = xs_ref[...]
    hg = jnp.dot(xs, wg_ref[...], preferred_element_type=F32)
    hu = jnp.dot(xs, wu_ref[...], preferred_element_type=F32)
    act = (hg * _sigmoid(hg) * hu).astype(BF16)
    y = jnp.dot(act, wd_ref[...], preferred_element_type=F32) * gs_ref[...]
    ys_ref[0:nslot, :] = y.astype(BF16)

    wrow = lax.broadcasted_iota(jnp.int32, (2 * BLOCK, BLOCK), 0)
    for i in range(ntb):
        w0 = pl.multiple_of(jnp.minimum(cnt_ref[cbase + i] // BLOCK, nsb - 1) * BLOCK, BLOCK)
        slots = slot_ref[i // per:i // per + 1, (i % per) * BLOCK:(i % per + 1) * BLOCK]
        pick = jnp.where(slots - w0 == wrow, 1.0, 0.0).astype(BF16)
        o_ref[i * BLOCK:(i + 1) * BLOCK, :] += lax.dot_general(
            pick, ys_ref[pl.ds(w0, 2 * BLOCK), :], (((0,), (0,)), ((), ())), preferred_element_type=F32)


def _experts(cnt, h2, slot, gate, wg, wu, wd, layer):
    nsup = h2.shape[0]
    nslot = CAPACITY * SUPER // N_EXPERTS
    ngt = SUPER // GATHER_TOKENS
    grid_spec = pltpu.PrefetchScalarGridSpec(
        num_scalar_prefetch=2, grid=(nsup, N_EXPERTS),
        in_specs=[pl.BlockSpec((None, SUPER, D_MODEL), lambda s, e, c, l: (s, 0, 0), pipeline_mode=pl.Buffered(1)),
                  pl.BlockSpec((None, None, ngt, GATHER_TOKENS), lambda s, e, c, l: (s, e, 0, 0)),
                  pl.BlockSpec((None, None, ngt, GATHER_TOKENS), lambda s, e, c, l: (s, e, 0, 0)),
                  pl.BlockSpec((None, None, D_MODEL, EXPERT_FF), lambda s, e, c, l: (l[0], e, 0, 0)),
                  pl.BlockSpec((None, None, D_MODEL, EXPERT_FF), lambda s, e, c, l: (l[0], e, 0, 0)),
                  pl.BlockSpec((None, None, EXPERT_FF, D_MODEL), lambda s, e, c, l: (l[0], e, 0, 0))],
        out_specs=pl.BlockSpec((None, SUPER, D_MODEL), lambda s, e, c, l: (s, 0, 0), pipeline_mode=pl.Buffered(1)),
        scratch_shapes=[pltpu.VMEM((nslot, D_MODEL), BF16), pltpu.VMEM((nslot + BLOCK, D_MODEL), BF16),
                        pltpu.VMEM((nslot, 1), F32)])
    return pl.pallas_call(
        functools.partial(_expert_kernel, nslot=nslot), grid_spec=grid_spec,
        out_shape=jax.ShapeDtypeStruct((nsup, SUPER, D_MODEL), F32),
        compiler_params=_params(("arbitrary", "arbitrary"), 56),
        name="experts",
    )(cnt, _layer_index(layer), h2, slot, gate, wg, wu, wd)


def _final_kernel(x_ref, moe_ref, mod_ref, g_ref, o_ref):
    x = x_ref[...] + mod_ref[5:6, :] * moe_ref[...]
    o_ref[...] = x * lax.rsqrt(jnp.mean(x * x, axis=-1, keepdims=True) + EPS) * g_ref[...]


def _final(x1, moe, mod, g, tm):
    bsz, t, _ = x1.shape
    tok = pl.BlockSpec((None, tm, D_MODEL), lambda b, i: (b, i, 0))
    return pl.pallas_call(
        _final_kernel, grid=(bsz, t // tm),
        in_specs=[tok, tok, _mod_spec(mod), _const_spec((1, D_MODEL))],
        out_specs=tok, out_shape=jax.ShapeDtypeStruct((bsz, t, D_MODEL), F32),
        compiler_params=_params(("parallel", "parallel"), 32),
        name="final_norm",
    )(x1, moe, mod, g)


def _head_perm():
    cols = []
    for m in range(GROUP):
        for kv in range(N_KV):
            head = kv * GROUP + m
            cols.extend(range(head * HEAD_DIM, (head + 1) * HEAD_DIM))
    return jnp.array(cols, jnp.int32)


def _rope_swap(width):
    idx = jnp.arange(width)
    nf = HEAD_DIM // 4
    return jnp.where((idx % (2 * nf)) < nf, idx + nf, idx - nf)


def _rope_tables(t):
    pos = jnp.arange(t)
    row = (pos // GRID_W).astype(F32)
    col = (pos % GRID_W).astype(F32)
    nf = HEAD_DIM // 4
    inv = ROPE_BASE ** (-jnp.arange(nf, dtype=F32) / nf)

    def tabs(p):
        ang = p[:, None] * inv[None, :]
        cos, sin = jnp.cos(ang), jnp.sin(ang)
        return jnp.concatenate([cos, cos], axis=1), jnp.concatenate([-sin, sin], axis=1)

    cr, sr = tabs(row)
    cc, sc = tabs(col)
    cos = jnp.concatenate([cr, cc], axis=1)
    sin = jnp.concatenate([sr, sc], axis=1)
    return jnp.tile(cos, (1, LANES // HEAD_DIM)), jnp.tile(sin, (1, LANES // HEAD_DIM))


def _route_and_experts(h2c, affc, h2l, affl, wg, wu, wd, layer):
    bc, tc, _ = h2c.shape
    bl, tl, _ = h2l.shape
    ntb = SUPER // BLOCK
    capc = CAPACITY * tc // N_EXPERTS
    capl = CAPACITY * tl // N_EXPERTS
    slot_c, cnt_c = _route(affc, capc, capc)
    slot_l, cnt_l = _route(affl, capl, 0)
    tbc = tc // BLOCK
    rows = (SUPER // GATHER_TOKENS, GATHER_TOKENS)
    slot_c = slot_c.transpose(1, 0, 2).reshape(1, N_EXPERTS, *rows)
    gate_c = affc.transpose(1, 0, 2).reshape(1, N_EXPERTS, *rows)
    cnt_c = jnp.concatenate([cnt_c[:, :, :tbc].transpose(1, 0, 2).reshape(N_EXPERTS, ntb),
                             cnt_c[bc - 1, :, tbc:tbc + 1]], axis=1)
    moe_c = _experts(cnt_c.reshape(-1), h2c.reshape(1, SUPER, D_MODEL), slot_c, gate_c, wg, wu, wd, layer)
    moe_l = _experts(cnt_l[:, :, :ntb + 1].reshape(-1), h2l, slot_l.reshape(bl, N_EXPERTS, *rows),
                     affl.reshape(bl, N_EXPERTS, *rows), wg, wu, wd, layer)
    return moe_c.reshape(bc, tc, D_MODEL), moe_l


def kernel(x_prompt, x_sample, cache_k, cache_v, state_ssm_re, state_ssm_im, c, c_ctx, ada_w, ada_b, norm1, norm2,
           final_norm, w_in, conv_w, w_conv_out, attn_sink, w_attn_out, ssm_a_re, ssm_a_im, ssm_log_dt, ssm_b_re,
           ssm_b_im, ssm_c_re, ssm_c_im, ssm_d, w_glu, w_out, router_w, w_gate, w_up, w_down):
    bc, tc, _ = x_prompt.shape
    bl, tl, _ = x_sample.shape
    assert bc * tc == SUPER and tl == SUPER and bl + 1 <= SUBLANES
    hp = _head_perm()
    cvecs = jnp.zeros((SUBLANES, D_MODEL), F32).at[:bl].set(c).at[bl].set(c_ctx)
    mods = _modulation(cvecs, ada_w, ada_b).reshape(DEPTH, SUBLANES, 6, D_MODEL)

    o_q, o_k, o_v, o_u, o_g = 3 * CONV_W, 3 * CONV_W + Q_W, 3 * CONV_W + Q_W + KV_W, 3 * CONV_W + Q_W + 2 * KV_W, \
        3 * CONV_W + Q_W + 2 * KV_W + SSM_W
    w_b = w_in.astype(BF16)
    wq = w_b[:, :, o_q:o_k][:, :, hp]
    wk = w_b[:, :, o_k:o_v]
    w_qkvu = jnp.concatenate([wq, w_b[:, :, o_k:o_g], wq[:, :, _rope_swap(Q_W)], wk[:, :, _rope_swap(KV_W)]], axis=-1)
    w_co = w_conv_out.astype(BF16)
    w_ao = w_attn_out[:, hp, :].astype(BF16)
    w_gl = w_glu.astype(BF16)
    w_o = w_out.astype(BF16)
    rwt = router_w.transpose(0, 2, 1)
    wg, wu, wd = w_gate.astype(BF16), w_up.astype(BF16), w_down.astype(BF16)
    n1 = norm1.reshape(DEPTH, 1, D_MODEL)
    n2 = norm2.reshape(DEPTH, 1, D_MODEL)
    sink = attn_sink.reshape(DEPTH, N_KV, GROUP).transpose(0, 2, 1).reshape(DEPTH, N_HEADS)
    rope_tabs = _rope_tables(tl)
    ssm_w = _ssm_weights(ssm_a_re, ssm_a_im, ssm_log_dt, ssm_b_re, ssm_b_im, ssm_c_re, ssm_c_im, ssm_d)
    h0_all = jnp.stack([state_ssm_re[:, :, 0], state_ssm_im[:, :, 0], state_ssm_re[:, :, 1], state_ssm_im[:, :, 1]],
                       axis=0)
    h0_all = h0_all.reshape(4, bl, DEPTH, SSM_PAIRS, 2 * SSM_P).transpose(2, 3, 0, 1, 4)

    xp, xs = x_prompt, x_sample
    moe_c = moe_l = None
    modp_c = modp_l = None
    ks, vs, fins = [], [], []
    for l in range(DEPTH):
        mod_l = mods[l, :bl]
        mod_c = mods[l, bl:bl + 1]
        outs = _qkvu(xp, moe_c, modp_c, mod_c, n1, w_qkvu, l, None, tc, F32)
        q_c, k_c, v_c, u_c = outs[:4]
        if moe_c is not None:
            xp = outs[4]
        ks.append(k_c)
        vs.append(v_c)
        attn_c = _ctx_attention(sink[l], q_c, k_c, v_c)
        ssm_c, fin = _ssm(u_c, ssm_w, l, None)
        fins.append(fin)
        x1c, h2c, affc = _merge(xp, mod_c, n1, n2, w_b, conv_w, w_co, attn_c, w_ao, ssm_c, w_gl, w_o, rwt, l, tc)
        outs = _qkvu(xs, moe_l, modp_l, mod_l, n1, w_qkvu, l, rope_tabs, 512, BF16)
        q_l, k_l, v_l, u_l = outs[:4]
        if moe_l is not None:
            xs = outs[4]
        attn_l = _lat_attention(sink[l], q_l, k_l, v_l, cache_k[:, l].reshape(bl, -1, KV_W),
                                cache_v[:, l].reshape(bl, -1, KV_W))
        ssm_l, _ = _ssm(u_l, ssm_w, l, h0_all[l])
        x1l, h2l, affl = _merge(xs, mod_l, n1, n2, w_b, conv_w, w_co, attn_l, w_ao, ssm_l, w_gl, w_o, rwt, l, 512)
        moe_c, moe_l = _route_and_experts(h2c, affc, h2l, affl, wg, wu, wd, l)
        xp, xs = x1c, x1l
        modp_c, modp_l = mod_c, mod_l

    fn = final_norm.reshape(1, D_MODEL)
    y_prompt = _final(xp, moe_c, modp_c, fn, tc)
    y_sample = _final(xs, moe_l, modp_l, fn, 512)
    new_k = jnp.stack(ks, axis=1).reshape(bc, DEPTH, tc, N_KV, HEAD_DIM)
    new_v = jnp.stack(vs, axis=1).reshape(bc, DEPTH, tc, N_KV, HEAD_DIM)
    fin = jnp.stack(fins, axis=0)
    fin = fin.reshape(DEPTH, SSM_PAIRS, 2, 2, bc, 2, SSM_P).transpose(3, 4, 0, 2, 1, 5, 6)
    fin = fin.reshape(2, bc, DEPTH, 2, SSM_G, SSM_P)
    return (y_prompt, y_sample, new_k, new_v, fin[0], fin[1])
```

```python
import functools
import math

import jax
import jax.numpy as jnp
from jax import lax
from jax.experimental import pallas as pl
from jax.experimental.pallas import tpu as pltpu

D_MODEL = 1024
DEPTH = 4
GRID_W = 64
CONV_W = 512
N_HEADS = 8
N_KV = 2
HEAD_DIM = 64
GROUP = N_HEADS // N_KV
Q_W = N_HEADS * HEAD_DIM
KV_W = N_KV * HEAD_DIM
BLOCK = 128
ROPE_BASE = 10000.0
NEG_INF = -1e30
SSM_W = 512
SSM_GC = 16
SSM_G = SSM_W // SSM_GC
SSM_P = 64
N_BRANCH = 3
N_EXPERTS = 16
EXPERT_FF = 1024
CAPACITY = 2
EPS = 1e-6

F32 = jnp.float32
BF16 = jnp.bfloat16
HIGHEST = lax.Precision.HIGHEST

LANES = 128
SUBLANES = 8
SSM_CHUNK = 16
SSM_PAIRS = SSM_G // 2
SUPER = 4096
GATHER_TOKENS = 512
MIB = 1024 * 1024


def _params(sem, vmem_mib):
    return pltpu.CompilerParams(dimension_semantics=sem, vmem_limit_bytes=vmem_mib * MIB)


def _const_spec(shape):
    nd = len(shape)
    return pl.BlockSpec(shape, lambda *_: (0,) * nd, pipeline_mode=pl.Buffered(1))


def _layer_spec(arr):
    nd = arr.ndim
    return pl.BlockSpec((None,) + arr.shape[1:], lambda *idx: (idx[-1][0],) + (0,) * (nd - 1),
                        pipeline_mode=pl.Buffered(1))


def _layer_index(layer):
    return jnp.full((1,), layer, jnp.int32)


def _sigmoid(x):
    return 1.0 / (1.0 + jnp.exp(-x))


def _rms_mod(x, g, sc, sh):
    y = x * lax.rsqrt(jnp.mean(x * x, axis=-1, keepdims=True) + EPS)
    return (y * g) * (1.0 + sc) + sh


def _mod_kernel(c_ref, w_ref, b_ref, o_ref):
    cv = c_ref[...]
    s = cv * _sigmoid(cv)
    o_ref[...] = jnp.dot(s, w_ref[...], precision=HIGHEST, preferred_element_type=F32) + b_ref[...]


def _modulation(cvecs, ada_w, ada_b):
    nt = 1536
    return pl.pallas_call(
        _mod_kernel,
        grid=(DEPTH, 6 * D_MODEL // nt),
        in_specs=[pl.BlockSpec((SUBLANES, D_MODEL), lambda l, j: (0, 0)),
                  pl.BlockSpec((None, D_MODEL, nt), lambda l, j: (l, 0, j)),
                  pl.BlockSpec((None, 1, nt), lambda l, j: (l, 0, j))],
        out_specs=pl.BlockSpec((None, SUBLANES, nt), lambda l, j: (l, 0, j)),
        out_shape=jax.ShapeDtypeStruct((DEPTH, SUBLANES, 6 * D_MODEL), F32),
        compiler_params=_params(("parallel", "parallel"), 32),
        name="modulation",
    )(cvecs, ada_w, ada_b.reshape(DEPTH, 1, 6 * D_MODEL))


def _qkvu_kernel(*refs, rope, fuse_res):
    it = iter(refs[1:])
    x_ref = next(it)
    if fuse_res:
        moe_ref = next(it)
        modp_ref = next(it)
    mod_ref = next(it)
    n1_ref = next(it)
    w_ref = next(it)
    if rope:
        cos_ref = next(it)
        sin_ref = next(it)
    q_ref, k_ref, v_ref, u_ref = next(it), next(it), next(it), next(it)
    x = x_ref[...]
    if fuse_res:
        xo_ref = next(it)
        x = x + modp_ref[5:6, :] * moe_ref[...]
        xo_ref[...] = x
    h = _rms_mod(x, n1_ref[...], mod_ref[1:2, :], mod_ref[0:1, :]).astype(BF16)
    ncol = w_ref.shape[1] if rope else Q_W + 2 * KV_W + SSM_W
    p = jnp.dot(h, w_ref[:, 0:ncol], preferred_element_type=F32)
    q = p[:, 0:Q_W]
    k = p[:, Q_W:Q_W + KV_W]
    if rope:
        cos = cos_ref[...]
        sin = sin_ref[...]
        o = Q_W + 2 * KV_W + SSM_W
        q = jnp.concatenate(
            [q[:, m * LANES:(m + 1) * LANES] * cos + p[:, o + m * LANES:o + (m + 1) * LANES] * sin
             for m in range(Q_W // LANES)], axis=1)
        k = k * cos + p[:, o + Q_W:o + Q_W + KV_W] * sin
    q_ref[...] = (q * HEAD_DIM ** -0.5).astype(q_ref.dtype)
    k_ref[...] = k.astype(k_ref.dtype)
    v_ref[...] = p[:, Q_W + KV_W:Q_W + 2 * KV_W].astype(v_ref.dtype)
    u_ref[...] = p[:, Q_W + 2 * KV_W:Q_W + 2 * KV_W + SSM_W]


def _mod_spec(mod):
    if mod.shape[0] == 1:
        return pl.BlockSpec((None, 6, D_MODEL), lambda b, i, *_: (0, 0, 0))
    return pl.BlockSpec((None, 6, D_MODEL), lambda b, i, *_: (b, 0, 0))


def _qkvu(x, moe, modp, mod, n1, w, layer, rope_tabs, tm, kv_dtype):
    bsz, t, _ = x.shape
    rope = rope_tabs is not None
    fuse_res = moe is not None
    tok = lambda wd: pl.BlockSpec((None, tm, wd), lambda b, i, *_: (b, i, 0))
    in_specs, args = [tok(D_MODEL)], [x]
    if fuse_res:
        in_specs += [tok(D_MODEL), _mod_spec(modp)]
        args += [moe, modp]
    in_specs += [_mod_spec(mod), _layer_spec(n1), _layer_spec(w)]
    args += [mod, n1, w]
    if rope:
        in_specs += [pl.BlockSpec((tm, LANES), lambda b, i, *_: (i, 0))] * 2
        args += list(rope_tabs)
    out_specs = [tok(Q_W), tok(KV_W), tok(KV_W), tok(SSM_W)]
    out_shape = [jax.ShapeDtypeStruct((bsz, t, Q_W), BF16),
                 jax.ShapeDtypeStruct((bsz, t, KV_W), kv_dtype),
                 jax.ShapeDtypeStruct((bsz, t, KV_W), kv_dtype),
                 jax.ShapeDtypeStruct((bsz, t, SSM_W), F32)]
    if fuse_res:
        out_specs.append(tok(D_MODEL))
        out_shape.append(jax.ShapeDtypeStruct((bsz, t, D_MODEL), F32))
    grid_spec = pltpu.PrefetchScalarGridSpec(num_scalar_prefetch=1, grid=(bsz, t // tm), in_specs=in_specs,
                                             out_specs=out_specs)
    return pl.pallas_call(
        functools.partial(_qkvu_kernel, rope=rope, fuse_res=fuse_res), grid_spec=grid_spec, out_shape=out_shape,
        compiler_params=_params(("parallel", "parallel"), 48),
        name="qkvu",
    )(_layer_index(layer), *args)


def _stack_heads(q_ref):
    lo = lax.broadcasted_iota(jnp.int32, (1, LANES), 1) < HEAD_DIM
    keep_lo = jnp.where(lo, 1.0, 0.0).astype(BF16)
    keep_hi = jnp.where(lo, 0.0, 1.0).astype(BF16)
    parts = []
    for m in range(GROUP):
        qm = q_ref[:, m * LANES:(m + 1) * LANES]
        parts.append(qm * keep_lo)
        parts.append(qm * keep_hi)
    return jnp.concatenate(parts, axis=0), lo


def _attend(s, bias, sink_ref, v, tq, lo, o_ref):
    ps, dens = [], []
    for h in range(N_HEADS):
        sh = s[h * tq:(h + 1) * tq]
        if bias is not None:
            sh = sh + bias
        sink = sink_ref[h]
        mx = jnp.maximum(jnp.max(sh, axis=-1, keepdims=True), sink)
        p = jnp.exp(sh - mx)
        dens.append(jnp.sum(p, axis=-1, keepdims=True) + jnp.exp(sink - mx))
        ps.append(p.astype(BF16))
    o = jnp.dot(jnp.concatenate(ps, axis=0), v, preferred_element_type=F32)
    for m in range(GROUP):
        o0 = o[(2 * m) * tq:(2 * m + 1) * tq] / dens[2 * m]
        o1 = o[(2 * m + 1) * tq:(2 * m + 2) * tq] / dens[2 * m + 1]
        o_ref[:, m * LANES:(m + 1) * LANES] = jnp.where(lo, o0, o1).astype(o_ref.dtype)


def _ctx_attn_kernel(sink_ref, q_ref, k_ref, v_ref, o_ref):
    tq = q_ref.shape[0]
    qx, lo = _stack_heads(q_ref)
    s = lax.dot_general(qx, k_ref[...].astype(BF16), (((1,), (1,)), ((), ())), preferred_element_type=F32)
    _attend(s, None, sink_ref, v_ref[...].astype(BF16), tq, lo, o_ref)


def _ctx_attention(sink, q, k, v):
    bsz, t, _ = q.shape
    tok = lambda wd: pl.BlockSpec((None, t, wd), lambda b: (b, 0, 0))
    return pl.pallas_call(
        _ctx_attn_kernel, grid=(bsz,),
        in_specs=[pl.BlockSpec(memory_space=pltpu.SMEM), tok(Q_W), tok(KV_W), tok(KV_W)],
        out_specs=tok(Q_W), out_shape=jax.ShapeDtypeStruct((bsz, t, Q_W), BF16),
        compiler_params=_params(("parallel",), 32),
        name="ctx_attention",
    )(sink, q, k, v)


def _lat_attn_kernel(sink_ref, q_ref, kc_ref, vc_ref, kp_ref, k0_ref, kn_ref, vp_ref, v0_ref, vn_ref, o_ref):
    tq = q_ref.shape[0]
    i = pl.program_id(1)
    nb = pl.num_programs(1)
    qx, lo = _stack_heads(q_ref)
    kall = jnp.concatenate([kc_ref[...].astype(BF16), kp_ref[...], k0_ref[...], kn_ref[...]], axis=0)
    vall = jnp.concatenate([vc_ref[...].astype(BF16), vp_ref[...], v0_ref[...], vn_ref[...]], axis=0)
    s = lax.dot_general(qx, kall, (((1,), (1,)), ((), ())), preferred_element_type=F32)
    past = kc_ref.shape[0]
    r = lax.broadcasted_iota(jnp.int32, (tq, BLOCK), 0)
    c = lax.broadcasted_iota(jnp.int32, (tq, BLOCK), 1)
    m_prev = jnp.where((c >= r) & (i > 0), 0.0, NEG_INF)
    m_next = jnp.where((c <= r) & (i < nb - 1), 0.0, NEG_INF)
    bias = jnp.concatenate([jnp.zeros((tq, past), F32), m_prev, jnp.zeros((tq, BLOCK), F32), m_next], axis=1)
    _attend(s, bias, sink_ref, vall, tq, lo, o_ref)


def _lat_attention(sink, q, k, v, kc, vc):
    bsz, t, _ = q.shape
    nb = t // BLOCK
    past = kc.shape[1]
    tok = lambda wd: pl.BlockSpec((None, BLOCK, wd), lambda b, i: (b, i, 0))
    prev = pl.BlockSpec((None, BLOCK, KV_W), lambda b, i: (b, jnp.maximum(i - 1, 0), 0))
    nxt = pl.BlockSpec((None, BLOCK, KV_W), lambda b, i: (b, jnp.minimum(i + 1, nb - 1), 0))
    ctx = pl.BlockSpec((None, past, KV_W), lambda b, i: (b, 0, 0))
    return pl.pallas_call(
        _lat_attn_kernel, grid=(bsz, nb),
        in_specs=[pl.BlockSpec(memory_space=pltpu.SMEM), tok(Q_W), ctx, ctx,
                  prev, tok(KV_W), nxt, prev, tok(KV_W), nxt],
        out_specs=tok(Q_W), out_shape=jax.ShapeDtypeStruct((bsz, t, Q_W), BF16),
        compiler_params=_params(("parallel", "parallel"), 32),
        name="lat_attention",
    )(sink, q, kc, vc, k, k, k, v, v, v)


SSM_QUAD = LANES // (2 * SSM_GC)
GRANULES = LANES // SSM_GC


def _ssm_kernel(l_ref, u_ref, strip_ref, wst_ref, w2_ref, a_ref, h0_ref, d_ref, y_ref, fin_ref, pk_ref, z_ref,
                ent_ref, toep_ref, *, bsz, t):
    del l_ref
    nk = t // SSM_CHUNK
    gran = lax.broadcasted_iota(jnp.int32, (1, LANES), 1) // SSM_GC
    half = SSM_CHUNK // GRANULES

    ck = min(nk, 4 * SUBLANES)

    def transpose_granules(xs):
        xs = list(xs)
        for d in (4, 2, 1):
            low = (gran & d) == 0
            for i in range(GRANULES):
                if not i & d:
                    a, b = xs[i], xs[i + d]
                    xs[i] = jnp.where(low, a, pltpu.roll(b, SSM_GC * d, axis=1))
                    xs[i + d] = jnp.where(low, pltpu.roll(a, LANES - SSM_GC * d, axis=1), b)
        return xs

    def pack(it, carry):
        b, k0 = it // (nk // ck), (it % (nk // ck)) * ck
        for sh in range(half):
            xs = [u_ref[pl.ds(b * t + k0 * SSM_CHUNK + sh * GRANULES + s8, ck, stride=SSM_CHUNK), :]
                  for s8 in range(GRANULES)]
            for g8, x in enumerate(transpose_granules(xs)):
                pk_ref[g8 // 2, (g8 % 2) * half + sh, pl.ds(k0 * bsz + b, ck, stride=bsz), :] = x
        return carry

    lax.fori_loop(0, bsz * (nk // ck), pack, 0)

    rows = max(bsz, SUBLANES)
    steps = rows // bsz
    niter = nk // steps
    ns = SSM_W

    def step(sr, si, dr, di, ar, ai):
        return ar * sr - ai * si + dr, ar * si + ai * sr + di

    for p in range(SSM_QUAD):
        u = jnp.concatenate([pk_ref[p, col] for col in range(2 * half)], axis=1)
        ub = u.astype(BF16)
        gw = SSM_CHUNK * SSM_GC
        for gl in range(2):
            strip = strip_ref[p, gl]
            for s in range(SSM_CHUNK):
                off = (SSM_CHUNK - 1 - s) * SSM_GC
                blk = pltpu.roll(strip, 2 * gw - off, axis=1) if off else strip
                toep_ref[s * SSM_GC:(s + 1) * SSM_GC, :] = blk[:, 0:gw].astype(BF16)
            z_ref[:, gl * gw:(gl + 1) * gw] = jnp.dot(ub[:, gl * gw:(gl + 1) * gw], toep_ref[...],
                                                      preferred_element_type=F32)
        z_ref[:, ns:ns + 4 * LANES] = jnp.dot(ub, wst_ref[p], preferred_element_type=F32)
        afr, afi, abr, abi = a_ref[p, 0:1, :], a_ref[p, 1:2, :], a_ref[p, 2:3, :], a_ref[p, 3:4, :]

        def body(it, carry):
            sr, si, gr, gi = carry
            rf = pl.multiple_of(it * rows, SUBLANES)
            rb = pl.multiple_of((niter - 1 - it) * rows, SUBLANES)
            dfr, dfi = z_ref[pl.ds(rf, rows), ns:ns + LANES], z_ref[pl.ds(rf, rows), ns + LANES:ns + 2 * LANES]
            dbr = z_ref[pl.ds(rb, rows), ns + 2 * LANES:ns + 3 * LANES]
            dbi = z_ref[pl.ds(rb, rows), ns + 3 * LANES:ns + 4 * LANES]
            efr, efi, ebr, ebi = [], [], [None] * steps, [None] * steps
            for j in range(steps):
                efr.append(sr)
                efi.append(si)
                sr, si = step(sr, si, dfr[j * bsz:(j + 1) * bsz], dfi[j * bsz:(j + 1) * bsz], afr, afi)
            for j in reversed(range(steps)):
                ebr[j] = gr
                ebi[j] = gi
                gr, gi = step(gr, gi, dbr[j * bsz:(j + 1) * bsz], dbi[j * bsz:(j + 1) * bsz], abr, abi)
            cat = lambda xs: xs[0] if len(xs) == 1 else jnp.concatenate(xs, axis=0)
            ent_ref[pl.ds(rf, rows), 0:LANES] = cat(efr)
            ent_ref[pl.ds(rf, rows), LANES:2 * LANES] = cat(efi)
            ent_ref[pl.ds(rb, rows), 2 * LANES:3 * LANES] = cat(ebr)
            ent_ref[pl.ds(rb, rows), 3 * LANES:4 * LANES] = cat(ebi)
            return sr, si, gr, gi

        init = (h0_ref[p, 0], h0_ref[p, 1], h0_ref[p, 2], h0_ref[p, 3])
        sr, si, gr, gi = lax.fori_loop(0, niter, body, init)
        fin_ref[p, 0] = sr
        fin_ref[p, 1] = si
        fin_ref[p, 2] = gr
        fin_ref[p, 3] = gi
        y = (z_ref[:, 0:SSM_W] + jnp.dot(ent_ref[...].astype(BF16), w2_ref[p], preferred_element_type=F32)
             + d_ref[p] * u)
        for col in range(2 * half):
            pk_ref[p, col] = y[:, col * LANES:(col + 1) * LANES]

    def unpack(it, carry):
        b, k0 = it // (nk // ck), (it % (nk // ck)) * ck
        for sh in range(half):
            xs = [pk_ref[g8 // 2, (g8 % 2) * half + sh, pl.ds(k0 * bsz + b, ck, stride=bsz), :] for g8 in range(GRANULES)]
            for s8, x in enumerate(transpose_granules(xs)):
                y_ref[pl.ds(b * t + k0 * SSM_CHUNK + sh * GRANULES + s8, ck, stride=SSM_CHUNK), :] = x
        return carry

    lax.fori_loop(0, bsz * (nk // ck), unpack, 0)


def _ssm(u, ssm_w, layer, h0):
    bsz, t, _ = u.shape
    strip, wst, w2, a16, dflat = ssm_w
    rows = (t // SSM_CHUNK) * bsz
    if h0 is None:
        h0 = jnp.zeros((SSM_PAIRS, 4, bsz, LANES), F32)
    gw = SSM_CHUNK * SSM_GC
    quad = lambda *tail: pl.BlockSpec((SSM_QUAD,) + tail, lambda q, lref: (q,) + (0,) * len(tail))
    lquad = lambda *tail: pl.BlockSpec((None, SSM_QUAD) + tail, lambda q, lref: (lref[0], q) + (0,) * len(tail))
    tok = pl.BlockSpec((bsz * t, LANES), lambda q, lref: (0, q), pipeline_mode=pl.Buffered(1))
    grid_spec = pltpu.PrefetchScalarGridSpec(
        num_scalar_prefetch=1, grid=(SSM_PAIRS // SSM_QUAD,),
        in_specs=[tok, lquad(2, SSM_GC, 2 * gw), lquad(SSM_W, 4 * LANES), lquad(4 * LANES, SSM_W), lquad(4, LANES),
                  quad(4, bsz, LANES), lquad(1, SSM_W)],
        out_specs=[tok, quad(4, bsz, LANES)],
        scratch_shapes=[pltpu.VMEM((SSM_QUAD, SSM_W // LANES, rows, LANES), F32),
                        pltpu.VMEM((rows, SSM_W + 4 * LANES), F32), pltpu.VMEM((rows, 4 * LANES), F32),
                        pltpu.VMEM((gw, gw), BF16)])
    y, fin = pl.pallas_call(
        functools.partial(_ssm_kernel, bsz=bsz, t=t), grid_spec=grid_spec,
        out_shape=[jax.ShapeDtypeStruct((bsz * t, SSM_W), F32),
                   jax.ShapeDtypeStruct((SSM_PAIRS, 4, bsz, LANES), F32)],
        compiler_params=_params(("parallel",), 52),
        name="ssm",
    )(_layer_index(layer), u.reshape(bsz * t, SSM_W), strip, wst, w2, a16, h0, dflat)
    return y.reshape(bsz, t, SSM_W), fin


def _ssm_weights(a_re, a_im, log_dt, b_re, b_im, c_re, c_im, ssm_d):
    L = SSM_CHUNK
    nl = a_re.shape[0]
    dt = jnp.exp(log_dt)[..., None, None]
    steps = jnp.arange(L + 1, dtype=F32)
    mag = jnp.exp(a_re[..., None] * dt * steps)
    ang = a_im[..., None] * dt * steps
    pr, pi = mag * jnp.cos(ang), mag * jnp.sin(ang)
    nr, ni = pr[..., 1] - 1.0, pi[..., 1]
    den = a_re * a_re + a_im * a_im
    qr, qi = (nr * a_re + ni * a_im) / den, (ni * a_re - nr * a_im) / den
    swap = lambda x: x.transpose(0, 1, 2, 4, 3)
    bbr = swap(qr[..., None] * b_re - qi[..., None] * b_im)
    bbi = swap(qr[..., None] * b_im + qi[..., None] * b_re)
    ctr, cti = swap(c_re), swap(c_im)
    cpr = ctr[..., None, :] * pr[..., :L, None] - cti[..., None, :] * pi[..., :L, None]
    cpi = ctr[..., None, :] * pi[..., :L, None] + cti[..., None, :] * pr[..., :L, None]
    kern = jnp.sum(bbr[..., None, None] * cpr[:, :, :, None] - bbi[..., None, None] * cpi[:, :, :, None], axis=4)
    kf, kb = kern[:, 0], kern[:, 1]
    lagk = jnp.concatenate([kb[..., :0:-1, :], kf[..., 0:1, :] + kb[..., 0:1, :], kf[..., 1:, :]], axis=-2)
    strip = jnp.pad(lagk.reshape(nl, SSM_G, SSM_GC, (2 * L - 1) * SSM_GC), ((0, 0), (0, 0), (0, 0), (0, SSM_GC)))
    strip = strip.reshape(nl, SSM_PAIRS, 2, SSM_GC, 2 * L * SSM_GC)
    idx = jnp.arange(L)

    def pair(w):
        w = w.reshape((nl, SSM_PAIRS, 2) + w.shape[2:])
        even, odd = w[:, :, 0], w[:, :, 1]
        zero = jnp.zeros_like(even)
        return jnp.stack([jnp.concatenate([even, zero], axis=-1), jnp.concatenate([zero, odd], axis=-1)], axis=2)

    def inject(kpow, d):
        ar = pr[:, d].transpose(0, 1, 3, 2)[:, :, kpow][:, :, :, None, :]
        ai = pi[:, d].transpose(0, 1, 3, 2)[:, :, kpow][:, :, :, None, :]
        br, bi = bbr[:, d][:, :, None], bbi[:, d][:, :, None]
        return [pair(w).reshape(nl, SSM_PAIRS, 2 * L * SSM_GC, 2 * SSM_P) for w in (ar * br - ai * bi, ar * bi + ai * br)]

    wst = jnp.concatenate(inject(L - 1 - idx, 0) + inject(idx, 1), axis=-1)

    def readout(kpow, d):
        ar, ai = pr[:, d][..., kpow][..., None], pi[:, d][..., kpow][..., None]
        cr, ci = ctr[:, d][:, :, :, None, :], cti[:, d][:, :, :, None, :]
        planes = (cr * ar - ci * ai, -(cr * ai + ci * ar))
        return [pair(w.reshape(nl, SSM_G, SSM_P, L * SSM_GC)).reshape(nl, SSM_PAIRS, 2 * SSM_P, 2 * L * SSM_GC)
                for w in planes]

    w2 = jnp.concatenate(readout(1 + idx, 0) + readout(L - idx, 1), axis=2)
    plane = lambda x: x.reshape(nl, SSM_PAIRS, 2 * SSM_P)
    a16 = jnp.stack([plane(pr[:, 0, :, :, L]), plane(pi[:, 0, :, :, L]), plane(pr[:, 1, :, :, L]),
                     plane(pi[:, 1, :, :, L])], axis=2)
    dflat = jnp.broadcast_to(ssm_d.reshape(nl, SSM_PAIRS, 2, 1, SSM_GC), (nl, SSM_PAIRS, 2, L, SSM_GC))
    return strip, wst.astype(BF16), w2.astype(BF16), a16, dflat.reshape(nl, SSM_PAIRS, 1, 2 * L * SSM_GC)


def _merge_kernel(l_ref, x_ref, xp_ref, xn_ref, mod_ref, n1_ref, n2_ref, wcg_ref, cw_ref, wco_ref, attn_ref, wao_ref,
                  ssm_ref, wglu_ref, wout_ref, rw_ref, x1_ref, h2_ref, aff_ref):
    del l_ref
    tm = x_ref.shape[0]
    i = pl.program_id(1)
    nt = pl.num_programs(1)
    x = x_ref[...]
    n1 = n1_ref[...]
    sh1, sc1, g1 = mod_ref[0:1, :], mod_ref[1:2, :], mod_ref[2:3, :]
    sh2, sc2 = mod_ref[3:4, :], mod_ref[4:5, :]
    cw = CONV_W

    def conv_in(h):
        xin = jnp.dot(h, wcg_ref[:, 0:cw], preferred_element_type=F32)
        cg = jnp.dot(h, wcg_ref[:, 2 * cw:3 * cw], preferred_element_type=F32)
        return cg * xin

    x_ext = jnp.concatenate([x, xp_ref[...], xn_ref[...]], axis=0)
    h_ext = _rms_mod(x_ext, n1, sc1, sh1).astype(BF16)
    h = h_ext[0:tm]
    z_ext = conv_in(h_ext)
    z = z_ext[0:tm]
    z_before = jnp.where(i > 0, z_ext[tm + SUBLANES - 1:tm + SUBLANES, :], 0.0)
    z_after = jnp.where(i < nt - 1, z_ext[tm + SUBLANES:tm + SUBLANES + 1, :], 0.0)
    rows = lax.broadcasted_iota(jnp.int32, (tm, 1), 0)
    zl = jnp.where(rows == 0, z_before, pltpu.roll(z, 1, axis=0))
    zr = jnp.where(rows == tm - 1, z_after, pltpu.roll(z, tm - 1, axis=0))
    y = zl * cw_ref[0:1, :] + z * cw_ref[1:2, :] + zr * cw_ref[2:3, :]
    bg = jnp.dot(h, wcg_ref[:, cw:2 * cw], preferred_element_type=F32)
    conv_y = jnp.dot((bg * y).astype(BF16), wco_ref[...], preferred_element_type=F32)
    o = wcg_ref.shape[1] - N_BRANCH * D_MODEL
    ga = _sigmoid(jnp.dot(h, wcg_ref[:, o:o + D_MODEL], preferred_element_type=F32))
    mixed = ga * conv_y
    attn_y = jnp.dot(attn_ref[...], wao_ref[...], preferred_element_type=F32)
    gb = _sigmoid(jnp.dot(h, wcg_ref[:, o + D_MODEL:o + 2 * D_MODEL], preferred_element_type=F32))
    mixed = mixed + gb * attn_y
    s = ssm_ref[...]
    gelu = s * (0.5 * (1.0 + jnp.tanh(math.sqrt(2.0 / math.pi) * (s + 0.044715 * (s * s * s)))))
    zab = jnp.dot(gelu.astype(BF16), wglu_ref[...], preferred_element_type=F32)
    ssm_y = zab[:, 0:D_MODEL] * _sigmoid(zab[:, D_MODEL:2 * D_MODEL])
    gc = _sigmoid(jnp.dot(h, wcg_ref[:, o + 2 * D_MODEL:o + 3 * D_MODEL], preferred_element_type=F32))
    mixed = mixed + gc * ssm_y
    x1 = x + g1 * jnp.dot(mixed.astype(BF16), wout_ref[...], preferred_element_type=F32)
    x1_ref[...] = x1
    h2 = _rms_mod(x1, n2_ref[...], sc2, sh2)
    h2_ref[...] = h2.astype(BF16)
    h_hi = h2.astype(BF16)
    h_lo = (h2 - h_hi.astype(F32)).astype(BF16)
    rw = rw_ref[...]
    r_hi = rw.astype(BF16)
    r_lo = (rw - r_hi.astype(F32)).astype(BF16)
    nt_dims = (((1,), (1,)), ((), ()))
    a = lax.dot_general(jnp.concatenate([r_hi, r_lo], axis=0), h_hi, nt_dims, preferred_element_type=F32)
    b = lax.dot_general(r_hi, h_lo, nt_dims, preferred_element_type=F32)
    logits = a[0:N_EXPERTS] + a[N_EXPERTS:2 * N_EXPERTS] + b
    e = jnp.exp(logits - jnp.max(logits, axis=0, keepdims=True))
    aff_ref[...] = e / jnp.sum(e, axis=0, keepdims=True)


def _merge(x, mod, n1, n2, wcg, cw, wco, attn, wao, ssm, wglu, wout, rwt, layer, tm):
    bsz, t, _ = x.shape
    nt = t // tm
    per = tm // SUBLANES
    tok = lambda wd: pl.BlockSpec((None, tm, wd), lambda b, i, *_: (b, i, 0))
    prev = pl.BlockSpec((None, SUBLANES, D_MODEL), lambda b, i, *_: (b, jnp.maximum(i * per - 1, 0), 0))
    nxt = pl.BlockSpec((None, SUBLANES, D_MODEL),
                       lambda b, i, *_: (b, jnp.minimum((i + 1) * per, t // SUBLANES - 1), 0))
    lay = _layer_spec
    grid_spec = pltpu.PrefetchScalarGridSpec(
        num_scalar_prefetch=1, grid=(bsz, nt),
        in_specs=[tok(D_MODEL), prev, nxt, _mod_spec(mod), lay(n1), lay(n2), lay(wcg), lay(cw), lay(wco), tok(Q_W),
                  lay(wao), tok(SSM_W), lay(wglu), lay(wout), lay(rwt)],
        out_specs=[tok(D_MODEL), tok(D_MODEL), pl.BlockSpec((None, N_EXPERTS, tm), lambda b, i, *_: (b, 0, i))])
    return pl.pallas_call(
        _merge_kernel, grid_spec=grid_spec,
        out_shape=[jax.ShapeDtypeStruct((bsz, t, D_MODEL), F32), jax.ShapeDtypeStruct((bsz, t, D_MODEL), BF16),
                   jax.ShapeDtypeStruct((bsz, N_EXPERTS, t), F32)],
        compiler_params=_params(("parallel", "parallel"), 56),
        name="merge",
    )(_layer_index(layer), x, x, x, mod, n1, n2, wcg, cw, wco, attn, wao, ssm, wglu, wout, rwt)


def _route_kernel(aff_ref, slot_ref, cnt_ref, *, cap, slot_stride):
    n = aff_ref.shape[1]
    bits = pltpu.bitcast(aff_ref[...], jnp.int32)
    thr = jnp.zeros((N_EXPERTS, 1), jnp.int32)
    for b in range(30, -1, -1):
        cand = thr | (1 << b)
        cnt = jnp.sum(jnp.where(bits >= cand, 1.0, 0.0), axis=1, keepdims=True)
        thr = jnp.where(cnt >= cap, cand, thr)
    need = cap - jnp.sum(jnp.where(bits > thr, 1.0, 0.0), axis=1, keepdims=True)
    ri = lax.broadcasted_iota(jnp.int32, (LANES, LANES), 0)
    ci = lax.broadcasted_iota(jnp.int32, (LANES, LANES), 1)
    tri = jnp.where(ri <= ci, 1.0, 0.0).astype(BF16)
    lane = lax.broadcasted_iota(jnp.int32, (N_EXPERTS, LANES), 1)
    base = pl.program_id(0) * slot_stride
    off_eq = jnp.zeros((N_EXPERTS, 1), F32)
    off = jnp.zeros((N_EXPERTS, 1), F32)
    cnts = jnp.zeros((N_EXPERTS, LANES), jnp.int32)
    for j in range(n // LANES):
        sl = slice(j * LANES, (j + 1) * LANES)
        bj = bits[:, sl]
        eqf = jnp.where(bj == thr, 1.0, 0.0)
        ceq = jnp.dot(eqf.astype(BF16), tri, preferred_element_type=F32) + off_eq
        off_eq = ceq[:, LANES - 1:LANES]
        self_ = jnp.where(bj > thr, 1.0, jnp.where(ceq <= need, eqf, 0.0))
        csel = jnp.dot(self_.astype(BF16), tri, preferred_element_type=F32) + off
        cnts = jnp.where(lane == j, off.astype(jnp.int32) + base, cnts)
        off = csel[:, LANES - 1:LANES]
        slot_ref[:, sl] = jnp.where(self_ > 0.0, csel.astype(jnp.int32) - 1 + base, -1)
    cnt_ref[...] = jnp.where(lane >= n // LANES, off.astype(jnp.int32) + base, cnts)


def _route(aff, cap, slot_stride):
    sets, _, n = aff.shape
    return pl.pallas_call(
        functools.partial(_route_kernel, cap=cap, slot_stride=slot_stride), grid=(sets,),
        in_specs=[pl.BlockSpec((None, N_EXPERTS, n), lambda s: (s, 0, 0))],
        out_specs=[pl.BlockSpec((None, N_EXPERTS, n), lambda s: (s, 0, 0)),
                   pl.BlockSpec((None, N_EXPERTS, LANES), lambda s: (s, 0, 0))],
        out_shape=[jax.ShapeDtypeStruct((sets, N_EXPERTS, n), jnp.int32),
                   jax.ShapeDtypeStruct((sets, N_EXPERTS, LANES), jnp.int32)],
        compiler_params=_params(("parallel",), 32),
        name="route",
    )(aff)


def _expert_kernel(cnt_ref, l_ref, h2_ref, slot_ref, gate_ref, wg_ref, wu_ref, wd_ref, o_ref, xs_ref, ys_ref, gs_ref,
                   *, nslot):
    del l_ref
    s = pl.program_id(0)
    e = pl.program_id(1)
    ntb = SUPER // BLOCK
    nsb = nslot // BLOCK
    per = GATHER_TOKENS // BLOCK
    cbase = (s * N_EXPERTS + e) * (ntb + 1)

    @pl.when(e == 0)
    def _():
        o_ref[...] = jnp.zeros_like(o_ref)

    xs_ref[...] = jnp.zeros_like(xs_ref)
    gs_ref[...] = jnp.zeros_like(gs_ref)
    ys_ref[nslot:nslot + BLOCK, :] = jnp.zeros((BLOCK, D_MODEL), BF16)

    srow = lax.broadcasted_iota(jnp.int32, (BLOCK, GATHER_TOKENS), 0)
    for g in range(SUPER // GATHER_TOKENS):
        slots = slot_ref[g:g + 1, :]
        gates = gate_ref[g:g + 1, :]
        for j in range(nsb):
            @pl.when((cnt_ref[cbase + g * per] < (j + 1) * BLOCK) & (cnt_ref[cbase + (g + 1) * per] > j * BLOCK))
            def _():
                pick = slots == srow + j * BLOCK
                x = jnp.dot(jnp.where(pick, 1.0, 0.0).astype(BF16), h2_ref[g * GATHER_TOKENS:(g + 1) * GATHER_TOKENS, :],
                            preferred_element_type=F32)
                xs_ref[j * BLOCK:(j + 1) * BLOCK, :] += x.astype(BF16)
                gs_ref[j * BLOCK:(j + 1) * BLOCK, :] += jnp.sum(jnp.where(pick, gates, 0.0), axis=1, keepdims=True)

    xs = xs_ref[...]
    hg = jnp.dot(xs, wg_ref[...], preferred_element_type=F32)
    hu = jnp.dot(xs, wu_ref[...], preferred_element_type=F32)
    act = (hg * _sigmoid(hg) * hu).astype(BF16)
    y = jnp.dot(act, wd_ref[...], preferred_element_type=F32) * gs_ref[...]
    ys_ref[0:nslot, :] = y.astype(BF16)

    wrow = lax.broadcasted_iota(jnp.int32, (2 * BLOCK, BLOCK), 0)
    for i in range(ntb):
        w0 = pl.multiple_of(jnp.minimum(cnt_ref[cbase + i] // BLOCK, nsb - 1) * BLOCK, BLOCK)
        slots = slot_ref[i // per:i // per + 1, (i % per) * BLOCK:(i % per + 1) * BLOCK]
        pick = jnp.where(slots - w0 == wrow, 1.0, 0.0).astype(BF16)
        o_ref[i * BLOCK:(i + 1) * BLOCK, :] += lax.dot_general(
            pick, ys_ref[pl.ds(w0, 2 * BLOCK), :], (((0,), (0,)), ((), ())), preferred_element_type=F32)


def _experts(cnt, h2, slot, gate, wg, wu, wd, layer):
    nsup = h2.shape[0]
    nslot = CAPACITY * SUPER // N_EXPERTS
    ngt = SUPER // GATHER_TOKENS
    grid_spec = pltpu.PrefetchScalarGridSpec(
        num_scalar_prefetch=2, grid=(nsup, N_EXPERTS),
        in_specs=[pl.BlockSpec((None, SUPER, D_MODEL), lambda s, e, c, l: (s, 0, 0), pipeline_mode=pl.Buffered(1)),
                  pl.BlockSpec((None, None, ngt, GATHER_TOKENS), lambda s, e, c, l: (s, e, 0, 0)),
                  pl.BlockSpec((None, None, ngt, GATHER_TOKENS), lambda s, e, c, l: (s, e, 0, 0)),
                  pl.BlockSpec((None, None, D_MODEL, EXPERT_FF), lambda s, e, c, l: (l[0], e, 0, 0)),
                  pl.BlockSpec((None, None, D_MODEL, EXPERT_FF), lambda s, e, c, l: (l[0], e, 0, 0)),
                  pl.BlockSpec((None, None, EXPERT_FF, D_MODEL), lambda s, e, c, l: (l[0], e, 0, 0))],
        out_specs=pl.BlockSpec((None, SUPER, D_MODEL), lambda s, e, c, l: (s, 0, 0), pipeline_mode=pl.Buffered(1)),
        scratch_shapes=[pltpu.VMEM((nslot, D_MODEL), BF16), pltpu.VMEM((nslot + BLOCK, D_MODEL), BF16),
                        pltpu.VMEM((nslot, 1), F32)])
    return pl.pallas_call(
        functools.partial(_expert_kernel, nslot=nslot), grid_spec=grid_spec,
        out_shape=jax.ShapeDtypeStruct((nsup, SUPER, D_MODEL), F32),
        compiler_params=_params(("arbitrary", "arbitrary"), 56),
        name="experts",
    )(cnt, _layer_index(layer), h2, slot, gate, wg, wu, wd)


def _final_kernel(x_ref, moe_ref, mod_ref, g_ref, o_ref):
    x = x_ref[...] + mod_ref[5:6, :] * moe_ref[...]
    o_ref[...] = x * lax.rsqrt(jnp.mean(x * x, axis=-1, keepdims=True) + EPS) * g_ref[...]


def _final(x1, moe, mod, g, tm):
    bsz, t, _ = x1.shape
    tok = pl.BlockSpec((None, tm, D_MODEL), lambda b, i: (b, i, 0))
    return pl.pallas_call(
        _final_kernel, grid=(bsz, t // tm),
        in_specs=[tok, tok, _mod_spec(mod), _const_spec((1, D_MODEL))],
        out_specs=tok, out_shape=jax.ShapeDtypeStruct((bsz, t, D_MODEL), F32),
        compiler_params=_params(("parallel", "parallel"), 32),
        name="final_norm",
    )(x1, moe, mod, g)


def _head_perm():
    cols = []
    for m in range(GROUP):
        for kv in range(N_KV):
            head = kv * GROUP + m
            cols.extend(range(head * HEAD_DIM, (head + 1) * HEAD_DIM))
    return jnp.array(cols, jnp.int32)


def _rope_swap(width):
    idx = jnp.arange(width)
    nf = HEAD_DIM // 4
    return jnp.where((idx % (2 * nf)) < nf, idx + nf, idx - nf)


def _rope_tables(t):
    pos = jnp.arange(t)
    row = (pos // GRID_W).astype(F32)
    col = (pos % GRID_W).astype(F32)
    nf = HEAD_DIM // 4
    inv = ROPE_BASE ** (-jnp.arange(nf, dtype=F32) / nf)

    def tabs(p):
        ang = p[:, None] * inv[None, :]
        cos, sin = jnp.cos(ang), jnp.sin(ang)
        return jnp.concatenate([cos, cos], axis=1), jnp.concatenate([-sin, sin], axis=1)

    cr, sr = tabs(row)
    cc, sc = tabs(col)
    cos = jnp.concatenate([cr, cc], axis=1)
    sin = jnp.concatenate([sr, sc], axis=1)
    return jnp.tile(cos, (1, LANES // HEAD_DIM)), jnp.tile(sin, (1, LANES // HEAD_DIM))


def _route_and_experts(h2c, affc, h2l, affl, wg, wu, wd, layer):
    bc, tc, _ = h2c.shape
    bl, tl, _ = h2l.shape
    ntb = SUPER // BLOCK
    capc = CAPACITY * tc // N_EXPERTS
    capl = CAPACITY * tl // N_EXPERTS
    slot_c, cnt_c = _route(affc, capc, capc)
    slot_l, cnt_l = _route(affl, capl, 0)
    tbc = tc // BLOCK
    rows = (SUPER // GATHER_TOKENS, GATHER_TOKENS)
    slot_c = slot_c.transpose(1, 0, 2).reshape(1, N_EXPERTS, *rows)
    gate_c = affc.transpose(1, 0, 2).reshape(1, N_EXPERTS, *rows)
    cnt_c = jnp.concatenate([cnt_c[:, :, :tbc].transpose(1, 0, 2).reshape(N_EXPERTS, ntb),
                             cnt_c[bc - 1, :, tbc:tbc + 1]], axis=1)
    moe_c = _experts(cnt_c.reshape(-1), h2c.reshape(1, SUPER, D_MODEL), slot_c, gate_c, wg, wu, wd, layer)
    moe_l = _experts(cnt_l[:, :, :ntb + 1].reshape(-1), h2l, slot_l.reshape(bl, N_EXPERTS, *rows),
                     affl.reshape(bl, N_EXPERTS, *rows), wg, wu, wd, layer)
    return moe_c.reshape(bc, tc, D_MODEL), moe_l


def kernel(x_prompt, x_sample, cache_k, cache_v, state_ssm_re, state_ssm_im, c, c_ctx, ada_w, ada_b, norm1, norm2,
           final_norm, w_in, conv_w, w_conv_out, attn_sink, w_attn_out, ssm_a_re, ssm_a_im, ssm_log_dt, ssm_b_re,
           ssm_b_im, ssm_c_re, ssm_c_im, ssm_d, w_glu, w_out, router_w, w_gate, w_up, w_down):
    bc, tc, _ = x_prompt.shape
    bl, tl, _ = x_sample.shape
    assert bc * tc == SUPER and tl == SUPER and bl + 1 <= SUBLANES
    hp = _head_perm()
    cvecs = jnp.zeros((SUBLANES, D_MODEL), F32).at[:bl].set(c).at[bl].set(c_ctx)
    mods = _modulation(cvecs, ada_w, ada_b).reshape(DEPTH, SUBLANES, 6, D_MODEL)

    o_q, o_k, o_v, o_u, o_g = 3 * CONV_W, 3 * CONV_W + Q_W, 3 * CONV_W + Q_W + KV_W, 3 * CONV_W + Q_W + 2 * KV_W, \
        3 * CONV_W + Q_W + 2 * KV_W + SSM_W
    w_b = w_in.astype(BF16)
    wq = w_b[:, :, o_q:o_k][:, :, hp]
    wk = w_b[:, :, o_k:o_v]
    w_qkvu = jnp.concatenate([wq, w_b[:, :, o_k:o_g], wq[:, :, _rope_swap(Q_W)], wk[:, :, _rope_swap(KV_W)]], axis=-1)
    w_co = w_conv_out.astype(BF16)
    w_ao = w_attn_out[:, hp, :].astype(BF16)
    w_gl = w_glu.astype(BF16)
    w_o = w_out.astype(BF16)
    rwt = router_w.transpose(0, 2, 1)
    wg, wu, wd = w_gate.astype(BF16), w_up.astype(BF16), w_down.astype(BF16)
    n1 = norm1.reshape(DEPTH, 1, D_MODEL)
    n2 = norm2.reshape(DEPTH, 1, D_MODEL)
    sink = attn_sink.reshape(DEPTH, N_KV, GROUP).transpose(0, 2, 1).reshape(DEPTH, N_HEADS)
    rope_tabs = _rope_tables(tl)
    ssm_w = _ssm_weights(ssm_a_re, ssm_a_im, ssm_log_dt, ssm_b_re, ssm_b_im, ssm_c_re, ssm_c_im, ssm_d)
    h0_all = jnp.stack([state_ssm_re[:, :, 0], state_ssm_im[:, :, 0], state_ssm_re[:, :, 1], state_ssm_im[:, :, 1]],
                       axis=0)
    h0_all = h0_all.reshape(4, bl, DEPTH, SSM_PAIRS, 2 * SSM_P).transpose(2, 3, 0, 1, 4)

    xp, xs = x_prompt, x_sample
    moe_c = moe_l = None
    modp_c = modp_l = None
    ks, vs, fins = [], [], []
    for l in range(DEPTH):
        mod_l = mods[l, :bl]
        mod_c = mods[l, bl:bl + 1]
        outs = _qkvu(xp, moe_c, modp_c, mod_c, n1, w_qkvu, l, None, tc, F32)
        q_c, k_c, v_c, u_c = outs[:4]
        if moe_c is not None:
            xp = outs[4]
        ks.append(k_c)
        vs.append(v_c)
        attn_c = _ctx_attention(sink[l], q_c, k_c, v_c)
        ssm_c, fin = _ssm(u_c, ssm_w, l, None)
        fins.append(fin)
        x1c, h2c, affc = _merge(xp, mod_c, n1, n2, w_b, conv_w, w_co, attn_c, w_ao, ssm_c, w_gl, w_o, rwt, l, tc)
        outs = _qkvu(xs, moe_l, modp_l, mod_l, n1, w_qkvu, l, rope_tabs, 512, BF16)
        q_l, k_l, v_l, u_l = outs[:4]
        if moe_l is not None:
            xs = outs[4]
        attn_l = _lat_attention(sink[l], q_l, k_l, v_l, cache_k[:, l].reshape(bl, -1, KV_W),
                                cache_v[:, l].reshape(bl, -1, KV_W))
        ssm_l, _ = _ssm(u_l, ssm_w, l, h0_all[l])
        x1l, h2l, affl = _merge(xs, mod_l, n1, n2, w_b, conv_w, w_co, attn_l, w_ao, ssm_l, w_gl, w_o, rwt, l, 512)
        moe_c, moe_l = _route_and_experts(h2c, affc, h2l, affl, wg, wu, wd, l)
        xp, xs = x1c, x1l
        modp_c, modp_l = mod_c, mod_l

    fn = final_norm.reshape(1, D_MODEL)
    y_prompt = _final(xp, moe_c, modp_c, fn, tc)
    y_sample = _final(xs, moe_l, modp_l, fn, 512)
    new_k = jnp.stack(ks, axis=1).reshape(bc, DEPTH, tc, N_KV, HEAD_DIM)
    new_v = jnp.stack(vs, axis=1).reshape(bc, DEPTH, tc, N_KV, HEAD_DIM)
    fin = jnp.stack(fins, axis=0)
    fin = fin.reshape(DEPTH, SSM_PAIRS, 2, 2, bc, 2, SSM_P).transpose(3, 4, 0, 2, 1, 5, 6)
    fin = fin.reshape(2, bc, DEPTH, 2, SSM_G, SSM_P)
    return (y_prompt, y_sample, new_k, new_v, fin[0], fin[1])
```

```python
import functools
import math

import jax
import jax.numpy as jnp
from jax import lax
from jax.experimental import pallas as pl
from jax.experimental.pallas import tpu as pltpu

D_MODEL = 1024
DEPTH = 4
GRID_W = 64
CONV_W = 512
N_HEADS = 8
N_KV = 2
HEAD_DIM = 64
GROUP = N_HEADS // N_KV
Q_W = N_HEADS * HEAD_DIM
KV_W = N_KV * HEAD_DIM
BLOCK = 128
ROPE_BASE = 10000.0
NEG_INF = -1e30
SSM_W = 512
SSM_GC = 16
SSM_G = SSM_W // SSM_GC
SSM_P = 64
N_BRANCH = 3
N_EXPERTS = 16
EXPERT_FF = 1024
CAPACITY = 2
EPS = 1e-6

F32 = jnp.float32
BF16 = jnp.bfloat16
HIGHEST = lax.Precision.HIGHEST

LANES = 128
SUBLANES = 8
SSM_CHUNK = 16
SSM_PAIRS = SSM_G // 2
SUPER = 4096
GATHER_TOKENS = 512
MIB = 1024 * 1024


def _params(sem, vmem_mib):
    return pltpu.CompilerParams(dimension_semantics=sem, vmem_limit_bytes=vmem_mib * MIB)


def _const_spec(shape):
    nd = len(shape)
    return pl.BlockSpec(shape, lambda *_: (0,) * nd, pipeline_mode=pl.Buffered(1))


def _layer_spec(arr):
    nd = arr.ndim
    return pl.BlockSpec((None,) + arr.shape[1:], lambda *idx: (idx[-1][0],) + (0,) * (nd - 1),
                        pipeline_mode=pl.Buffered(1))


def _layer_index(layer):
    return jnp.full((1,), layer, jnp.int32)


def _sigmoid(x):
    return 1.0 / (1.0 + jnp.exp(-x))


def _rms_mod(x, g, sc, sh):
    y = x * lax.rsqrt(jnp.mean(x * x, axis=-1, keepdims=True) + EPS)
    return (y * g) * (1.0 + sc) + sh


def _mod_kernel(c_ref, w_ref, b_ref, o_ref):
    cv = c_ref[...]
    s = cv * _sigmoid(cv)
    o_ref[...] = jnp.dot(s, w_ref[...], precision=HIGHEST, preferred_element_type=F32) + b_ref[...]


def _modulation(cvecs, ada_w, ada_b):
    nt = 1536
    return pl.pallas_call(
        _mod_kernel,
        grid=(DEPTH, 6 * D_MODEL // nt),
        in_specs=[pl.BlockSpec((SUBLANES, D_MODEL), lambda l, j: (0, 0)),
                  pl.BlockSpec((None, D_MODEL, nt), lambda l, j: (l, 0, j)),
                  pl.BlockSpec((None, 1, nt), lambda l, j: (l, 0, j))],
        out_specs=pl.BlockSpec((None, SUBLANES, nt), lambda l, j: (l, 0, j)),
        out_shape=jax.ShapeDtypeStruct((DEPTH, SUBLANES, 6 * D_MODEL), F32),
        compiler_params=_params(("parallel", "parallel"), 32),
        name="modulation",
    )(cvecs, ada_w, ada_b.reshape(DEPTH, 1, 6 * D_MODEL))


def _qkvu_kernel(*refs, rope, fuse_res):
    it = iter(refs[1:])
    x_ref = next(it)
    if fuse_res:
        moe_ref = next(it)
        modp_ref = next(it)
    mod_ref = next(it)
    n1_ref = next(it)
    w_ref = next(it)
    if rope:
        cos_ref = next(it)
        sin_ref = next(it)
    q_ref, k_ref, v_ref, u_ref = next(it), next(it), next(it), next(it)
    x = x_ref[...]
    if fuse_res:
        xo_ref = next(it)
        x = x + modp_ref[5:6, :] * moe_ref[...]
        xo_ref[...] = x
    h = _rms_mod(x, n1_ref[...], mod_ref[1:2, :], mod_ref[0:1, :]).astype(BF16)
    ncol = w_ref.shape[1] if rope else Q_W + 2 * KV_W + SSM_W
    p = jnp.dot(h, w_ref[:, 0:ncol], preferred_element_type=F32)
    q = p[:, 0:Q_W]
    k = p[:, Q_W:Q_W + KV_W]
    if rope:
        cos = cos_ref[...]
        sin = sin_ref[...]
        o = Q_W + 2 * KV_W + SSM_W
        q = jnp.concatenate(
            [q[:, m * LANES:(m + 1) * LANES] * cos + p[:, o + m * LANES:o + (m + 1) * LANES] * sin
             for m in range(Q_W // LANES)], axis=1)
        k = k * cos + p[:, o + Q_W:o + Q_W + KV_W] * sin
    q_ref[...] = (q * HEAD_DIM ** -0.5).astype(q_ref.dtype)
    k_ref[...] = k.astype(k_ref.dtype)
    v_ref[...] = p[:, Q_W + KV_W:Q_W + 2 * KV_W].astype(v_ref.dtype)
    u_ref[...] = p[:, Q_W + 2 * KV_W:Q_W + 2 * KV_W + SSM_W]


def _mod_spec(mod):
    if mod.shape[0] == 1:
        return pl.BlockSpec((None, 6, D_MODEL), lambda b, i, *_: (0, 0, 0))
    return pl.BlockSpec((None, 6, D_MODEL), lambda b, i, *_: (b, 0, 0))


def _qkvu(x, moe, modp, mod, n1, w, layer, rope_tabs, tm, kv_dtype):
    bsz, t, _ = x.shape
    rope = rope_tabs is not None
    fuse_res = moe is not None
    tok = lambda wd: pl.BlockSpec((None, tm, wd), lambda b, i, *_: (b, i, 0))
    in_specs, args = [tok(D_MODEL)], [x]
    if fuse_res:
        in_specs += [tok(D_MODEL), _mod_spec(modp)]
        args += [moe, modp]
    in_specs += [_mod_spec(mod), _layer_spec(n1), _layer_spec(w)]
    args += [mod, n1, w]
    if rope:
        in_specs += [pl.BlockSpec((tm, LANES), lambda b, i, *_: (i, 0))] * 2
        args += list(rope_tabs)
    out_specs = [tok(Q_W), tok(KV_W), tok(KV_W), tok(SSM_W)]
    out_shape = [jax.ShapeDtypeStruct((bsz, t, Q_W), BF16),
                 jax.ShapeDtypeStruct((bsz, t, KV_W), kv_dtype),
                 jax.ShapeDtypeStruct((bsz, t, KV_W), kv_dtype),
                 jax.ShapeDtypeStruct((bsz, t, SSM_W), F32)]
    if fuse_res:
        out_specs.append(tok(D_MODEL))
        out_shape.append(jax.ShapeDtypeStruct((bsz, t, D_MODEL), F32))
    grid_spec = pltpu.PrefetchScalarGridSpec(num_scalar_prefetch=1, grid=(bsz, t // tm), in_specs=in_specs,
                                             out_specs=out_specs)
    return pl.pallas_call(
        functools.partial(_qkvu_kernel, rope=rope, fuse_res=fuse_res), grid_spec=grid_spec, out_shape=out_shape,
        compiler_params=_params(("parallel", "parallel"), 48),
        name="qkvu",
    )(_layer_index(layer), *args)


def _stack_heads(q_ref):
    lo = lax.broadcasted_iota(jnp.int32, (1, LANES), 1) < HEAD_DIM
    keep_lo = jnp.where(lo, 1.0, 0.0).astype(BF16)
    keep_hi = jnp.where(lo, 0.0, 1.0).astype(BF16)
    parts = []
    for m in range(GROUP):
        qm = q_ref[:, m * LANES:(m + 1) * LANES]
        parts.append(qm * keep_lo)
        parts.append(qm * keep_hi)
    return jnp.concatenate(parts, axis=0), lo


def _attend(s, bias, sink_ref, v, tq, lo, o_ref):
    ps, dens = [], []
    for h in range(N_HEADS):
        sh = s[h * tq:(h + 1) * tq]
        if bias is not None:
            sh = sh + bias
        sink = sink_ref[h]
        mx = jnp.maximum(jnp.max(sh, axis=-1, keepdims=True), sink)
        p = jnp.exp(sh - mx)
        dens.append(jnp.sum(p, axis=-1, keepdims=True) + jnp.exp(sink - mx))
        ps.append(p.astype(BF16))
    o = jnp.dot(jnp.concatenate(ps, axis=0), v, preferred_element_type=F32)
    for m in range(GROUP):
        o0 = o[(2 * m) * tq:(2 * m + 1) * tq] / dens[2 * m]
        o1 = o[(2 * m + 1) * tq:(2 * m + 2) * tq] / dens[2 * m + 1]
        o_ref[:, m * LANES:(m + 1) * LANES] = jnp.where(lo, o0, o1).astype(o_ref.dtype)


def _ctx_attn_kernel(sink_ref, q_ref, k_ref, v_ref, o_ref):
    tq = q_ref.shape[0]
    qx, lo = _stack_heads(q_ref)
    s = lax.dot_general(qx, k_ref[...].astype(BF16), (((1,), (1,)), ((), ())), preferred_element_type=F32)
    _attend(s, None, sink_ref, v_ref[...].astype(BF16), tq, lo, o_ref)


def _ctx_attention(sink, q, k, v):
    bsz, t, _ = q.shape
    tok = lambda wd: pl.BlockSpec((None, t, wd), lambda b: (b, 0, 0))
    return pl.pallas_call(
        _ctx_attn_kernel, grid=(bsz,),
        in_specs=[pl.BlockSpec(memory_space=pltpu.SMEM), tok(Q_W), tok(KV_W), tok(KV_W)],
        out_specs=tok(Q_W), out_shape=jax.ShapeDtypeStruct((bsz, t, Q_W), BF16),
        compiler_params=_params(("parallel",), 32),
        name="ctx_attention",
    )(sink, q, k, v)


def _lat_attn_kernel(sink_ref, q_ref, kc_ref, vc_ref, kp_ref, k0_ref, kn_ref, vp_ref, v0_ref, vn_ref, o_ref):
    tq = q_ref.shape[0]
    i = pl.program_id(1)
    nb = pl.num_programs(1)
    qx, lo = _stack_heads(q_ref)
    kall = jnp.concatenate([kc_ref[...].astype(BF16), kp_ref[...], k0_ref[...], kn_ref[...]], axis=0)
    vall = jnp.concatenate([vc_ref[...].astype(BF16), vp_ref[...], v0_ref[...], vn_ref[...]], axis=0)
    s = lax.dot_general(qx, kall, (((1,), (1,)), ((), ())), preferred_element_type=F32)
    past = kc_ref.shape[0]
    r = lax.broadcasted_iota(jnp.int32, (tq, BLOCK), 0)
    c = lax.broadcasted_iota(jnp.int32, (tq, BLOCK), 1)
    m_prev = jnp.where((c >= r) & (i > 0), 0.0, NEG_INF)
    m_next = jnp.where((c <= r) & (i < nb - 1), 0.0, NEG_INF)
    bias = jnp.concatenate([jnp.zeros((tq, past), F32), m_prev, jnp.zeros((tq, BLOCK), F32), m_next], axis=1)
    _attend(s, bias, sink_ref, vall, tq, lo, o_ref)


def _lat_attention(sink, q, k, v, kc, vc):
    bsz, t, _ = q.shape
    nb = t // BLOCK
    past = kc.shape[1]
    tok = lambda wd: pl.BlockSpec((None, BLOCK, wd), lambda b, i: (b, i, 0))
    prev = pl.BlockSpec((None, BLOCK, KV_W), lambda b, i: (b, jnp.maximum(i - 1, 0), 0))
    nxt = pl.BlockSpec((None, BLOCK, KV_W), lambda b, i: (b, jnp.minimum(i + 1, nb - 1), 0))
    ctx = pl.BlockSpec((None, past, KV_W), lambda b, i: (b, 0, 0))
    return pl.pallas_call(
        _lat_attn_kernel, grid=(bsz, nb),
        in_specs=[pl.BlockSpec(memory_space=pltpu.SMEM), tok(Q_W), ctx, ctx,
                  prev, tok(KV_W), nxt, prev, tok(KV_W), nxt],
        out_specs=tok(Q_W), out_shape=jax.ShapeDtypeStruct((bsz, t, Q_W), BF16),
        compiler_params=_params(("parallel", "parallel"), 32),
        name="lat_attention",
    )(sink, q, kc, vc, k, k, k, v, v, v)


SSM_QUAD = LANES // (2 * SSM_GC)
GRANULES = LANES // SSM_GC


def _ssm_kernel(l_ref, u_ref, strip_ref, wst_ref, w2_ref, a_ref, h0_ref, d_ref, y_ref, fin_ref, pk_ref, z_ref,
                ent_ref, toep_ref, *, bsz, t):
    del l_ref
    nk = t // SSM_CHUNK
    gran = lax.broadcasted_iota(jnp.int32, (1, LANES), 1) // SSM_GC
    half = SSM_CHUNK // GRANULES

    ck = min(nk, 4 * SUBLANES)

    def transpose_granules(xs):
        xs = list(xs)
        for d in (4, 2, 1):
            low = (gran & d) == 0
            for i in range(GRANULES):
                if not i & d:
                    a, b = xs[i], xs[i + d]
                    xs[i] = jnp.where(low, a, pltpu.roll(b, SSM_GC * d, axis=1))
                    xs[i + d] = jnp.where(low, pltpu.roll(a, LANES - SSM_GC * d, axis=1), b)
        return xs

    def pack(it, carry):
        b, k0 = it // (nk // ck), (it % (nk // ck)) * ck
        for sh in range(half):
            xs = [u_ref[pl.ds(b * t + k0 * SSM_CHUNK + sh * GRANULES + s8, ck, stride=SSM_CHUNK), :]
                  for s8 in range(GRANULES)]
            for g8, x in enumerate(transpose_granules(xs)):
                pk_ref[g8 // 2, (g8 % 2) * half + sh, pl.ds(k0 * bsz + b, ck, stride=bsz), :] = x
        return carry

    lax.fori_loop(0, bsz * (nk // ck), pack, 0)

    rows = max(bsz, SUBLANES)
    steps = rows // bsz
    niter = nk // steps
    ns = SSM_W

    def step(sr, si, dr, di, ar, ai):
        return ar * sr - ai * si + dr, ar * si + ai * sr + di

    for p in range(SSM_QUAD):
        u = jnp.concatenate([pk_ref[p, col] for col in range(2 * half)], axis=1)
        ub = u.astype(BF16)
        gw = SSM_CHUNK * SSM_GC
        for gl in range(2):
            strip = strip_ref[p, gl]
            for s in range(SSM_CHUNK):
                off = (SSM_CHUNK - 1 - s) * SSM_GC
                blk = pltpu.roll(strip, 2 * gw - off, axis=1) if off else strip
                toep_ref[s * SSM_GC:(s + 1) * SSM_GC, :] = blk[:, 0:gw].astype(BF16)
            z_ref[:, gl * gw:(gl + 1) * gw] = jnp.dot(ub[:, gl * gw:(gl + 1) * gw], toep_ref[...],
                                                      preferred_element_type=F32)
        z_ref[:, ns:ns + 4 * LANES] = jnp.dot(ub, wst_ref[p], preferred_element_type=F32)
        afr, afi, abr, abi = a_ref[p, 0:1, :], a_ref[p, 1:2, :], a_ref[p, 2:3, :], a_ref[p, 3:4, :]

        def body(it, carry):
            sr, si, gr, gi = carry
            rf = pl.multiple_of(it * rows, SUBLANES)
            rb = pl.multiple_of((niter - 1 - it) * rows, SUBLANES)
            dfr, dfi = z_ref[pl.ds(rf, rows), ns:ns + LANES], z_ref[pl.ds(rf, rows), ns + LANES:ns + 2 * LANES]
            dbr = z_ref[pl.ds(rb, rows), ns + 2 * LANES:ns + 3 * LANES]
            dbi = z_ref[pl.ds(rb, rows), ns + 3 * LANES:ns + 4 * LANES]
            efr, efi, ebr, ebi = [], [], [None] * steps, [None] * steps
            for j in range(steps):
                efr.append(sr)
                efi.append(si)
                sr, si = step(sr, si, dfr[j * bsz:(j + 1) * bsz], dfi[j * bsz:(j + 1) * bsz], afr, afi)
            for j in reversed(range(steps)):
                ebr[j] = gr
                ebi[j] = gi
                gr, gi = step(gr, gi, dbr[j * bsz:(j + 1) * bsz], dbi[j * bsz:(j + 1) * bsz], abr, abi)
            cat = lambda xs: xs[0] if len(xs) == 1 else jnp.concatenate(xs, axis=0)
            ent_ref[pl.ds(rf, rows), 0:LANES] = cat(efr)
            ent_ref[pl.ds(rf, rows), LANES:2 * LANES] = cat(efi)
            ent_ref[pl.ds(rb, rows), 2 * LANES:3 * LANES] = cat(ebr)
            ent_ref[pl.ds(rb, rows), 3 * LANES:4 * LANES] = cat(ebi)
            return sr, si, gr, gi

        init = (h0_ref[p, 0], h0_ref[p, 1], h0_ref[p, 2], h0_ref[p, 3])
        sr, si, gr, gi = lax.fori_loop(0, niter, body, init)
        fin_ref[p, 0] = sr
        fin_ref[p, 1] = si
        fin_ref[p, 2] = gr
        fin_ref[p, 3] = gi
        y = (z_ref[:, 0:SSM_W] + jnp.dot(ent_ref[...].astype(BF16), w2_ref[p], preferred_element_type=F32)
             + d_ref[p] * u)
        for col in range(2 * half):
            pk_ref[p, col] = y[:, col * LANES:(col + 1) * LANES]

    def unpack(it, carry):
        b, k0 = it // (nk // ck), (it % (nk // ck)) * ck
        for sh in range(half):
            xs = [pk_ref[g8 // 2, (g8 % 2) * half + sh, pl.ds(k0 * bsz + b, ck, stride=bsz), :] for g8 in range(GRANULES)]
            for s8, x in enumerate(transpose_granules(xs)):
                y_ref[pl.ds(b * t + k0 * SSM_CHUNK + sh * GRANULES + s8, ck, stride=SSM_CHUNK), :] = x
        return carry

    lax.fori_loop(0, bsz * (nk // ck), unpack, 0)


def _ssm(u, ssm_w, layer, h0):
    bsz, t, _ = u.shape
    strip, wst, w2, a16, dflat = ssm_w
    rows = (t // SSM_CHUNK) * bsz
    if h0 is None:
        h0 = jnp.zeros((SSM_PAIRS, 4, bsz, LANES), F32)
    gw = SSM_CHUNK * SSM_GC
    quad = lambda *tail: pl.BlockSpec((SSM_QUAD,) + tail, lambda q, lref: (q,) + (0,) * len(tail))
    lquad = lambda *tail: pl.BlockSpec((None, SSM_QUAD) + tail, lambda q, lref: (lref[0], q) + (0,) * len(tail))
    tok = pl.BlockSpec((bsz * t, LANES), lambda q, lref: (0, q), pipeline_mode=pl.Buffered(1))
    grid_spec = pltpu.PrefetchScalarGridSpec(
        num_scalar_prefetch=1, grid=(SSM_PAIRS // SSM_QUAD,),
        in_specs=[tok, lquad(2, SSM_GC, 2 * gw), lquad(SSM_W, 4 * LANES), lquad(4 * LANES, SSM_W), lquad(4, LANES),
                  quad(4, bsz, LANES), lquad(1, SSM_W)],
        out_specs=[tok, quad(4, bsz, LANES)],
        scratch_shapes=[pltpu.VMEM((SSM_QUAD, SSM_W // LANES, rows, LANES), F32),
                        pltpu.VMEM((rows, SSM_W + 4 * LANES), F32), pltpu.VMEM((rows, 4 * LANES), F32),
                        pltpu.VMEM((gw, gw), BF16)])
    y, fin = pl.pallas_call(
        functools.partial(_ssm_kernel, bsz=bsz, t=t), grid_spec=grid_spec,
        out_shape=[jax.ShapeDtypeStruct((bsz * t, SSM_W), F32),
                   jax.ShapeDtypeStruct((SSM_PAIRS, 4, bsz, LANES), F32)],
        compiler_params=_params(("parallel",), 52),
        name="ssm",
    )(_layer_index(layer), u.reshape(bsz * t, SSM_W), strip, wst, w2, a16, h0, dflat)
    return y.reshape(bsz, t, SSM_W), fin


def _ssm_weights(a_re, a_im, log_dt, b_re, b_im, c_re, c_im, ssm_d):
    L = SSM_CHUNK
    nl = a_re.shape[0]
    dt = jnp.exp(log_dt)[..., None, None]
    steps = jnp.arange(L + 1, dtype=F32)
    mag = jnp.exp(a_re[..., None] * dt * steps)
    ang = a_im[..., None] * dt * steps
    pr, pi = mag * jnp.cos(ang), mag * jnp.sin(ang)
    nr, ni = pr[..., 1] - 1.0, pi[..., 1]
    den = a_re * a_re + a_im * a_im
    qr, qi = (nr * a_re + ni * a_im) / den, (ni * a_re - nr * a_im) / den
    swap = lambda x: x.transpose(0, 1, 2, 4, 3)
    bbr = swap(qr[..., None] * b_re - qi[..., None] * b_im)
    bbi = swap(qr[..., None] * b_im + qi[..., None] * b_re)
    ctr, cti = swap(c_re), swap(c_im)
    cpr = ctr[..., None, :] * pr[..., :L, None] - cti[..., None, :] * pi[..., :L, None]
    cpi = ctr[..., None, :] * pi[..., :L, None] + cti[..., None, :] * pr[..., :L, None]
    kern = jnp.sum(bbr[..., None, None] * cpr[:, :, :, None] - bbi[..., None, None] * cpi[:, :, :, None], axis=4)
    kf, kb = kern[:, 0], kern[:, 1]
    lagk = jnp.concatenate([kb[..., :0:-1, :], kf[..., 0:1, :] + kb[..., 0:1, :], kf[..., 1:, :]], axis=-2)
    strip = jnp.pad(lagk.reshape(nl, SSM_G, SSM_GC, (2 * L - 1) * SSM_GC), ((0, 0), (0, 0), (0, 0), (0, SSM_GC)))
    strip = strip.reshape(nl, SSM_PAIRS, 2, SSM_GC, 2 * L * SSM_GC)
    idx = jnp.arange(L)

    def pair(w):
        w = w.reshape((nl, SSM_PAIRS, 2) + w.shape[2:])
        even, odd = w[:, :, 0], w[:, :, 1]
        zero = jnp.zeros_like(even)
        return jnp.stack([jnp.concatenate([even, zero], axis=-1), jnp.concatenate([zero, odd], axis=-1)], axis=2)

    def inject(kpow, d):
        ar = pr[:, d].transpose(0, 1, 3, 2)[:, :, kpow][:, :, :, None, :]
        ai = pi[:, d].transpose(0, 1, 3, 2)[:, :, kpow][:, :, :, None, :]
        br, bi = bbr[:, d][:, :, None], bbi[:, d][:, :, None]
        return [pair(w).reshape(nl, SSM_PAIRS, 2 * L * SSM_GC, 2 * SSM_P) for w in (ar * br - ai * bi, ar * bi + ai * br)]

    wst = jnp.concatenate(inject(L - 1 - idx, 0) + inject(idx, 1), axis=-1)

    def readout(kpow, d):
        ar, ai = pr[:, d][..., kpow][..., None], pi[:, d][..., kpow][..., None]
        cr, ci = ctr[:, d][:, :, :, None, :], cti[:, d][:, :, :, None, :]
        planes = (cr * ar - ci * ai, -(cr * ai + ci * ar))
        return [pair(w.reshape(nl, SSM_G, SSM_P, L * SSM_GC)).reshape(nl, SSM_PAIRS, 2 * SSM_P, 2 * L * SSM_GC)
                for w in planes]

    w2 = jnp.concatenate(readout(1 + idx, 0) + readout(L - idx, 1), axis=2)
    plane = lambda x: x.reshape(nl, SSM_PAIRS, 2 * SSM_P)
    a16 = jnp.stack([plane(pr[:, 0, :, :, L]), plane(pi[:, 0, :, :, L]), plane(pr[:, 1, :, :, L]),
                     plane(pi[:, 1, :, :, L])], axis=2)
    dflat = jnp.broadcast_to(ssm_d.reshape(nl, SSM_PAIRS, 2, 1, SSM_GC), (nl, SSM_PAIRS, 2, L, SSM_GC))
    return strip, wst.astype(BF16), w2.astype(BF16), a16, dflat.reshape(nl, SSM_PAIRS, 1, 2 * L * SSM_GC)


def _merge_kernel(l_ref, x_ref, xp_ref, xn_ref, mod_ref, n1_ref, n2_ref, wcg_ref, cw_ref, wco_ref, attn_ref, wao_ref,
                  ssm_ref, wglu_ref, wout_ref, rw_ref, x1_ref, h2_ref, aff_ref):
    del l_ref
    tm = x_ref.shape[0]
    i = pl.program_id(1)
    nt = pl.num_programs(1)
    x = x_ref[...]
    n1 = n1_ref[...]
    sh1, sc1, g1 = mod_ref[0:1, :], mod_ref[1:2, :], mod_ref[2:3, :]
    sh2, sc2 = mod_ref[3:4, :], mod_ref[4:5, :]
    cw = CONV_W

    def conv_in(h):
        xin = jnp.dot(h, wcg_ref[:, 0:cw], preferred_element_type=F32)
        cg = jnp.dot(h, wcg_ref[:, 2 * cw:3 * cw], preferred_element_type=F32)
        return cg * xin

    x_ext = jnp.concatenate([x, xp_ref[...], xn_ref[...]], axis=0)
    h_ext = _rms_mod(x_ext, n1, sc1, sh1).astype(BF16)
    h = h_ext[0:tm]
    z_ext = conv_in(h_ext)
    z = z_ext[0:tm]
    z_before = jnp.where(i > 0, z_ext[tm + SUBLANES - 1:tm + SUBLANES, :], 0.0)
    z_after = jnp.where(i < nt - 1, z_ext[tm + SUBLANES:tm + SUBLANES + 1, :], 0.0)
    rows = lax.broadcasted_iota(jnp.int32, (tm, 1), 0)
    zl = jnp.where(rows == 0, z_before, pltpu.roll(z, 1, axis=0))
    zr = jnp.where(rows == tm - 1, z_after, pltpu.roll(z, tm - 1, axis=0))
    y = zl * cw_ref[0:1, :] + z * cw_ref[1:2, :] + zr * cw_ref[2:3, :]
    bg = jnp.dot(h, wcg_ref[:, cw:2 * cw], preferred_element_type=F32)
    conv_y = jnp.dot((bg * y).astype(BF16), wco_ref[...], preferred_element_type=F32)
    o = wcg_ref.shape[1] - N_BRANCH * D_MODEL
    ga = _sigmoid(jnp.dot(h, wcg_ref[:, o:o + D_MODEL], preferred_element_type=F32))
    mixed = ga * conv_y
    attn_y = jnp.dot(attn_ref[...], wao_ref[...], preferred_element_type=F32)
    gb = _sigmoid(jnp.dot(h, wcg_ref[:, o + D_MODEL:o + 2 * D_MODEL], preferred_element_type=F32))
    mixed = mixed + gb * attn_y
    s = ssm_ref[...]
    gelu = s * (0.5 * (1.0 + jnp.tanh(math.sqrt(2.0 / math.pi) * (s + 0.044715 * (s * s * s)))))
    zab = jnp.dot(gelu.astype(BF16), wglu_ref[...], preferred_element_type=F32)
    ssm_y = zab[:, 0:D_MODEL] * _sigmoid(zab[:, D_MODEL:2 * D_MODEL])
    gc = _sigmoid(jnp.dot(h, wcg_ref[:, o + 2 * D_MODEL:o + 3 * D_MODEL], preferred_element_type=F32))
    mixed = mixed + gc * ssm_y
    x1 = x + g1 * jnp.dot(mixed.astype(BF16), wout_ref[...], preferred_element_type=F32)
    x1_ref[...] = x1
    h2 = _rms_mod(x1, n2_ref[...], sc2, sh2)
    h2_ref[...] = h2.astype(BF16)
    h_hi = h2.astype(BF16)
    h_lo = (h2 - h_hi.astype(F32)).astype(BF16)
    rw = rw_ref[...]
    r_hi = rw.astype(BF16)
    r_lo = (rw - r_hi.astype(F32)).astype(BF16)
    nt_dims = (((1,), (1,)), ((), ()))
    a = lax.dot_general(jnp.concatenate([r_hi, r_lo], axis=0), h_hi, nt_dims, preferred_element_type=F32)
    b = lax.dot_general(r_hi, h_lo, nt_dims, preferred_element_type=F32)
    logits = a[0:N_EXPERTS] + a[N_EXPERTS:2 * N_EXPERTS] + b
    e = jnp.exp(logits - jnp.max(logits, axis=0, keepdims=True))
    aff_ref[...] = e / jnp.sum(e, axis=0, keepdims=True)


def _merge(x, mod, n1, n2, wcg, cw, wco, attn, wao, ssm, wglu, wout, rwt, layer, tm):
    bsz, t, _ = x.shape
    nt = t // tm
    per = tm // SUBLANES
    tok = lambda wd: pl.BlockSpec((None, tm, wd), lambda b, i, *_: (b, i, 0))
    prev = pl.BlockSpec((None, SUBLANES, D_MODEL), lambda b, i, *_: (b, jnp.maximum(i * per - 1, 0), 0))
    nxt = pl.BlockSpec((None, SUBLANES, D_MODEL),
                       lambda b, i, *_: (b, jnp.minimum((i + 1) * per, t // SUBLANES - 1), 0))
    lay = _layer_spec
    grid_spec = pltpu.PrefetchScalarGridSpec(
        num_scalar_prefetch=1, grid=(bsz, nt),
        in_specs=[tok(D_MODEL), prev, nxt, _mod_spec(mod), lay(n1), lay(n2), lay(wcg), lay(cw), lay(wco), tok(Q_W),
                  lay(wao), tok(SSM_W), lay(wglu), lay(wout), lay(rwt)],
        out_specs=[tok(D_MODEL), tok(D_MODEL), pl.BlockSpec((None, N_EXPERTS, tm), lambda b, i, *_: (b, 0, i))])
    return pl.pallas_call(
        _merge_kernel, grid_spec=grid_spec,
        out_shape=[jax.ShapeDtypeStruct((bsz, t, D_MODEL), F32), jax.ShapeDtypeStruct((bsz, t, D_MODEL), BF16),
                   jax.ShapeDtypeStruct((bsz, N_EXPERTS, t), F32)],
        compiler_params=_params(("parallel", "parallel"), 56),
        name="merge",
    )(_layer_index(layer), x, x, x, mod, n1, n2, wcg, cw, wco, attn, wao, ssm, wglu, wout, rwt)


def _route_kernel(aff_ref, slot_ref, cnt_ref, *, cap, slot_stride):
    n = aff_ref.shape[1]
    bits = pltpu.bitcast(aff_ref[...], jnp.int32)
    thr = jnp.zeros((N_EXPERTS, 1), jnp.int32)
    for b in range(30, -1, -1):
        cand = thr | (1 << b)
        cnt = jnp.sum(jnp.where(bits >= cand, 1.0, 0.0), axis=1, keepdims=True)
        thr = jnp.where(cnt >= cap, cand, thr)
    need = cap - jnp.sum(jnp.where(bits > thr, 1.0, 0.0), axis=1, keepdims=True)
    ri = lax.broadcasted_iota(jnp.int32, (LANES, LANES), 0)
    ci = lax.broadcasted_iota(jnp.int32, (LANES, LANES), 1)
    tri = jnp.where(ri <= ci, 1.0, 0.0).astype(BF16)
    lane = lax.broadcasted_iota(jnp.int32, (N_EXPERTS, LANES), 1)
    base = pl.program_id(0) * slot_stride
    off_eq = jnp.zeros((N_EXPERTS, 1), F32)
    off = jnp.zeros((N_EXPERTS, 1), F32)
    cnts = jnp.zeros((N_EXPERTS, LANES), jnp.int32)
    for j in range(n // LANES):
        sl = slice(j * LANES, (j + 1) * LANES)
        bj = bits[:, sl]
        eqf = jnp.where(bj == thr, 1.0, 0.0)
        ceq = jnp.dot(eqf.astype(BF16), tri, preferred_element_type=F32) + off_eq
        off_eq = ceq[:, LANES - 1:LANES]
        self_ = jnp.where(bj > thr, 1.0, jnp.where(ceq <= need, eqf, 0.0))
        csel = jnp.dot(self_.astype(BF16), tri, preferred_element_type=F32) + off
        cnts = jnp.where(lane == j, off.astype(jnp.int32) + base, cnts)
        off = csel[:, LANES - 1:LANES]
        slot_ref[:, sl] = jnp.where(self_ > 0.0, csel.astype(jnp.int32) - 1 + base, -1)
    cnt_ref[...] = jnp.where(lane >= n // LANES, off.astype(jnp.int32) + base, cnts)


def _route(aff, cap, slot_stride):
    sets, _, n = aff.shape
    return pl.pallas_call(
        functools.partial(_route_kernel, cap=cap, slot_stride=slot_stride), grid=(sets,),
        in_specs=[pl.BlockSpec((None, N_EXPERTS, n), lambda s: (s, 0, 0))],
        out_specs=[pl.BlockSpec((None, N_EXPERTS, n), lambda s: (s, 0, 0)),
                   pl.BlockSpec((None, N_EXPERTS, LANES), lambda s: (s, 0, 0))],
        out_shape=[jax.ShapeDtypeStruct((sets, N_EXPERTS, n), jnp.int32),
                   jax.ShapeDtypeStruct((sets, N_EXPERTS, LANES), jnp.int32)],
        compiler_params=_params(("parallel",), 32),
        name="route",
    )(aff)


def _expert_kernel(cnt_ref, l_ref, h2_ref, slot_ref, gate_ref, wg_ref, wu_ref, wd_ref, o_ref, xs_ref, ys_ref, gs_ref,
                   *, nslot):
    del l_ref
    s = pl.program_id(0)
    e = pl.program_id(1)
    ntb = SUPER // BLOCK
    nsb = nslot // BLOCK
    per = GATHER_TOKENS // BLOCK
    cbase = (s * N_EXPERTS + e) * (ntb + 1)

    @pl.when(e == 0)
    def _():
        o_ref[...] = jnp.zeros_like(o_ref)

    xs_ref[...] = jnp.zeros_like(xs_ref)
    gs_ref[...] = jnp.zeros_like(gs_ref)
    ys_ref[nslot:nslot + BLOCK, :] = jnp.zeros((BLOCK, D_MODEL), BF16)

    srow = lax.broadcasted_iota(jnp.int32, (BLOCK, GATHER_TOKENS), 0)
    for g in range(SUPER // GATHER_TOKENS):
        slots = slot_ref[g:g + 1, :]
        gates = gate_ref[g:g + 1, :]
        for j in range(nsb):
            @pl.when((cnt_ref[cbase + g * per] < (j + 1) * BLOCK) & (cnt_ref[cbase + (g + 1) * per] > j * BLOCK))
            def _():
                pick = slots == srow + j * BLOCK
                x = jnp.dot(jnp.where(pick, 1.0, 0.0).astype(BF16), h2_ref[g * GATHER_TOKENS:(g + 1) * GATHER_TOKENS, :],
                            preferred_element_type=F32)
                xs_ref[j * BLOCK:(j + 1) * BLOCK, :] += x.astype(BF16)
                gs_ref[j * BLOCK:(j + 1) * BLOCK, :] += jnp.sum(jnp.where(pick, gates, 0.0), axis=1, keepdims=True)

    xs = xs_ref[...]
    hg = jnp.dot(xs, wg_ref[...], preferred_element_type=F32)
    hu = jnp.dot(xs, wu_ref[...], preferred_element_type=F32)
    act = (hg * _sigmoid(hg) * hu).astype(BF16)
    y = jnp.dot(act, wd_ref[...], preferred_element_type=F32) * gs_ref[...]
    ys_ref[0:nslot, :] = y.astype(BF16)

    def scatter(tok0, ntok, w0):
        wrow = lax.broadcasted_iota(jnp.int32, (2 * BLOCK, ntok), 0)
        g, c0 = tok0 // GATHER_TOKENS, tok0 % GATHER_TOKENS
        slots = slot_ref[g:g + 1, c0:c0 + ntok]
        pick = jnp.where(slots - w0 == wrow, 1.0, 0.0).astype(BF16)
        o_ref[tok0:tok0 + ntok, :] += lax.dot_general(
            pick, ys_ref[pl.ds(w0, 2 * BLOCK), :], (((0,), (0,)), ((), ())), preferred_element_type=F32)

    def window(i):
        return pl.multiple_of(jnp.minimum(cnt_ref[cbase + i] // BLOCK, nsb - 1) * BLOCK, BLOCK)

    for g in range(SUPER // GATHER_TOKENS):
        fits = cnt_ref[cbase + (g + 1) * per] <= window(g * per) + 2 * BLOCK

        @pl.when(fits)
        def _():
            scatter(g * GATHER_TOKENS, GATHER_TOKENS, window(g * per))

        @pl.when(jnp.logical_not(fits))
        def _():
            for i in range(g * per, (g + 1) * per):
                scatter(i * BLOCK, BLOCK, window(i))


def _experts(cnt, h2, slot, gate, wg, wu, wd, layer):
    nsup = h2.shape[0]
    nslot = CAPACITY * SUPER // N_EXPERTS
    ngt = SUPER // GATHER_TOKENS
    grid_spec = pltpu.PrefetchScalarGridSpec(
        num_scalar_prefetch=2, grid=(nsup, N_EXPERTS),
        in_specs=[pl.BlockSpec((None, SUPER, D_MODEL), lambda s, e, c, l: (s, 0, 0), pipeline_mode=pl.Buffered(1)),
                  pl.BlockSpec((None, None, ngt, GATHER_TOKENS), lambda s, e, c, l: (s, e, 0, 0)),
                  pl.BlockSpec((None, None, ngt, GATHER_TOKENS), lambda s, e, c, l: (s, e, 0, 0)),
                  pl.BlockSpec((None, None, D_MODEL, EXPERT_FF), lambda s, e, c, l: (l[0], e, 0, 0)),
                  pl.BlockSpec((None, None, D_MODEL, EXPERT_FF), lambda s, e, c, l: (l[0], e, 0, 0)),
                  pl.BlockSpec((None, None, EXPERT_FF, D_MODEL), lambda s, e, c, l: (l[0], e, 0, 0))],
        out_specs=pl.BlockSpec((None, SUPER, D_MODEL), lambda s, e, c, l: (s, 0, 0), pipeline_mode=pl.Buffered(1)),
        scratch_shapes=[pltpu.VMEM((nslot, D_MODEL), BF16), pltpu.VMEM((nslot + BLOCK, D_MODEL), BF16),
                        pltpu.VMEM((nslot, 1), F32)])
    return pl.pallas_call(
        functools.partial(_expert_kernel, nslot=nslot), grid_spec=grid_spec,
        out_shape=jax.ShapeDtypeStruct((nsup, SUPER, D_MODEL), F32),
        compiler_params=_params(("arbitrary", "arbitrary"), 56),
        name="experts",
    )(cnt, _layer_index(layer), h2, slot, gate, wg, wu, wd)


def _final_kernel(x_ref, moe_ref, mod_ref, g_ref, o_ref):
    x = x_ref[...] + mod_ref[5:6, :] * moe_ref[...]
    o_ref[...] = x * lax.rsqrt(jnp.mean(x * x, axis=-1, keepdims=True) + EPS) * g_ref[...]


def _final(x1, moe, mod, g, tm):
    bsz, t, _ = x1.shape
    tok = pl.BlockSpec((None, tm, D_MODEL), lambda b, i: (b, i, 0))
    return pl.pallas_call(
        _final_kernel, grid=(bsz, t // tm),
        in_specs=[tok, tok, _mod_spec(mod), _const_spec((1, D_MODEL))],
        out_specs=tok, out_shape=jax.ShapeDtypeStruct((bsz, t, D_MODEL), F32),
        compiler_params=_params(("parallel", "parallel"), 32),
        name="final_norm",
    )(x1, moe, mod, g)


def _head_perm():
    cols = []
    for m in range(GROUP):
        for kv in range(N_KV):
            head = kv * GROUP + m
            cols.extend(range(head * HEAD_DIM, (head + 1) * HEAD_DIM))
    return jnp.array(cols, jnp.int32)


def _rope_swap(width):
    idx = jnp.arange(width)
    nf = HEAD_DIM // 4
    return jnp.where((idx % (2 * nf)) < nf, idx + nf, idx - nf)


def _rope_tables(t):
    pos = jnp.arange(t)
    row = (pos // GRID_W).astype(F32)
    col = (pos % GRID_W).astype(F32)
    nf = HEAD_DIM // 4
    inv = ROPE_BASE ** (-jnp.arange(nf, dtype=F32) / nf)

    def tabs(p):
        ang = p[:, None] * inv[None, :]
        cos, sin = jnp.cos(ang), jnp.sin(ang)
        return jnp.concatenate([cos, cos], axis=1), jnp.concatenate([-sin, sin], axis=1)

    cr, sr = tabs(row)
    cc, sc = tabs(col)
    cos = jnp.concatenate([cr, cc], axis=1)
    sin = jnp.concatenate([sr, sc], axis=1)
    return jnp.tile(cos, (1, LANES // HEAD_DIM)), jnp.tile(sin, (1, LANES // HEAD_DIM))


def _route_and_experts(h2c, affc, h2l, affl, wg, wu, wd, layer):
    bc, tc, _ = h2c.shape
    bl, tl, _ = h2l.shape
    ntb = SUPER // BLOCK
    capc = CAPACITY * tc // N_EXPERTS
    capl = CAPACITY * tl // N_EXPERTS
    slot_c, cnt_c = _route(affc, capc, capc)
    slot_l, cnt_l = _route(affl, capl, 0)
    tbc = tc // BLOCK
    rows = (SUPER // GATHER_TOKENS, GATHER_TOKENS)
    slot_c = slot_c.transpose(1, 0, 2).reshape(1, N_EXPERTS, *rows)
    gate_c = affc.transpose(1, 0, 2).reshape(1, N_EXPERTS, *rows)
    cnt_c = jnp.concatenate([cnt_c[:, :, :tbc].transpose(1, 0, 2).reshape(N_EXPERTS, ntb),
                             cnt_c[bc - 1, :, tbc:tbc + 1]], axis=1)
    moe_c = _experts(cnt_c.reshape(-1), h2c.reshape(1, SUPER, D_MODEL), slot_c, gate_c, wg, wu, wd, layer)
    moe_l = _experts(cnt_l[:, :, :ntb + 1].reshape(-1), h2l, slot_l.reshape(bl, N_EXPERTS, *rows),
                     affl.reshape(bl, N_EXPERTS, *rows), wg, wu, wd, layer)
    return moe_c.reshape(bc, tc, D_MODEL), moe_l


def kernel(x_prompt, x_sample, cache_k, cache_v, state_ssm_re, state_ssm_im, c, c_ctx, ada_w, ada_b, norm1, norm2,
           final_norm, w_in, conv_w, w_conv_out, attn_sink, w_attn_out, ssm_a_re, ssm_a_im, ssm_log_dt, ssm_b_re,
           ssm_b_im, ssm_c_re, ssm_c_im, ssm_d, w_glu, w_out, router_w, w_gate, w_up, w_down):
    bc, tc, _ = x_prompt.shape
    bl, tl, _ = x_sample.shape
    assert bc * tc == SUPER and tl == SUPER and bl + 1 <= SUBLANES
    hp = _head_perm()
    cvecs = jnp.zeros((SUBLANES, D_MODEL), F32).at[:bl].set(c).at[bl].set(c_ctx)
    mods = _modulation(cvecs, ada_w, ada_b).reshape(DEPTH, SUBLANES, 6, D_MODEL)

    o_q, o_k, o_v, o_u, o_g = 3 * CONV_W, 3 * CONV_W + Q_W, 3 * CONV_W + Q_W + KV_W, 3 * CONV_W + Q_W + 2 * KV_W, \
        3 * CONV_W + Q_W + 2 * KV_W + SSM_W
    w_b = w_in.astype(BF16)
    wq = w_b[:, :, o_q:o_k][:, :, hp]
    wk = w_b[:, :, o_k:o_v]
    w_qkvu = jnp.concatenate([wq, w_b[:, :, o_k:o_g], wq[:, :, _rope_swap(Q_W)], wk[:, :, _rope_swap(KV_W)]], axis=-1)
    w_co = w_conv_out.astype(BF16)
    w_ao = w_attn_out[:, hp, :].astype(BF16)
    w_gl = w_glu.astype(BF16)
    w_o = w_out.astype(BF16)
    rwt = router_w.transpose(0, 2, 1)
    wg, wu, wd = w_gate.astype(BF16), w_up.astype(BF16), w_down.astype(BF16)
    n1 = norm1.reshape(DEPTH, 1, D_MODEL)
    n2 = norm2.reshape(DEPTH, 1, D_MODEL)
    sink = attn_sink.reshape(DEPTH, N_KV, GROUP).transpose(0, 2, 1).reshape(DEPTH, N_HEADS)
    rope_tabs = _rope_tables(tl)
    ssm_w = _ssm_weights(ssm_a_re, ssm_a_im, ssm_log_dt, ssm_b_re, ssm_b_im, ssm_c_re, ssm_c_im, ssm_d)
    h0_all = jnp.stack([state_ssm_re[:, :, 0], state_ssm_im[:, :, 0], state_ssm_re[:, :, 1], state_ssm_im[:, :, 1]],
                       axis=0)
    h0_all = h0_all.reshape(4, bl, DEPTH, SSM_PAIRS, 2 * SSM_P).transpose(2, 3, 0, 1, 4)

    xp, xs = x_prompt, x_sample
    moe_c = moe_l = None
    modp_c = modp_l = None
    ks, vs, fins = [], [], []
    for l in range(DEPTH):
        mod_l = mods[l, :bl]
        mod_c = mods[l, bl:bl + 1]
        outs = _qkvu(xp, moe_c, modp_c, mod_c, n1, w_qkvu, l, None, tc, F32)
        q_c, k_c, v_c, u_c = outs[:4]
        if moe_c is not None:
            xp = outs[4]
        ks.append(k_c)
        vs.append(v_c)
        attn_c = _ctx_attention(sink[l], q_c, k_c, v_c)
        ssm_c, fin = _ssm(u_c, ssm_w, l, None)
        fins.append(fin)
        x1c, h2c, affc = _merge(xp, mod_c, n1, n2, w_b, conv_w, w_co, attn_c, w_ao, ssm_c, w_gl, w_o, rwt, l, tc)
        outs = _qkvu(xs, moe_l, modp_l, mod_l, n1, w_qkvu, l, rope_tabs, 512, BF16)
        q_l, k_l, v_l, u_l = outs[:4]
        if moe_l is not None:
            xs = outs[4]
        attn_l = _lat_attention(sink[l], q_l, k_l, v_l, cache_k[:, l].reshape(bl, -1, KV_W),
                                cache_v[:, l].reshape(bl, -1, KV_W))
        ssm_l, _ = _ssm(u_l, ssm_w, l, h0_all[l])
        x1l, h2l, affl = _merge(xs, mod_l, n1, n2, w_b, conv_w, w_co, attn_l, w_ao, ssm_l, w_gl, w_o, rwt, l, 512)
        moe_c, moe_l = _route_and_experts(h2c, affc, h2l, affl, wg, wu, wd, l)
        xp, xs = x1c, x1l
        modp_c, modp_l = mod_c, mod_l

    fn = final_norm.reshape(1, D_MODEL)
    y_prompt = _final(xp, moe_c, modp_c, fn, tc)
    y_sample = _final(xs, moe_l, modp_l, fn, 512)
    new_k = jnp.stack(ks, axis=1).reshape(bc, DEPTH, tc, N_KV, HEAD_DIM)
    new_v = jnp.stack(vs, axis=1).reshape(bc, DEPTH, tc, N_KV, HEAD_DIM)
    fin = jnp.stack(fins, axis=0)
    fin = fin.reshape(DEPTH, SSM_PAIRS, 2, 2, bc, 2, SSM_P).transpose(3, 4, 0, 2, 1, 5, 6)
    fin = fin.reshape(2, bc, DEPTH, 2, SSM_G, SSM_P)
    return (y_prompt, y_sample, new_k, new_v, fin[0], fin[1])
```

```python
import functools
import math

import jax
import jax.numpy as jnp
from jax import lax
from jax.experimental import pallas as pl
from jax.experimental.pallas import tpu as pltpu

D_MODEL = 1024
DEPTH = 4
GRID_W = 64
CONV_W = 512
N_HEADS = 8
N_KV = 2
HEAD_DIM = 64
GROUP = N_HEADS // N_KV
Q_W = N_HEADS * HEAD_DIM
KV_W = N_KV * HEAD_DIM
BLOCK = 128
ROPE_BASE = 10000.0
NEG_INF = -1e30
SSM_W = 512
SSM_GC = 16
SSM_G = SSM_W // SSM_GC
SSM_P = 64
N_BRANCH = 3
N_EXPERTS = 16
EXPERT_FF = 1024
CAPACITY = 2
EPS = 1e-6

F32 = jnp.float32
BF16 = jnp.bfloat16
HIGHEST = lax.Precision.HIGHEST

LANES = 128
SUBLANES = 8
SSM_CHUNK = 16
SSM_PAIRS = SSM_G // 2
SUPER = 4096
GATHER_TOKENS = 512
MIB = 1024 * 1024


def _params(sem, vmem_mib):
    return pltpu.CompilerParams(dimension_semantics=sem, vmem_limit_bytes=vmem_mib * MIB)


def _const_spec(shape):
    nd = len(shape)
    return pl.BlockSpec(shape, lambda *_: (0,) * nd, pipeline_mode=pl.Buffered(1))


def _layer_spec(arr):
    nd = arr.ndim
    return pl.BlockSpec((None,) + arr.shape[1:], lambda *idx: (idx[-1][0],) + (0,) * (nd - 1),
                        pipeline_mode=pl.Buffered(1))


def _layer_index(layer):
    return jnp.full((1,), layer, jnp.int32)


def _sigmoid(x):
    return 1.0 / (1.0 + jnp.exp(-x))


def _rms_mod(x, g, sc, sh):
    y = x * lax.rsqrt(jnp.mean(x * x, axis=-1, keepdims=True) + EPS)
    return (y * g) * (1.0 + sc) + sh


def _mod_kernel(c_ref, w_ref, b_ref, o_ref):
    cv = c_ref[...]
    s = cv * _sigmoid(cv)
    o_ref[...] = jnp.dot(s, w_ref[...], precision=HIGHEST, preferred_element_type=F32) + b_ref[...]


def _modulation(cvecs, ada_w, ada_b):
    nt = 1536
    return pl.pallas_call(
        _mod_kernel,
        grid=(DEPTH, 6 * D_MODEL // nt),
        in_specs=[pl.BlockSpec((SUBLANES, D_MODEL), lambda l, j: (0, 0)),
                  pl.BlockSpec((None, D_MODEL, nt), lambda l, j: (l, 0, j)),
                  pl.BlockSpec((None, 1, nt), lambda l, j: (l, 0, j))],
        out_specs=pl.BlockSpec((None, SUBLANES, nt), lambda l, j: (l, 0, j)),
        out_shape=jax.ShapeDtypeStruct((DEPTH, SUBLANES, 6 * D_MODEL), F32),
        compiler_params=_params(("parallel", "parallel"), 32),
        name="modulation",
    )(cvecs, ada_w, ada_b.reshape(DEPTH, 1, 6 * D_MODEL))


def _qkvu_kernel(*refs, rope, fuse_res):
    it = iter(refs[1:])
    x_ref = next(it)
    if fuse_res:
        moe_ref = next(it)
        modp_ref = next(it)
    mod_ref = next(it)
    n1_ref = next(it)
    w_ref = next(it)
    if rope:
        cos_ref = next(it)
        sin_ref = next(it)
    q_ref, k_ref, v_ref, u_ref = next(it), next(it), next(it), next(it)
    x = x_ref[...]
    if fuse_res:
        xo_ref = next(it)
        x = x + modp_ref[5:6, :] * moe_ref[...]
        xo_ref[...] = x
    h = _rms_mod(x, n1_ref[...], mod_ref[1:2, :], mod_ref[0:1, :]).astype(BF16)
    ncol = w_ref.shape[1] if rope else Q_W + 2 * KV_W + SSM_W
    p = jnp.dot(h, w_ref[:, 0:ncol], preferred_element_type=F32)
    q = p[:, 0:Q_W]
    k = p[:, Q_W:Q_W + KV_W]
    if rope:
        cos = cos_ref[...]
        sin = sin_ref[...]
        o = Q_W + 2 * KV_W + SSM_W
        q = jnp.concatenate(
            [q[:, m * LANES:(m + 1) * LANES] * cos + p[:, o + m * LANES:o + (m + 1) * LANES] * sin
             for m in range(Q_W // LANES)], axis=1)
        k = k * cos + p[:, o + Q_W:o + Q_W + KV_W] * sin
    q_ref[...] = (q * HEAD_DIM ** -0.5).astype(q_ref.dtype)
    k_ref[...] = k.astype(k_ref.dtype)
    v_ref[...] = p[:, Q_W + KV_W:Q_W + 2 * KV_W].astype(v_ref.dtype)
    u_ref[...] = p[:, Q_W + 2 * KV_W:Q_W + 2 * KV_W + SSM_W]


def _mod_spec(mod):
    if mod.shape[0] == 1:
        return pl.BlockSpec((None, 6, D_MODEL), lambda b, i, *_: (0, 0, 0))
    return pl.BlockSpec((None, 6, D_MODEL), lambda b, i, *_: (b, 0, 0))


def _qkvu(x, moe, modp, mod, n1, w, layer, rope_tabs, tm, kv_dtype):
    bsz, t, _ = x.shape
    rope = rope_tabs is not None
    fuse_res = moe is not None
    tok = lambda wd: pl.BlockSpec((None, tm, wd), lambda b, i, *_: (b, i, 0))
    in_specs, args = [tok(D_MODEL)], [x]
    if fuse_res:
        in_specs += [tok(D_MODEL), _mod_spec(modp)]
        args += [moe, modp]
    in_specs += [_mod_spec(mod), _layer_spec(n1), _layer_spec(w)]
    args += [mod, n1, w]
    if rope:
        in_specs += [pl.BlockSpec((tm, LANES), lambda b, i, *_: (i, 0))] * 2
        args += list(rope_tabs)
    out_specs = [tok(Q_W), tok(KV_W), tok(KV_W), tok(SSM_W)]
    out_shape = [jax.ShapeDtypeStruct((bsz, t, Q_W), BF16),
                 jax.ShapeDtypeStruct((bsz, t, KV_W), kv_dtype),
                 jax.ShapeDtypeStruct((bsz, t, KV_W), kv_dtype),
                 jax.ShapeDtypeStruct((bsz, t, SSM_W), F32)]
    if fuse_res:
        out_specs.append(tok(D_MODEL))
        out_shape.append(jax.ShapeDtypeStruct((bsz, t, D_MODEL), F32))
    grid_spec = pltpu.PrefetchScalarGridSpec(num_scalar_prefetch=1, grid=(bsz, t // tm), in_specs=in_specs,
                                             out_specs=out_specs)
    return pl.pallas_call(
        functools.partial(_qkvu_kernel, rope=rope, fuse_res=fuse_res), grid_spec=grid_spec, out_shape=out_shape,
        compiler_params=_params(("parallel", "parallel"), 48),
        name="qkvu",
    )(_layer_index(layer), *args)


def _stack_heads(q_ref):
    lo = lax.broadcasted_iota(jnp.int32, (1, LANES), 1) < HEAD_DIM
    keep_lo = jnp.where(lo, 1.0, 0.0).astype(BF16)
    keep_hi = jnp.where(lo, 0.0, 1.0).astype(BF16)
    parts = []
    for m in range(GROUP):
        qm = q_ref[:, m * LANES:(m + 1) * LANES]
        parts.append(qm * keep_lo)
        parts.append(qm * keep_hi)
    return jnp.concatenate(parts, axis=0), lo


def _attend(s, bias, sink_ref, v, tq, lo, o_ref):
    ps, dens = [], []
    for h in range(N_HEADS):
        sh = s[h * tq:(h + 1) * tq]
        if bias is not None:
            sh = sh + bias
        sink = sink_ref[h]
        mx = jnp.maximum(jnp.max(sh, axis=-1, keepdims=True), sink)
        p = jnp.exp(sh - mx)
        dens.append(jnp.sum(p, axis=-1, keepdims=True) + jnp.exp(sink - mx))
        ps.append(p.astype(BF16))
    o = jnp.dot(jnp.concatenate(ps, axis=0), v, preferred_element_type=F32)
    for m in range(GROUP):
        o0 = o[(2 * m) * tq:(2 * m + 1) * tq] / dens[2 * m]
        o1 = o[(2 * m + 1) * tq:(2 * m + 2) * tq] / dens[2 * m + 1]
        o_ref[:, m * LANES:(m + 1) * LANES] = jnp.where(lo, o0, o1).astype(o_ref.dtype)


def _ctx_attn_kernel(sink_ref, q_ref, k_ref, v_ref, o_ref):
    tq = q_ref.shape[0]
    qx, lo = _stack_heads(q_ref)
    s = lax.dot_general(qx, k_ref[...].astype(BF16), (((1,), (1,)), ((), ())), preferred_element_type=F32)
    _attend(s, None, sink_ref, v_ref[...].astype(BF16), tq, lo, o_ref)


def _ctx_attention(sink, q, k, v):
    bsz, t, _ = q.shape
    tok = lambda wd: pl.BlockSpec((None, t, wd), lambda b: (b, 0, 0))
    return pl.pallas_call(
        _ctx_attn_kernel, grid=(bsz,),
        in_specs=[pl.BlockSpec(memory_space=pltpu.SMEM), tok(Q_W), tok(KV_W), tok(KV_W)],
        out_specs=tok(Q_W), out_shape=jax.ShapeDtypeStruct((bsz, t, Q_W), BF16),
        compiler_params=_params(("parallel",), 32),
        name="ctx_attention",
    )(sink, q, k, v)


def _lat_attn_kernel(sink_ref, q_ref, kc_ref, vc_ref, kp_ref, k0_ref, kn_ref, vp_ref, v0_ref, vn_ref, o_ref):
    tq = q_ref.shape[0]
    i = pl.program_id(1)
    nb = pl.num_programs(1)
    qx, lo = _stack_heads(q_ref)
    kall = jnp.concatenate([kc_ref[...].astype(BF16), kp_ref[...], k0_ref[...], kn_ref[...]], axis=0)
    vall = jnp.concatenate([vc_ref[...].astype(BF16), vp_ref[...], v0_ref[...], vn_ref[...]], axis=0)
    s = lax.dot_general(qx, kall, (((1,), (1,)), ((), ())), preferred_element_type=F32)
    past = kc_ref.shape[0]
    r = lax.broadcasted_iota(jnp.int32, (tq, BLOCK), 0)
    c = lax.broadcasted_iota(jnp.int32, (tq, BLOCK), 1)
    m_prev = jnp.where((c >= r) & (i > 0), 0.0, NEG_INF)
    m_next = jnp.where((c <= r) & (i < nb - 1), 0.0, NEG_INF)
    bias = jnp.concatenate([jnp.zeros((tq, past), F32), m_prev, jnp.zeros((tq, BLOCK), F32), m_next], axis=1)
    _attend(s, bias, sink_ref, vall, tq, lo, o_ref)


def _lat_attention(sink, q, k, v, kc, vc):
    bsz, t, _ = q.shape
    nb = t // BLOCK
    past = kc.shape[1]
    tok = lambda wd: pl.BlockSpec((None, BLOCK, wd), lambda b, i: (b, i, 0))
    prev = pl.BlockSpec((None, BLOCK, KV_W), lambda b, i: (b, jnp.maximum(i - 1, 0), 0))
    nxt = pl.BlockSpec((None, BLOCK, KV_W), lambda b, i: (b, jnp.minimum(i + 1, nb - 1), 0))
    ctx = pl.BlockSpec((None, past, KV_W), lambda b, i: (b, 0, 0))
    return pl.pallas_call(
        _lat_attn_kernel, grid=(bsz, nb),
        in_specs=[pl.BlockSpec(memory_space=pltpu.SMEM), tok(Q_W), ctx, ctx,
                  prev, tok(KV_W), nxt, prev, tok(KV_W), nxt],
        out_specs=tok(Q_W), out_shape=jax.ShapeDtypeStruct((bsz, t, Q_W), BF16),
        compiler_params=_params(("parallel", "parallel"), 32),
        name="lat_attention",
    )(sink, q, kc, vc, k, k, k, v, v, v)


SSM_QUAD = LANES // (2 * SSM_GC)
GRANULES = LANES // SSM_GC


def _ssm_kernel(l_ref, u_ref, strip_ref, wst_ref, w2_ref, a_ref, h0_ref, d_ref, y_ref, fin_ref, pk_ref, z_ref,
                ent_ref, toep_ref, *, bsz, t):
    del l_ref
    nk = t // SSM_CHUNK
    gran = lax.broadcasted_iota(jnp.int32, (1, LANES), 1) // SSM_GC
    half = SSM_CHUNK // GRANULES

    ck = min(nk, 4 * SUBLANES)

    def transpose_granules(xs):
        xs = list(xs)
        for d in (4, 2, 1):
            low = (gran & d) == 0
            for i in range(GRANULES):
                if not i & d:
                    a, b = xs[i], xs[i + d]
                    xs[i] = jnp.where(low, a, pltpu.roll(b, SSM_GC * d, axis=1))
                    xs[i + d] = jnp.where(low, pltpu.roll(a, LANES - SSM_GC * d, axis=1), b)
        return xs

    def pack(it, carry):
        b, k0 = it // (nk // ck), (it % (nk // ck)) * ck
        for sh in range(half):
            xs = [u_ref[pl.ds(b * t + k0 * SSM_CHUNK + sh * GRANULES + s8, ck, stride=SSM_CHUNK), :]
                  for s8 in range(GRANULES)]
            for g8, x in enumerate(transpose_granules(xs)):
                pk_ref[g8 // 2, (g8 % 2) * half + sh, pl.ds(k0 * bsz + b, ck, stride=bsz), :] = x
        return carry

    lax.fori_loop(0, bsz * (nk // ck), pack, 0)

    rows = max(bsz, SUBLANES)
    steps = rows // bsz
    niter = nk // steps
    ns = SSM_W

    def step(sr, si, dr, di, ar, ai):
        return ar * sr - ai * si + dr, ar * si + ai * sr + di

    for p in range(SSM_QUAD):
        u = jnp.concatenate([pk_ref[p, col] for col in range(2 * half)], axis=1)
        ub = u.astype(BF16)
        gw = SSM_CHUNK * SSM_GC
        for gl in range(2):
            strip = strip_ref[p, gl]
            for s in range(SSM_CHUNK):
                off = (SSM_CHUNK - 1 - s) * SSM_GC
                blk = pltpu.roll(strip, 2 * gw - off, axis=1) if off else strip
                toep_ref[s * SSM_GC:(s + 1) * SSM_GC, :] = blk[:, 0:gw].astype(BF16)
            z_ref[:, gl * gw:(gl + 1) * gw] = jnp.dot(ub[:, gl * gw:(gl + 1) * gw], toep_ref[...],
                                                      preferred_element_type=F32)
        z_ref[:, ns:ns + 4 * LANES] = jnp.dot(ub, wst_ref[p], preferred_element_type=F32)
        afr, afi, abr, abi = a_ref[p, 0:1, :], a_ref[p, 1:2, :], a_ref[p, 2:3, :], a_ref[p, 3:4, :]

        def body(it, carry):
            sr, si, gr, gi = carry
            rf = pl.multiple_of(it * rows, SUBLANES)
            rb = pl.multiple_of((niter - 1 - it) * rows, SUBLANES)
            dfr, dfi = z_ref[pl.ds(rf, rows), ns:ns + LANES], z_ref[pl.ds(rf, rows), ns + LANES:ns + 2 * LANES]
            dbr = z_ref[pl.ds(rb, rows), ns + 2 * LANES:ns + 3 * LANES]
            dbi = z_ref[pl.ds(rb, rows), ns + 3 * LANES:ns + 4 * LANES]
            efr, efi, ebr, ebi = [], [], [None] * steps, [None] * steps
            for j in range(steps):
                efr.append(sr)
                efi.append(si)
                sr, si = step(sr, si, dfr[j * bsz:(j + 1) * bsz], dfi[j * bsz:(j + 1) * bsz], afr, afi)
            for j in reversed(range(steps)):
                ebr[j] = gr
                ebi[j] = gi
                gr, gi = step(gr, gi, dbr[j * bsz:(j + 1) * bsz], dbi[j * bsz:(j + 1) * bsz], abr, abi)
            cat = lambda xs: xs[0] if len(xs) == 1 else jnp.concatenate(xs, axis=0)
            ent_ref[pl.ds(rf, rows), 0:LANES] = cat(efr)
            ent_ref[pl.ds(rf, rows), LANES:2 * LANES] = cat(efi)
            ent_ref[pl.ds(rb, rows), 2 * LANES:3 * LANES] = cat(ebr)
            ent_ref[pl.ds(rb, rows), 3 * LANES:4 * LANES] = cat(ebi)
            return sr, si, gr, gi

        init = (h0_ref[p, 0], h0_ref[p, 1], h0_ref[p, 2], h0_ref[p, 3])
        sr, si, gr, gi = lax.fori_loop(0, niter, body, init)
        fin_ref[p, 0] = sr
        fin_ref[p, 1] = si
        fin_ref[p, 2] = gr
        fin_ref[p, 3] = gi
        y = (z_ref[:, 0:SSM_W] + jnp.dot(ent_ref[...].astype(BF16), w2_ref[p], preferred_element_type=F32)
             + d_ref[p] * u)
        for col in range(2 * half):
            pk_ref[p, col] = y[:, col * LANES:(col + 1) * LANES]

    def unpack(it, carry):
        b, k0 = it // (nk // ck), (it % (nk // ck)) * ck
        for sh in range(half):
            xs = [pk_ref[g8 // 2, (g8 % 2) * half + sh, pl.ds(k0 * bsz + b, ck, stride=bsz), :] for g8 in range(GRANULES)]
            for s8, x in enumerate(transpose_granules(xs)):
                y_ref[pl.ds(b * t + k0 * SSM_CHUNK + sh * GRANULES + s8, ck, stride=SSM_CHUNK), :] = x
        return carry

    lax.fori_loop(0, bsz * (nk // ck), unpack, 0)


def _ssm(u, ssm_w, layer, h0):
    bsz, t, _ = u.shape
    strip, wst, w2, a16, dflat = ssm_w
    rows = (t // SSM_CHUNK) * bsz
    if h0 is None:
        h0 = jnp.zeros((SSM_PAIRS, 4, bsz, LANES), F32)
    gw = SSM_CHUNK * SSM_GC
    quad = lambda *tail: pl.BlockSpec((SSM_QUAD,) + tail, lambda q, lref: (q,) + (0,) * len(tail))
    lquad = lambda *tail: pl.BlockSpec((None, SSM_QUAD) + tail, lambda q, lref: (lref[0], q) + (0,) * len(tail))
    tok = pl.BlockSpec((bsz * t, LANES), lambda q, lref: (0, q), pipeline_mode=pl.Buffered(1))
    grid_spec = pltpu.PrefetchScalarGridSpec(
        num_scalar_prefetch=1, grid=(SSM_PAIRS // SSM_QUAD,),
        in_specs=[tok, lquad(2, SSM_GC, 2 * gw), lquad(SSM_W, 4 * LANES), lquad(4 * LANES, SSM_W), lquad(4, LANES),
                  quad(4, bsz, LANES), lquad(1, SSM_W)],
        out_specs=[tok, quad(4, bsz, LANES)],
        scratch_shapes=[pltpu.VMEM((SSM_QUAD, SSM_W // LANES, rows, LANES), F32),
                        pltpu.VMEM((rows, SSM_W + 4 * LANES), F32), pltpu.VMEM((rows, 4 * LANES), F32),
                        pltpu.VMEM((gw, gw), BF16)])
    y, fin = pl.pallas_call(
        functools.partial(_ssm_kernel, bsz=bsz, t=t), grid_spec=grid_spec,
        out_shape=[jax.ShapeDtypeStruct((bsz * t, SSM_W), F32),
                   jax.ShapeDtypeStruct((SSM_PAIRS, 4, bsz, LANES), F32)],
        compiler_params=_params(("parallel",), 52),
        name="ssm",
    )(_layer_index(layer), u.reshape(bsz * t, SSM_W), strip, wst, w2, a16, h0, dflat)
    return y.reshape(bsz, t, SSM_W), fin


def _ssm_weights(a_re, a_im, log_dt, b_re, b_im, c_re, c_im, ssm_d):
    L = SSM_CHUNK
    nl = a_re.shape[0]
    dt = jnp.exp(log_dt)[..., None, None]
    steps = jnp.arange(L + 1, dtype=F32)
    mag = jnp.exp(a_re[..., None] * dt * steps)
    ang = a_im[..., None] * dt * steps
    pr, pi = mag * jnp.cos(ang), mag * jnp.sin(ang)
    nr, ni = pr[..., 1] - 1.0, pi[..., 1]
    den = a_re * a_re + a_im * a_im
    qr, qi = (nr * a_re + ni * a_im) / den, (ni * a_re - nr * a_im) / den
    swap = lambda x: x.transpose(0, 1, 2, 4, 3)
    bbr = swap(qr[..., None] * b_re - qi[..., None] * b_im)
    bbi = swap(qr[..., None] * b_im + qi[..., None] * b_re)
    ctr, cti = swap(c_re), swap(c_im)
    cpr = ctr[..., None, :] * pr[..., :L, None] - cti[..., None, :] * pi[..., :L, None]
    cpi = ctr[..., None, :] * pi[..., :L, None] + cti[..., None, :] * pr[..., :L, None]
    kern = jnp.sum(bbr[..., None, None] * cpr[:, :, :, None] - bbi[..., None, None] * cpi[:, :, :, None], axis=4)
    kf, kb = kern[:, 0], kern[:, 1]
    lagk = jnp.concatenate([kb[..., :0:-1, :], kf[..., 0:1, :] + kb[..., 0:1, :], kf[..., 1:, :]], axis=-2)
    strip = jnp.pad(lagk.reshape(nl, SSM_G, SSM_GC, (2 * L - 1) * SSM_GC), ((0, 0), (0, 0), (0, 0), (0, SSM_GC)))
    strip = strip.reshape(nl, SSM_PAIRS, 2, SSM_GC, 2 * L * SSM_GC)
    idx = jnp.arange(L)

    def pair(w):
        w = w.reshape((nl, SSM_PAIRS, 2) + w.shape[2:])
        even, odd = w[:, :, 0], w[:, :, 1]
        zero = jnp.zeros_like(even)
        return jnp.stack([jnp.concatenate([even, zero], axis=-1), jnp.concatenate([zero, odd], axis=-1)], axis=2)

    def inject(kpow, d):
        ar = pr[:, d].transpose(0, 1, 3, 2)[:, :, kpow][:, :, :, None, :]
        ai = pi[:, d].transpose(0, 1, 3, 2)[:, :, kpow][:, :, :, None, :]
        br, bi = bbr[:, d][:, :, None], bbi[:, d][:, :, None]
        return [pair(w).reshape(nl, SSM_PAIRS, 2 * L * SSM_GC, 2 * SSM_P) for w in (ar * br - ai * bi, ar * bi + ai * br)]

    wst = jnp.concatenate(inject(L - 1 - idx, 0) + inject(idx, 1), axis=-1)

    def readout(kpow, d):
        ar, ai = pr[:, d][..., kpow][..., None], pi[:, d][..., kpow][..., None]
        cr, ci = ctr[:, d][:, :, :, None, :], cti[:, d][:, :, :, None, :]
        planes = (cr * ar - ci * ai, -(cr * ai + ci * ar))
        return [pair(w.reshape(nl, SSM_G, SSM_P, L * SSM_GC)).reshape(nl, SSM_PAIRS, 2 * SSM_P, 2 * L * SSM_GC)
                for w in planes]

    w2 = jnp.concatenate(readout(1 + idx, 0) + readout(L - idx, 1), axis=2)
    plane = lambda x: x.reshape(nl, SSM_PAIRS, 2 * SSM_P)
    a16 = jnp.stack([plane(pr[:, 0, :, :, L]), plane(pi[:, 0, :, :, L]), plane(pr[:, 1, :, :, L]),
                     plane(pi[:, 1, :, :, L])], axis=2)
    dflat = jnp.broadcast_to(ssm_d.reshape(nl, SSM_PAIRS, 2, 1, SSM_GC), (nl, SSM_PAIRS, 2, L, SSM_GC))
    return strip, wst.astype(BF16), w2.astype(BF16), a16, dflat.reshape(nl, SSM_PAIRS, 1, 2 * L * SSM_GC)


def _merge_kernel(l_ref, x_ref, xp_ref, xn_ref, mod_ref, n1_ref, n2_ref, wcg_ref, cw_ref, wco_ref, attn_ref, wao_ref,
                  ssm_ref, wglu_ref, wout_ref, rw_ref, x1_ref, h2_ref, aff_ref):
    del l_ref
    tm = x_ref.shape[0]
    i = pl.program_id(1)
    nt = pl.num_programs(1)
    x = x_ref[...]
    n1 = n1_ref[...]
    sh1, sc1, g1 = mod_ref[0:1, :], mod_ref[1:2, :], mod_ref[2:3, :]
    sh2, sc2 = mod_ref[3:4, :], mod_ref[4:5, :]
    cw = CONV_W

    def conv_in(h):
        xin = jnp.dot(h, wcg_ref[:, 0:cw], preferred_element_type=F32)
        cg = jnp.dot(h, wcg_ref[:, 2 * cw:3 * cw], preferred_element_type=F32)
        return cg * xin

    x_ext = jnp.concatenate([x, xp_ref[...], xn_ref[...]], axis=0)
    h_ext = _rms_mod(x_ext, n1, sc1, sh1).astype(BF16)
    h = h_ext[0:tm]
    z_ext = conv_in(h_ext)
    z = z_ext[0:tm]
    z_before = jnp.where(i > 0, z_ext[tm + SUBLANES - 1:tm + SUBLANES, :], 0.0)
    z_after = jnp.where(i < nt - 1, z_ext[tm + SUBLANES:tm + SUBLANES + 1, :], 0.0)
    rows = lax.broadcasted_iota(jnp.int32, (tm, 1), 0)
    zl = jnp.where(rows == 0, z_before, pltpu.roll(z, 1, axis=0))
    zr = jnp.where(rows == tm - 1, z_after, pltpu.roll(z, tm - 1, axis=0))
    y = zl * cw_ref[0:1, :] + z * cw_ref[1:2, :] + zr * cw_ref[2:3, :]
    bg = jnp.dot(h, wcg_ref[:, cw:2 * cw], preferred_element_type=F32)
    conv_y = jnp.dot((bg * y).astype(BF16), wco_ref[...], preferred_element_type=F32)
    o = wcg_ref.shape[1] - N_BRANCH * D_MODEL
    ga = _sigmoid(jnp.dot(h, wcg_ref[:, o:o + D_MODEL], preferred_element_type=F32))
    mixed = ga * conv_y
    attn_y = jnp.dot(attn_ref[...], wao_ref[...], preferred_element_type=F32)
    gb = _sigmoid(jnp.dot(h, wcg_ref[:, o + D_MODEL:o + 2 * D_MODEL], preferred_element_type=F32))
    mixed = mixed + gb * attn_y
    s = ssm_ref[...]
    gelu = s * (0.5 * (1.0 + jnp.tanh(math.sqrt(2.0 / math.pi) * (s + 0.044715 * (s * s * s)))))
    zab = jnp.dot(gelu.astype(BF16), wglu_ref[...], preferred_element_type=F32)
    ssm_y = zab[:, 0:D_MODEL] * _sigmoid(zab[:, D_MODEL:2 * D_MODEL])
    gc = _sigmoid(jnp.dot(h, wcg_ref[:, o + 2 * D_MODEL:o + 3 * D_MODEL], preferred_element_type=F32))
    mixed = mixed + gc * ssm_y
    x1 = x + g1 * jnp.dot(mixed.astype(BF16), wout_ref[...], preferred_element_type=F32)
    x1_ref[...] = x1
    h2 = _rms_mod(x1, n2_ref[...], sc2, sh2)
    h2_ref[...] = h2.astype(BF16)
    h_hi = h2.astype(BF16)
    h_lo = (h2 - h_hi.astype(F32)).astype(BF16)
    rw = rw_ref[...]
    r_hi = rw.astype(BF16)
    r_lo = (rw - r_hi.astype(F32)).astype(BF16)
    nt_dims = (((1,), (1,)), ((), ()))
    a = lax.dot_general(jnp.concatenate([r_hi, r_lo], axis=0), h_hi, nt_dims, preferred_element_type=F32)
    b = lax.dot_general(r_hi, h_lo, nt_dims, preferred_element_type=F32)
    logits = a[0:N_EXPERTS] + a[N_EXPERTS:2 * N_EXPERTS] + b
    e = jnp.exp(logits - jnp.max(logits, axis=0, keepdims=True))
    aff_ref[...] = e / jnp.sum(e, axis=0, keepdims=True)


def _merge(x, mod, n1, n2, wcg, cw, wco, attn, wao, ssm, wglu, wout, rwt, layer, tm):
    bsz, t, _ = x.shape
    nt = t // tm
    per = tm // SUBLANES
    tok = lambda wd: pl.BlockSpec((None, tm, wd), lambda b, i, *_: (b, i, 0))
    prev = pl.BlockSpec((None, SUBLANES, D_MODEL), lambda b, i, *_: (b, jnp.maximum(i * per - 1, 0), 0))
    nxt = pl.BlockSpec((None, SUBLANES, D_MODEL),
                       lambda b, i, *_: (b, jnp.minimum((i + 1) * per, t // SUBLANES - 1), 0))
    lay = _layer_spec
    grid_spec = pltpu.PrefetchScalarGridSpec(
        num_scalar_prefetch=1, grid=(bsz, nt),
        in_specs=[tok(D_MODEL), prev, nxt, _mod_spec(mod), lay(n1), lay(n2), lay(wcg), lay(cw), lay(wco), tok(Q_W),
                  lay(wao), tok(SSM_W), lay(wglu), lay(wout), lay(rwt)],
        out_specs=[tok(D_MODEL), tok(D_MODEL), pl.BlockSpec((None, N_EXPERTS, tm), lambda b, i, *_: (b, 0, i))])
    return pl.pallas_call(
        _merge_kernel, grid_spec=grid_spec,
        out_shape=[jax.ShapeDtypeStruct((bsz, t, D_MODEL), F32), jax.ShapeDtypeStruct((bsz, t, D_MODEL), BF16),
                   jax.ShapeDtypeStruct((bsz, N_EXPERTS, t), F32)],
        compiler_params=_params(("parallel", "parallel"), 56),
        name="merge",
    )(_layer_index(layer), x, x, x, mod, n1, n2, wcg, cw, wco, attn, wao, ssm, wglu, wout, rwt)


def _route_kernel(aff_ref, slot_ref, cnt_ref, *, cap, slot_stride):
    n = aff_ref.shape[1]
    bits = pltpu.bitcast(aff_ref[...], jnp.int32)
    thr = jnp.zeros((N_EXPERTS, 1), jnp.int32)
    for b in range(30, -1, -1):
        cand = thr | (1 << b)
        cnt = jnp.sum(jnp.where(bits >= cand, 1.0, 0.0), axis=1, keepdims=True)
        thr = jnp.where(cnt >= cap, cand, thr)
    need = cap - jnp.sum(jnp.where(bits > thr, 1.0, 0.0), axis=1, keepdims=True)
    ri = lax.broadcasted_iota(jnp.int32, (LANES, LANES), 0)
    ci = lax.broadcasted_iota(jnp.int32, (LANES, LANES), 1)
    tri = jnp.where(ri <= ci, 1.0, 0.0).astype(BF16)
    lane = lax.broadcasted_iota(jnp.int32, (N_EXPERTS, LANES), 1)
    base = pl.program_id(0) * slot_stride
    off_eq = jnp.zeros((N_EXPERTS, 1), F32)
    off = jnp.zeros((N_EXPERTS, 1), F32)
    cnts = jnp.zeros((N_EXPERTS, LANES), jnp.int32)
    for j in range(n // LANES):
        sl = slice(j * LANES, (j + 1) * LANES)
        bj = bits[:, sl]
        eqf = jnp.where(bj == thr, 1.0, 0.0)
        ceq = jnp.dot(eqf.astype(BF16), tri, preferred_element_type=F32) + off_eq
        off_eq = ceq[:, LANES - 1:LANES]
        self_ = jnp.where(bj > thr, 1.0, jnp.where(ceq <= need, eqf, 0.0))
        csel = jnp.dot(self_.astype(BF16), tri, preferred_element_type=F32) + off
        cnts = jnp.where(lane == j, off.astype(jnp.int32) + base, cnts)
        off = csel[:, LANES - 1:LANES]
        slot_ref[:, sl] = jnp.where(self_ > 0.0, csel.astype(jnp.int32) - 1 + base, -1)
    cnt_ref[...] = jnp.where(lane >= n // LANES, off.astype(jnp.int32) + base, cnts)


def _route(aff, cap, slot_stride):
    sets, _, n = aff.shape
    return pl.pallas_call(
        functools.partial(_route_kernel, cap=cap, slot_stride=slot_stride), grid=(sets,),
        in_specs=[pl.BlockSpec((None, N_EXPERTS, n), lambda s: (s, 0, 0))],
        out_specs=[pl.BlockSpec((None, N_EXPERTS, n), lambda s: (s, 0, 0)),
                   pl.BlockSpec((None, N_EXPERTS, LANES), lambda s: (s, 0, 0))],
        out_shape=[jax.ShapeDtypeStruct((sets, N_EXPERTS, n), jnp.int32),
                   jax.ShapeDtypeStruct((sets, N_EXPERTS, LANES), jnp.int32)],
        compiler_params=_params(("parallel",), 32),
        name="route",
    )(aff)


def _expert_kernel(cnt_ref, l_ref, h2_ref, slot_ref, gate_ref, wg_ref, wu_ref, wd_ref, o_ref, xs_ref, ys_ref, gs_ref,
                   *, nslot):
    del l_ref
    s = pl.program_id(0)
    e = pl.program_id(1)
    ntb = SUPER // BLOCK
    nsb = nslot // BLOCK
    per = GATHER_TOKENS // BLOCK
    cbase = (s * N_EXPERTS + e) * (ntb + 1)

    @pl.when(e == 0)
    def _():
        o_ref[...] = jnp.zeros_like(o_ref)

    xs_ref[...] = jnp.zeros_like(xs_ref)
    gs_ref[...] = jnp.zeros_like(gs_ref)
    ys_ref[nslot:nslot + BLOCK, :] = jnp.zeros((BLOCK, D_MODEL), BF16)

    wrow = lax.broadcasted_iota(jnp.int32, (2 * BLOCK, BLOCK), 0)

    def window(i):
        w0 = pl.multiple_of(jnp.minimum(cnt_ref[cbase + i] // BLOCK, nsb - 1) * BLOCK, BLOCK)
        cols = slice((i % per) * BLOCK, (i % per + 1) * BLOCK)
        pick = slot_ref[i // per:i // per + 1, cols] - w0 == wrow
        return w0, cols, pick

    for i in range(ntb):
        w0, cols, pick = window(i)
        x = jnp.dot(jnp.where(pick, 1.0, 0.0).astype(BF16), h2_ref[i * BLOCK:(i + 1) * BLOCK, :],
                    preferred_element_type=F32)
        xs_ref[pl.ds(w0, 2 * BLOCK), :] += x.astype(BF16)
        gates = gate_ref[i // per:i // per + 1, cols]
        gs_ref[pl.ds(w0, 2 * BLOCK), :] += jnp.sum(jnp.where(pick, gates, 0.0), axis=1, keepdims=True)

    xs = xs_ref[0:nslot, :]
    hg = jnp.dot(xs, wg_ref[...], preferred_element_type=F32)
    hu = jnp.dot(xs, wu_ref[...], preferred_element_type=F32)
    act = (hg * _sigmoid(hg) * hu).astype(BF16)
    y = jnp.dot(act, wd_ref[...], preferred_element_type=F32) * gs_ref[0:nslot, :]
    ys_ref[0:nslot, :] = y.astype(BF16)

    for i in range(ntb):
        w0, _, pick = window(i)
        o_ref[i * BLOCK:(i + 1) * BLOCK, :] += lax.dot_general(
            jnp.where(pick, 1.0, 0.0).astype(BF16), ys_ref[pl.ds(w0, 2 * BLOCK), :], (((0,), (0,)), ((), ())),
            preferred_element_type=F32)


def _experts(cnt, h2, slot, gate, wg, wu, wd, layer):
    nsup = h2.shape[0]
    nslot = CAPACITY * SUPER // N_EXPERTS
    ngt = SUPER // GATHER_TOKENS
    grid_spec = pltpu.PrefetchScalarGridSpec(
        num_scalar_prefetch=2, grid=(nsup, N_EXPERTS),
        in_specs=[pl.BlockSpec((None, SUPER, D_MODEL), lambda s, e, c, l: (s, 0, 0), pipeline_mode=pl.Buffered(1)),
                  pl.BlockSpec((None, None, ngt, GATHER_TOKENS), lambda s, e, c, l: (s, e, 0, 0)),
                  pl.BlockSpec((None, None, ngt, GATHER_TOKENS), lambda s, e, c, l: (s, e, 0, 0)),
                  pl.BlockSpec((None, None, D_MODEL, EXPERT_FF), lambda s, e, c, l: (l[0], e, 0, 0)),
                  pl.BlockSpec((None, None, D_MODEL, EXPERT_FF), lambda s, e, c, l: (l[0], e, 0, 0)),
                  pl.BlockSpec((None, None, EXPERT_FF, D_MODEL), lambda s, e, c, l: (l[0], e, 0, 0))],
        out_specs=pl.BlockSpec((None, SUPER, D_MODEL), lambda s, e, c, l: (s, 0, 0), pipeline_mode=pl.Buffered(1)),
        scratch_shapes=[pltpu.VMEM((nslot + BLOCK, D_MODEL), BF16), pltpu.VMEM((nslot + BLOCK, D_MODEL), BF16),
                        pltpu.VMEM((nslot + BLOCK, 1), F32)])
    return pl.pallas_call(
        functools.partial(_expert_kernel, nslot=nslot), grid_spec=grid_spec,
        out_shape=jax.ShapeDtypeStruct((nsup, SUPER, D_MODEL), F32),
        compiler_params=_params(("arbitrary", "arbitrary"), 56),
        name="experts",
    )(cnt, _layer_index(layer), h2, slot, gate, wg, wu, wd)


def _final_kernel(x_ref, moe_ref, mod_ref, g_ref, o_ref):
    x = x_ref[...] + mod_ref[5:6, :] * moe_ref[...]
    o_ref[...] = x * lax.rsqrt(jnp.mean(x * x, axis=-1, keepdims=True) + EPS) * g_ref[...]


def _final(x1, moe, mod, g, tm):
    bsz, t, _ = x1.shape
    tok = pl.BlockSpec((None, tm, D_MODEL), lambda b, i: (b, i, 0))
    return pl.pallas_call(
        _final_kernel, grid=(bsz, t // tm),
        in_specs=[tok, tok, _mod_spec(mod), _const_spec((1, D_MODEL))],
        out_specs=tok, out_shape=jax.ShapeDtypeStruct((bsz, t, D_MODEL), F32),
        compiler_params=_params(("parallel", "parallel"), 32),
        name="final_norm",
    )(x1, moe, mod, g)


def _head_perm():
    cols = []
    for m in range(GROUP):
        for kv in range(N_KV):
            head = kv * GROUP + m
            cols.extend(range(head * HEAD_DIM, (head + 1) * HEAD_DIM))
    return jnp.array(cols, jnp.int32)


def _rope_swap(width):
    idx = jnp.arange(width)
    nf = HEAD_DIM // 4
    return jnp.where((idx % (2 * nf)) < nf, idx + nf, idx - nf)


def _rope_tables(t):
    pos = jnp.arange(t)
    row = (pos // GRID_W).astype(F32)
    col = (pos % GRID_W).astype(F32)
    nf = HEAD_DIM // 4
    inv = ROPE_BASE ** (-jnp.arange(nf, dtype=F32) / nf)

    def tabs(p):
        ang = p[:, None] * inv[None, :]
        cos, sin = jnp.cos(ang), jnp.sin(ang)
        return jnp.concatenate([cos, cos], axis=1), jnp.concatenate([-sin, sin], axis=1)

    cr, sr = tabs(row)
    cc, sc = tabs(col)
    cos = jnp.concatenate([cr, cc], axis=1)
    sin = jnp.concatenate([sr, sc], axis=1)
    return jnp.tile(cos, (1, LANES // HEAD_DIM)), jnp.tile(sin, (1, LANES // HEAD_DIM))


def _route_and_experts(h2c, affc, h2l, affl, wg, wu, wd, layer):
    bc, tc, _ = h2c.shape
    bl, tl, _ = h2l.shape
    ntb = SUPER // BLOCK
    capc = CAPACITY * tc // N_EXPERTS
    capl = CAPACITY * tl // N_EXPERTS
    slot_c, cnt_c = _route(affc, capc, capc)
    slot_l, cnt_l = _route(affl, capl, 0)
    tbc = tc // BLOCK
    rows = (SUPER // GATHER_TOKENS, GATHER_TOKENS)
    slot_c = slot_c.transpose(1, 0, 2).reshape(1, N_EXPERTS, *rows)
    gate_c = affc.transpose(1, 0, 2).reshape(1, N_EXPERTS, *rows)
    cnt_c = jnp.concatenate([cnt_c[:, :, :tbc].transpose(1, 0, 2).reshape(N_EXPERTS, ntb),
                             cnt_c[bc - 1, :, tbc:tbc + 1]], axis=1)
    moe_c = _experts(cnt_c.reshape(-1), h2c.reshape(1, SUPER, D_MODEL), slot_c, gate_c, wg, wu, wd, layer)
    moe_l = _experts(cnt_l[:, :, :ntb + 1].reshape(-1), h2l, slot_l.reshape(bl, N_EXPERTS, *rows),
                     affl.reshape(bl, N_EXPERTS, *rows), wg, wu, wd, layer)
    return moe_c.reshape(bc, tc, D_MODEL), moe_l


def kernel(x_prompt, x_sample, cache_k, cache_v, state_ssm_re, state_ssm_im, c, c_ctx, ada_w, ada_b, norm1, norm2,
           final_norm, w_in, conv_w, w_conv_out, attn_sink, w_attn_out, ssm_a_re, ssm_a_im, ssm_log_dt, ssm_b_re,
           ssm_b_im, ssm_c_re, ssm_c_im, ssm_d, w_glu, w_out, router_w, w_gate, w_up, w_down):
    bc, tc, _ = x_prompt.shape
    bl, tl, _ = x_sample.shape
    assert bc * tc == SUPER and tl == SUPER and bl + 1 <= SUBLANES
    hp = _head_perm()
    cvecs = jnp.zeros((SUBLANES, D_MODEL), F32).at[:bl].set(c).at[bl].set(c_ctx)
    mods = _modulation(cvecs, ada_w, ada_b).reshape(DEPTH, SUBLANES, 6, D_MODEL)

    o_q, o_k, o_v, o_u, o_g = 3 * CONV_W, 3 * CONV_W + Q_W, 3 * CONV_W + Q_W + KV_W, 3 * CONV_W + Q_W + 2 * KV_W, \
        3 * CONV_W + Q_W + 2 * KV_W + SSM_W
    w_b = w_in.astype(BF16)
    wq = w_b[:, :, o_q:o_k][:, :, hp]
    wk = w_b[:, :, o_k:o_v]
    w_qkvu = jnp.concatenate([wq, w_b[:, :, o_k:o_g], wq[:, :, _rope_swap(Q_W)], wk[:, :, _rope_swap(KV_W)]], axis=-1)
    w_co = w_conv_out.astype(BF16)
    w_ao = w_attn_out[:, hp, :].astype(BF16)
    w_gl = w_glu.astype(BF16)
    w_o = w_out.astype(BF16)
    rwt = router_w.transpose(0, 2, 1)
    wg, wu, wd = w_gate.astype(BF16), w_up.astype(BF16), w_down.astype(BF16)
    n1 = norm1.reshape(DEPTH, 1, D_MODEL)
    n2 = norm2.reshape(DEPTH, 1, D_MODEL)
    sink = attn_sink.reshape(DEPTH, N_KV, GROUP).transpose(0, 2, 1).reshape(DEPTH, N_HEADS)
    rope_tabs = _rope_tables(tl)
    ssm_w = _ssm_weights(ssm_a_re, ssm_a_im, ssm_log_dt, ssm_b_re, ssm_b_im, ssm_c_re, ssm_c_im, ssm_d)
    h0_all = jnp.stack([state_ssm_re[:, :, 0], state_ssm_im[:, :, 0], state_ssm_re[:, :, 1], state_ssm_im[:, :, 1]],
                       axis=0)
    h0_all = h0_all.reshape(4, bl, DEPTH, SSM_PAIRS, 2 * SSM_P).transpose(2, 3, 0, 1, 4)

    xp, xs = x_prompt, x_sample
    moe_c = moe_l = None
    modp_c = modp_l = None
    ks, vs, fins = [], [], []
    for l in range(DEPTH):
        mod_l = mods[l, :bl]
        mod_c = mods[l, bl:bl + 1]
        outs = _qkvu(xp, moe_c, modp_c, mod_c, n1, w_qkvu, l, None, tc, F32)
        q_c, k_c, v_c, u_c = outs[:4]
        if moe_c is not None:
            xp = outs[4]
        ks.append(k_c)
        vs.append(v_c)
        attn_c = _ctx_attention(sink[l], q_c, k_c, v_c)
        ssm_c, fin = _ssm(u_c, ssm_w, l, None)
        fins.append(fin)
        x1c, h2c, affc = _merge(xp, mod_c, n1, n2, w_b, conv_w, w_co, attn_c, w_ao, ssm_c, w_gl, w_o, rwt, l, tc)
        outs = _qkvu(xs, moe_l, modp_l, mod_l, n1, w_qkvu, l, rope_tabs, 512, BF16)
        q_l, k_l, v_l, u_l = outs[:4]
        if moe_l is not None:
            xs = outs[4]
        attn_l = _lat_attention(sink[l], q_l, k_l, v_l, cache_k[:, l].reshape(bl, -1, KV_W),
                                cache_v[:, l].reshape(bl, -1, KV_W))
        ssm_l, _ = _ssm(u_l, ssm_w, l, h0_all[l])
        x1l, h2l, affl = _merge(xs, mod_l, n1, n2, w_b, conv_w, w_co, attn_l, w_ao, ssm_l, w_gl, w_o, rwt, l, 512)
        moe_c, moe_l = _route_and_experts(h2c, affc, h2l, affl, wg, wu, wd, l)
        xp, xs = x1c, x1l
        modp_c, modp_l = mod_c, mod_l

    fn = final_norm.reshape(1, D_MODEL)
    y_prompt = _final(xp, moe_c, modp_c, fn, tc)
    y_sample = _final(xs, moe_l, modp_l, fn, 512)
    new_k = jnp.stack(ks, axis=1).reshape(bc, DEPTH, tc, N_KV, HEAD_DIM)
    new_v = jnp.stack(vs, axis=1).reshape(bc, DEPTH, tc, N_KV, HEAD_DIM)
    fin = jnp.stack(fins, axis=0)
    fin = fin.reshape(DEPTH, SSM_PAIRS, 2, 2, bc, 2, SSM_P).transpose(3, 4, 0, 2, 1, 5, 6)
    fin = fin.reshape(2, bc, DEPTH, 2, SSM_G, SSM_P)
    return (y_prompt, y_sample, new_k, new_v, fin[0], fin[1])
```

```python
import functools
import math

import jax
import jax.numpy as jnp
from jax import lax
from jax.experimental import pallas as pl
from jax.experimental.pallas import tpu as pltpu

D_MODEL = 1024
DEPTH = 4
GRID_W = 64
CONV_W = 512
N_HEADS = 8
N_KV = 2
HEAD_DIM = 64
GROUP = N_HEADS // N_KV
Q_W = N_HEADS * HEAD_DIM
KV_W = N_KV * HEAD_DIM
BLOCK = 128
ROPE_BASE = 10000.0
NEG_INF = -1e30
SSM_W = 512
SSM_GC = 16
SSM_G = SSM_W // SSM_GC
SSM_P = 64
N_BRANCH = 3
N_EXPERTS = 16
EXPERT_FF = 1024
CAPACITY = 2
EPS = 1e-6

F32 = jnp.float32
BF16 = jnp.bfloat16
HIGHEST = lax.Precision.HIGHEST

LANES = 128
SUBLANES = 8
SSM_CHUNK = 16
SSM_PAIRS = SSM_G // 2
SUPER = 4096
GATHER_TOKENS = 512
MIB = 1024 * 1024


def _params(sem, vmem_mib):
    return pltpu.CompilerParams(dimension_semantics=sem, vmem_limit_bytes=vmem_mib * MIB)


def _const_spec(shape):
    nd = len(shape)
    return pl.BlockSpec(shape, lambda *_: (0,) * nd, pipeline_mode=pl.Buffered(1))


def _layer_spec(arr):
    nd = arr.ndim
    return pl.BlockSpec((None,) + arr.shape[1:], lambda *idx: (idx[-1][0],) + (0,) * (nd - 1),
                        pipeline_mode=pl.Buffered(1))


def _layer_index(layer):
    return jnp.full((1,), layer, jnp.int32)


def _sigmoid(x):
    return 1.0 / (1.0 + jnp.exp(-x))


def _rms_mod(x, g, sc, sh):
    y = x * lax.rsqrt(jnp.mean(x * x, axis=-1, keepdims=True) + EPS)
    return (y * g) * (1.0 + sc) + sh


def _mod_kernel(c_ref, w_ref, b_ref, o_ref):
    cv = c_ref[...]
    s = cv * _sigmoid(cv)
    o_ref[...] = jnp.dot(s, w_ref[...], precision=HIGHEST, preferred_element_type=F32) + b_ref[...]


def _modulation(cvecs, ada_w, ada_b):
    nt = 1536
    return pl.pallas_call(
        _mod_kernel,
        grid=(DEPTH, 6 * D_MODEL // nt),
        in_specs=[pl.BlockSpec((SUBLANES, D_MODEL), lambda l, j: (0, 0)),
                  pl.BlockSpec((None, D_MODEL, nt), lambda l, j: (l, 0, j)),
                  pl.BlockSpec((None, 1, nt), lambda l, j: (l, 0, j))],
        out_specs=pl.BlockSpec((None, SUBLANES, nt), lambda l, j: (l, 0, j)),
        out_shape=jax.ShapeDtypeStruct((DEPTH, SUBLANES, 6 * D_MODEL), F32),
        compiler_params=_params(("parallel", "parallel"), 32),
        name="modulation",
    )(cvecs, ada_w, ada_b.reshape(DEPTH, 1, 6 * D_MODEL))


def _qkvu_kernel(*refs, rope, fuse_res):
    it = iter(refs[1:])
    x_ref = next(it)
    if fuse_res:
        moe_ref = next(it)
        modp_ref = next(it)
    mod_ref = next(it)
    n1_ref = next(it)
    w_ref = next(it)
    if rope:
        cos_ref = next(it)
        sin_ref = next(it)
    q_ref, k_ref, v_ref, u_ref = next(it), next(it), next(it), next(it)
    x = x_ref[...]
    if fuse_res:
        xo_ref = next(it)
        x = x + modp_ref[5:6, :] * moe_ref[...]
        xo_ref[...] = x
    h = _rms_mod(x, n1_ref[...], mod_ref[1:2, :], mod_ref[0:1, :]).astype(BF16)
    ncol = w_ref.shape[1] if rope else Q_W + 2 * KV_W + SSM_W
    p = jnp.dot(h, w_ref[:, 0:ncol], preferred_element_type=F32)
    q = p[:, 0:Q_W]
    k = p[:, Q_W:Q_W + KV_W]
    if rope:
        cos = cos_ref[...]
        sin = sin_ref[...]
        o = Q_W + 2 * KV_W + SSM_W
        q = jnp.concatenate(
            [q[:, m * LANES:(m + 1) * LANES] * cos + p[:, o + m * LANES:o + (m + 1) * LANES] * sin
             for m in range(Q_W // LANES)], axis=1)
        k = k * cos + p[:, o + Q_W:o + Q_W + KV_W] * sin
    q_ref[...] = (q * HEAD_DIM ** -0.5).astype(q_ref.dtype)
    k_ref[...] = k.astype(k_ref.dtype)
    v_ref[...] = p[:, Q_W + KV_W:Q_W + 2 * KV_W].astype(v_ref.dtype)
    u_ref[...] = p[:, Q_W + 2 * KV_W:Q_W + 2 * KV_W + SSM_W]


def _mod_spec(mod):
    if mod.shape[0] == 1:
        return pl.BlockSpec((None, 6, D_MODEL), lambda b, i, *_: (0, 0, 0))
    return pl.BlockSpec((None, 6, D_MODEL), lambda b, i, *_: (b, 0, 0))


def _qkvu(x, moe, modp, mod, n1, w, layer, rope_tabs, tm, kv_dtype):
    bsz, t, _ = x.shape
    rope = rope_tabs is not None
    fuse_res = moe is not None
    tok = lambda wd: pl.BlockSpec((None, tm, wd), lambda b, i, *_: (b, i, 0))
    in_specs, args = [tok(D_MODEL)], [x]
    if fuse_res:
        in_specs += [tok(D_MODEL), _mod_spec(modp)]
        args += [moe, modp]
    in_specs += [_mod_spec(mod), _layer_spec(n1), _layer_spec(w)]
    args += [mod, n1, w]
    if rope:
        in_specs += [pl.BlockSpec((tm, LANES), lambda b, i, *_: (i, 0))] * 2
        args += list(rope_tabs)
    out_specs = [tok(Q_W), tok(KV_W), tok(KV_W), tok(SSM_W)]
    out_shape = [jax.ShapeDtypeStruct((bsz, t, Q_W), BF16),
                 jax.ShapeDtypeStruct((bsz, t, KV_W), kv_dtype),
                 jax.ShapeDtypeStruct((bsz, t, KV_W), kv_dtype),
                 jax.ShapeDtypeStruct((bsz, t, SSM_W), F32)]
    if fuse_res:
        out_specs.append(tok(D_MODEL))
        out_shape.append(jax.ShapeDtypeStruct((bsz, t, D_MODEL), F32))
    grid_spec = pltpu.PrefetchScalarGridSpec(num_scalar_prefetch=1, grid=(bsz, t // tm), in_specs=in_specs,
                                             out_specs=out_specs)
    return pl.pallas_call(
        functools.partial(_qkvu_kernel, rope=rope, fuse_res=fuse_res), grid_spec=grid_spec, out_shape=out_shape,
        compiler_params=_params(("parallel", "parallel"), 48),
        name="qkvu",
    )(_layer_index(layer), *args)


def _stack_heads(q_ref):
    lo = lax.broadcasted_iota(jnp.int32, (1, LANES), 1) < HEAD_DIM
    keep_lo = jnp.where(lo, 1.0, 0.0).astype(BF16)
    keep_hi = jnp.where(lo, 0.0, 1.0).astype(BF16)
    parts = []
    for m in range(GROUP):
        qm = q_ref[:, m * LANES:(m + 1) * LANES]
        parts.append(qm * keep_lo)
        parts.append(qm * keep_hi)
    return jnp.concatenate(parts, axis=0), lo


def _attend(s, bias, sink_ref, v, tq, lo, o_ref):
    ps, dens = [], []
    for h in range(N_HEADS):
        sh = s[h * tq:(h + 1) * tq]
        if bias is not None:
            sh = sh + bias
        sink = sink_ref[h]
        mx = jnp.maximum(jnp.max(sh, axis=-1, keepdims=True), sink)
        p = jnp.exp(sh - mx)
        dens.append(jnp.sum(p, axis=-1, keepdims=True) + jnp.exp(sink - mx))
        ps.append(p.astype(BF16))
    o = jnp.dot(jnp.concatenate(ps, axis=0), v, preferred_element_type=F32)
    for m in range(GROUP):
        o0 = o[(2 * m) * tq:(2 * m + 1) * tq] / dens[2 * m]
        o1 = o[(2 * m + 1) * tq:(2 * m + 2) * tq] / dens[2 * m + 1]
        o_ref[:, m * LANES:(m + 1) * LANES] = jnp.where(lo, o0, o1).astype(o_ref.dtype)


def _ctx_attn_kernel(sink_ref, q_ref, k_ref, v_ref, o_ref):
    tq = q_ref.shape[0]
    qx, lo = _stack_heads(q_ref)
    s = lax.dot_general(qx, k_ref[...].astype(BF16), (((1,), (1,)), ((), ())), preferred_element_type=F32)
    _attend(s, None, sink_ref, v_ref[...].astype(BF16), tq, lo, o_ref)


def _ctx_attention(sink, q, k, v):
    bsz, t, _ = q.shape
    tok = lambda wd: pl.BlockSpec((None, t, wd), lambda b: (b, 0, 0))
    return pl.pallas_call(
        _ctx_attn_kernel, grid=(bsz,),
        in_specs=[pl.BlockSpec(memory_space=pltpu.SMEM), tok(Q_W), tok(KV_W), tok(KV_W)],
        out_specs=tok(Q_W), out_shape=jax.ShapeDtypeStruct((bsz, t, Q_W), BF16),
        compiler_params=_params(("parallel",), 32),
        name="ctx_attention",
    )(sink, q, k, v)


def _lat_attn_kernel(sink_ref, q_ref, kc_ref, vc_ref, kp_ref, k0_ref, kn_ref, vp_ref, v0_ref, vn_ref, o_ref):
    tq = q_ref.shape[0]
    i = pl.program_id(1)
    nb = pl.num_programs(1)
    qx, lo = _stack_heads(q_ref)
    kall = jnp.concatenate([kc_ref[...].astype(BF16), kp_ref[...], k0_ref[...], kn_ref[...]], axis=0)
    vall = jnp.concatenate([vc_ref[...].astype(BF16), vp_ref[...], v0_ref[...], vn_ref[...]], axis=0)
    s = lax.dot_general(qx, kall, (((1,), (1,)), ((), ())), preferred_element_type=F32)
    past = kc_ref.shape[0]
    r = lax.broadcasted_iota(jnp.int32, (tq, BLOCK), 0)
    c = lax.broadcasted_iota(jnp.int32, (tq, BLOCK), 1)
    m_prev = jnp.where((c >= r) & (i > 0), 0.0, NEG_INF)
    m_next = jnp.where((c <= r) & (i < nb - 1), 0.0, NEG_INF)
    bias = jnp.concatenate([jnp.zeros((tq, past), F32), m_prev, jnp.zeros((tq, BLOCK), F32), m_next], axis=1)
    _attend(s, bias, sink_ref, vall, tq, lo, o_ref)


def _lat_attention(sink, q, k, v, kc, vc):
    bsz, t, _ = q.shape
    nb = t // BLOCK
    past = kc.shape[1]
    tok = lambda wd: pl.BlockSpec((None, BLOCK, wd), lambda b, i: (b, i, 0))
    prev = pl.BlockSpec((None, BLOCK, KV_W), lambda b, i: (b, jnp.maximum(i - 1, 0), 0))
    nxt = pl.BlockSpec((None, BLOCK, KV_W), lambda b, i: (b, jnp.minimum(i + 1, nb - 1), 0))
    ctx = pl.BlockSpec((None, past, KV_W), lambda b, i: (b, 0, 0))
    return pl.pallas_call(
        _lat_attn_kernel, grid=(bsz, nb),
        in_specs=[pl.BlockSpec(memory_space=pltpu.SMEM), tok(Q_W), ctx, ctx,
                  prev, tok(KV_W), nxt, prev, tok(KV_W), nxt],
        out_specs=tok(Q_W), out_shape=jax.ShapeDtypeStruct((bsz, t, Q_W), BF16),
        compiler_params=_params(("parallel", "parallel"), 32),
        name="lat_attention",
    )(sink, q, kc, vc, k, k, k, v, v, v)


SSM_QUAD = LANES // (2 * SSM_GC)
GRANULES = LANES // SSM_GC


def _ssm_kernel(l_ref, u_ref, strip_ref, wst_ref, w2_ref, a_ref, h0_ref, d_ref, y_ref, fin_ref, pk_ref, z_ref,
                ent_ref, toep_ref, *, bsz, t):
    del l_ref
    nk = t // SSM_CHUNK
    gran = lax.broadcasted_iota(jnp.int32, (1, LANES), 1) // SSM_GC
    half = SSM_CHUNK // GRANULES

    ck = min(nk, 4 * SUBLANES)

    def transpose_granules(xs):
        xs = list(xs)
        for d in (4, 2, 1):
            low = (gran & d) == 0
            for i in range(GRANULES):
                if not i & d:
                    a, b = xs[i], xs[i + d]
                    xs[i] = jnp.where(low, a, pltpu.roll(b, SSM_GC * d, axis=1))
                    xs[i + d] = jnp.where(low, pltpu.roll(a, LANES - SSM_GC * d, axis=1), b)
        return xs

    def pack(it, carry):
        b, k0 = it // (nk // ck), (it % (nk // ck)) * ck
        for sh in range(half):
            xs = [u_ref[pl.ds(b * t + k0 * SSM_CHUNK + sh * GRANULES + s8, ck, stride=SSM_CHUNK), :]
                  for s8 in range(GRANULES)]
            for g8, x in enumerate(transpose_granules(xs)):
                pk_ref[g8 // 2, (g8 % 2) * half + sh, pl.ds(k0 * bsz + b, ck, stride=bsz), :] = x
        return carry

    lax.fori_loop(0, bsz * (nk // ck), pack, 0)

    rows = max(bsz, SUBLANES)
    steps = rows // bsz
    niter = nk // steps
    ns = SSM_W

    def step(sr, si, dr, di, ar, ai):
        return ar * sr - ai * si + dr, ar * si + ai * sr + di

    for p in range(SSM_QUAD):
        u = jnp.concatenate([pk_ref[p, col] for col in range(2 * half)], axis=1)
        ub = u.astype(BF16)
        gw = SSM_CHUNK * SSM_GC
        for gl in range(2):
            strip = strip_ref[p, gl]
            for s in range(SSM_CHUNK):
                off = (SSM_CHUNK - 1 - s) * SSM_GC
                blk = pltpu.roll(strip, 2 * gw - off, axis=1) if off else strip
                toep_ref[s * SSM_GC:(s + 1) * SSM_GC, :] = blk[:, 0:gw].astype(BF16)
            z_ref[:, gl * gw:(gl + 1) * gw] = jnp.dot(ub[:, gl * gw:(gl + 1) * gw], toep_ref[...],
                                                      preferred_element_type=F32)
        z_ref[:, ns:ns + 4 * LANES] = jnp.dot(ub, wst_ref[p], preferred_element_type=F32)
        afr, afi, abr, abi = a_ref[p, 0:1, :], a_ref[p, 1:2, :], a_ref[p, 2:3, :], a_ref[p, 3:4, :]

        def body(it, carry):
            sr, si, gr, gi = carry
            rf = pl.multiple_of(it * rows, SUBLANES)
            rb = pl.multiple_of((niter - 1 - it) * rows, SUBLANES)
            dfr, dfi = z_ref[pl.ds(rf, rows), ns:ns + LANES], z_ref[pl.ds(rf, rows), ns + LANES:ns + 2 * LANES]
            dbr = z_ref[pl.ds(rb, rows), ns + 2 * LANES:ns + 3 * LANES]
            dbi = z_ref[pl.ds(rb, rows), ns + 3 * LANES:ns + 4 * LANES]
            efr, efi, ebr, ebi = [], [], [None] * steps, [None] * steps
            for j in range(steps):
                efr.append(sr)
                efi.append(si)
                sr, si = step(sr, si, dfr[j * bsz:(j + 1) * bsz], dfi[j * bsz:(j + 1) * bsz], afr, afi)
            for j in reversed(range(steps)):
                ebr[j] = gr
                ebi[j] = gi
                gr, gi = step(gr, gi, dbr[j * bsz:(j + 1) * bsz], dbi[j * bsz:(j + 1) * bsz], abr, abi)
            cat = lambda xs: xs[0] if len(xs) == 1 else jnp.concatenate(xs, axis=0)
            ent_ref[pl.ds(rf, rows), 0:LANES] = cat(efr)
            ent_ref[pl.ds(rf, rows), LANES:2 * LANES] = cat(efi)
            ent_ref[pl.ds(rb, rows), 2 * LANES:3 * LANES] = cat(ebr)
            ent_ref[pl.ds(rb, rows), 3 * LANES:4 * LANES] = cat(ebi)
            return sr, si, gr, gi

        init = (h0_ref[p, 0], h0_ref[p, 1], h0_ref[p, 2], h0_ref[p, 3])
        sr, si, gr, gi = lax.fori_loop(0, niter, body, init)
        fin_ref[p, 0] = sr
        fin_ref[p, 1] = si
        fin_ref[p, 2] = gr
        fin_ref[p, 3] = gi
        y = (z_ref[:, 0:SSM_W] + jnp.dot(ent_ref[...].astype(BF16), w2_ref[p], preferred_element_type=F32)
             + d_ref[p] * u)
        for col in range(2 * half):
            pk_ref[p, col] = y[:, col * LANES:(col + 1) * LANES]

    def unpack(it, carry):
        b, k0 = it // (nk // ck), (it % (nk // ck)) * ck
        for sh in range(half):
            xs = [pk_ref[g8 // 2, (g8 % 2) * half + sh, pl.ds(k0 * bsz + b, ck, stride=bsz), :] for g8 in range(GRANULES)]
            for s8, x in enumerate(transpose_granules(xs)):
                y_ref[pl.ds(b * t + k0 * SSM_CHUNK + sh * GRANULES + s8, ck, stride=SSM_CHUNK), :] = x
        return carry

    lax.fori_loop(0, bsz * (nk // ck), unpack, 0)


def _ssm(u, ssm_w, layer, h0):
    bsz, t, _ = u.shape
    strip, wst, w2, a16, dflat = ssm_w
    rows = (t // SSM_CHUNK) * bsz
    if h0 is None:
        h0 = jnp.zeros((SSM_PAIRS, 4, bsz, LANES), F32)
    gw = SSM_CHUNK * SSM_GC
    quad = lambda *tail: pl.BlockSpec((SSM_QUAD,) + tail, lambda q, lref: (q,) + (0,) * len(tail))
    lquad = lambda *tail: pl.BlockSpec((None, SSM_QUAD) + tail, lambda q, lref: (lref[0], q) + (0,) * len(tail))
    tok = pl.BlockSpec((bsz * t, LANES), lambda q, lref: (0, q), pipeline_mode=pl.Buffered(1))
    grid_spec = pltpu.PrefetchScalarGridSpec(
        num_scalar_prefetch=1, grid=(SSM_PAIRS // SSM_QUAD,),
        in_specs=[tok, lquad(2, SSM_GC, 2 * gw), lquad(SSM_W, 4 * LANES), lquad(4 * LANES, SSM_W), lquad(4, LANES),
                  quad(4, bsz, LANES), lquad(1, SSM_W)],
        out_specs=[tok, quad(4, bsz, LANES)],
        scratch_shapes=[pltpu.VMEM((SSM_QUAD, SSM_W // LANES, rows, LANES), F32),
                        pltpu.VMEM((rows, SSM_W + 4 * LANES), F32), pltpu.VMEM((rows, 4 * LANES), F32),
                        pltpu.VMEM((gw, gw), BF16)])
    y, fin = pl.pallas_call(
        functools.partial(_ssm_kernel, bsz=bsz, t=t), grid_spec=grid_spec,
        out_shape=[jax.ShapeDtypeStruct((bsz * t, SSM_W), F32),
                   jax.ShapeDtypeStruct((SSM_PAIRS, 4, bsz, LANES), F32)],
        compiler_params=_params(("parallel",), 52),
        name="ssm",
    )(_layer_index(layer), u.reshape(bsz * t, SSM_W), strip, wst, w2, a16, h0, dflat)
    return y.reshape(bsz, t, SSM_W), fin


def _ssm_weights(a_re, a_im, log_dt, b_re, b_im, c_re, c_im, ssm_d):
    L = SSM_CHUNK
    nl = a_re.shape[0]
    dt = jnp.exp(log_dt)[..., None, None]
    steps = jnp.arange(L + 1, dtype=F32)
    mag = jnp.exp(a_re[..., None] * dt * steps)
    ang = a_im[..., None] * dt * steps
    pr, pi = mag * jnp.cos(ang), mag * jnp.sin(ang)
    nr, ni = pr[..., 1] - 1.0, pi[..., 1]
    den = a_re * a_re + a_im * a_im
    qr, qi = (nr * a_re + ni * a_im) / den, (ni * a_re - nr * a_im) / den
    swap = lambda x: x.transpose(0, 1, 2, 4, 3)
    bbr = swap(qr[..., None] * b_re - qi[..., None] * b_im)
    bbi = swap(qr[..., None] * b_im + qi[..., None] * b_re)
    ctr, cti = swap(c_re), swap(c_im)
    cpr = ctr[..., None, :] * pr[..., :L, None] - cti[..., None, :] * pi[..., :L, None]
    cpi = ctr[..., None, :] * pi[..., :L, None] + cti[..., None, :] * pr[..., :L, None]
    flat = lambda x: x.reshape(nl, 2, SSM_G, SSM_P, L * SSM_GC)
    kern = (jnp.einsum('ldgkp,ldgpn->ldgkn', bbr, flat(cpr), precision=HIGHEST)
            - jnp.einsum('ldgkp,ldgpn->ldgkn', bbi, flat(cpi), precision=HIGHEST))
    kern = kern.reshape(nl, 2, SSM_G, SSM_GC, L, SSM_GC)
    kf, kb = kern[:, 0], kern[:, 1]
    lagk = jnp.concatenate([kb[..., :0:-1, :], kf[..., 0:1, :] + kb[..., 0:1, :], kf[..., 1:, :]], axis=-2)
    strip = jnp.pad(lagk.reshape(nl, SSM_G, SSM_GC, (2 * L - 1) * SSM_GC), ((0, 0), (0, 0), (0, 0), (0, SSM_GC)))
    strip = strip.reshape(nl, SSM_PAIRS, 2, SSM_GC, 2 * L * SSM_GC)
    idx = jnp.arange(L)

    def pair(w):
        w = w.reshape((nl, SSM_PAIRS, 2) + w.shape[2:])
        even, odd = w[:, :, 0], w[:, :, 1]
        zero = jnp.zeros_like(even)
        return jnp.stack([jnp.concatenate([even, zero], axis=-1), jnp.concatenate([zero, odd], axis=-1)], axis=2)

    def inject(kpow, d):
        ar = pr[:, d].transpose(0, 1, 3, 2)[:, :, kpow][:, :, :, None, :]
        ai = pi[:, d].transpose(0, 1, 3, 2)[:, :, kpow][:, :, :, None, :]
        br, bi = bbr[:, d][:, :, None], bbi[:, d][:, :, None]
        return [pair(w).reshape(nl, SSM_PAIRS, 2 * L * SSM_GC, 2 * SSM_P) for w in (ar * br - ai * bi, ar * bi + ai * br)]

    wst = jnp.concatenate(inject(L - 1 - idx, 0) + inject(idx, 1), axis=-1)

    def readout(kpow, d):
        ar, ai = pr[:, d][..., kpow][..., None], pi[:, d][..., kpow][..., None]
        cr, ci = ctr[:, d][:, :, :, None, :], cti[:, d][:, :, :, None, :]
        planes = (cr * ar - ci * ai, -(cr * ai + ci * ar))
        return [pair(w.reshape(nl, SSM_G, SSM_P, L * SSM_GC)).reshape(nl, SSM_PAIRS, 2 * SSM_P, 2 * L * SSM_GC)
                for w in planes]

    w2 = jnp.concatenate(readout(1 + idx, 0) + readout(L - idx, 1), axis=2)
    plane = lambda x: x.reshape(nl, SSM_PAIRS, 2 * SSM_P)
    a16 = jnp.stack([plane(pr[:, 0, :, :, L]), plane(pi[:, 0, :, :, L]), plane(pr[:, 1, :, :, L]),
                     plane(pi[:, 1, :, :, L])], axis=2)
    dflat = jnp.broadcast_to(ssm_d.reshape(nl, SSM_PAIRS, 2, 1, SSM_GC), (nl, SSM_PAIRS, 2, L, SSM_GC))
    return strip, wst.astype(BF16), w2.astype(BF16), a16, dflat.reshape(nl, SSM_PAIRS, 1, 2 * L * SSM_GC)


def _merge_kernel(l_ref, x_ref, xp_ref, xn_ref, mod_ref, n1_ref, n2_ref, wcg_ref, cw_ref, wco_ref, attn_ref, wao_ref,
                  ssm_ref, wglu_ref, wout_ref, rw_ref, x1_ref, h2_ref, aff_ref):
    del l_ref
    tm = x_ref.shape[0]
    i = pl.program_id(1)
    nt = pl.num_programs(1)
    x = x_ref[...]
    n1 = n1_ref[...]
    sh1, sc1, g1 = mod_ref[0:1, :], mod_ref[1:2, :], mod_ref[2:3, :]
    sh2, sc2 = mod_ref[3:4, :], mod_ref[4:5, :]
    cw = CONV_W

    def conv_in(h):
        xin = jnp.dot(h, wcg_ref[:, 0:cw], preferred_element_type=F32)
        cg = jnp.dot(h, wcg_ref[:, 2 * cw:3 * cw], preferred_element_type=F32)
        return cg * xin

    x_ext = jnp.concatenate([x, xp_ref[...], xn_ref[...]], axis=0)
    h_ext = _rms_mod(x_ext, n1, sc1, sh1).astype(BF16)
    h = h_ext[0:tm]
    z_ext = conv_in(h_ext)
    z = z_ext[0:tm]
    z_before = jnp.where(i > 0, z_ext[tm + SUBLANES - 1:tm + SUBLANES, :], 0.0)
    z_after = jnp.where(i < nt - 1, z_ext[tm + SUBLANES:tm + SUBLANES + 1, :], 0.0)
    rows = lax.broadcasted_iota(jnp.int32, (tm, 1), 0)
    zl = jnp.where(rows == 0, z_before, pltpu.roll(z, 1, axis=0))
    zr = jnp.where(rows == tm - 1, z_after, pltpu.roll(z, tm - 1, axis=0))
    y = zl * cw_ref[0:1, :] + z * cw_ref[1:2, :] + zr * cw_ref[2:3, :]
    bg = jnp.dot(h, wcg_ref[:, cw:2 * cw], preferred_element_type=F32)
    conv_y = jnp.dot((bg * y).astype(BF16), wco_ref[...], preferred_element_type=F32)
    o = wcg_ref.shape[1] - N_BRANCH * D_MODEL
    ga = _sigmoid(jnp.dot(h, wcg_ref[:, o:o + D_MODEL], preferred_element_type=F32))
    mixed = ga * conv_y
    attn_y = jnp.dot(attn_ref[...], wao_ref[...], preferred_element_type=F32)
    gb = _sigmoid(jnp.dot(h, wcg_ref[:, o + D_MODEL:o + 2 * D_MODEL], preferred_element_type=F32))
    mixed = mixed + gb * attn_y
    s = ssm_ref[...]
    gelu = s * (0.5 * (1.0 + jnp.tanh(math.sqrt(2.0 / math.pi) * (s + 0.044715 * (s * s * s)))))
    zab = jnp.dot(gelu.astype(BF16), wglu_ref[...], preferred_element_type=F32)
    ssm_y = zab[:, 0:D_MODEL] * _sigmoid(zab[:, D_MODEL:2 * D_MODEL])
    gc = _sigmoid(jnp.dot(h, wcg_ref[:, o + 2 * D_MODEL:o + 3 * D_MODEL], preferred_element_type=F32))
    mixed = mixed + gc * ssm_y
    x1 = x + g1 * jnp.dot(mixed.astype(BF16), wout_ref[...], preferred_element_type=F32)
    x1_ref[...] = x1
    h2 = _rms_mod(x1, n2_ref[...], sc2, sh2)
    h2_ref[...] = h2.astype(BF16)
    h_hi = h2.astype(BF16)
    h_lo = (h2 - h_hi.astype(F32)).astype(BF16)
    rw = rw_ref[...]
    r_hi = rw.astype(BF16)
    r_lo = (rw - r_hi.astype(F32)).astype(BF16)
    nt_dims = (((1,), (1,)), ((), ()))
    a = lax.dot_general(jnp.concatenate([r_hi, r_lo], axis=0), h_hi, nt_dims, preferred_element_type=F32)
    b = lax.dot_general(r_hi, h_lo, nt_dims, preferred_element_type=F32)
    logits = a[0:N_EXPERTS] + a[N_EXPERTS:2 * N_EXPERTS] + b
    e = jnp.exp(logits - jnp.max(logits, axis=0, keepdims=True))
    aff_ref[...] = e / jnp.sum(e, axis=0, keepdims=True)


def _merge(x, mod, n1, n2, wcg, cw, wco, attn, wao, ssm, wglu, wout, rwt, layer, tm):
    bsz, t, _ = x.shape
    nt = t // tm
    per = tm // SUBLANES
    tok = lambda wd: pl.BlockSpec((None, tm, wd), lambda b, i, *_: (b, i, 0))
    prev = pl.BlockSpec((None, SUBLANES, D_MODEL), lambda b, i, *_: (b, jnp.maximum(i * per - 1, 0), 0))
    nxt = pl.BlockSpec((None, SUBLANES, D_MODEL),
                       lambda b, i, *_: (b, jnp.minimum((i + 1) * per, t // SUBLANES - 1), 0))
    lay = _layer_spec
    grid_spec = pltpu.PrefetchScalarGridSpec(
        num_scalar_prefetch=1, grid=(bsz, nt),
        in_specs=[tok(D_MODEL), prev, nxt, _mod_spec(mod), lay(n1), lay(n2), lay(wcg), lay(cw), lay(wco), tok(Q_W),
                  lay(wao), tok(SSM_W), lay(wglu), lay(wout), lay(rwt)],
        out_specs=[tok(D_MODEL), tok(D_MODEL), pl.BlockSpec((None, N_EXPERTS, tm), lambda b, i, *_: (b, 0, i))])
    return pl.pallas_call(
        _merge_kernel, grid_spec=grid_spec,
        out_shape=[jax.ShapeDtypeStruct((bsz, t, D_MODEL), F32), jax.ShapeDtypeStruct((bsz, t, D_MODEL), BF16),
                   jax.ShapeDtypeStruct((bsz, N_EXPERTS, t), F32)],
        compiler_params=_params(("parallel", "parallel"), 56),
        name="merge",
    )(_layer_index(layer), x, x, x, mod, n1, n2, wcg, cw, wco, attn, wao, ssm, wglu, wout, rwt)


def _route_kernel(aff_ref, slot_ref, cnt_ref, *, cap, slot_stride):
    nrow, n = aff_ref.shape
    bits = pltpu.bitcast(aff_ref[...], jnp.int32)
    thr = jnp.zeros((nrow, 1), jnp.int32)
    for b in range(30, -1, -1):
        cand = thr | (1 << b)
        cnt = jnp.sum(jnp.where(bits >= cand, 1.0, 0.0), axis=1, keepdims=True)
        thr = jnp.where(cnt >= cap, cand, thr)
    need = cap - jnp.sum(jnp.where(bits > thr, 1.0, 0.0), axis=1, keepdims=True)
    ri = lax.broadcasted_iota(jnp.int32, (LANES, LANES), 0)
    ci = lax.broadcasted_iota(jnp.int32, (LANES, LANES), 1)
    tri = jnp.where(ri <= ci, 1.0, 0.0).astype(BF16)
    lane = lax.broadcasted_iota(jnp.int32, (nrow, LANES), 1)
    base = (lax.broadcasted_iota(jnp.int32, (nrow, 1), 0) // N_EXPERTS) * slot_stride
    off_eq = jnp.zeros((nrow, 1), F32)
    off = jnp.zeros((nrow, 1), F32)
    cnts = jnp.zeros((nrow, LANES), jnp.int32)
    for j in range(n // LANES):
        sl = slice(j * LANES, (j + 1) * LANES)
        bj = bits[:, sl]
        eqf = jnp.where(bj == thr, 1.0, 0.0)
        ceq = jnp.dot(eqf.astype(BF16), tri, preferred_element_type=F32) + off_eq
        off_eq = ceq[:, LANES - 1:LANES]
        self_ = jnp.where(bj > thr, 1.0, jnp.where(ceq <= need, eqf, 0.0))
        csel = jnp.dot(self_.astype(BF16), tri, preferred_element_type=F32) + off
        cnts = jnp.where(lane == j, off.astype(jnp.int32) + base, cnts)
        off = csel[:, LANES - 1:LANES]
        slot_ref[:, sl] = jnp.where(self_ > 0.0, csel.astype(jnp.int32) - 1 + base, -1)
    cnt_ref[...] = jnp.where(lane >= n // LANES, off.astype(jnp.int32) + base, cnts)


def _route(aff, cap, slot_stride):
    sets, _, n = aff.shape
    nrow = sets * N_EXPERTS
    slot, cnt = pl.pallas_call(
        functools.partial(_route_kernel, cap=cap, slot_stride=slot_stride), grid=(1,),
        in_specs=[pl.BlockSpec((nrow, n), lambda s: (0, 0))],
        out_specs=[pl.BlockSpec((nrow, n), lambda s: (0, 0)), pl.BlockSpec((nrow, LANES), lambda s: (0, 0))],
        out_shape=[jax.ShapeDtypeStruct((nrow, n), jnp.int32), jax.ShapeDtypeStruct((nrow, LANES), jnp.int32)],
        compiler_params=_params(("arbitrary",), 32),
        name="route",
    )(aff.reshape(nrow, n))
    return slot.reshape(sets, N_EXPERTS, n), cnt.reshape(sets, N_EXPERTS, LANES)


def _expert_kernel(cnt_ref, l_ref, h2_ref, slot_ref, gate_ref, wg_ref, wu_ref, wd_ref, o_ref, xs_ref, ys_ref, gs_ref,
                   *, nslot):
    del l_ref
    s = pl.program_id(0)
    e = pl.program_id(1)
    ntb = SUPER // BLOCK
    nsb = nslot // BLOCK
    per = GATHER_TOKENS // BLOCK
    cbase = (s * N_EXPERTS + e) * (ntb + 1)

    @pl.when(e == 0)
    def _():
        o_ref[...] = jnp.zeros_like(o_ref)

    xs_ref[...] = jnp.zeros_like(xs_ref)
    gs_ref[...] = jnp.zeros_like(gs_ref)
    ys_ref[nslot:nslot + BLOCK, :] = jnp.zeros((BLOCK, D_MODEL), BF16)

    srow = lax.broadcasted_iota(jnp.int32, (BLOCK, GATHER_TOKENS), 0)
    for g in range(SUPER // GATHER_TOKENS):
        slots = slot_ref[g:g + 1, :]
        gates = gate_ref[g:g + 1, :]
        for j in range(nsb):
            @pl.when((cnt_ref[cbase + g * per] < (j + 1) * BLOCK) & (cnt_ref[cbase + (g + 1) * per] > j * BLOCK))
            def _():
                pick = slots == srow + j * BLOCK
                x = jnp.dot(jnp.where(pick, 1.0, 0.0).astype(BF16), h2_ref[g * GATHER_TOKENS:(g + 1) * GATHER_TOKENS, :],
                            preferred_element_type=F32)
                xs_ref[j * BLOCK:(j + 1) * BLOCK, :] += x.astype(BF16)
                gs_ref[j * BLOCK:(j + 1) * BLOCK, :] += jnp.sum(jnp.where(pick, gates, 0.0), axis=1, keepdims=True)

    xs = xs_ref[...]
    hg = jnp.dot(xs, wg_ref[...], preferred_element_type=F32)
    hu = jnp.dot(xs, wu_ref[...], preferred_element_type=F32)
    act = (hg * _sigmoid(hg) * hu).astype(BF16)
    y = jnp.dot(act, wd_ref[...], preferred_element_type=F32) * gs_ref[...]
    ys_ref[0:nslot, :] = y.astype(BF16)

    wrow = lax.broadcasted_iota(jnp.int32, (2 * BLOCK, BLOCK), 0)
    for i in range(ntb):
        w0 = pl.multiple_of(jnp.minimum(cnt_ref[cbase + i] // BLOCK, nsb - 1) * BLOCK, BLOCK)
        slots = slot_ref[i // per:i // per + 1, (i % per) * BLOCK:(i % per + 1) * BLOCK]
        pick = jnp.where(slots - w0 == wrow, 1.0, 0.0).astype(BF16)
        o_ref[i * BLOCK:(i + 1) * BLOCK, :] += lax.dot_general(
            pick, ys_ref[pl.ds(w0, 2 * BLOCK), :], (((0,), (0,)), ((), ())), preferred_element_type=F32)


def _experts(cnt, h2, slot, gate, wg, wu, wd, layer):
    nsup = h2.shape[0]
    nslot = CAPACITY * SUPER // N_EXPERTS
    ngt = SUPER // GATHER_TOKENS
    grid_spec = pltpu.PrefetchScalarGridSpec(
        num_scalar_prefetch=2, grid=(nsup, N_EXPERTS),
        in_specs=[pl.BlockSpec((None, SUPER, D_MODEL), lambda s, e, c, l: (s, 0, 0), pipeline_mode=pl.Buffered(1)),
                  pl.BlockSpec((None, None, ngt, GATHER_TOKENS), lambda s, e, c, l: (s, e, 0, 0)),
                  pl.BlockSpec((None, None, ngt, GATHER_TOKENS), lambda s, e, c, l: (s, e, 0, 0)),
                  pl.BlockSpec((None, None, D_MODEL, EXPERT_FF), lambda s, e, c, l: (l[0], e, 0, 0)),
                  pl.BlockSpec((None, None, D_MODEL, EXPERT_FF), lambda s, e, c, l: (l[0], e, 0, 0)),
                  pl.BlockSpec((None, None, EXPERT_FF, D_MODEL), lambda s, e, c, l: (l[0], e, 0, 0))],
        out_specs=pl.BlockSpec((None, SUPER, D_MODEL), lambda s, e, c, l: (s, 0, 0), pipeline_mode=pl.Buffered(1)),
        scratch_shapes=[pltpu.VMEM((nslot, D_MODEL), BF16), pltpu.VMEM((nslot + BLOCK, D_MODEL), BF16),
                        pltpu.VMEM((nslot, 1), F32)])
    return pl.pallas_call(
        functools.partial(_expert_kernel, nslot=nslot), grid_spec=grid_spec,
        out_shape=jax.ShapeDtypeStruct((nsup, SUPER, D_MODEL), F32),
        compiler_params=_params(("arbitrary", "arbitrary"), 56),
        name="experts",
    )(cnt, _layer_index(layer), h2, slot, gate, wg, wu, wd)


def _final_kernel(x_ref, moe_ref, mod_ref, g_ref, o_ref):
    x = x_ref[...] + mod_ref[5:6, :] * moe_ref[...]
    o_ref[...] = x * lax.rsqrt(jnp.mean(x * x, axis=-1, keepdims=True) + EPS) * g_ref[...]


def _final(x1, moe, mod, g, tm):
    bsz, t, _ = x1.shape
    tok = pl.BlockSpec((None, tm, D_MODEL), lambda b, i: (b, i, 0))
    return pl.pallas_call(
        _final_kernel, grid=(bsz, t // tm),
        in_specs=[tok, tok, _mod_spec(mod), _const_spec((1, D_MODEL))],
        out_specs=tok, out_shape=jax.ShapeDtypeStruct((bsz, t, D_MODEL), F32),
        compiler_params=_params(("parallel", "parallel"), 32),
        name="final_norm",
    )(x1, moe, mod, g)


def _head_perm():
    cols = []
    for m in range(GROUP):
        for kv in range(N_KV):
            head = kv * GROUP + m
            cols.extend(range(head * HEAD_DIM, (head + 1) * HEAD_DIM))
    return jnp.array(cols, jnp.int32)


def _rope_swap(width):
    idx = jnp.arange(width)
    nf = HEAD_DIM // 4
    return jnp.where((idx % (2 * nf)) < nf, idx + nf, idx - nf)


def _rope_tables(t):
    pos = jnp.arange(t)
    row = (pos // GRID_W).astype(F32)
    col = (pos % GRID_W).astype(F32)
    nf = HEAD_DIM // 4
    inv = ROPE_BASE ** (-jnp.arange(nf, dtype=F32) / nf)

    def tabs(p):
        ang = p[:, None] * inv[None, :]
        cos, sin = jnp.cos(ang), jnp.sin(ang)
        return jnp.concatenate([cos, cos], axis=1), jnp.concatenate([-sin, sin], axis=1)

    cr, sr = tabs(row)
    cc, sc = tabs(col)
    cos = jnp.concatenate([cr, cc], axis=1)
    sin = jnp.concatenate([sr, sc], axis=1)
    return jnp.tile(cos, (1, LANES // HEAD_DIM)), jnp.tile(sin, (1, LANES // HEAD_DIM))


def _route_and_experts(h2c, affc, h2l, affl, wg, wu, wd, layer):
    bc, tc, _ = h2c.shape
    bl, tl, _ = h2l.shape
    ntb = SUPER // BLOCK
    capc = CAPACITY * tc // N_EXPERTS
    capl = CAPACITY * tl // N_EXPERTS
    slot_c, cnt_c = _route(affc, capc, capc)
    slot_l, cnt_l = _route(affl, capl, 0)
    tbc = tc // BLOCK
    rows = (SUPER // GATHER_TOKENS, GATHER_TOKENS)
    slot_c = slot_c.transpose(1, 0, 2).reshape(1, N_EXPERTS, *rows)
    gate_c = affc.transpose(1, 0, 2).reshape(1, N_EXPERTS, *rows)
    cnt_c = jnp.concatenate([cnt_c[:, :, :tbc].transpose(1, 0, 2).reshape(N_EXPERTS, ntb),
                             cnt_c[bc - 1, :, tbc:tbc + 1]], axis=1)
    moe_c = _experts(cnt_c.reshape(-1), h2c.reshape(1, SUPER, D_MODEL), slot_c, gate_c, wg, wu, wd, layer)
    moe_l = _experts(cnt_l[:, :, :ntb + 1].reshape(-1), h2l, slot_l.reshape(bl, N_EXPERTS, *rows),
                     affl.reshape(bl, N_EXPERTS, *rows), wg, wu, wd, layer)
    return moe_c.reshape(bc, tc, D_MODEL), moe_l


def kernel(x_prompt, x_sample, cache_k, cache_v, state_ssm_re, state_ssm_im, c, c_ctx, ada_w, ada_b, norm1, norm2,
           final_norm, w_in, conv_w, w_conv_out, attn_sink, w_attn_out, ssm_a_re, ssm_a_im, ssm_log_dt, ssm_b_re,
           ssm_b_im, ssm_c_re, ssm_c_im, ssm_d, w_glu, w_out, router_w, w_gate, w_up, w_down):
    bc, tc, _ = x_prompt.shape
    bl, tl, _ = x_sample.shape
    assert bc * tc == SUPER and tl == SUPER and bl + 1 <= SUBLANES
    hp = _head_perm()
    cvecs = jnp.zeros((SUBLANES, D_MODEL), F32).at[:bl].set(c).at[bl].set(c_ctx)
    mods = _modulation(cvecs, ada_w, ada_b).reshape(DEPTH, SUBLANES, 6, D_MODEL)

    o_q, o_k, o_v, o_u, o_g = 3 * CONV_W, 3 * CONV_W + Q_W, 3 * CONV_W + Q_W + KV_W, 3 * CONV_W + Q_W + 2 * KV_W, \
        3 * CONV_W + Q_W + 2 * KV_W + SSM_W
    w_b = w_in.astype(BF16)
    wq = w_b[:, :, o_q:o_k][:, :, hp]
    wk = w_b[:, :, o_k:o_v]
    w_qkvu = jnp.concatenate([wq, w_b[:, :, o_k:o_g], wq[:, :, _rope_swap(Q_W)], wk[:, :, _rope_swap(KV_W)]], axis=-1)
    w_co = w_conv_out.astype(BF16)
    w_ao = w_attn_out[:, hp, :].astype(BF16)
    w_gl = w_glu.astype(BF16)
    w_o = w_out.astype(BF16)
    rwt = router_w.transpose(0, 2, 1)
    wg, wu, wd = w_gate.astype(BF16), w_up.astype(BF16), w_down.astype(BF16)
    n1 = norm1.reshape(DEPTH, 1, D_MODEL)
    n2 = norm2.reshape(DEPTH, 1, D_MODEL)
    sink = attn_sink.reshape(DEPTH, N_KV, GROUP).transpose(0, 2, 1).reshape(DEPTH, N_HEADS)
    rope_tabs = _rope_tables(tl)
    ssm_w = _ssm_weights(ssm_a_re, ssm_a_im, ssm_log_dt, ssm_b_re, ssm_b_im, ssm_c_re, ssm_c_im, ssm_d)
    h0_all = jnp.stack([state_ssm_re[:, :, 0], state_ssm_im[:, :, 0], state_ssm_re[:, :, 1], state_ssm_im[:, :, 1]],
                       axis=0)
    h0_all = h0_all.reshape(4, bl, DEPTH, SSM_PAIRS, 2 * SSM_P).transpose(2, 3, 0, 1, 4)

    xp, xs = x_prompt, x_sample
    moe_c = moe_l = None
    modp_c = modp_l = None
    ks, vs, fins = [], [], []
    for l in range(DEPTH):
        mod_l = mods[l, :bl]
        mod_c = mods[l, bl:bl + 1]
        outs = _qkvu(xp, moe_c, modp_c, mod_c, n1, w_qkvu, l, None, tc, F32)
        q_c, k_c, v_c, u_c = outs[:4]
        if moe_c is not None:
            xp = outs[4]
        ks.append(k_c)
        vs.append(v_c)
        attn_c = _ctx_attention(sink[l], q_c, k_c, v_c)
        ssm_c, fin = _ssm(u_c, ssm_w, l, None)
        fins.append(fin)
        x1c, h2c, affc = _merge(xp, mod_c, n1, n2, w_b, conv_w, w_co, attn_c, w_ao, ssm_c, w_gl, w_o, rwt, l, tc)
        outs = _qkvu(xs, moe_l, modp_l, mod_l, n1, w_qkvu, l, rope_tabs, 512, BF16)
        q_l, k_l, v_l, u_l = outs[:4]
        if moe_l is not None:
            xs = outs[4]
        attn_l = _lat_attention(sink[l], q_l, k_l, v_l, cache_k[:, l].reshape(bl, -1, KV_W),
                                cache_v[:, l].reshape(bl, -1, KV_W))
        ssm_l, _ = _ssm(u_l, ssm_w, l, h0_all[l])
        x1l, h2l, affl = _merge(xs, mod_l, n1, n2, w_b, conv_w, w_co, attn_l, w_ao, ssm_l, w_gl, w_o, rwt, l, 512)
        moe_c, moe_l = _route_and_experts(h2c, affc, h2l, affl, wg, wu, wd, l)
        xp, xs = x1c, x1l
        modp_c, modp_l = mod_c, mod_l

    fn = final_norm.reshape(1, D_MODEL)
    y_prompt = _final(xp, moe_c, modp_c, fn, tc)
    y_sample = _final(xs, moe_l, modp_l, fn, 512)
    new_k = jnp.stack(ks, axis=1).reshape(bc, DEPTH, tc, N_KV, HEAD_DIM)
    new_v = jnp.stack(vs, axis=1).reshape(bc, DEPTH, tc, N_KV, HEAD_DIM)
    fin = jnp.stack(fins, axis=0)
    fin = fin.reshape(DEPTH, SSM_PAIRS, 2, 2, bc, 2, SSM_P).transpose(3, 4, 0, 2, 1, 5, 6)
    fin = fin.reshape(2, bc, DEPTH, 2, SSM_G, SSM_P)
    return (y_prompt, y_sample, new_k, new_v, fin[0], fin[1])
```

```python
import functools
import math

import jax
import jax.numpy as jnp
from jax import lax
from jax.experimental import pallas as pl
from jax.experimental.pallas import tpu as pltpu

D_MODEL = 1024
DEPTH = 4
GRID_W = 64
CONV_W = 512
N_HEADS = 8
N_KV = 2
HEAD_DIM = 64
GROUP = N_HEADS // N_KV
Q_W = N_HEADS * HEAD_DIM
KV_W = N_KV * HEAD_DIM
BLOCK = 128
ROPE_BASE = 10000.0
NEG_INF = -1e30
SSM_W = 512
SSM_GC = 16
SSM_G = SSM_W // SSM_GC
SSM_P = 64
N_BRANCH = 3
N_EXPERTS = 16
EXPERT_FF = 1024
CAPACITY = 2
EPS = 1e-6

F32 = jnp.float32
BF16 = jnp.bfloat16
HIGHEST = lax.Precision.HIGHEST

LANES = 128
SUBLANES = 8
SSM_CHUNK = 16
SSM_PAIRS = SSM_G // 2
SUPER = 4096
GATHER_TOKENS = 512
MIB = 1024 * 1024


def _params(sem, vmem_mib):
    return pltpu.CompilerParams(dimension_semantics=sem, vmem_limit_bytes=vmem_mib * MIB)


def _const_spec(shape):
    nd = len(shape)
    return pl.BlockSpec(shape, lambda *_: (0,) * nd, pipeline_mode=pl.Buffered(1))


def _layer_spec(arr):
    nd = arr.ndim
    return pl.BlockSpec((None,) + arr.shape[1:], lambda *idx: (idx[-1][0],) + (0,) * (nd - 1),
                        pipeline_mode=pl.Buffered(1))


def _layer_index(layer):
    return jnp.full((1,), layer, jnp.int32)


def _sigmoid(x):
    return 0.5 * jnp.tanh(0.5 * x) + 0.5


def _rms_mod(x, g, sc, sh):
    y = x * lax.rsqrt(jnp.mean(x * x, axis=-1, keepdims=True) + EPS)
    return (y * g) * (1.0 + sc) + sh


def _mod_kernel(c_ref, w_ref, b_ref, o_ref):
    cv = c_ref[...]
    s = cv * _sigmoid(cv)
    o_ref[...] = jnp.dot(s, w_ref[...], precision=HIGHEST, preferred_element_type=F32) + b_ref[...]


def _modulation(cvecs, ada_w, ada_b):
    nt = 1536
    return pl.pallas_call(
        _mod_kernel,
        grid=(DEPTH, 6 * D_MODEL // nt),
        in_specs=[pl.BlockSpec((SUBLANES, D_MODEL), lambda l, j: (0, 0)),
                  pl.BlockSpec((None, D_MODEL, nt), lambda l, j: (l, 0, j)),
                  pl.BlockSpec((None, 1, nt), lambda l, j: (l, 0, j))],
        out_specs=pl.BlockSpec((None, SUBLANES, nt), lambda l, j: (l, 0, j)),
        out_shape=jax.ShapeDtypeStruct((DEPTH, SUBLANES, 6 * D_MODEL), F32),
        compiler_params=_params(("parallel", "parallel"), 32),
        name="modulation",
    )(cvecs, ada_w, ada_b.reshape(DEPTH, 1, 6 * D_MODEL))


def _qkvu_kernel(*refs, rope, fuse_res):
    it = iter(refs[1:])
    x_ref = next(it)
    if fuse_res:
        moe_ref = next(it)
        modp_ref = next(it)
    mod_ref = next(it)
    n1_ref = next(it)
    w_ref = next(it)
    if rope:
        cos_ref = next(it)
        sin_ref = next(it)
    q_ref, k_ref, v_ref, u_ref = next(it), next(it), next(it), next(it)
    x = x_ref[...]
    if fuse_res:
        xo_ref = next(it)
        x = x + modp_ref[5:6, :] * moe_ref[...]
        xo_ref[...] = x
    h = _rms_mod(x, n1_ref[...], mod_ref[1:2, :], mod_ref[0:1, :]).astype(BF16)
    ncol = w_ref.shape[1] if rope else Q_W + 2 * KV_W + SSM_W
    p = jnp.dot(h, w_ref[:, 0:ncol], preferred_element_type=F32)
    q = p[:, 0:Q_W]
    k = p[:, Q_W:Q_W + KV_W]
    if rope:
        cos = cos_ref[...]
        sin = sin_ref[...]
        o = Q_W + 2 * KV_W + SSM_W
        q = jnp.concatenate(
            [q[:, m * LANES:(m + 1) * LANES] * cos + p[:, o + m * LANES:o + (m + 1) * LANES] * sin
             for m in range(Q_W // LANES)], axis=1)
        k = k * cos + p[:, o + Q_W:o + Q_W + KV_W] * sin
    q_ref[...] = (q * HEAD_DIM ** -0.5).astype(q_ref.dtype)
    k_ref[...] = k.astype(k_ref.dtype)
    v_ref[...] = p[:, Q_W + KV_W:Q_W + 2 * KV_W].astype(v_ref.dtype)
    u_ref[...] = p[:, Q_W + 2 * KV_W:Q_W + 2 * KV_W + SSM_W]


def _mod_spec(mod):
    if mod.shape[0] == 1:
        return pl.BlockSpec((None, 6, D_MODEL), lambda b, i, *_: (0, 0, 0))
    return pl.BlockSpec((None, 6, D_MODEL), lambda b, i, *_: (b, 0, 0))


def _qkvu(x, moe, modp, mod, n1, w, layer, rope_tabs, tm, kv_dtype):
    bsz, t, _ = x.shape
    rope = rope_tabs is not None
    fuse_res = moe is not None
    tok = lambda wd: pl.BlockSpec((None, tm, wd), lambda b, i, *_: (b, i, 0))
    in_specs, args = [tok(D_MODEL)], [x]
    if fuse_res:
        in_specs += [tok(D_MODEL), _mod_spec(modp)]
        args += [moe, modp]
    in_specs += [_mod_spec(mod), _layer_spec(n1), _layer_spec(w)]
    args += [mod, n1, w]
    if rope:
        in_specs += [pl.BlockSpec((tm, LANES), lambda b, i, *_: (i, 0))] * 2
        args += list(rope_tabs)
    out_specs = [tok(Q_W), tok(KV_W), tok(KV_W), tok(SSM_W)]
    out_shape = [jax.ShapeDtypeStruct((bsz, t, Q_W), BF16),
                 jax.ShapeDtypeStruct((bsz, t, KV_W), kv_dtype),
                 jax.ShapeDtypeStruct((bsz, t, KV_W), kv_dtype),
                 jax.ShapeDtypeStruct((bsz, t, SSM_W), F32)]
    if fuse_res:
        out_specs.append(tok(D_MODEL))
        out_shape.append(jax.ShapeDtypeStruct((bsz, t, D_MODEL), F32))
    grid_spec = pltpu.PrefetchScalarGridSpec(num_scalar_prefetch=1, grid=(bsz, t // tm), in_specs=in_specs,
                                             out_specs=out_specs)
    return pl.pallas_call(
        functools.partial(_qkvu_kernel, rope=rope, fuse_res=fuse_res), grid_spec=grid_spec, out_shape=out_shape,
        compiler_params=_params(("parallel", "parallel"), 56),
        name="qkvu",
    )(_layer_index(layer), *args)


def _stack_heads(q_ref):
    lo = lax.broadcasted_iota(jnp.int32, (1, LANES), 1) < HEAD_DIM
    keep_lo = jnp.where(lo, 1.0, 0.0).astype(BF16)
    keep_hi = jnp.where(lo, 0.0, 1.0).astype(BF16)
    parts = []
    for m in range(GROUP):
        qm = q_ref[:, m * LANES:(m + 1) * LANES]
        parts.append(qm * keep_lo)
        parts.append(qm * keep_hi)
    return jnp.concatenate(parts, axis=0), lo


def _attend(s, bias, sink_ref, v, tq, lo, o_ref):
    ps, dens = [], []
    for h in range(N_HEADS):
        sh = s[h * tq:(h + 1) * tq]
        if bias is not None:
            sh = sh + bias
        sink = sink_ref[h]
        mx = jnp.maximum(jnp.max(sh, axis=-1, keepdims=True), sink)
        p = jnp.exp(sh - mx)
        dens.append(jnp.sum(p, axis=-1, keepdims=True) + jnp.exp(sink - mx))
        ps.append(p.astype(BF16))
    o = jnp.dot(jnp.concatenate(ps, axis=0), v, preferred_element_type=F32)
    for m in range(GROUP):
        o0 = o[(2 * m) * tq:(2 * m + 1) * tq] / dens[2 * m]
        o1 = o[(2 * m + 1) * tq:(2 * m + 2) * tq] / dens[2 * m + 1]
        o_ref[:, m * LANES:(m + 1) * LANES] = jnp.where(lo, o0, o1).astype(o_ref.dtype)


def _ctx_attn_kernel(sink_ref, q_ref, k_ref, v_ref, o_ref):
    tq = q_ref.shape[0]
    qx, lo = _stack_heads(q_ref)
    s = lax.dot_general(qx, k_ref[...].astype(BF16), (((1,), (1,)), ((), ())), preferred_element_type=F32)
    _attend(s, None, sink_ref, v_ref[...].astype(BF16), tq, lo, o_ref)


def _ctx_attention(sink, q, k, v):
    bsz, t, _ = q.shape
    tok = lambda wd: pl.BlockSpec((None, t, wd), lambda b: (b, 0, 0))
    return pl.pallas_call(
        _ctx_attn_kernel, grid=(bsz,),
        in_specs=[pl.BlockSpec(memory_space=pltpu.SMEM), tok(Q_W), tok(KV_W), tok(KV_W)],
        out_specs=tok(Q_W), out_shape=jax.ShapeDtypeStruct((bsz, t, Q_W), BF16),
        compiler_params=_params(("parallel",), 32),
        name="ctx_attention",
    )(sink, q, k, v)


def _lat_attn_kernel(sink_ref, q_ref, kc_ref, vc_ref, kp_ref, k0_ref, kn_ref, vp_ref, v0_ref, vn_ref, o_ref):
    tq = q_ref.shape[0]
    i = pl.program_id(1)
    nb = pl.num_programs(1)
    qx, lo = _stack_heads(q_ref)
    kall = jnp.concatenate([kc_ref[...].astype(BF16), kp_ref[...], k0_ref[...], kn_ref[...]], axis=0)
    vall = jnp.concatenate([vc_ref[...].astype(BF16), vp_ref[...], v0_ref[...], vn_ref[...]], axis=0)
    s = lax.dot_general(qx, kall, (((1,), (1,)), ((), ())), preferred_element_type=F32)
    past = kc_ref.shape[0]
    r = lax.broadcasted_iota(jnp.int32, (tq, BLOCK), 0)
    c = lax.broadcasted_iota(jnp.int32, (tq, BLOCK), 1)
    m_prev = jnp.where((c >= r) & (i > 0), 0.0, NEG_INF)
    m_next = jnp.where((c <= r) & (i < nb - 1), 0.0, NEG_INF)
    bias = jnp.concatenate([jnp.zeros((tq, past), F32), m_prev, jnp.zeros((tq, BLOCK), F32), m_next], axis=1)
    _attend(s, bias, sink_ref, vall, tq, lo, o_ref)


def _lat_attention(sink, q, k, v, kc, vc):
    bsz, t, _ = q.shape
    nb = t // BLOCK
    past = kc.shape[1]
    tok = lambda wd: pl.BlockSpec((None, BLOCK, wd), lambda b, i: (b, i, 0))
    prev = pl.BlockSpec((None, BLOCK, KV_W), lambda b, i: (b, jnp.maximum(i - 1, 0), 0))
    nxt = pl.BlockSpec((None, BLOCK, KV_W), lambda b, i: (b, jnp.minimum(i + 1, nb - 1), 0))
    ctx = pl.BlockSpec((None, past, KV_W), lambda b, i: (b, 0, 0))
    return pl.pallas_call(
        _lat_attn_kernel, grid=(bsz, nb),
        in_specs=[pl.BlockSpec(memory_space=pltpu.SMEM), tok(Q_W), ctx, ctx,
                  prev, tok(KV_W), nxt, prev, tok(KV_W), nxt],
        out_specs=tok(Q_W), out_shape=jax.ShapeDtypeStruct((bsz, t, Q_W), BF16),
        compiler_params=_params(("parallel", "parallel"), 32),
        name="lat_attention",
    )(sink, q, kc, vc, k, k, k, v, v, v)


SSM_QUAD = LANES // (2 * SSM_GC)
GRANULES = LANES // SSM_GC


def _ssm_kernel(l_ref, u_ref, strip_ref, wst_ref, w2_ref, a_ref, h0_ref, d_ref, y_ref, fin_ref, pk_ref, z_ref,
                ent_ref, toep_ref, *, bsz, t):
    del l_ref
    nk = t // SSM_CHUNK
    gran = lax.broadcasted_iota(jnp.int32, (1, LANES), 1) // SSM_GC
    half = SSM_CHUNK // GRANULES

    ck = min(nk, 4 * SUBLANES)

    def transpose_granules(xs):
        xs = list(xs)
        for d in (4, 2, 1):
            low = (gran & d) == 0
            for i in range(GRANULES):
                if not i & d:
                    a, b = xs[i], xs[i + d]
                    xs[i] = jnp.where(low, a, pltpu.roll(b, SSM_GC * d, axis=1))
                    xs[i + d] = jnp.where(low, pltpu.roll(a, LANES - SSM_GC * d, axis=1), b)
        return xs

    def pack(it, carry):
        b, k0 = it // (nk // ck), (it % (nk // ck)) * ck
        for sh in range(half):
            xs = [u_ref[pl.ds(b * t + k0 * SSM_CHUNK + sh * GRANULES + s8, ck, stride=SSM_CHUNK), :]
                  for s8 in range(GRANULES)]
            for g8, x in enumerate(transpose_granules(xs)):
                pk_ref[g8 // 2, (g8 % 2) * half + sh, pl.ds(k0 * bsz + b, ck, stride=bsz), :] = x
        return carry

    lax.fori_loop(0, bsz * (nk // ck), pack, 0)

    rows = max(bsz, SUBLANES)
    steps = rows // bsz
    niter = nk // steps
    ns = SSM_W

    def step(sr, si, dr, di, ar, ai):
        return ar * sr - ai * si + dr, ar * si + ai * sr + di

    for p in range(SSM_QUAD):
        u = jnp.concatenate([pk_ref[p, col] for col in range(2 * half)], axis=1)
        ub = u.astype(BF16)
        gw = SSM_CHUNK * SSM_GC
        for gl in range(2):
            strip = strip_ref[p, gl]
            for s in range(SSM_CHUNK):
                off = (SSM_CHUNK - 1 - s) * SSM_GC
                blk = pltpu.roll(strip, 2 * gw - off, axis=1) if off else strip
                toep_ref[s * SSM_GC:(s + 1) * SSM_GC, :] = blk[:, 0:gw].astype(BF16)
            z_ref[:, gl * gw:(gl + 1) * gw] = jnp.dot(ub[:, gl * gw:(gl + 1) * gw], toep_ref[...],
                                                      preferred_element_type=F32)
        z_ref[:, ns:ns + 4 * LANES] = jnp.dot(ub, wst_ref[p], preferred_element_type=F32)
        afr, afi, abr, abi = a_ref[p, 0:1, :], a_ref[p, 1:2, :], a_ref[p, 2:3, :], a_ref[p, 3:4, :]

        def body(it, carry):
            sr, si, gr, gi = carry
            rf = pl.multiple_of(it * rows, SUBLANES)
            rb = pl.multiple_of((niter - 1 - it) * rows, SUBLANES)
            dfr, dfi = z_ref[pl.ds(rf, rows), ns:ns + LANES], z_ref[pl.ds(rf, rows), ns + LANES:ns + 2 * LANES]
            dbr = z_ref[pl.ds(rb, rows), ns + 2 * LANES:ns + 3 * LANES]
            dbi = z_ref[pl.ds(rb, rows), ns + 3 * LANES:ns + 4 * LANES]
            efr, efi, ebr, ebi = [], [], [None] * steps, [None] * steps
            for j in range(steps):
                efr.append(sr)
                efi.append(si)
                sr, si = step(sr, si, dfr[j * bsz:(j + 1) * bsz], dfi[j * bsz:(j + 1) * bsz], afr, afi)
            for j in reversed(range(steps)):
                ebr[j] = gr
                ebi[j] = gi
                gr, gi = step(gr, gi, dbr[j * bsz:(j + 1) * bsz], dbi[j * bsz:(j + 1) * bsz], abr, abi)
            cat = lambda xs: xs[0] if len(xs) == 1 else jnp.concatenate(xs, axis=0)
            ent_ref[pl.ds(rf, rows), 0:LANES] = cat(efr)
            ent_ref[pl.ds(rf, rows), LANES:2 * LANES] = cat(efi)
            ent_ref[pl.ds(rb, rows), 2 * LANES:3 * LANES] = cat(ebr)
            ent_ref[pl.ds(rb, rows), 3 * LANES:4 * LANES] = cat(ebi)
            return sr, si, gr, gi

        init = (h0_ref[p, 0], h0_ref[p, 1], h0_ref[p, 2], h0_ref[p, 3])
        sr, si, gr, gi = lax.fori_loop(0, niter, body, init)
        fin_ref[p, 0] = sr
        fin_ref[p, 1] = si
        fin_ref[p, 2] = gr
        fin_ref[p, 3] = gi
        y = (z_ref[:, 0:SSM_W] + jnp.dot(ent_ref[...].astype(BF16), w2_ref[p], preferred_element_type=F32)
             + d_ref[p] * u)
        for col in range(2 * half):
            pk_ref[p, col] = y[:, col * LANES:(col + 1) * LANES]

    def unpack(it, carry):
        b, k0 = it // (nk // ck), (it % (nk // ck)) * ck
        for sh in range(half):
            xs = [pk_ref[g8 // 2, (g8 % 2) * half + sh, pl.ds(k0 * bsz + b, ck, stride=bsz), :] for g8 in range(GRANULES)]
            for s8, x in enumerate(transpose_granules(xs)):
                y_ref[pl.ds(b * t + k0 * SSM_CHUNK + sh * GRANULES + s8, ck, stride=SSM_CHUNK), :] = x
        return carry

    lax.fori_loop(0, bsz * (nk // ck), unpack, 0)


def _ssm(u, ssm_w, layer, h0):
    bsz, t, _ = u.shape
    strip, wst, w2, a16, dflat = ssm_w
    rows = (t // SSM_CHUNK) * bsz
    if h0 is None:
        h0 = jnp.zeros((SSM_PAIRS, 4, bsz, LANES), F32)
    gw = SSM_CHUNK * SSM_GC
    quad = lambda *tail: pl.BlockSpec((SSM_QUAD,) + tail, lambda q, lref: (q,) + (0,) * len(tail))
    lquad = lambda *tail: pl.BlockSpec((None, SSM_QUAD) + tail, lambda q, lref: (lref[0], q) + (0,) * len(tail))
    tok = pl.BlockSpec((bsz * t, LANES), lambda q, lref: (0, q), pipeline_mode=pl.Buffered(1))
    grid_spec = pltpu.PrefetchScalarGridSpec(
        num_scalar_prefetch=1, grid=(SSM_PAIRS // SSM_QUAD,),
        in_specs=[tok, lquad(2, SSM_GC, 2 * gw), lquad(SSM_W, 4 * LANES), lquad(4 * LANES, SSM_W), lquad(4, LANES),
                  quad(4, bsz, LANES), lquad(1, SSM_W)],
        out_specs=[tok, quad(4, bsz, LANES)],
        scratch_shapes=[pltpu.VMEM((SSM_QUAD, SSM_W // LANES, rows, LANES), F32),
                        pltpu.VMEM((rows, SSM_W + 4 * LANES), F32), pltpu.VMEM((rows, 4 * LANES), F32),
                        pltpu.VMEM((gw, gw), BF16)])
    y, fin = pl.pallas_call(
        functools.partial(_ssm_kernel, bsz=bsz, t=t), grid_spec=grid_spec,
        out_shape=[jax.ShapeDtypeStruct((bsz * t, SSM_W), F32),
                   jax.ShapeDtypeStruct((SSM_PAIRS, 4, bsz, LANES), F32)],
        compiler_params=_params(("parallel",), 52),
        name="ssm",
    )(_layer_index(layer), u.reshape(bsz * t, SSM_W), strip, wst, w2, a16, h0, dflat)
    return y.reshape(bsz, t, SSM_W), fin


def _ssm_weights(a_re, a_im, log_dt, b_re, b_im, c_re, c_im, ssm_d):
    L = SSM_CHUNK
    nl = a_re.shape[0]
    dt = jnp.exp(log_dt)[..., None, None]
    steps = jnp.arange(L + 1, dtype=F32)
    mag = jnp.exp(a_re[..., None] * dt * steps)
    ang = a_im[..., None] * dt * steps
    pr, pi = mag * jnp.cos(ang), mag * jnp.sin(ang)
    nr, ni = pr[..., 1] - 1.0, pi[..., 1]
    den = a_re * a_re + a_im * a_im
    qr, qi = (nr * a_re + ni * a_im) / den, (ni * a_re - nr * a_im) / den
    swap = lambda x: x.transpose(0, 1, 2, 4, 3)
    bbr = swap(qr[..., None] * b_re - qi[..., None] * b_im)
    bbi = swap(qr[..., None] * b_im + qi[..., None] * b_re)
    ctr, cti = swap(c_re), swap(c_im)
    cpr = ctr[..., None, :] * pr[..., :L, None] - cti[..., None, :] * pi[..., :L, None]
    cpi = ctr[..., None, :] * pi[..., :L, None] + cti[..., None, :] * pr[..., :L, None]
    flat = lambda x: x.reshape(nl, 2, SSM_G, SSM_P, L * SSM_GC)
    kern = (jnp.einsum('ldgkp,ldgpn->ldgkn', bbr, flat(cpr), precision=HIGHEST)
            - jnp.einsum('ldgkp,ldgpn->ldgkn', bbi, flat(cpi), precision=HIGHEST))
    kern = kern.reshape(nl, 2, SSM_G, SSM_GC, L, SSM_GC)
    kf, kb = kern[:, 0], kern[:, 1]
    lagk = jnp.concatenate([kb[..., :0:-1, :], kf[..., 0:1, :] + kb[..., 0:1, :], kf[..., 1:, :]], axis=-2)
    strip = jnp.pad(lagk.reshape(nl, SSM_G, SSM_GC, (2 * L - 1) * SSM_GC), ((0, 0), (0, 0), (0, 0), (0, SSM_GC)))
    strip = strip.reshape(nl, SSM_PAIRS, 2, SSM_GC, 2 * L * SSM_GC)
    idx = jnp.arange(L)

    def pair(w):
        w = w.reshape((nl, SSM_PAIRS, 2) + w.shape[2:])
        even, odd = w[:, :, 0], w[:, :, 1]
        zero = jnp.zeros_like(even)
        return jnp.stack([jnp.concatenate([even, zero], axis=-1), jnp.concatenate([zero, odd], axis=-1)], axis=2)

    def inject(kpow, d):
        ar = pr[:, d].transpose(0, 1, 3, 2)[:, :, kpow][:, :, :, None, :]
        ai = pi[:, d].transpose(0, 1, 3, 2)[:, :, kpow][:, :, :, None, :]
        br, bi = bbr[:, d][:, :, None], bbi[:, d][:, :, None]
        return [pair(w).reshape(nl, SSM_PAIRS, 2 * L * SSM_GC, 2 * SSM_P) for w in (ar * br - ai * bi, ar * bi + ai * br)]

    wst = jnp.concatenate(inject(L - 1 - idx, 0) + inject(idx, 1), axis=-1)

    def readout(kpow, d):
        ar, ai = pr[:, d][..., kpow][..., None], pi[:, d][..., kpow][..., None]
        cr, ci = ctr[:, d][:, :, :, None, :], cti[:, d][:, :, :, None, :]
        planes = (cr * ar - ci * ai, -(cr * ai + ci * ar))
        return [pair(w.reshape(nl, SSM_G, SSM_P, L * SSM_GC)).reshape(nl, SSM_PAIRS, 2 * SSM_P, 2 * L * SSM_GC)
                for w in planes]

    w2 = jnp.concatenate(readout(1 + idx, 0) + readout(L - idx, 1), axis=2)
    plane = lambda x: x.reshape(nl, SSM_PAIRS, 2 * SSM_P)
    a16 = jnp.stack([plane(pr[:, 0, :, :, L]), plane(pi[:, 0, :, :, L]), plane(pr[:, 1, :, :, L]),
                     plane(pi[:, 1, :, :, L])], axis=2)
    dflat = jnp.broadcast_to(ssm_d.reshape(nl, SSM_PAIRS, 2, 1, SSM_GC), (nl, SSM_PAIRS, 2, L, SSM_GC))
    return strip, wst.astype(BF16), w2.astype(BF16), a16, dflat.reshape(nl, SSM_PAIRS, 1, 2 * L * SSM_GC)


def _merge_kernel(l_ref, x_ref, xp_ref, xn_ref, mod_ref, n1_ref, n2_ref, wcg_ref, cw_ref, wco_ref, attn_ref, wao_ref,
                  ssm_ref, wglu_ref, wout_ref, rw_ref, x1_ref, h2_ref, aff_ref):
    del l_ref
    tm = x_ref.shape[0]
    i = pl.program_id(1)
    nt = pl.num_programs(1)
    x = x_ref[...]
    n1 = n1_ref[...]
    sh1, sc1, g1 = mod_ref[0:1, :], mod_ref[1:2, :], mod_ref[2:3, :]
    sh2, sc2 = mod_ref[3:4, :], mod_ref[4:5, :]
    cw = CONV_W

    def conv_in(h):
        xin = jnp.dot(h, wcg_ref[:, 0:cw], preferred_element_type=F32)
        cg = jnp.dot(h, wcg_ref[:, 2 * cw:3 * cw], preferred_element_type=F32)
        return cg * xin

    x_ext = jnp.concatenate([x, xp_ref[...], xn_ref[...]], axis=0)
    h_ext = _rms_mod(x_ext, n1, sc1, sh1).astype(BF16)
    h = h_ext[0:tm]
    z_ext = conv_in(h_ext)
    z = z_ext[0:tm]
    z_before = jnp.where(i > 0, z_ext[tm + SUBLANES - 1:tm + SUBLANES, :], 0.0)
    z_after = jnp.where(i < nt - 1, z_ext[tm + SUBLANES:tm + SUBLANES + 1, :], 0.0)
    rows = lax.broadcasted_iota(jnp.int32, (tm, 1), 0)
    zl = jnp.where(rows == 0, z_before, pltpu.roll(z, 1, axis=0))
    zr = jnp.where(rows == tm - 1, z_after, pltpu.roll(z, tm - 1, axis=0))
    y = zl * cw_ref[0:1, :] + z * cw_ref[1:2, :] + zr * cw_ref[2:3, :]
    bg = jnp.dot(h, wcg_ref[:, cw:2 * cw], preferred_element_type=F32)
    conv_y = jnp.dot((bg * y).astype(BF16), wco_ref[...], preferred_element_type=F32)
    o = wcg_ref.shape[1] - N_BRANCH * D_MODEL
    ga = _sigmoid(jnp.dot(h, wcg_ref[:, o:o + D_MODEL], preferred_element_type=F32))
    mixed = ga * conv_y
    attn_y = jnp.dot(attn_ref[...], wao_ref[...], preferred_element_type=F32)
    gb = _sigmoid(jnp.dot(h, wcg_ref[:, o + D_MODEL:o + 2 * D_MODEL], preferred_element_type=F32))
    mixed = mixed + gb * attn_y
    s = ssm_ref[...]
    gelu = s * (0.5 * (1.0 + jnp.tanh(math.sqrt(2.0 / math.pi) * (s + 0.044715 * (s * s * s)))))
    zab = jnp.dot(gelu.astype(BF16), wglu_ref[...], preferred_element_type=F32)
    ssm_y = zab[:, 0:D_MODEL] * _sigmoid(zab[:, D_MODEL:2 * D_MODEL])
    gc = _sigmoid(jnp.dot(h, wcg_ref[:, o + 2 * D_MODEL:o + 3 * D_MODEL], preferred_element_type=F32))
    mixed = mixed + gc * ssm_y
    x1 = x + g1 * jnp.dot(mixed.astype(BF16), wout_ref[...], preferred_element_type=F32)
    x1_ref[...] = x1
    h2 = _rms_mod(x1, n2_ref[...], sc2, sh2)
    h2_ref[...] = h2.astype(BF16)
    h_hi = h2.astype(BF16)
    h_lo = (h2 - h_hi.astype(F32)).astype(BF16)
    rw = rw_ref[...]
    r_hi = rw.astype(BF16)
    r_lo = (rw - r_hi.astype(F32)).astype(BF16)
    nt_dims = (((1,), (1,)), ((), ()))
    a = lax.dot_general(jnp.concatenate([r_hi, r_lo], axis=0), h_hi, nt_dims, preferred_element_type=F32)
    b = lax.dot_general(r_hi, h_lo, nt_dims, preferred_element_type=F32)
    logits = a[0:N_EXPERTS] + a[N_EXPERTS:2 * N_EXPERTS] + b
    e = jnp.exp(logits - jnp.max(logits, axis=0, keepdims=True))
    aff_ref[...] = e / jnp.sum(e, axis=0, keepdims=True)


def _merge(x, mod, n1, n2, wcg, cw, wco, attn, wao, ssm, wglu, wout, rwt, layer, tm):
    bsz, t, _ = x.shape
    nt = t // tm
    per = tm // SUBLANES
    tok = lambda wd: pl.BlockSpec((None, tm, wd), lambda b, i, *_: (b, i, 0))
    prev = pl.BlockSpec((None, SUBLANES, D_MODEL), lambda b, i, *_: (b, jnp.maximum(i * per - 1, 0), 0))
    nxt = pl.BlockSpec((None, SUBLANES, D_MODEL),
                       lambda b, i, *_: (b, jnp.minimum((i + 1) * per, t // SUBLANES - 1), 0))
    lay = _layer_spec
    grid_spec = pltpu.PrefetchScalarGridSpec(
        num_scalar_prefetch=1, grid=(bsz, nt),
        in_specs=[tok(D_MODEL), prev, nxt, _mod_spec(mod), lay(n1), lay(n2), lay(wcg), lay(cw), lay(wco), tok(Q_W),
                  lay(wao), tok(SSM_W), lay(wglu), lay(wout), lay(rwt)],
        out_specs=[tok(D_MODEL), tok(D_MODEL), pl.BlockSpec((None, N_EXPERTS, tm), lambda b, i, *_: (b, 0, i))])
    return pl.pallas_call(
        _merge_kernel, grid_spec=grid_spec,
        out_shape=[jax.ShapeDtypeStruct((bsz, t, D_MODEL), F32), jax.ShapeDtypeStruct((bsz, t, D_MODEL), BF16),
                   jax.ShapeDtypeStruct((bsz, N_EXPERTS, t), F32)],
        compiler_params=_params(("parallel", "parallel"), 56),
        name="merge",
    )(_layer_index(layer), x, x, x, mod, n1, n2, wcg, cw, wco, attn, wao, ssm, wglu, wout, rwt)


def _route_kernel(aff_ref, slot_ref, cnt_ref, *, cap, slot_stride):
    nrow, n = aff_ref.shape
    bits = pltpu.bitcast(aff_ref[...], jnp.int32)
    thr = jnp.zeros((nrow, 1), jnp.int32)
    for b in range(30, -1, -1):
        cand = thr | (1 << b)
        cnt = jnp.sum(jnp.where(bits >= cand, 1.0, 0.0), axis=1, keepdims=True)
        thr = jnp.where(cnt >= cap, cand, thr)
    need = cap - jnp.sum(jnp.where(bits > thr, 1.0, 0.0), axis=1, keepdims=True)
    ri = lax.broadcasted_iota(jnp.int32, (LANES, LANES), 0)
    ci = lax.broadcasted_iota(jnp.int32, (LANES, LANES), 1)
    tri = jnp.where(ri <= ci, 1.0, 0.0).astype(BF16)
    lane = lax.broadcasted_iota(jnp.int32, (nrow, LANES), 1)
    base = (lax.broadcasted_iota(jnp.int32, (nrow, 1), 0) // N_EXPERTS) * slot_stride
    off_eq = jnp.zeros((nrow, 1), F32)
    off = jnp.zeros((nrow, 1), F32)
    cnts = jnp.zeros((nrow, LANES), jnp.int32)
    for j in range(n // LANES):
        sl = slice(j * LANES, (j + 1) * LANES)
        bj = bits[:, sl]
        eqf = jnp.where(bj == thr, 1.0, 0.0)
        ceq = jnp.dot(eqf.astype(BF16), tri, preferred_element_type=F32) + off_eq
        off_eq = ceq[:, LANES - 1:LANES]
        self_ = jnp.where(bj > thr, 1.0, jnp.where(ceq <= need, eqf, 0.0))
        csel = jnp.dot(self_.astype(BF16), tri, preferred_element_type=F32) + off
        cnts = jnp.where(lane == j, off.astype(jnp.int32) + base, cnts)
        off = csel[:, LANES - 1:LANES]
        slot_ref[:, sl] = jnp.where(self_ > 0.0, csel.astype(jnp.int32) - 1 + base, -1)
    cnt_ref[...] = jnp.where(lane >= n // LANES, off.astype(jnp.int32) + base, cnts)


def _route(aff, cap, slot_stride):
    sets, _, n = aff.shape
    nrow = sets * N_EXPERTS
    slot, cnt = pl.pallas_call(
        functools.partial(_route_kernel, cap=cap, slot_stride=slot_stride), grid=(1,),
        in_specs=[pl.BlockSpec((nrow, n), lambda s: (0, 0))],
        out_specs=[pl.BlockSpec((nrow, n), lambda s: (0, 0)), pl.BlockSpec((nrow, LANES), lambda s: (0, 0))],
        out_shape=[jax.ShapeDtypeStruct((nrow, n), jnp.int32), jax.ShapeDtypeStruct((nrow, LANES), jnp.int32)],
        compiler_params=_params(("arbitrary",), 32),
        name="route",
    )(aff.reshape(nrow, n))
    return slot.reshape(sets, N_EXPERTS, n), cnt.reshape(sets, N_EXPERTS, LANES)


def _expert_kernel(cnt_ref, l_ref, h2_ref, slot_ref, gate_ref, wg_ref, wu_ref, wd_ref, o_ref, xs_ref, ys_ref, gs_ref,
                   *, nslot):
    del l_ref
    s = pl.program_id(0)
    e = pl.program_id(1)
    ntb = SUPER // BLOCK
    nsb = nslot // BLOCK
    per = GATHER_TOKENS // BLOCK
    cbase = (s * N_EXPERTS + e) * (ntb + 1)

    @pl.when(e == 0)
    def _():
        o_ref[...] = jnp.zeros_like(o_ref)

    xs_ref[...] = jnp.zeros_like(xs_ref)
    gs_ref[...] = jnp.zeros_like(gs_ref)
    ys_ref[nslot:nslot + BLOCK, :] = jnp.zeros((BLOCK, D_MODEL), BF16)

    srow = lax.broadcasted_iota(jnp.int32, (BLOCK, GATHER_TOKENS), 0)
    for g in range(SUPER // GATHER_TOKENS):
        slots = slot_ref[g:g + 1, :]
        gates = gate_ref[g:g + 1, :]
        for j in range(nsb):
            @pl.when((cnt_ref[cbase + g * per] < (j + 1) * BLOCK) & (cnt_ref[cbase + (g + 1) * per] > j * BLOCK))
            def _():
                pick = slots == srow + j * BLOCK
                x = jnp.dot(jnp.where(pick, 1.0, 0.0).astype(BF16), h2_ref[g * GATHER_TOKENS:(g + 1) * GATHER_TOKENS, :],
                            preferred_element_type=F32)
                xs_ref[j * BLOCK:(j + 1) * BLOCK, :] += x.astype(BF16)
                gs_ref[j * BLOCK:(j + 1) * BLOCK, :] += jnp.sum(jnp.where(pick, gates, 0.0), axis=1, keepdims=True)

    xs = xs_ref[...]
    hg = jnp.dot(xs, wg_ref[...], preferred_element_type=F32)
    hu = jnp.dot(xs, wu_ref[...], preferred_element_type=F32)
    act = (hg * _sigmoid(hg) * hu).astype(BF16)
    y = jnp.dot(act, wd_ref[...], preferred_element_type=F32) * gs_ref[...]
    ys_ref[0:nslot, :] = y.astype(BF16)

    wrow = lax.broadcasted_iota(jnp.int32, (2 * BLOCK, BLOCK), 0)
    for i in range(ntb):
        w0 = pl.multiple_of(jnp.minimum(cnt_ref[cbase + i] // BLOCK, nsb - 1) * BLOCK, BLOCK)
        slots = slot_ref[i // per:i // per + 1, (i % per) * BLOCK:(i % per + 1) * BLOCK]
        pick = jnp.where(slots - w0 == wrow, 1.0, 0.0).astype(BF16)
        o_ref[i * BLOCK:(i + 1) * BLOCK, :] += lax.dot_general(
            pick, ys_ref[pl.ds(w0, 2 * BLOCK), :], (((0,), (0,)), ((), ())), preferred_element_type=F32)


def _experts(cnt, h2, slot, gate, wg, wu, wd, layer):
    nsup = h2.shape[0]
    nslot = CAPACITY * SUPER // N_EXPERTS
    ngt = SUPER // GATHER_TOKENS
    grid_spec = pltpu.PrefetchScalarGridSpec(
        num_scalar_prefetch=2, grid=(nsup, N_EXPERTS),
        in_specs=[pl.BlockSpec((None, SUPER, D_MODEL), lambda s, e, c, l: (s, 0, 0)),
                  pl.BlockSpec((None, None, ngt, GATHER_TOKENS), lambda s, e, c, l: (s, e, 0, 0)),
                  pl.BlockSpec((None, None, ngt, GATHER_TOKENS), lambda s, e, c, l: (s, e, 0, 0)),
                  pl.BlockSpec((None, None, D_MODEL, EXPERT_FF), lambda s, e, c, l: (l[0], e, 0, 0)),
                  pl.BlockSpec((None, None, D_MODEL, EXPERT_FF), lambda s, e, c, l: (l[0], e, 0, 0)),
                  pl.BlockSpec((None, None, EXPERT_FF, D_MODEL), lambda s, e, c, l: (l[0], e, 0, 0))],
        out_specs=pl.BlockSpec((None, SUPER, D_MODEL), lambda s, e, c, l: (s, 0, 0), pipeline_mode=pl.Buffered(1)),
        scratch_shapes=[pltpu.VMEM((nslot, D_MODEL), BF16), pltpu.VMEM((nslot + BLOCK, D_MODEL), BF16),
                        pltpu.VMEM((nslot, 1), F32)])
    return pl.pallas_call(
        functools.partial(_expert_kernel, nslot=nslot), grid_spec=grid_spec,
        out_shape=jax.ShapeDtypeStruct((nsup, SUPER, D_MODEL), F32),
        compiler_params=_params(("arbitrary", "arbitrary"), 56),
        name="experts",
    )(cnt, _layer_index(layer), h2, slot, gate, wg, wu, wd)


def _final_kernel(x_ref, moe_ref, mod_ref, g_ref, o_ref):
    x = x_ref[...] + mod_ref[5:6, :] * moe_ref[...]
    o_ref[...] = x * lax.rsqrt(jnp.mean(x * x, axis=-1, keepdims=True) + EPS) * g_ref[...]


def _final(x1, moe, mod, g, tm):
    bsz, t, _ = x1.shape
    tok = pl.BlockSpec((None, tm, D_MODEL), lambda b, i: (b, i, 0))
    return pl.pallas_call(
        _final_kernel, grid=(bsz, t // tm),
        in_specs=[tok, tok, _mod_spec(mod), _const_spec((1, D_MODEL))],
        out_specs=tok, out_shape=jax.ShapeDtypeStruct((bsz, t, D_MODEL), F32),
        compiler_params=_params(("parallel", "parallel"), 32),
        name="final_norm",
    )(x1, moe, mod, g)


def _head_perm():
    cols = []
    for m in range(GROUP):
        for kv in range(N_KV):
            head = kv * GROUP + m
            cols.extend(range(head * HEAD_DIM, (head + 1) * HEAD_DIM))
    return jnp.array(cols, jnp.int32)


def _rope_swap(width):
    idx = jnp.arange(width)
    nf = HEAD_DIM // 4
    return jnp.where((idx % (2 * nf)) < nf, idx + nf, idx - nf)


def _rope_tables(t):
    pos = jnp.arange(t)
    row = (pos // GRID_W).astype(F32)
    col = (pos % GRID_W).astype(F32)
    nf = HEAD_DIM // 4
    inv = ROPE_BASE ** (-jnp.arange(nf, dtype=F32) / nf)

    def tabs(p):
        ang = p[:, None] * inv[None, :]
        cos, sin = jnp.cos(ang), jnp.sin(ang)
        return jnp.concatenate([cos, cos], axis=1), jnp.concatenate([-sin, sin], axis=1)

    cr, sr = tabs(row)
    cc, sc = tabs(col)
    cos = jnp.concatenate([cr, cc], axis=1)
    sin = jnp.concatenate([sr, sc], axis=1)
    return jnp.tile(cos, (1, LANES // HEAD_DIM)), jnp.tile(sin, (1, LANES // HEAD_DIM))


def _route_and_experts(h2c, affc, h2l, affl, wg, wu, wd, layer):
    bc, tc, _ = h2c.shape
    bl, tl, _ = h2l.shape
    ntb = SUPER // BLOCK
    capc = CAPACITY * tc // N_EXPERTS
    capl = CAPACITY * tl // N_EXPERTS
    slot_c, cnt_c = _route(affc, capc, capc)
    slot_l, cnt_l = _route(affl, capl, 0)
    tbc = tc // BLOCK
    rows = (SUPER // GATHER_TOKENS, GATHER_TOKENS)
    slot_c = slot_c.transpose(1, 0, 2).reshape(1, N_EXPERTS, *rows)
    gate_c = affc.transpose(1, 0, 2).reshape(1, N_EXPERTS, *rows)
    cnt_c = jnp.concatenate([cnt_c[:, :, :tbc].transpose(1, 0, 2).reshape(N_EXPERTS, ntb),
                             cnt_c[bc - 1, :, tbc:tbc + 1]], axis=1)
    moe_c = _experts(cnt_c.reshape(-1), h2c.reshape(1, SUPER, D_MODEL), slot_c, gate_c, wg, wu, wd, layer)
    moe_l = _experts(cnt_l[:, :, :ntb + 1].reshape(-1), h2l, slot_l.reshape(bl, N_EXPERTS, *rows),
                     affl.reshape(bl, N_EXPERTS, *rows), wg, wu, wd, layer)
    return moe_c.reshape(bc, tc, D_MODEL), moe_l


def kernel(x_prompt, x_sample, cache_k, cache_v, state_ssm_re, state_ssm_im, c, c_ctx, ada_w, ada_b, norm1, norm2,
           final_norm, w_in, conv_w, w_conv_out, attn_sink, w_attn_out, ssm_a_re, ssm_a_im, ssm_log_dt, ssm_b_re,
           ssm_b_im, ssm_c_re, ssm_c_im, ssm_d, w_glu, w_out, router_w, w_gate, w_up, w_down):
    bc, tc, _ = x_prompt.shape
    bl, tl, _ = x_sample.shape
    assert bc * tc == SUPER and tl == SUPER and bl + 1 <= SUBLANES
    hp = _head_perm()
    cvecs = jnp.zeros((SUBLANES, D_MODEL), F32).at[:bl].set(c).at[bl].set(c_ctx)
    mods = _modulation(cvecs, ada_w, ada_b).reshape(DEPTH, SUBLANES, 6, D_MODEL)

    o_q, o_k, o_v, o_u, o_g = 3 * CONV_W, 3 * CONV_W + Q_W, 3 * CONV_W + Q_W + KV_W, 3 * CONV_W + Q_W + 2 * KV_W, \
        3 * CONV_W + Q_W + 2 * KV_W + SSM_W
    w_b = w_in.astype(BF16)
    wq = w_b[:, :, o_q:o_k][:, :, hp]
    wk = w_b[:, :, o_k:o_v]
    w_qkvu = jnp.concatenate([wq, w_b[:, :, o_k:o_g], wq[:, :, _rope_swap(Q_W)], wk[:, :, _rope_swap(KV_W)]], axis=-1)
    w_co = w_conv_out.astype(BF16)
    w_ao = w_attn_out[:, hp, :].astype(BF16)
    w_gl = w_glu.astype(BF16)
    w_o = w_out.astype(BF16)
    rwt = router_w.transpose(0, 2, 1)
    wg, wu, wd = w_gate.astype(BF16), w_up.astype(BF16), w_down.astype(BF16)
    n1 = norm1.reshape(DEPTH, 1, D_MODEL)
    n2 = norm2.reshape(DEPTH, 1, D_MODEL)
    sink = attn_sink.reshape(DEPTH, N_KV, GROUP).transpose(0, 2, 1).reshape(DEPTH, N_HEADS)
    rope_tabs = _rope_tables(tl)
    ssm_w = _ssm_weights(ssm_a_re, ssm_a_im, ssm_log_dt, ssm_b_re, ssm_b_im, ssm_c_re, ssm_c_im, ssm_d)
    h0_all = jnp.stack([state_ssm_re[:, :, 0], state_ssm_im[:, :, 0], state_ssm_re[:, :, 1], state_ssm_im[:, :, 1]],
                       axis=0)
    h0_all = h0_all.reshape(4, bl, DEPTH, SSM_PAIRS, 2 * SSM_P).transpose(2, 3, 0, 1, 4)

    xp, xs = x_prompt, x_sample
    moe_c = moe_l = None
    modp_c = modp_l = None
    ks, vs, fins = [], [], []
    for l in range(DEPTH):
        mod_l = mods[l, :bl]
        mod_c = mods[l, bl:bl + 1]
        outs = _qkvu(xp, moe_c, modp_c, mod_c, n1, w_qkvu, l, None, tc, F32)
        q_c, k_c, v_c, u_c = outs[:4]
        if moe_c is not None:
            xp = outs[4]
        ks.append(k_c)
        vs.append(v_c)
        attn_c = _ctx_attention(sink[l], q_c, k_c, v_c)
        ssm_c, fin = _ssm(u_c, ssm_w, l, None)
        fins.append(fin)
        x1c, h2c, affc = _merge(xp, mod_c, n1, n2, w_b, conv_w, w_co, attn_c, w_ao, ssm_c, w_gl, w_o, rwt, l, tc)
        outs = _qkvu(xs, moe_l, modp_l, mod_l, n1, w_qkvu, l, rope_tabs, 1024, BF16)
        q_l, k_l, v_l, u_l = outs[:4]
        if moe_l is not None:
            xs = outs[4]
        attn_l = _lat_attention(sink[l], q_l, k_l, v_l, cache_k[:, l].reshape(bl, -1, KV_W),
                                cache_v[:, l].reshape(bl, -1, KV_W))
        ssm_l, _ = _ssm(u_l, ssm_w, l, h0_all[l])
        x1l, h2l, affl = _merge(xs, mod_l, n1, n2, w_b, conv_w, w_co, attn_l, w_ao, ssm_l, w_gl, w_o, rwt, l, 512)
        moe_c, moe_l = _route_and_experts(h2c, affc, h2l, affl, wg, wu, wd, l)
        xp, xs = x1c, x1l
        modp_c, modp_l = mod_c, mod_l

    fn = final_norm.reshape(1, D_MODEL)
    y_prompt = _final(xp, moe_c, modp_c, fn, tc)
    y_sample = _final(xs, moe_l, modp_l, fn, 512)
    new_k = jnp.stack(ks, axis=1).reshape(bc, DEPTH, tc, N_KV, HEAD_DIM)
    new_v = jnp.stack(vs, axis=1).reshape(bc, DEPTH, tc, N_KV, HEAD_DIM)
    fin = jnp.stack(fins, axis=0)
    fin = fin.reshape(DEPTH, SSM_PAIRS, 2, 2, bc, 2, SSM_P).transpose(3, 4, 0, 2, 1, 5, 6)
    fin = fin.reshape(2, bc, DEPTH, 2, SSM_G, SSM_P)
    return (y_prompt, y_sample, new_k, new_v, fin[0], fin[1])
```

```python
import functools
import math

import jax
import jax.numpy as jnp
from jax import lax
from jax.experimental import pallas as pl
from jax.experimental.pallas import tpu as pltpu

D_MODEL = 1024
DEPTH = 4
GRID_W = 64
CONV_W = 512
N_HEADS = 8
N_KV = 2
HEAD_DIM = 64
GROUP = N_HEADS // N_KV
Q_W = N_HEADS * HEAD_DIM
KV_W = N_KV * HEAD_DIM
BLOCK = 128
ROPE_BASE = 10000.0
NEG_INF = -1e30
SSM_W = 512
SSM_GC = 16
SSM_G = SSM_W // SSM_GC
SSM_P = 64
N_BRANCH = 3
N_EXPERTS = 16
EXPERT_FF = 1024
CAPACITY = 2
EPS = 1e-6

F32 = jnp.float32
BF16 = jnp.bfloat16
HIGHEST = lax.Precision.HIGHEST

LANES = 128
SUBLANES = 8
SSM_CHUNK = 16
SSM_PAIRS = SSM_G // 2
SUPER = 4096
GATHER_TOKENS = 512
MIB = 1024 * 1024


def _params(sem, vmem_mib):
    return pltpu.CompilerParams(dimension_semantics=sem, vmem_limit_bytes=vmem_mib * MIB)


def _const_spec(shape):
    nd = len(shape)
    return pl.BlockSpec(shape, lambda *_: (0,) * nd, pipeline_mode=pl.Buffered(1))


def _layer_spec(arr):
    nd = arr.ndim
    return pl.BlockSpec((None,) + arr.shape[1:], lambda *idx: (idx[-1][0],) + (0,) * (nd - 1),
                        pipeline_mode=pl.Buffered(1))


def _layer_index(layer):
    return jnp.full((1,), layer, jnp.int32)


def _sigmoid(x):
    return 0.5 * jnp.tanh(0.5 * x) + 0.5


def _rms_mod(x, g, sc, sh):
    y = x * lax.rsqrt(jnp.mean(x * x, axis=-1, keepdims=True) + EPS)
    return (y * g) * (1.0 + sc) + sh


def _mod_kernel(c_ref, w_ref, b_ref, o_ref):
    cv = c_ref[...]
    s = cv * _sigmoid(cv)
    o_ref[...] = jnp.dot(s, w_ref[...], precision=HIGHEST, preferred_element_type=F32) + b_ref[...]


def _modulation(cvecs, ada_w, ada_b):
    nt = 1536
    return pl.pallas_call(
        _mod_kernel,
        grid=(DEPTH, 6 * D_MODEL // nt),
        in_specs=[pl.BlockSpec((SUBLANES, D_MODEL), lambda l, j: (0, 0)),
                  pl.BlockSpec((None, D_MODEL, nt), lambda l, j: (l, 0, j)),
                  pl.BlockSpec((None, 1, nt), lambda l, j: (l, 0, j))],
        out_specs=pl.BlockSpec((None, SUBLANES, nt), lambda l, j: (l, 0, j)),
        out_shape=jax.ShapeDtypeStruct((DEPTH, SUBLANES, 6 * D_MODEL), F32),
        compiler_params=_params(("parallel", "parallel"), 32),
        name="modulation",
    )(cvecs, ada_w, ada_b.reshape(DEPTH, 1, 6 * D_MODEL))


def _qkvu_kernel(*refs, rope, fuse_res):
    it = iter(refs[1:])
    x_ref = next(it)
    if fuse_res:
        moe_ref = next(it)
        modp_ref = next(it)
    mod_ref = next(it)
    n1_ref = next(it)
    w_ref = next(it)
    if rope:
        cos_ref = next(it)
        sin_ref = next(it)
    q_ref, k_ref, v_ref, u_ref = next(it), next(it), next(it), next(it)
    x = x_ref[...]
    if fuse_res:
        xo_ref = next(it)
        x = x + modp_ref[5:6, :] * moe_ref[...]
        xo_ref[...] = x
    h = _rms_mod(x, n1_ref[...], mod_ref[1:2, :], mod_ref[0:1, :]).astype(BF16)
    ncol = w_ref.shape[1] if rope else Q_W + 2 * KV_W + SSM_W
    p = jnp.dot(h, w_ref[:, 0:ncol], preferred_element_type=F32)
    q = p[:, 0:Q_W]
    k = p[:, Q_W:Q_W + KV_W]
    if rope:
        cos = cos_ref[...]
        sin = sin_ref[...]
        o = Q_W + 2 * KV_W + SSM_W
        q = jnp.concatenate(
            [q[:, m * LANES:(m + 1) * LANES] * cos + p[:, o + m * LANES:o + (m + 1) * LANES] * sin
             for m in range(Q_W // LANES)], axis=1)
        k = k * cos + p[:, o + Q_W:o + Q_W + KV_W] * sin
    q_ref[...] = (q * HEAD_DIM ** -0.5).astype(q_ref.dtype)
    k_ref[...] = k.astype(k_ref.dtype)
    v_ref[...] = p[:, Q_W + KV_W:Q_W + 2 * KV_W].astype(v_ref.dtype)
    u_ref[...] = p[:, Q_W + 2 * KV_W:Q_W + 2 * KV_W + SSM_W]


def _mod_spec(mod):
    if mod.shape[0] == 1:
        return pl.BlockSpec((None, 6, D_MODEL), lambda b, i, *_: (0, 0, 0))
    return pl.BlockSpec((None, 6, D_MODEL), lambda b, i, *_: (b, 0, 0))


def _qkvu(x, moe, modp, mod, n1, w, layer, rope_tabs, tm, kv_dtype):
    bsz, t, _ = x.shape
    rope = rope_tabs is not None
    fuse_res = moe is not None
    tok = lambda wd: pl.BlockSpec((None, tm, wd), lambda b, i, *_: (b, i, 0))
    in_specs, args = [tok(D_MODEL)], [x]
    if fuse_res:
        in_specs += [tok(D_MODEL), _mod_spec(modp)]
        args += [moe, modp]
    in_specs += [_mod_spec(mod), _layer_spec(n1), _layer_spec(w)]
    args += [mod, n1, w]
    if rope:
        in_specs += [pl.BlockSpec((tm, LANES), lambda b, i, *_: (i, 0))] * 2
        args += list(rope_tabs)
    out_specs = [tok(Q_W), tok(KV_W), tok(KV_W), tok(SSM_W)]
    out_shape = [jax.ShapeDtypeStruct((bsz, t, Q_W), BF16),
                 jax.ShapeDtypeStruct((bsz, t, KV_W), kv_dtype),
                 jax.ShapeDtypeStruct((bsz, t, KV_W), kv_dtype),
                 jax.ShapeDtypeStruct((bsz, t, SSM_W), F32)]
    if fuse_res:
        out_specs.append(tok(D_MODEL))
        out_shape.append(jax.ShapeDtypeStruct((bsz, t, D_MODEL), F32))
    grid_spec = pltpu.PrefetchScalarGridSpec(num_scalar_prefetch=1, grid=(bsz, t // tm), in_specs=in_specs,
                                             out_specs=out_specs)
    return pl.pallas_call(
        functools.partial(_qkvu_kernel, rope=rope, fuse_res=fuse_res), grid_spec=grid_spec, out_shape=out_shape,
        compiler_params=_params(("parallel", "parallel"), 56),
        name="qkvu",
    )(_layer_index(layer), *args)


def _stack_heads(q_ref):
    lo = lax.broadcasted_iota(jnp.int32, (1, LANES), 1) < HEAD_DIM
    keep_lo = jnp.where(lo, 1.0, 0.0).astype(BF16)
    keep_hi = jnp.where(lo, 0.0, 1.0).astype(BF16)
    parts = []
    for m in range(GROUP):
        qm = q_ref[:, m * LANES:(m + 1) * LANES]
        parts.append(qm * keep_lo)
        parts.append(qm * keep_hi)
    return jnp.concatenate(parts, axis=0), lo


def _attend(s, bias, sink_ref, v, tq, lo, o_ref):
    ps, dens = [], []
    for h in range(N_HEADS):
        sh = s[h * tq:(h + 1) * tq]
        if bias is not None:
            sh = sh + bias
        sink = sink_ref[h]
        mx = jnp.maximum(jnp.max(sh, axis=-1, keepdims=True), sink)
        p = jnp.exp(sh - mx)
        dens.append(jnp.sum(p, axis=-1, keepdims=True) + jnp.exp(sink - mx))
        ps.append(p.astype(BF16))
    o = jnp.dot(jnp.concatenate(ps, axis=0), v, preferred_element_type=F32)
    for m in range(GROUP):
        o0 = o[(2 * m) * tq:(2 * m + 1) * tq] / dens[2 * m]
        o1 = o[(2 * m + 1) * tq:(2 * m + 2) * tq] / dens[2 * m + 1]
        o_ref[:, m * LANES:(m + 1) * LANES] = jnp.where(lo, o0, o1).astype(o_ref.dtype)


def _ctx_attn_kernel(sink_ref, q_ref, k_ref, v_ref, o_ref):
    tq = q_ref.shape[0]
    qx, lo = _stack_heads(q_ref)
    s = lax.dot_general(qx, k_ref[...].astype(BF16), (((1,), (1,)), ((), ())), preferred_element_type=F32)
    _attend(s, None, sink_ref, v_ref[...].astype(BF16), tq, lo, o_ref)


def _ctx_attention(sink, q, k, v):
    bsz, t, _ = q.shape
    tok = lambda wd: pl.BlockSpec((None, t, wd), lambda b: (b, 0, 0))
    return pl.pallas_call(
        _ctx_attn_kernel, grid=(bsz,),
        in_specs=[pl.BlockSpec(memory_space=pltpu.SMEM), tok(Q_W), tok(KV_W), tok(KV_W)],
        out_specs=tok(Q_W), out_shape=jax.ShapeDtypeStruct((bsz, t, Q_W), BF16),
        compiler_params=_params(("parallel",), 32),
        name="ctx_attention",
    )(sink, q, k, v)


def _lat_attn_kernel(sink_ref, q_ref, kc_ref, vc_ref, kp_ref, k0_ref, kn_ref, vp_ref, v0_ref, vn_ref, o_ref):
    tq = q_ref.shape[0]
    i = pl.program_id(1)
    nb = pl.num_programs(1)
    qx, lo = _stack_heads(q_ref)
    kall = jnp.concatenate([kc_ref[...].astype(BF16), kp_ref[...], k0_ref[...], kn_ref[...]], axis=0)
    vall = jnp.concatenate([vc_ref[...].astype(BF16), vp_ref[...], v0_ref[...], vn_ref[...]], axis=0)
    s = lax.dot_general(qx, kall, (((1,), (1,)), ((), ())), preferred_element_type=F32)
    past = kc_ref.shape[0]
    r = lax.broadcasted_iota(jnp.int32, (tq, BLOCK), 0)
    c = lax.broadcasted_iota(jnp.int32, (tq, BLOCK), 1)
    m_prev = jnp.where((c >= r) & (i > 0), 0.0, NEG_INF)
    m_next = jnp.where((c <= r) & (i < nb - 1), 0.0, NEG_INF)
    bias = jnp.concatenate([jnp.zeros((tq, past), F32), m_prev, jnp.zeros((tq, BLOCK), F32), m_next], axis=1)
    _attend(s, bias, sink_ref, vall, tq, lo, o_ref)


def _lat_attention(sink, q, k, v, kc, vc):
    bsz, t, _ = q.shape
    nb = t // BLOCK
    past = kc.shape[1]
    tok = lambda wd: pl.BlockSpec((None, BLOCK, wd), lambda b, i: (b, i, 0))
    prev = pl.BlockSpec((None, BLOCK, KV_W), lambda b, i: (b, jnp.maximum(i - 1, 0), 0))
    nxt = pl.BlockSpec((None, BLOCK, KV_W), lambda b, i: (b, jnp.minimum(i + 1, nb - 1), 0))
    ctx = pl.BlockSpec((None, past, KV_W), lambda b, i: (b, 0, 0))
    return pl.pallas_call(
        _lat_attn_kernel, grid=(bsz, nb),
        in_specs=[pl.BlockSpec(memory_space=pltpu.SMEM), tok(Q_W), ctx, ctx,
                  prev, tok(KV_W), nxt, prev, tok(KV_W), nxt],
        out_specs=tok(Q_W), out_shape=jax.ShapeDtypeStruct((bsz, t, Q_W), BF16),
        compiler_params=_params(("parallel", "parallel"), 32),
        name="lat_attention",
    )(sink, q, kc, vc, k, k, k, v, v, v)


SSM_QUAD = LANES // (2 * SSM_GC)
GRANULES = LANES // SSM_GC


def _ssm_kernel(l_ref, u_ref, strip_ref, wst_ref, w2_ref, a_ref, h0_ref, d_ref, y_ref, fin_ref, pk_ref, z_ref,
                ent_ref, toep_ref, *, bsz, t):
    del l_ref
    nk = t // SSM_CHUNK
    gran = lax.broadcasted_iota(jnp.int32, (1, LANES), 1) // SSM_GC
    half = SSM_CHUNK // GRANULES

    ck = min(nk, 4 * SUBLANES)

    def transpose_granules(xs):
        xs = list(xs)
        for d in (4, 2, 1):
            low = (gran & d) == 0
            for i in range(GRANULES):
                if not i & d:
                    a, b = xs[i], xs[i + d]
                    xs[i] = jnp.where(low, a, pltpu.roll(b, SSM_GC * d, axis=1))
                    xs[i + d] = jnp.where(low, pltpu.roll(a, LANES - SSM_GC * d, axis=1), b)
        return xs

    def pack(it, carry):
        b, k0 = it // (nk // ck), (it % (nk // ck)) * ck
        for sh in range(half):
            xs = [u_ref[pl.ds(b * t + k0 * SSM_CHUNK + sh * GRANULES + s8, ck, stride=SSM_CHUNK), :]
                  for s8 in range(GRANULES)]
            for g8, x in enumerate(transpose_granules(xs)):
                pk_ref[g8 // 2, (g8 % 2) * half + sh, pl.ds(k0 * bsz + b, ck, stride=bsz), :] = x
        return carry

    lax.fori_loop(0, bsz * (nk // ck), pack, 0)

    rows = max(bsz, SUBLANES)
    steps = rows // bsz
    niter = nk // steps
    ns = SSM_W

    def step(sr, si, dr, di, ar, ai):
        return ar * sr - ai * si + dr, ar * si + ai * sr + di

    for p in range(SSM_QUAD):
        u = jnp.concatenate([pk_ref[p, col] for col in range(2 * half)], axis=1)
        ub = u.astype(BF16)
        gw = SSM_CHUNK * SSM_GC
        for gl in range(2):
            strip = strip_ref[p, gl]
            for s in range(SSM_CHUNK):
                off = (SSM_CHUNK - 1 - s) * SSM_GC
                blk = pltpu.roll(strip, 2 * gw - off, axis=1) if off else strip
                toep_ref[s * SSM_GC:(s + 1) * SSM_GC, :] = blk[:, 0:gw].astype(BF16)
            z_ref[:, gl * gw:(gl + 1) * gw] = jnp.dot(ub[:, gl * gw:(gl + 1) * gw], toep_ref[...],
                                                      preferred_element_type=F32)
        z_ref[:, ns:ns + 4 * LANES] = jnp.dot(ub, wst_ref[p], preferred_element_type=F32)
        afr, afi, abr, abi = a_ref[p, 0:1, :], a_ref[p, 1:2, :], a_ref[p, 2:3, :], a_ref[p, 3:4, :]

        def body(it, carry):
            sr, si, gr, gi = carry
            rf = pl.multiple_of(it * rows, SUBLANES)
            rb = pl.multiple_of((niter - 1 - it) * rows, SUBLANES)
            dfr, dfi = z_ref[pl.ds(rf, rows), ns:ns + LANES], z_ref[pl.ds(rf, rows), ns + LANES:ns + 2 * LANES]
            dbr = z_ref[pl.ds(rb, rows), ns + 2 * LANES:ns + 3 * LANES]
            dbi = z_ref[pl.ds(rb, rows), ns + 3 * LANES:ns + 4 * LANES]
            efr, efi, ebr, ebi = [], [], [None] * steps, [None] * steps
            for j in range(steps):
                efr.append(sr)
                efi.append(si)
                sr, si = step(sr, si, dfr[j * bsz:(j + 1) * bsz], dfi[j * bsz:(j + 1) * bsz], afr, afi)
            for j in reversed(range(steps)):
                ebr[j] = gr
                ebi[j] = gi
                gr, gi = step(gr, gi, dbr[j * bsz:(j + 1) * bsz], dbi[j * bsz:(j + 1) * bsz], abr, abi)
            cat = lambda xs: xs[0] if len(xs) == 1 else jnp.concatenate(xs, axis=0)
            ent_ref[pl.ds(rf, rows), 0:LANES] = cat(efr)
            ent_ref[pl.ds(rf, rows), LANES:2 * LANES] = cat(efi)
            ent_ref[pl.ds(rb, rows), 2 * LANES:3 * LANES] = cat(ebr)
            ent_ref[pl.ds(rb, rows), 3 * LANES:4 * LANES] = cat(ebi)
            return sr, si, gr, gi

        init = (h0_ref[p, 0], h0_ref[p, 1], h0_ref[p, 2], h0_ref[p, 3])
        sr, si, gr, gi = lax.fori_loop(0, niter, body, init)
        fin_ref[p, 0] = sr
        fin_ref[p, 1] = si
        fin_ref[p, 2] = gr
        fin_ref[p, 3] = gi
        y = (z_ref[:, 0:SSM_W] + jnp.dot(ent_ref[...].astype(BF16), w2_ref[p], preferred_element_type=F32)
             + d_ref[p] * u)
        for col in range(2 * half):
            pk_ref[p, col] = y[:, col * LANES:(col + 1) * LANES]

    def unpack(it, carry):
        b, k0 = it // (nk // ck), (it % (nk // ck)) * ck
        for sh in range(half):
            xs = [pk_ref[g8 // 2, (g8 % 2) * half + sh, pl.ds(k0 * bsz + b, ck, stride=bsz), :] for g8 in range(GRANULES)]
            for s8, x in enumerate(transpose_granules(xs)):
                y_ref[pl.ds(b * t + k0 * SSM_CHUNK + sh * GRANULES + s8, ck, stride=SSM_CHUNK), :] = x
        return carry

    lax.fori_loop(0, bsz * (nk // ck), unpack, 0)


def _ssm(u, ssm_w, layer, h0):
    bsz, t, _ = u.shape
    strip, wst, w2, a16, dflat = ssm_w
    rows = (t // SSM_CHUNK) * bsz
    if h0 is None:
        h0 = jnp.zeros((SSM_PAIRS, 4, bsz, LANES), F32)
    gw = SSM_CHUNK * SSM_GC
    quad = lambda *tail: pl.BlockSpec((SSM_QUAD,) + tail, lambda q, lref: (q,) + (0,) * len(tail))
    lquad = lambda *tail: pl.BlockSpec((None, SSM_QUAD) + tail, lambda q, lref: (lref[0], q) + (0,) * len(tail))
    tok = pl.BlockSpec((bsz * t, LANES), lambda q, lref: (0, q), pipeline_mode=pl.Buffered(1))
    grid_spec = pltpu.PrefetchScalarGridSpec(
        num_scalar_prefetch=1, grid=(SSM_PAIRS // SSM_QUAD,),
        in_specs=[tok, lquad(2, SSM_GC, 2 * gw), lquad(SSM_W, 4 * LANES), lquad(4 * LANES, SSM_W), lquad(4, LANES),
                  quad(4, bsz, LANES), lquad(1, SSM_W)],
        out_specs=[tok, quad(4, bsz, LANES)],
        scratch_shapes=[pltpu.VMEM((SSM_QUAD, SSM_W // LANES, rows, LANES), F32),
                        pltpu.VMEM((rows, SSM_W + 4 * LANES), F32), pltpu.VMEM((rows, 4 * LANES), F32),
                        pltpu.VMEM((gw, gw), BF16)])
    y, fin = pl.pallas_call(
        functools.partial(_ssm_kernel, bsz=bsz, t=t), grid_spec=grid_spec,
        out_shape=[jax.ShapeDtypeStruct((bsz * t, SSM_W), F32),
                   jax.ShapeDtypeStruct((SSM_PAIRS, 4, bsz, LANES), F32)],
        compiler_params=_params(("parallel",), 52),
        name="ssm",
    )(_layer_index(layer), u.reshape(bsz * t, SSM_W), strip, wst, w2, a16, h0, dflat)
    return y.reshape(bsz, t, SSM_W), fin


def _ssm_weights(a_re, a_im, log_dt, b_re, b_im, c_re, c_im, ssm_d):
    L = SSM_CHUNK
    nl = a_re.shape[0]
    dt = jnp.exp(log_dt)[..., None, None]
    steps = jnp.arange(L + 1, dtype=F32)
    mag = jnp.exp(a_re[..., None] * dt * steps)
    ang = a_im[..., None] * dt * steps
    pr, pi = mag * jnp.cos(ang), mag * jnp.sin(ang)
    nr, ni = pr[..., 1] - 1.0, pi[..., 1]
    den = a_re * a_re + a_im * a_im
    qr, qi = (nr * a_re + ni * a_im) / den, (ni * a_re - nr * a_im) / den
    swap = lambda x: x.transpose(0, 1, 2, 4, 3)
    bbr = swap(qr[..., None] * b_re - qi[..., None] * b_im)
    bbi = swap(qr[..., None] * b_im + qi[..., None] * b_re)
    ctr, cti = swap(c_re), swap(c_im)
    cpr = ctr[..., None, :] * pr[..., :L, None] - cti[..., None, :] * pi[..., :L, None]
    cpi = ctr[..., None, :] * pi[..., :L, None] + cti[..., None, :] * pr[..., :L, None]
    flat = lambda x: x.reshape(nl, 2, SSM_G, SSM_P, L * SSM_GC)
    kern = (jnp.einsum('ldgkp,ldgpn->ldgkn', bbr, flat(cpr), precision=HIGHEST)
            - jnp.einsum('ldgkp,ldgpn->ldgkn', bbi, flat(cpi), precision=HIGHEST))
    kern = kern.reshape(nl, 2, SSM_G, SSM_GC, L, SSM_GC)
    kf, kb = kern[:, 0], kern[:, 1]
    lagk = jnp.concatenate([kb[..., :0:-1, :], kf[..., 0:1, :] + kb[..., 0:1, :], kf[..., 1:, :]], axis=-2)
    strip = jnp.pad(lagk.reshape(nl, SSM_G, SSM_GC, (2 * L - 1) * SSM_GC), ((0, 0), (0, 0), (0, 0), (0, SSM_GC)))
    strip = strip.reshape(nl, SSM_PAIRS, 2, SSM_GC, 2 * L * SSM_GC)
    idx = jnp.arange(L)

    def pair(w):
        w = w.reshape((nl, SSM_PAIRS, 2) + w.shape[2:])
        even, odd = w[:, :, 0], w[:, :, 1]
        zero = jnp.zeros_like(even)
        return jnp.stack([jnp.concatenate([even, zero], axis=-1), jnp.concatenate([zero, odd], axis=-1)], axis=2)

    def inject(kpow, d):
        ar = pr[:, d].transpose(0, 1, 3, 2)[:, :, kpow][:, :, :, None, :]
        ai = pi[:, d].transpose(0, 1, 3, 2)[:, :, kpow][:, :, :, None, :]
        br, bi = bbr[:, d][:, :, None], bbi[:, d][:, :, None]
        return [ar * br - ai * bi, ar * bi + ai * br]

    wst = pair(jnp.stack(inject(L - 1 - idx, 0) + inject(idx, 1), axis=4))
    wst = wst.reshape(nl, SSM_PAIRS, 2 * L * SSM_GC, 4 * 2 * SSM_P)

    def readout(kpow, d):
        ar, ai = pr[:, d][..., kpow][..., None], pi[:, d][..., kpow][..., None]
        cr, ci = ctr[:, d][:, :, :, None, :], cti[:, d][:, :, :, None, :]
        return [w.reshape(nl, SSM_G, SSM_P, L * SSM_GC) for w in (cr * ar - ci * ai, -(cr * ai + ci * ar))]

    w2 = pair(jnp.stack(readout(1 + idx, 0) + readout(L - idx, 1), axis=2))
    w2 = w2.swapaxes(2, 3).reshape(nl, SSM_PAIRS, 4 * 2 * SSM_P, 2 * L * SSM_GC)
    plane = lambda x: x.reshape(nl, SSM_PAIRS, 2 * SSM_P)
    a16 = jnp.stack([plane(pr[:, 0, :, :, L]), plane(pi[:, 0, :, :, L]), plane(pr[:, 1, :, :, L]),
                     plane(pi[:, 1, :, :, L])], axis=2)
    dflat = jnp.broadcast_to(ssm_d.reshape(nl, SSM_PAIRS, 2, 1, SSM_GC), (nl, SSM_PAIRS, 2, L, SSM_GC))
    return strip, wst.astype(BF16), w2.astype(BF16), a16, dflat.reshape(nl, SSM_PAIRS, 1, 2 * L * SSM_GC)


def _merge_kernel(l_ref, x_ref, xp_ref, xn_ref, mod_ref, n1_ref, n2_ref, wcg_ref, cw_ref, wco_ref, attn_ref, wao_ref,
                  ssm_ref, wglu_ref, wout_ref, rw_ref, x1_ref, h2_ref, aff_ref):
    del l_ref
    tm = x_ref.shape[0]
    i = pl.program_id(1)
    nt = pl.num_programs(1)
    x = x_ref[...]
    n1 = n1_ref[...]
    sh1, sc1, g1 = mod_ref[0:1, :], mod_ref[1:2, :], mod_ref[2:3, :]
    sh2, sc2 = mod_ref[3:4, :], mod_ref[4:5, :]
    cw = CONV_W

    def conv_in(h):
        xin = jnp.dot(h, wcg_ref[:, 0:cw], preferred_element_type=F32)
        cg = jnp.dot(h, wcg_ref[:, 2 * cw:3 * cw], preferred_element_type=F32)
        return cg * xin

    x_ext = jnp.concatenate([x, xp_ref[...], xn_ref[...]], axis=0)
    h_ext = _rms_mod(x_ext, n1, sc1, sh1).astype(BF16)
    h = h_ext[0:tm]
    z_ext = conv_in(h_ext)
    z = z_ext[0:tm]
    z_before = jnp.where(i > 0, z_ext[tm + SUBLANES - 1:tm + SUBLANES, :], 0.0)
    z_after = jnp.where(i < nt - 1, z_ext[tm + SUBLANES:tm + SUBLANES + 1, :], 0.0)
    rows = lax.broadcasted_iota(jnp.int32, (tm, 1), 0)
    zl = jnp.where(rows == 0, z_before, pltpu.roll(z, 1, axis=0))
    zr = jnp.where(rows == tm - 1, z_after, pltpu.roll(z, tm - 1, axis=0))
    y = zl * cw_ref[0:1, :] + z * cw_ref[1:2, :] + zr * cw_ref[2:3, :]
    bg = jnp.dot(h, wcg_ref[:, cw:2 * cw], preferred_element_type=F32)
    conv_y = jnp.dot((bg * y).astype(BF16), wco_ref[...], preferred_element_type=F32)
    o = wcg_ref.shape[1] - N_BRANCH * D_MODEL
    ga = _sigmoid(jnp.dot(h, wcg_ref[:, o:o + D_MODEL], preferred_element_type=F32))
    mixed = ga * conv_y
    attn_y = jnp.dot(attn_ref[...], wao_ref[...], preferred_element_type=F32)
    gb = _sigmoid(jnp.dot(h, wcg_ref[:, o + D_MODEL:o + 2 * D_MODEL], preferred_element_type=F32))
    mixed = mixed + gb * attn_y
    s = ssm_ref[...]
    gelu = s * (0.5 * (1.0 + jnp.tanh(math.sqrt(2.0 / math.pi) * (s + 0.044715 * (s * s * s)))))
    zab = jnp.dot(gelu.astype(BF16), wglu_ref[...], preferred_element_type=F32)
    ssm_y = zab[:, 0:D_MODEL] * _sigmoid(zab[:, D_MODEL:2 * D_MODEL])
    gc = _sigmoid(jnp.dot(h, wcg_ref[:, o + 2 * D_MODEL:o + 3 * D_MODEL], preferred_element_type=F32))
    mixed = mixed + gc * ssm_y
    x1 = x + g1 * jnp.dot(mixed.astype(BF16), wout_ref[...], preferred_element_type=F32)
    x1_ref[...] = x1
    h2 = _rms_mod(x1, n2_ref[...], sc2, sh2)
    h2_ref[...] = h2.astype(BF16)
    h_hi = h2.astype(BF16)
    h_lo = (h2 - h_hi.astype(F32)).astype(BF16)
    rw = rw_ref[...]
    r_hi = rw.astype(BF16)
    r_lo = (rw - r_hi.astype(F32)).astype(BF16)
    nt_dims = (((1,), (1,)), ((), ()))
    a = lax.dot_general(jnp.concatenate([r_hi, r_lo], axis=0), h_hi, nt_dims, preferred_element_type=F32)
    b = lax.dot_general(r_hi, h_lo, nt_dims, preferred_element_type=F32)
    logits = a[0:N_EXPERTS] + a[N_EXPERTS:2 * N_EXPERTS] + b
    e = jnp.exp(logits - jnp.max(logits, axis=0, keepdims=True))
    aff_ref[...] = e / jnp.sum(e, axis=0, keepdims=True)


def _merge(x, mod, n1, n2, wcg, cw, wco, attn, wao, ssm, wglu, wout, rwt, layer, tm):
    bsz, t, _ = x.shape
    nt = t // tm
    per = tm // SUBLANES
    tok = lambda wd: pl.BlockSpec((None, tm, wd), lambda b, i, *_: (b, i, 0))
    prev = pl.BlockSpec((None, SUBLANES, D_MODEL), lambda b, i, *_: (b, jnp.maximum(i * per - 1, 0), 0))
    nxt = pl.BlockSpec((None, SUBLANES, D_MODEL),
                       lambda b, i, *_: (b, jnp.minimum((i + 1) * per, t // SUBLANES - 1), 0))
    lay = _layer_spec
    grid_spec = pltpu.PrefetchScalarGridSpec(
        num_scalar_prefetch=1, grid=(bsz, nt),
        in_specs=[tok(D_MODEL), prev, nxt, _mod_spec(mod), lay(n1), lay(n2), lay(wcg), lay(cw), lay(wco), tok(Q_W),
                  lay(wao), tok(SSM_W), lay(wglu), lay(wout), lay(rwt)],
        out_specs=[tok(D_MODEL), tok(D_MODEL), pl.BlockSpec((None, N_EXPERTS, tm), lambda b, i, *_: (b, 0, i))])
    return pl.pallas_call(
        _merge_kernel, grid_spec=grid_spec,
        out_shape=[jax.ShapeDtypeStruct((bsz, t, D_MODEL), F32), jax.ShapeDtypeStruct((bsz, t, D_MODEL), BF16),
                   jax.ShapeDtypeStruct((bsz, N_EXPERTS, t), F32)],
        compiler_params=_params(("parallel", "parallel"), 56),
        name="merge",
    )(_layer_index(layer), x, x, x, mod, n1, n2, wcg, cw, wco, attn, wao, ssm, wglu, wout, rwt)


def _route_kernel(aff_ref, slot_ref, cnt_ref, *, cap, slot_stride):
    nrow, n = aff_ref.shape
    bits = pltpu.bitcast(aff_ref[...], jnp.int32)
    thr = jnp.zeros((nrow, 1), jnp.int32)
    for b in range(30, -1, -1):
        cand = thr | (1 << b)
        cnt = jnp.sum(jnp.where(bits >= cand, 1.0, 0.0), axis=1, keepdims=True)
        thr = jnp.where(cnt >= cap, cand, thr)
    need = cap - jnp.sum(jnp.where(bits > thr, 1.0, 0.0), axis=1, keepdims=True)
    ri = lax.broadcasted_iota(jnp.int32, (LANES, LANES), 0)
    ci = lax.broadcasted_iota(jnp.int32, (LANES, LANES), 1)
    tri = jnp.where(ri <= ci, 1.0, 0.0).astype(BF16)
    lane = lax.broadcasted_iota(jnp.int32, (nrow, LANES), 1)
    base = (lax.broadcasted_iota(jnp.int32, (nrow, 1), 0) // N_EXPERTS) * slot_stride
    off_eq = jnp.zeros((nrow, 1), F32)
    off = jnp.zeros((nrow, 1), F32)
    cnts = jnp.zeros((nrow, LANES), jnp.int32)
    for j in range(n // LANES):
        sl = slice(j * LANES, (j + 1) * LANES)
        bj = bits[:, sl]
        eqf = jnp.where(bj == thr, 1.0, 0.0)
        ceq = jnp.dot(eqf.astype(BF16), tri, preferred_element_type=F32) + off_eq
        off_eq = ceq[:, LANES - 1:LANES]
        self_ = jnp.where(bj > thr, 1.0, jnp.where(ceq <= need, eqf, 0.0))
        csel = jnp.dot(self_.astype(BF16), tri, preferred_element_type=F32) + off
        cnts = jnp.where(lane == j, off.astype(jnp.int32) + base, cnts)
        off = csel[:, LANES - 1:LANES]
        slot_ref[:, sl] = jnp.where(self_ > 0.0, csel.astype(jnp.int32) - 1 + base, -1)
    cnt_ref[...] = jnp.where(lane >= n // LANES, off.astype(jnp.int32) + base, cnts)


def _route(aff, cap, slot_stride):
    sets, _, n = aff.shape
    nrow = sets * N_EXPERTS
    slot, cnt = pl.pallas_call(
        functools.partial(_route_kernel, cap=cap, slot_stride=slot_stride), grid=(1,),
        in_specs=[pl.BlockSpec((nrow, n), lambda s: (0, 0))],
        out_specs=[pl.BlockSpec((nrow, n), lambda s: (0, 0)), pl.BlockSpec((nrow, LANES), lambda s: (0, 0))],
        out_shape=[jax.ShapeDtypeStruct((nrow, n), jnp.int32), jax.ShapeDtypeStruct((nrow, LANES), jnp.int32)],
        compiler_params=_params(("arbitrary",), 32),
        name="route",
    )(aff.reshape(nrow, n))
    return slot.reshape(sets, N_EXPERTS, n), cnt.reshape(sets, N_EXPERTS, LANES)


def _expert_kernel(cnt_ref, l_ref, h2_ref, slot_ref, gate_ref, wg_ref, wu_ref, wd_ref, o_ref, xs_ref, ys_ref, gs_ref,
                   *, nslot):
    del l_ref
    s = pl.program_id(0)
    e = pl.program_id(1)
    ntb = SUPER // BLOCK
    nsb = nslot // BLOCK
    per = GATHER_TOKENS // BLOCK
    cbase = (s * N_EXPERTS + e) * (ntb + 1)

    @pl.when(e == 0)
    def _():
        o_ref[...] = jnp.zeros_like(o_ref)

    xs_ref[...] = jnp.zeros_like(xs_ref)
    gs_ref[...] = jnp.zeros_like(gs_ref)
    ys_ref[nslot:nslot + BLOCK, :] = jnp.zeros((BLOCK, D_MODEL), BF16)

    srow = lax.broadcasted_iota(jnp.int32, (BLOCK, GATHER_TOKENS), 0)
    for g in range(SUPER // GATHER_TOKENS):
        slots = slot_ref[g:g + 1, :]
        gates = gate_ref[g:g + 1, :]
        for j in range(nsb):
            @pl.when((cnt_ref[cbase + g * per] < (j + 1) * BLOCK) & (cnt_ref[cbase + (g + 1) * per] > j * BLOCK))
            def _():
                pick = slots == srow + j * BLOCK
                x = jnp.dot(jnp.where(pick, 1.0, 0.0).astype(BF16), h2_ref[g * GATHER_TOKENS:(g + 1) * GATHER_TOKENS, :],
                            preferred_element_type=F32)
                xs_ref[j * BLOCK:(j + 1) * BLOCK, :] += x.astype(BF16)
                gs_ref[j * BLOCK:(j + 1) * BLOCK, :] += jnp.sum(jnp.where(pick, gates, 0.0), axis=1, keepdims=True)

    xs = xs_ref[...]
    hg = jnp.dot(xs, wg_ref[...], preferred_element_type=F32)
    hu = jnp.dot(xs, wu_ref[...], preferred_element_type=F32)
    act = (hg * _sigmoid(hg) * hu).astype(BF16)
    y = jnp.dot(act, wd_ref[...], preferred_element_type=F32) * gs_ref[...]
    ys_ref[0:nslot, :] = y.astype(BF16)

    wrow = lax.broadcasted_iota(jnp.int32, (2 * BLOCK, BLOCK), 0)
    for i in range(ntb):
        w0 = pl.multiple_of(jnp.minimum(cnt_ref[cbase + i] // BLOCK, nsb - 1) * BLOCK, BLOCK)
        slots = slot_ref[i // per:i // per + 1, (i % per) * BLOCK:(i % per + 1) * BLOCK]
        pick = jnp.where(slots - w0 == wrow, 1.0, 0.0).astype(BF16)
        o_ref[i * BLOCK:(i + 1) * BLOCK, :] += lax.dot_general(
            pick, ys_ref[pl.ds(w0, 2 * BLOCK), :], (((0,), (0,)), ((), ())), preferred_element_type=F32)


def _experts(cnt, h2, slot, gate, wg, wu, wd, layer):
    nsup = h2.shape[0]
    nslot = CAPACITY * SUPER // N_EXPERTS
    ngt = SUPER // GATHER_TOKENS
    grid_spec = pltpu.PrefetchScalarGridSpec(
        num_scalar_prefetch=2, grid=(nsup, N_EXPERTS),
        in_specs=[pl.BlockSpec((None, SUPER, D_MODEL), lambda s, e, c, l: (s, 0, 0)),
                  pl.BlockSpec((None, None, ngt, GATHER_TOKENS), lambda s, e, c, l: (s, e, 0, 0)),
                  pl.BlockSpec((None, None, ngt, GATHER_TOKENS), lambda s, e, c, l: (s, e, 0, 0)),
                  pl.BlockSpec((None, None, D_MODEL, EXPERT_FF), lambda s, e, c, l: (l[0], e, 0, 0)),
                  pl.BlockSpec((None, None, D_MODEL, EXPERT_FF), lambda s, e, c, l: (l[0], e, 0, 0)),
                  pl.BlockSpec((None, None, EXPERT_FF, D_MODEL), lambda s, e, c, l: (l[0], e, 0, 0))],
        out_specs=pl.BlockSpec((None, SUPER, D_MODEL), lambda s, e, c, l: (s, 0, 0), pipeline_mode=pl.Buffered(1)),
        scratch_shapes=[pltpu.VMEM((nslot, D_MODEL), BF16), pltpu.VMEM((nslot + BLOCK, D_MODEL), BF16),
                        pltpu.VMEM((nslot, 1), F32)])
    return pl.pallas_call(
        functools.partial(_expert_kernel, nslot=nslot), grid_spec=grid_spec,
        out_shape=jax.ShapeDtypeStruct((nsup, SUPER, D_MODEL), F32),
        compiler_params=_params(("arbitrary", "arbitrary"), 56),
        name="experts",
    )(cnt, _layer_index(layer), h2, slot, gate, wg, wu, wd)


def _final_kernel(x_ref, moe_ref, mod_ref, g_ref, o_ref):
    x = x_ref[...] + mod_ref[5:6, :] * moe_ref[...]
    o_ref[...] = x * lax.rsqrt(jnp.mean(x * x, axis=-1, keepdims=True) + EPS) * g_ref[...]


def _final(x1, moe, mod, g, tm):
    bsz, t, _ = x1.shape
    tok = pl.BlockSpec((None, tm, D_MODEL), lambda b, i: (b, i, 0))
    return pl.pallas_call(
        _final_kernel, grid=(bsz, t // tm),
        in_specs=[tok, tok, _mod_spec(mod), _const_spec((1, D_MODEL))],
        out_specs=tok, out_shape=jax.ShapeDtypeStruct((bsz, t, D_MODEL), F32),
        compiler_params=_params(("parallel", "parallel"), 32),
        name="final_norm",
    )(x1, moe, mod, g)


def _head_perm():
    cols = []
    for m in range(GROUP):
        for kv in range(N_KV):
            head = kv * GROUP + m
            cols.extend(range(head * HEAD_DIM, (head + 1) * HEAD_DIM))
    return jnp.array(cols, jnp.int32)


def _rope_swap(width):
    idx = jnp.arange(width)
    nf = HEAD_DIM // 4
    return jnp.where((idx % (2 * nf)) < nf, idx + nf, idx - nf)


def _rope_tables(t):
    pos = jnp.arange(t)
    row = (pos // GRID_W).astype(F32)
    col = (pos % GRID_W).astype(F32)
    nf = HEAD_DIM // 4
    inv = ROPE_BASE ** (-jnp.arange(nf, dtype=F32) / nf)

    def tabs(p):
        ang = p[:, None] * inv[None, :]
        cos, sin = jnp.cos(ang), jnp.sin(ang)
        return jnp.concatenate([cos, cos], axis=1), jnp.concatenate([-sin, sin], axis=1)

    cr, sr = tabs(row)
    cc, sc = tabs(col)
    cos = jnp.concatenate([cr, cc], axis=1)
    sin = jnp.concatenate([sr, sc], axis=1)
    return jnp.tile(cos, (1, LANES // HEAD_DIM)), jnp.tile(sin, (1, LANES // HEAD_DIM))


def _route_and_experts(h2c, affc, h2l, affl, wg, wu, wd, layer):
    bc, tc, _ = h2c.shape
    bl, tl, _ = h2l.shape
    ntb = SUPER // BLOCK
    capc = CAPACITY * tc // N_EXPERTS
    capl = CAPACITY * tl // N_EXPERTS
    slot_c, cnt_c = _route(affc, capc, capc)
    slot_l, cnt_l = _route(affl, capl, 0)
    tbc = tc // BLOCK
    rows = (SUPER // GATHER_TOKENS, GATHER_TOKENS)
    slot_c = slot_c.transpose(1, 0, 2).reshape(1, N_EXPERTS, *rows)
    gate_c = affc.transpose(1, 0, 2).reshape(1, N_EXPERTS, *rows)
    cnt_c = jnp.concatenate([cnt_c[:, :, :tbc].transpose(1, 0, 2).reshape(N_EXPERTS, ntb),
                             cnt_c[bc - 1, :, tbc:tbc + 1]], axis=1)
    moe_c = _experts(cnt_c.reshape(-1), h2c.reshape(1, SUPER, D_MODEL), slot_c, gate_c, wg, wu, wd, layer)
    moe_l = _experts(cnt_l[:, :, :ntb + 1].reshape(-1), h2l, slot_l.reshape(bl, N_EXPERTS, *rows),
                     affl.reshape(bl, N_EXPERTS, *rows), wg, wu, wd, layer)
    return moe_c.reshape(bc, tc, D_MODEL), moe_l


def kernel(x_prompt, x_sample, cache_k, cache_v, state_ssm_re, state_ssm_im, c, c_ctx, ada_w, ada_b, norm1, norm2,
           final_norm, w_in, conv_w, w_conv_out, attn_sink, w_attn_out, ssm_a_re, ssm_a_im, ssm_log_dt, ssm_b_re,
           ssm_b_im, ssm_c_re, ssm_c_im, ssm_d, w_glu, w_out, router_w, w_gate, w_up, w_down):
    bc, tc, _ = x_prompt.shape
    bl, tl, _ = x_sample.shape
    assert bc * tc == SUPER and tl == SUPER and bl + 1 <= SUBLANES
    hp = _head_perm()
    cvecs = jnp.zeros((SUBLANES, D_MODEL), F32).at[:bl].set(c).at[bl].set(c_ctx)
    mods = _modulation(cvecs, ada_w, ada_b).reshape(DEPTH, SUBLANES, 6, D_MODEL)

    o_q, o_k, o_v, o_u, o_g = 3 * CONV_W, 3 * CONV_W + Q_W, 3 * CONV_W + Q_W + KV_W, 3 * CONV_W + Q_W + 2 * KV_W, \
        3 * CONV_W + Q_W + 2 * KV_W + SSM_W
    w_b = w_in.astype(BF16)
    wq = w_b[:, :, o_q:o_k][:, :, hp]
    wk = w_b[:, :, o_k:o_v]
    w_qkvu = jnp.concatenate([wq, w_b[:, :, o_k:o_g], wq[:, :, _rope_swap(Q_W)], wk[:, :, _rope_swap(KV_W)]], axis=-1)
    w_co = w_conv_out.astype(BF16)
    w_ao = w_attn_out[:, hp, :].astype(BF16)
    w_gl = w_glu.astype(BF16)
    w_o = w_out.astype(BF16)
    rwt = router_w.transpose(0, 2, 1)
    wg, wu, wd = w_gate.astype(BF16), w_up.astype(BF16), w_down.astype(BF16)
    n1 = norm1.reshape(DEPTH, 1, D_MODEL)
    n2 = norm2.reshape(DEPTH, 1, D_MODEL)
    sink = attn_sink.reshape(DEPTH, N_KV, GROUP).transpose(0, 2, 1).reshape(DEPTH, N_HEADS)
    rope_tabs = _rope_tables(tl)
    ssm_w = _ssm_weights(ssm_a_re, ssm_a_im, ssm_log_dt, ssm_b_re, ssm_b_im, ssm_c_re, ssm_c_im, ssm_d)
    h0_all = jnp.stack([state_ssm_re[:, :, 0], state_ssm_im[:, :, 0], state_ssm_re[:, :, 1], state_ssm_im[:, :, 1]],
                       axis=0)
    h0_all = h0_all.reshape(4, bl, DEPTH, SSM_PAIRS, 2 * SSM_P).transpose(2, 3, 0, 1, 4)

    xp, xs = x_prompt, x_sample
    moe_c = moe_l = None
    modp_c = modp_l = None
    ks, vs, fins = [], [], []
    for l in range(DEPTH):
        mod_l = mods[l, :bl]
        mod_c = mods[l, bl:bl + 1]
        outs = _qkvu(xp, moe_c, modp_c, mod_c, n1, w_qkvu, l, None, tc, F32)
        q_c, k_c, v_c, u_c = outs[:4]
        if moe_c is not None:
            xp = outs[4]
        ks.append(k_c)
        vs.append(v_c)
        attn_c = _ctx_attention(sink[l], q_c, k_c, v_c)
        ssm_c, fin = _ssm(u_c, ssm_w, l, None)
        fins.append(fin)
        x1c, h2c, affc = _merge(xp, mod_c, n1, n2, w_b, conv_w, w_co, attn_c, w_ao, ssm_c, w_gl, w_o, rwt, l, tc)
        outs = _qkvu(xs, moe_l, modp_l, mod_l, n1, w_qkvu, l, rope_tabs, 1024, BF16)
        q_l, k_l, v_l, u_l = outs[:4]
        if moe_l is not None:
            xs = outs[4]
        attn_l = _lat_attention(sink[l], q_l, k_l, v_l, cache_k[:, l].reshape(bl, -1, KV_W),
                                cache_v[:, l].reshape(bl, -1, KV_W))
        ssm_l, _ = _ssm(u_l, ssm_w, l, h0_all[l])
        x1l, h2l, affl = _merge(xs, mod_l, n1, n2, w_b, conv_w, w_co, attn_l, w_ao, ssm_l, w_gl, w_o, rwt, l, 512)
        moe_c, moe_l = _route_and_experts(h2c, affc, h2l, affl, wg, wu, wd, l)
        xp, xs = x1c, x1l
        modp_c, modp_l = mod_c, mod_l

    fn = final_norm.reshape(1, D_MODEL)
    y_prompt = _final(xp, moe_c, modp_c, fn, tc)
    y_sample = _final(xs, moe_l, modp_l, fn, 512)
    new_k = jnp.stack(ks, axis=1).reshape(bc, DEPTH, tc, N_KV, HEAD_DIM)
    new_v = jnp.stack(vs, axis=1).reshape(bc, DEPTH, tc, N_KV, HEAD_DIM)
    fin = jnp.stack(fins, axis=0)
    fin = fin.reshape(DEPTH, SSM_PAIRS, 2, 2, bc, 2, SSM_P).transpose(3, 4, 0, 2, 1, 5, 6)
    fin = fin.reshape(2, bc, DEPTH, 2, SSM_G, SSM_P)
    return (y_prompt, y_sample, new_k, new_v, fin[0], fin[1])
```

```python
import functools
import math

import jax
import jax.numpy as jnp
from jax import lax
from jax.experimental import pallas as pl
from jax.experimental.pallas import tpu as pltpu

D_MODEL = 1024
DEPTH = 4
GRID_W = 64
CONV_W = 512
N_HEADS = 8
N_KV = 2
HEAD_DIM = 64
GROUP = N_HEADS // N_KV
Q_W = N_HEADS * HEAD_DIM
KV_W = N_KV * HEAD_DIM
BLOCK = 128
ROPE_BASE = 10000.0
NEG_INF = -1e30
SSM_W = 512
SSM_GC = 16
SSM_G = SSM_W // SSM_GC
SSM_P = 64
N_BRANCH = 3
N_EXPERTS = 16
EXPERT_FF = 1024
CAPACITY = 2
EPS = 1e-6

F32 = jnp.float32
BF16 = jnp.bfloat16
HIGHEST = lax.Precision.HIGHEST

LANES = 128
SUBLANES = 8
SSM_CHUNK = 16
SSM_PAIRS = SSM_G // 2
SUPER = 4096
GATHER_TOKENS = 512
MIB = 1024 * 1024


def _params(sem, vmem_mib):
    return pltpu.CompilerParams(dimension_semantics=sem, vmem_limit_bytes=vmem_mib * MIB)


def _const_spec(shape):
    nd = len(shape)
    return pl.BlockSpec(shape, lambda *_: (0,) * nd, pipeline_mode=pl.Buffered(1))


def _layer_spec(arr):
    nd = arr.ndim
    return pl.BlockSpec((None,) + arr.shape[1:], lambda *idx: (idx[-1][0],) + (0,) * (nd - 1),
                        pipeline_mode=pl.Buffered(1))


def _layer_index(layer):
    return jnp.full((1,), layer, jnp.int32)


def _sigmoid(x):
    return 0.5 * jnp.tanh(0.5 * x) + 0.5


def _rms_mod(x, g, sc, sh):
    y = x * lax.rsqrt(jnp.mean(x * x, axis=-1, keepdims=True) + EPS)
    return (y * g) * (1.0 + sc) + sh


def _mod_kernel(c_ref, w_ref, b_ref, o_ref):
    cv = c_ref[...]
    s = cv * _sigmoid(cv)
    o_ref[...] = jnp.dot(s, w_ref[...], precision=HIGHEST, preferred_element_type=F32) + b_ref[...]


def _modulation(cvecs, ada_w, ada_b):
    nt = 1536
    return pl.pallas_call(
        _mod_kernel,
        grid=(DEPTH, 6 * D_MODEL // nt),
        in_specs=[pl.BlockSpec((SUBLANES, D_MODEL), lambda l, j: (0, 0)),
                  pl.BlockSpec((None, D_MODEL, nt), lambda l, j: (l, 0, j)),
                  pl.BlockSpec((None, 1, nt), lambda l, j: (l, 0, j))],
        out_specs=pl.BlockSpec((None, SUBLANES, nt), lambda l, j: (l, 0, j)),
        out_shape=jax.ShapeDtypeStruct((DEPTH, SUBLANES, 6 * D_MODEL), F32),
        compiler_params=_params(("parallel", "parallel"), 32),
        name="modulation",
    )(cvecs, ada_w, ada_b.reshape(DEPTH, 1, 6 * D_MODEL))


def _qkvu_kernel(*refs, rope, fuse_res):
    it = iter(refs[1:])
    x_ref = next(it)
    if fuse_res:
        moe_ref = next(it)
        modp_ref = next(it)
    mod_ref = next(it)
    n1_ref = next(it)
    w_ref = next(it)
    if rope:
        cos_ref = next(it)
        sin_ref = next(it)
    q_ref, k_ref, v_ref, u_ref = next(it), next(it), next(it), next(it)
    x = x_ref[...]
    if fuse_res:
        xo_ref = next(it)
        x = x + modp_ref[5:6, :] * moe_ref[...]
        xo_ref[...] = x
    h = _rms_mod(x, n1_ref[...], mod_ref[1:2, :], mod_ref[0:1, :]).astype(BF16)
    ncol = w_ref.shape[1] if rope else Q_W + 2 * KV_W + SSM_W
    p = jnp.dot(h, w_ref[:, 0:ncol], preferred_element_type=F32)
    q = p[:, 0:Q_W]
    k = p[:, Q_W:Q_W + KV_W]
    if rope:
        cos = cos_ref[...]
        sin = sin_ref[...]
        o = Q_W + 2 * KV_W + SSM_W
        q = jnp.concatenate(
            [q[:, m * LANES:(m + 1) * LANES] * cos + p[:, o + m * LANES:o + (m + 1) * LANES] * sin
             for m in range(Q_W // LANES)], axis=1)
        k = k * cos + p[:, o + Q_W:o + Q_W + KV_W] * sin
    q_ref[...] = (q * HEAD_DIM ** -0.5).astype(q_ref.dtype)
    k_ref[...] = k.astype(k_ref.dtype)
    v_ref[...] = p[:, Q_W + KV_W:Q_W + 2 * KV_W].astype(v_ref.dtype)
    u_ref[...] = p[:, Q_W + 2 * KV_W:Q_W + 2 * KV_W + SSM_W]


def _mod_spec(mod):
    if mod.shape[0] == 1:
        return pl.BlockSpec((None, 6, D_MODEL), lambda b, i, *_: (0, 0, 0))
    return pl.BlockSpec((None, 6, D_MODEL), lambda b, i, *_: (b, 0, 0))


def _qkvu(x, moe, modp, mod, n1, w, layer, rope_tabs, tm, kv_dtype):
    bsz, t, _ = x.shape
    rope = rope_tabs is not None
    fuse_res = moe is not None
    tok = lambda wd: pl.BlockSpec((None, tm, wd), lambda b, i, *_: (b, i, 0))
    in_specs, args = [tok(D_MODEL)], [x]
    if fuse_res:
        in_specs += [tok(D_MODEL), _mod_spec(modp)]
        args += [moe, modp]
    in_specs += [_mod_spec(mod), _layer_spec(n1), _layer_spec(w)]
    args += [mod, n1, w]
    if rope:
        in_specs += [pl.BlockSpec((tm, LANES), lambda b, i, *_: (i, 0))] * 2
        args += list(rope_tabs)
    out_specs = [tok(Q_W), tok(KV_W), tok(KV_W), tok(SSM_W)]
    out_shape = [jax.ShapeDtypeStruct((bsz, t, Q_W), BF16),
                 jax.ShapeDtypeStruct((bsz, t, KV_W), kv_dtype),
                 jax.ShapeDtypeStruct((bsz, t, KV_W), kv_dtype),
                 jax.ShapeDtypeStruct((bsz, t, SSM_W), F32)]
    if fuse_res:
        out_specs.append(tok(D_MODEL))
        out_shape.append(jax.ShapeDtypeStruct((bsz, t, D_MODEL), F32))
    grid_spec = pltpu.PrefetchScalarGridSpec(num_scalar_prefetch=1, grid=(bsz, t // tm), in_specs=in_specs,
                                             out_specs=out_specs)
    return pl.pallas_call(
        functools.partial(_qkvu_kernel, rope=rope, fuse_res=fuse_res), grid_spec=grid_spec, out_shape=out_shape,
        compiler_params=_params(("parallel", "parallel"), 56),
        name="qkvu",
    )(_layer_index(layer), *args)


def _stack_heads(q_ref):
    lo = lax.broadcasted_iota(jnp.int32, (1, LANES), 1) < HEAD_DIM
    keep_lo = jnp.where(lo, 1.0, 0.0).astype(BF16)
    keep_hi = jnp.where(lo, 0.0, 1.0).astype(BF16)
    parts = []
    for m in range(GROUP):
        qm = q_ref[:, m * LANES:(m + 1) * LANES]
        parts.append(qm * keep_lo)
        parts.append(qm * keep_hi)
    return jnp.concatenate(parts, axis=0), lo


def _attend(s, bias, sink_ref, v, tq, lo, o_ref):
    ps, dens = [], []
    for h in range(N_HEADS):
        sh = s[h * tq:(h + 1) * tq]
        if bias is not None:
            sh = sh + bias
        sink = sink_ref[h]
        mx = jnp.maximum(jnp.max(sh, axis=-1, keepdims=True), sink)
        p = jnp.exp(sh - mx)
        dens.append(jnp.sum(p, axis=-1, keepdims=True) + jnp.exp(sink - mx))
        ps.append(p.astype(BF16))
    o = jnp.dot(jnp.concatenate(ps, axis=0), v, preferred_element_type=F32)
    for m in range(GROUP):
        o0 = o[(2 * m) * tq:(2 * m + 1) * tq] / dens[2 * m]
        o1 = o[(2 * m + 1) * tq:(2 * m + 2) * tq] / dens[2 * m + 1]
        o_ref[:, m * LANES:(m + 1) * LANES] = jnp.where(lo, o0, o1).astype(o_ref.dtype)


def _ctx_attn_kernel(sink_ref, q_ref, k_ref, v_ref, o_ref):
    tq = q_ref.shape[0]
    qx, lo = _stack_heads(q_ref)
    s = lax.dot_general(qx, k_ref[...].astype(BF16), (((1,), (1,)), ((), ())), preferred_element_type=F32)
    _attend(s, None, sink_ref, v_ref[...].astype(BF16), tq, lo, o_ref)


def _ctx_attention(sink, q, k, v):
    bsz, t, _ = q.shape
    tok = lambda wd: pl.BlockSpec((None, t, wd), lambda b: (b, 0, 0))
    return pl.pallas_call(
        _ctx_attn_kernel, grid=(bsz,),
        in_specs=[pl.BlockSpec(memory_space=pltpu.SMEM), tok(Q_W), tok(KV_W), tok(KV_W)],
        out_specs=tok(Q_W), out_shape=jax.ShapeDtypeStruct((bsz, t, Q_W), BF16),
        compiler_params=_params(("parallel",), 32),
        name="ctx_attention",
    )(sink, q, k, v)


def _lat_attn_kernel(sink_ref, q_ref, kc_ref, vc_ref, kp_ref, k0_ref, kn_ref, vp_ref, v0_ref, vn_ref, o_ref):
    tq = q_ref.shape[0]
    i = pl.program_id(1)
    nb = pl.num_programs(1)
    qx, lo = _stack_heads(q_ref)
    kall = jnp.concatenate([kc_ref[...].astype(BF16), kp_ref[...], k0_ref[...], kn_ref[...]], axis=0)
    vall = jnp.concatenate([vc_ref[...].astype(BF16), vp_ref[...], v0_ref[...], vn_ref[...]], axis=0)
    s = lax.dot_general(qx, kall, (((1,), (1,)), ((), ())), preferred_element_type=F32)
    past = kc_ref.shape[0]
    r = lax.broadcasted_iota(jnp.int32, (tq, BLOCK), 0)
    c = lax.broadcasted_iota(jnp.int32, (tq, BLOCK), 1)
    m_prev = jnp.where((c >= r) & (i > 0), 0.0, NEG_INF)
    m_next = jnp.where((c <= r) & (i < nb - 1), 0.0, NEG_INF)
    bias = jnp.concatenate([jnp.zeros((tq, past), F32), m_prev, jnp.zeros((tq, BLOCK), F32), m_next], axis=1)
    _attend(s, bias, sink_ref, vall, tq, lo, o_ref)


def _lat_attention(sink, q, k, v, kc, vc):
    bsz, t, _ = q.shape
    nb = t // BLOCK
    past = kc.shape[1]
    tok = lambda wd: pl.BlockSpec((None, BLOCK, wd), lambda b, i: (b, i, 0))
    prev = pl.BlockSpec((None, BLOCK, KV_W), lambda b, i: (b, jnp.maximum(i - 1, 0), 0))
    nxt = pl.BlockSpec((None, BLOCK, KV_W), lambda b, i: (b, jnp.minimum(i + 1, nb - 1), 0))
    ctx = pl.BlockSpec((None, past, KV_W), lambda b, i: (b, 0, 0))
    return pl.pallas_call(
        _lat_attn_kernel, grid=(bsz, nb),
        in_specs=[pl.BlockSpec(memory_space=pltpu.SMEM), tok(Q_W), ctx, ctx,
                  prev, tok(KV_W), nxt, prev, tok(KV_W), nxt],
        out_specs=tok(Q_W), out_shape=jax.ShapeDtypeStruct((bsz, t, Q_W), BF16),
        compiler_params=_params(("parallel", "parallel"), 32),
        name="lat_attention",
    )(sink, q, kc, vc, k, k, k, v, v, v)


SSM_QUAD = LANES // (2 * SSM_GC)
GRANULES = LANES // SSM_GC


def _ssm_kernel(l_ref, u_ref, strip_ref, wst_ref, w2_ref, a_ref, h0_ref, d_ref, y_ref, fin_ref, pk_ref, z_ref,
                ent_ref, toep_ref, *, bsz, t):
    del l_ref
    nk = t // SSM_CHUNK
    gran = lax.broadcasted_iota(jnp.int32, (1, LANES), 1) // SSM_GC
    half = SSM_CHUNK // GRANULES

    ck = min(nk, 4 * SUBLANES)

    def transpose_granules(xs):
        xs = list(xs)
        for d in (4, 2, 1):
            low = (gran & d) == 0
            for i in range(GRANULES):
                if not i & d:
                    a, b = xs[i], xs[i + d]
                    xs[i] = jnp.where(low, a, pltpu.roll(b, SSM_GC * d, axis=1))
                    xs[i + d] = jnp.where(low, pltpu.roll(a, LANES - SSM_GC * d, axis=1), b)
        return xs

    def pack(it, carry):
        b, k0 = it // (nk // ck), (it % (nk // ck)) * ck
        for sh in range(half):
            xs = [u_ref[pl.ds(b * t + k0 * SSM_CHUNK + sh * GRANULES + s8, ck, stride=SSM_CHUNK), :]
                  for s8 in range(GRANULES)]
            for g8, x in enumerate(transpose_granules(xs)):
                pk_ref[g8 // 2, (g8 % 2) * half + sh, pl.ds(k0 * bsz + b, ck, stride=bsz), :] = x
        return carry

    lax.fori_loop(0, bsz * (nk // ck), pack, 0)

    rows = max(bsz, SUBLANES)
    steps = rows // bsz
    niter = nk // steps
    ns = SSM_W

    def step(sr, si, dr, di, ar, ai):
        return ar * sr - ai * si + dr, ar * si + ai * sr + di

    for p in range(SSM_QUAD):
        u = jnp.concatenate([pk_ref[p, col] for col in range(2 * half)], axis=1)
        ub = u.astype(BF16)
        gw = SSM_CHUNK * SSM_GC
        for gl in range(2):
            strip = strip_ref[p, gl]
            for s in range(SSM_CHUNK):
                off = (SSM_CHUNK - 1 - s) * SSM_GC
                blk = pltpu.roll(strip, 2 * gw - off, axis=1) if off else strip
                toep_ref[s * SSM_GC:(s + 1) * SSM_GC, :] = blk[:, 0:gw].astype(BF16)
            z_ref[:, gl * gw:(gl + 1) * gw] = jnp.dot(ub[:, gl * gw:(gl + 1) * gw], toep_ref[...],
                                                      preferred_element_type=F32)
        z_ref[:, ns:ns + 4 * LANES] = jnp.dot(ub, wst_ref[p], preferred_element_type=F32)
        afr, afi, abr, abi = a_ref[p, 0:1, :], a_ref[p, 1:2, :], a_ref[p, 2:3, :], a_ref[p, 3:4, :]

        def body(it, carry):
            sr, si, gr, gi = carry
            rf = pl.multiple_of(it * rows, SUBLANES)
            rb = pl.multiple_of((niter - 1 - it) * rows, SUBLANES)
            dfr, dfi = z_ref[pl.ds(rf, rows), ns:ns + LANES], z_ref[pl.ds(rf, rows), ns + LANES:ns + 2 * LANES]
            dbr = z_ref[pl.ds(rb, rows), ns + 2 * LANES:ns + 3 * LANES]
            dbi = z_ref[pl.ds(rb, rows), ns + 3 * LANES:ns + 4 * LANES]
            efr, efi, ebr, ebi = [], [], [None] * steps, [None] * steps
            for j in range(steps):
                efr.append(sr)
                efi.append(si)
                sr, si = step(sr, si, dfr[j * bsz:(j + 1) * bsz], dfi[j * bsz:(j + 1) * bsz], afr, afi)
            for j in reversed(range(steps)):
                ebr[j] = gr
                ebi[j] = gi
                gr, gi = step(gr, gi, dbr[j * bsz:(j + 1) * bsz], dbi[j * bsz:(j + 1) * bsz], abr, abi)
            cat = lambda xs: xs[0] if len(xs) == 1 else jnp.concatenate(xs, axis=0)
            ent_ref[pl.ds(rf, rows), 0:LANES] = cat(efr)
            ent_ref[pl.ds(rf, rows), LANES:2 * LANES] = cat(efi)
            ent_ref[pl.ds(rb, rows), 2 * LANES:3 * LANES] = cat(ebr)
            ent_ref[pl.ds(rb, rows), 3 * LANES:4 * LANES] = cat(ebi)
            return sr, si, gr, gi

        init = (h0_ref[p, 0], h0_ref[p, 1], h0_ref[p, 2], h0_ref[p, 3])
        sr, si, gr, gi = lax.fori_loop(0, niter, body, init)
        fin_ref[p, 0] = sr
        fin_ref[p, 1] = si
        fin_ref[p, 2] = gr
        fin_ref[p, 3] = gi
        y = (z_ref[:, 0:SSM_W] + jnp.dot(ent_ref[...].astype(BF16), w2_ref[p], preferred_element_type=F32)
             + d_ref[p] * u)
        for col in range(2 * half):
            pk_ref[p, col] = y[:, col * LANES:(col + 1) * LANES]

    def unpack(it, carry):
        b, k0 = it // (nk // ck), (it % (nk // ck)) * ck
        for sh in range(half):
            xs = [pk_ref[g8 // 2, (g8 % 2) * half + sh, pl.ds(k0 * bsz + b, ck, stride=bsz), :] for g8 in range(GRANULES)]
            for s8, x in enumerate(transpose_granules(xs)):
                y_ref[pl.ds(b * t + k0 * SSM_CHUNK + sh * GRANULES + s8, ck, stride=SSM_CHUNK), :] = x
        return carry

    lax.fori_loop(0, bsz * (nk // ck), unpack, 0)


def _ssm(u, ssm_w, layer, h0):
    bsz, t, _ = u.shape
    strip, wst, w2, a16, dflat = ssm_w
    rows = (t // SSM_CHUNK) * bsz
    if h0 is None:
        h0 = jnp.zeros((SSM_PAIRS, 4, bsz, LANES), F32)
    gw = SSM_CHUNK * SSM_GC
    quad = lambda *tail: pl.BlockSpec((SSM_QUAD,) + tail, lambda q, lref: (q,) + (0,) * len(tail))
    lquad = lambda *tail: pl.BlockSpec((None, SSM_QUAD) + tail, lambda q, lref: (lref[0], q) + (0,) * len(tail))
    tok = pl.BlockSpec((bsz * t, LANES), lambda q, lref: (0, q), pipeline_mode=pl.Buffered(1))
    tok_in = pl.BlockSpec((bsz * t, LANES), lambda q, lref: (0, q))
    grid_spec = pltpu.PrefetchScalarGridSpec(
        num_scalar_prefetch=1, grid=(SSM_PAIRS // SSM_QUAD,),
        in_specs=[tok_in, lquad(2, SSM_GC, 2 * gw), lquad(SSM_W, 4 * LANES), lquad(4 * LANES, SSM_W), lquad(4, LANES),
                  quad(4, bsz, LANES), lquad(1, SSM_W)],
        out_specs=[tok, quad(4, bsz, LANES)],
        scratch_shapes=[pltpu.VMEM((SSM_QUAD, SSM_W // LANES, rows, LANES), F32),
                        pltpu.VMEM((rows, SSM_W + 4 * LANES), F32), pltpu.VMEM((rows, 4 * LANES), F32),
                        pltpu.VMEM((gw, gw), BF16)])
    y, fin = pl.pallas_call(
        functools.partial(_ssm_kernel, bsz=bsz, t=t), grid_spec=grid_spec,
        out_shape=[jax.ShapeDtypeStruct((bsz * t, SSM_W), F32),
                   jax.ShapeDtypeStruct((SSM_PAIRS, 4, bsz, LANES), F32)],
        compiler_params=_params(("parallel",), 52),
        name="ssm",
    )(_layer_index(layer), u.reshape(bsz * t, SSM_W), strip, wst, w2, a16, h0, dflat)
    return y.reshape(bsz, t, SSM_W), fin


def _ssm_weights(a_re, a_im, log_dt, b_re, b_im, c_re, c_im, ssm_d):
    L = SSM_CHUNK
    nl = a_re.shape[0]
    dt = jnp.exp(log_dt)[..., None, None]
    steps = jnp.arange(L + 1, dtype=F32)
    mag = jnp.exp(a_re[..., None] * dt * steps)
    ang = a_im[..., None] * dt * steps
    pr, pi = mag * jnp.cos(ang), mag * jnp.sin(ang)
    nr, ni = pr[..., 1] - 1.0, pi[..., 1]
    den = a_re * a_re + a_im * a_im
    qr, qi = (nr * a_re + ni * a_im) / den, (ni * a_re - nr * a_im) / den
    swap = lambda x: x.transpose(0, 1, 2, 4, 3)
    bbr = swap(qr[..., None] * b_re - qi[..., None] * b_im)
    bbi = swap(qr[..., None] * b_im + qi[..., None] * b_re)
    ctr, cti = swap(c_re), swap(c_im)
    cpr = ctr[..., None, :] * pr[..., :L, None] - cti[..., None, :] * pi[..., :L, None]
    cpi = ctr[..., None, :] * pi[..., :L, None] + cti[..., None, :] * pr[..., :L, None]
    flat = lambda x: x.reshape(nl, 2, SSM_G, SSM_P, L * SSM_GC)
    kern = (jnp.einsum('ldgkp,ldgpn->ldgkn', bbr, flat(cpr), precision=HIGHEST)
            - jnp.einsum('ldgkp,ldgpn->ldgkn', bbi, flat(cpi), precision=HIGHEST))
    kern = kern.reshape(nl, 2, SSM_G, SSM_GC, L, SSM_GC)
    kf, kb = kern[:, 0], kern[:, 1]
    lagk = jnp.concatenate([kb[..., :0:-1, :], kf[..., 0:1, :] + kb[..., 0:1, :], kf[..., 1:, :]], axis=-2)
    strip = jnp.pad(lagk.reshape(nl, SSM_G, SSM_GC, (2 * L - 1) * SSM_GC), ((0, 0), (0, 0), (0, 0), (0, SSM_GC)))
    strip = strip.reshape(nl, SSM_PAIRS, 2, SSM_GC, 2 * L * SSM_GC)
    idx = jnp.arange(L)

    def pair(w):
        w = w.reshape((nl, SSM_PAIRS, 2) + w.shape[2:])
        even, odd = w[:, :, 0], w[:, :, 1]
        zero = jnp.zeros_like(even)
        return jnp.stack([jnp.concatenate([even, zero], axis=-1), jnp.concatenate([zero, odd], axis=-1)], axis=2)

    def inject(kpow, d):
        ar = pr[:, d].transpose(0, 1, 3, 2)[:, :, kpow][:, :, :, None, :]
        ai = pi[:, d].transpose(0, 1, 3, 2)[:, :, kpow][:, :, :, None, :]
        br, bi = bbr[:, d][:, :, None], bbi[:, d][:, :, None]
        return [pair(w).reshape(nl, SSM_PAIRS, 2 * L * SSM_GC, 2 * SSM_P) for w in (ar * br - ai * bi, ar * bi + ai * br)]

    wst = jnp.concatenate(inject(L - 1 - idx, 0) + inject(idx, 1), axis=-1)

    def readout(kpow, d):
        ar, ai = pr[:, d][..., kpow][..., None], pi[:, d][..., kpow][..., None]
        cr, ci = ctr[:, d][:, :, :, None, :], cti[:, d][:, :, :, None, :]
        planes = (cr * ar - ci * ai, -(cr * ai + ci * ar))
        return [pair(w.reshape(nl, SSM_G, SSM_P, L * SSM_GC)).reshape(nl, SSM_PAIRS, 2 * SSM_P, 2 * L * SSM_GC)
                for w in planes]

    w2 = jnp.concatenate(readout(1 + idx, 0) + readout(L - idx, 1), axis=2)
    plane = lambda x: x.reshape(nl, SSM_PAIRS, 2 * SSM_P)
    a16 = jnp.stack([plane(pr[:, 0, :, :, L]), plane(pi[:, 0, :, :, L]), plane(pr[:, 1, :, :, L]),
                     plane(pi[:, 1, :, :, L])], axis=2)
    dflat = jnp.broadcast_to(ssm_d.reshape(nl, SSM_PAIRS, 2, 1, SSM_GC), (nl, SSM_PAIRS, 2, L, SSM_GC))
    return strip, wst.astype(BF16), w2.astype(BF16), a16, dflat.reshape(nl, SSM_PAIRS, 1, 2 * L * SSM_GC)


def _merge_kernel(l_ref, x_ref, xp_ref, xn_ref, mod_ref, n1_ref, n2_ref, wcg_ref, cw_ref, wco_ref, attn_ref, wao_ref,
                  ssm_ref, wglu_ref, wout_ref, rw_ref, x1_ref, h2_ref, aff_ref):
    del l_ref
    tm = x_ref.shape[0]
    i = pl.program_id(1)
    nt = pl.num_programs(1)
    x = x_ref[...]
    n1 = n1_ref[...]
    sh1, sc1, g1 = mod_ref[0:1, :], mod_ref[1:2, :], mod_ref[2:3, :]
    sh2, sc2 = mod_ref[3:4, :], mod_ref[4:5, :]
    cw = CONV_W

    def conv_in(h):
        xin = jnp.dot(h, wcg_ref[:, 0:cw], preferred_element_type=F32)
        cg = jnp.dot(h, wcg_ref[:, 2 * cw:3 * cw], preferred_element_type=F32)
        return cg * xin

    x_ext = jnp.concatenate([x, xp_ref[...], xn_ref[...]], axis=0)
    h_ext = _rms_mod(x_ext, n1, sc1, sh1).astype(BF16)
    h = h_ext[0:tm]
    z_ext = conv_in(h_ext)
    z = z_ext[0:tm]
    z_before = jnp.where(i > 0, z_ext[tm + SUBLANES - 1:tm + SUBLANES, :], 0.0)
    z_after = jnp.where(i < nt - 1, z_ext[tm + SUBLANES:tm + SUBLANES + 1, :], 0.0)
    rows = lax.broadcasted_iota(jnp.int32, (tm, 1), 0)
    zl = jnp.where(rows == 0, z_before, pltpu.roll(z, 1, axis=0))
    zr = jnp.where(rows == tm - 1, z_after, pltpu.roll(z, tm - 1, axis=0))
    y = zl * cw_ref[0:1, :] + z * cw_ref[1:2, :] + zr * cw_ref[2:3, :]
    bg = jnp.dot(h, wcg_ref[:, cw:2 * cw], preferred_element_type=F32)
    conv_y = jnp.dot((bg * y).astype(BF16), wco_ref[...], preferred_element_type=F32)
    o = wcg_ref.shape[1] - N_BRANCH * D_MODEL
    ga = _sigmoid(jnp.dot(h, wcg_ref[:, o:o + D_MODEL], preferred_element_type=F32))
    mixed = ga * conv_y
    attn_y = jnp.dot(attn_ref[...], wao_ref[...], preferred_element_type=F32)
    gb = _sigmoid(jnp.dot(h, wcg_ref[:, o + D_MODEL:o + 2 * D_MODEL], preferred_element_type=F32))
    mixed = mixed + gb * attn_y
    s = ssm_ref[...]
    gelu = s * (0.5 * (1.0 + jnp.tanh(math.sqrt(2.0 / math.pi) * (s + 0.044715 * (s * s * s)))))
    zab = jnp.dot(gelu.astype(BF16), wglu_ref[...], preferred_element_type=F32)
    ssm_y = zab[:, 0:D_MODEL] * _sigmoid(zab[:, D_MODEL:2 * D_MODEL])
    gc = _sigmoid(jnp.dot(h, wcg_ref[:, o + 2 * D_MODEL:o + 3 * D_MODEL], preferred_element_type=F32))
    mixed = mixed + gc * ssm_y
    x1 = x + g1 * jnp.dot(mixed.astype(BF16), wout_ref[...], preferred_element_type=F32)
    x1_ref[...] = x1
    h2 = _rms_mod(x1, n2_ref[...], sc2, sh2)
    h2_ref[...] = h2.astype(BF16)
    h_hi = h2.astype(BF16)
    h_lo = (h2 - h_hi.astype(F32)).astype(BF16)
    rw = rw_ref[...]
    r_hi = rw.astype(BF16)
    r_lo = (rw - r_hi.astype(F32)).astype(BF16)
    nt_dims = (((1,), (1,)), ((), ()))
    a = lax.dot_general(jnp.concatenate([r_hi, r_lo], axis=0), h_hi, nt_dims, preferred_element_type=F32)
    b = lax.dot_general(r_hi, h_lo, nt_dims, preferred_element_type=F32)
    logits = a[0:N_EXPERTS] + a[N_EXPERTS:2 * N_EXPERTS] + b
    e = jnp.exp(logits - jnp.max(logits, axis=0, keepdims=True))
    aff_ref[...] = e / jnp.sum(e, axis=0, keepdims=True)


def _merge(x, mod, n1, n2, wcg, cw, wco, attn, wao, ssm, wglu, wout, rwt, layer, tm):
    bsz, t, _ = x.shape
    nt = t // tm
    per = tm // SUBLANES
    tok = lambda wd: pl.BlockSpec((None, tm, wd), lambda b, i, *_: (b, i, 0))
    prev = pl.BlockSpec((None, SUBLANES, D_MODEL), lambda b, i, *_: (b, jnp.maximum(i * per - 1, 0), 0))
    nxt = pl.BlockSpec((None, SUBLANES, D_MODEL),
                       lambda b, i, *_: (b, jnp.minimum((i + 1) * per, t // SUBLANES - 1), 0))
    lay = _layer_spec
    grid_spec = pltpu.PrefetchScalarGridSpec(
        num_scalar_prefetch=1, grid=(bsz, nt),
        in_specs=[tok(D_MODEL), prev, nxt, _mod_spec(mod), lay(n1), lay(n2), lay(wcg), lay(cw), lay(wco), tok(Q_W),
                  lay(wao), tok(SSM_W), lay(wglu), lay(wout), lay(rwt)],
        out_specs=[tok(D_MODEL), tok(D_MODEL), pl.BlockSpec((None, N_EXPERTS, tm), lambda b, i, *_: (b, 0, i))])
    return pl.pallas_call(
        _merge_kernel, grid_spec=grid_spec,
        out_shape=[jax.ShapeDtypeStruct((bsz, t, D_MODEL), F32), jax.ShapeDtypeStruct((bsz, t, D_MODEL), BF16),
                   jax.ShapeDtypeStruct((bsz, N_EXPERTS, t), F32)],
        compiler_params=_params(("parallel", "parallel"), 56),
        name="merge",
    )(_layer_index(layer), x, x, x, mod, n1, n2, wcg, cw, wco, attn, wao, ssm, wglu, wout, rwt)


def _route_kernel(aff_ref, slot_ref, cnt_ref, *, cap, slot_stride):
    nrow, n = aff_ref.shape
    bits = pltpu.bitcast(aff_ref[...], jnp.int32)
    thr = jnp.zeros((nrow, 1), jnp.int32)
    for b in range(30, -1, -1):
        cand = thr | (1 << b)
        cnt = jnp.sum(jnp.where(bits >= cand, 1.0, 0.0), axis=1, keepdims=True)
        thr = jnp.where(cnt >= cap, cand, thr)
    need = cap - jnp.sum(jnp.where(bits > thr, 1.0, 0.0), axis=1, keepdims=True)
    ri = lax.broadcasted_iota(jnp.int32, (LANES, LANES), 0)
    ci = lax.broadcasted_iota(jnp.int32, (LANES, LANES), 1)
    tri = jnp.where(ri <= ci, 1.0, 0.0).astype(BF16)
    lane = lax.broadcasted_iota(jnp.int32, (nrow, LANES), 1)
    base = (lax.broadcasted_iota(jnp.int32, (nrow, 1), 0) // N_EXPERTS) * slot_stride
    off_eq = jnp.zeros((nrow, 1), F32)
    off = jnp.zeros((nrow, 1), F32)
    cnts = jnp.zeros((nrow, LANES), jnp.int32)
    for j in range(n // LANES):
        sl = slice(j * LANES, (j + 1) * LANES)
        bj = bits[:, sl]
        eqf = jnp.where(bj == thr, 1.0, 0.0)
        ceq = jnp.dot(eqf.astype(BF16), tri, preferred_element_type=F32) + off_eq
        off_eq = ceq[:, LANES - 1:LANES]
        self_ = jnp.where(bj > thr, 1.0, jnp.where(ceq <= need, eqf, 0.0))
        csel = jnp.dot(self_.astype(BF16), tri, preferred_element_type=F32) + off
        cnts = jnp.where(lane == j, off.astype(jnp.int32) + base, cnts)
        off = csel[:, LANES - 1:LANES]
        slot_ref[:, sl] = jnp.where(self_ > 0.0, csel.astype(jnp.int32) - 1 + base, -1)
    cnt_ref[...] = jnp.where(lane >= n // LANES, off.astype(jnp.int32) + base, cnts)


def _route(aff, cap, slot_stride):
    sets, _, n = aff.shape
    nrow = sets * N_EXPERTS
    slot, cnt = pl.pallas_call(
        functools.partial(_route_kernel, cap=cap, slot_stride=slot_stride), grid=(1,),
        in_specs=[pl.BlockSpec((nrow, n), lambda s: (0, 0))],
        out_specs=[pl.BlockSpec((nrow, n), lambda s: (0, 0)), pl.BlockSpec((nrow, LANES), lambda s: (0, 0))],
        out_shape=[jax.ShapeDtypeStruct((nrow, n), jnp.int32), jax.ShapeDtypeStruct((nrow, LANES), jnp.int32)],
        compiler_params=_params(("arbitrary",), 32),
        name="route",
    )(aff.reshape(nrow, n))
    return slot.reshape(sets, N_EXPERTS, n), cnt.reshape(sets, N_EXPERTS, LANES)


def _expert_kernel(cnt_ref, l_ref, h2_ref, slot_ref, gate_ref, wg_ref, wu_ref, wd_ref, o_ref, xs_ref, ys_ref, gs_ref,
                   *, nslot):
    del l_ref
    s = pl.program_id(0)
    e = pl.program_id(1)
    ntb = SUPER // BLOCK
    nsb = nslot // BLOCK
    per = GATHER_TOKENS // BLOCK
    cbase = (s * N_EXPERTS + e) * (ntb + 1)

    @pl.when(e == 0)
    def _():
        o_ref[...] = jnp.zeros_like(o_ref)

    xs_ref[...] = jnp.zeros_like(xs_ref)
    gs_ref[...] = jnp.zeros_like(gs_ref)
    ys_ref[nslot:nslot + BLOCK, :] = jnp.zeros((BLOCK, D_MODEL), BF16)

    srow = lax.broadcasted_iota(jnp.int32, (BLOCK, GATHER_TOKENS), 0)
    for g in range(SUPER // GATHER_TOKENS):
        slots = slot_ref[g:g + 1, :]
        gates = gate_ref[g:g + 1, :]
        for j in range(nsb):
            @pl.when((cnt_ref[cbase + g * per] < (j + 1) * BLOCK) & (cnt_ref[cbase + (g + 1) * per] > j * BLOCK))
            def _():
                pick = slots == srow + j * BLOCK
                x = jnp.dot(jnp.where(pick, 1.0, 0.0).astype(BF16), h2_ref[g * GATHER_TOKENS:(g + 1) * GATHER_TOKENS, :],
                            preferred_element_type=F32)
                xs_ref[j * BLOCK:(j + 1) * BLOCK, :] += x.astype(BF16)
                gs_ref[j * BLOCK:(j + 1) * BLOCK, :] += jnp.sum(jnp.where(pick, gates, 0.0), axis=1, keepdims=True)

    xs = xs_ref[...]
    hg = jnp.dot(xs, wg_ref[...], preferred_element_type=F32)
    hu = jnp.dot(xs, wu_ref[...], preferred_element_type=F32)
    act = (hg * _sigmoid(hg) * hu).astype(BF16)
    y = jnp.dot(act, wd_ref[...], preferred_element_type=F32) * gs_ref[...]
    ys_ref[0:nslot, :] = y.astype(BF16)

    wrow = lax.broadcasted_iota(jnp.int32, (2 * BLOCK, BLOCK), 0)
    for i in range(ntb):
        w0 = pl.multiple_of(jnp.minimum(cnt_ref[cbase + i] // BLOCK, nsb - 1) * BLOCK, BLOCK)
        slots = slot_ref[i // per:i // per + 1, (i % per) * BLOCK:(i % per + 1) * BLOCK]
        pick = jnp.where(slots - w0 == wrow, 1.0, 0.0).astype(BF16)
        o_ref[i * BLOCK:(i + 1) * BLOCK, :] += lax.dot_general(
            pick, ys_ref[pl.ds(w0, 2 * BLOCK), :], (((0,), (0,)), ((), ())), preferred_element_type=F32)


def _experts(cnt, h2, slot, gate, wg, wu, wd, layer):
    nsup = h2.shape[0]
    nslot = CAPACITY * SUPER // N_EXPERTS
    ngt = SUPER // GATHER_TOKENS
    grid_spec = pltpu.PrefetchScalarGridSpec(
        num_scalar_prefetch=2, grid=(nsup, N_EXPERTS),
        in_specs=[pl.BlockSpec((None, SUPER, D_MODEL), lambda s, e, c, l: (s, 0, 0)),
                  pl.BlockSpec((None, None, ngt, GATHER_TOKENS), lambda s, e, c, l: (s, e, 0, 0)),
                  pl.BlockSpec((None, None, ngt, GATHER_TOKENS), lambda s, e, c, l: (s, e, 0, 0)),
                  pl.BlockSpec((None, None, D_MODEL, EXPERT_FF), lambda s, e, c, l: (l[0], e, 0, 0)),
                  pl.BlockSpec((None, None, D_MODEL, EXPERT_FF), lambda s, e, c, l: (l[0], e, 0, 0)),
                  pl.BlockSpec((None, None, EXPERT_FF, D_MODEL), lambda s, e, c, l: (l[0], e, 0, 0))],
        out_specs=pl.BlockSpec((None, SUPER, D_MODEL), lambda s, e, c, l: (s, 0, 0), pipeline_mode=pl.Buffered(1)),
        scratch_shapes=[pltpu.VMEM((nslot, D_MODEL), BF16), pltpu.VMEM((nslot + BLOCK, D_MODEL), BF16),
                        pltpu.VMEM((nslot, 1), F32)])
    return pl.pallas_call(
        functools.partial(_expert_kernel, nslot=nslot), grid_spec=grid_spec,
        out_shape=jax.ShapeDtypeStruct((nsup, SUPER, D_MODEL), F32),
        compiler_params=_params(("arbitrary", "arbitrary"), 56),
        name="experts",
    )(cnt, _layer_index(layer), h2, slot, gate, wg, wu, wd)


def _final_kernel(x_ref, moe_ref, mod_ref, g_ref, o_ref):
    x = x_ref[...] + mod_ref[5:6, :] * moe_ref[...]
    o_ref[...] = x * lax.rsqrt(jnp.mean(x * x, axis=-1, keepdims=True) + EPS) * g_ref[...]


def _final(x1, moe, mod, g, tm):
    bsz, t, _ = x1.shape
    tok = pl.BlockSpec((None, tm, D_MODEL), lambda b, i: (b, i, 0))
    return pl.pallas_call(
        _final_kernel, grid=(bsz, t // tm),
        in_specs=[tok, tok, _mod_spec(mod), _const_spec((1, D_MODEL))],
        out_specs=tok, out_shape=jax.ShapeDtypeStruct((bsz, t, D_MODEL), F32),
        compiler_params=_params(("parallel", "parallel"), 32),
        name="final_norm",
    )(x1, moe, mod, g)


def _head_perm():
    cols = []
    for m in range(GROUP):
        for kv in range(N_KV):
            head = kv * GROUP + m
            cols.extend(range(head * HEAD_DIM, (head + 1) * HEAD_DIM))
    return jnp.array(cols, jnp.int32)


def _rope_swap(width):
    idx = jnp.arange(width)
    nf = HEAD_DIM // 4
    return jnp.where((idx % (2 * nf)) < nf, idx + nf, idx - nf)


def _rope_tables(t):
    pos = jnp.arange(t)
    row = (pos // GRID_W).astype(F32)
    col = (pos % GRID_W).astype(F32)
    nf = HEAD_DIM // 4
    inv = ROPE_BASE ** (-jnp.arange(nf, dtype=F32) / nf)

    def tabs(p):
        ang = p[:, None] * inv[None, :]
        cos, sin = jnp.cos(ang), jnp.sin(ang)
        return jnp.concatenate([cos, cos], axis=1), jnp.concatenate([-sin, sin], axis=1)

    cr, sr = tabs(row)
    cc, sc = tabs(col)
    cos = jnp.concatenate([cr, cc], axis=1)
    sin = jnp.concatenate([sr, sc], axis=1)
    return jnp.tile(cos, (1, LANES // HEAD_DIM)), jnp.tile(sin, (1, LANES // HEAD_DIM))


def _route_and_experts(h2c, affc, h2l, affl, wg, wu, wd, layer):
    bc, tc, _ = h2c.shape
    bl, tl, _ = h2l.shape
    ntb = SUPER // BLOCK
    capc = CAPACITY * tc // N_EXPERTS
    capl = CAPACITY * tl // N_EXPERTS
    slot_c, cnt_c = _route(affc, capc, capc)
    slot_l, cnt_l = _route(affl, capl, 0)
    tbc = tc // BLOCK
    rows = (SUPER // GATHER_TOKENS, GATHER_TOKENS)
    slot_c = slot_c.transpose(1, 0, 2).reshape(1, N_EXPERTS, *rows)
    gate_c = affc.transpose(1, 0, 2).reshape(1, N_EXPERTS, *rows)
    cnt_c = jnp.concatenate([cnt_c[:, :, :tbc].transpose(1, 0, 2).reshape(N_EXPERTS, ntb),
                             cnt_c[bc - 1, :, tbc:tbc + 1]], axis=1)
    moe_c = _experts(cnt_c.reshape(-1), h2c.reshape(1, SUPER, D_MODEL), slot_c, gate_c, wg, wu, wd, layer)
    moe_l = _experts(cnt_l[:, :, :ntb + 1].reshape(-1), h2l, slot_l.reshape(bl, N_EXPERTS, *rows),
                     affl.reshape(bl, N_EXPERTS, *rows), wg, wu, wd, layer)
    return moe_c.reshape(bc, tc, D_MODEL), moe_l


def kernel(x_prompt, x_sample, cache_k, cache_v, state_ssm_re, state_ssm_im, c, c_ctx, ada_w, ada_b, norm1, norm2,
           final_norm, w_in, conv_w, w_conv_out, attn_sink, w_attn_out, ssm_a_re, ssm_a_im, ssm_log_dt, ssm_b_re,
           ssm_b_im, ssm_c_re, ssm_c_im, ssm_d, w_glu, w_out, router_w, w_gate, w_up, w_down):
    bc, tc, _ = x_prompt.shape
    bl, tl, _ = x_sample.shape
    assert bc * tc == SUPER and tl == SUPER and bl + 1 <= SUBLANES
    hp = _head_perm()
    cvecs = jnp.zeros((SUBLANES, D_MODEL), F32).at[:bl].set(c).at[bl].set(c_ctx)
    mods = _modulation(cvecs, ada_w, ada_b).reshape(DEPTH, SUBLANES, 6, D_MODEL)

    o_q, o_k, o_v, o_u, o_g = 3 * CONV_W, 3 * CONV_W + Q_W, 3 * CONV_W + Q_W + KV_W, 3 * CONV_W + Q_W + 2 * KV_W, \
        3 * CONV_W + Q_W + 2 * KV_W + SSM_W
    w_b = w_in.astype(BF16)
    wq = w_b[:, :, o_q:o_k][:, :, hp]
    wk = w_b[:, :, o_k:o_v]
    w_qkvu = jnp.concatenate([wq, w_b[:, :, o_k:o_g], wq[:, :, _rope_swap(Q_W)], wk[:, :, _rope_swap(KV_W)]], axis=-1)
    w_co = w_conv_out.astype(BF16)
    w_ao = w_attn_out[:, hp, :].astype(BF16)
    w_gl = w_glu.astype(BF16)
    w_o = w_out.astype(BF16)
    rwt = router_w.transpose(0, 2, 1)
    wg, wu, wd = w_gate.astype(BF16), w_up.astype(BF16), w_down.astype(BF16)
    n1 = norm1.reshape(DEPTH, 1, D_MODEL)
    n2 = norm2.reshape(DEPTH, 1, D_MODEL)
    sink = attn_sink.reshape(DEPTH, N_KV, GROUP).transpose(0, 2, 1).reshape(DEPTH, N_HEADS)
    rope_tabs = _rope_tables(tl)
    ssm_w = _ssm_weights(ssm_a_re, ssm_a_im, ssm_log_dt, ssm_b_re, ssm_b_im, ssm_c_re, ssm_c_im, ssm_d)
    h0_all = jnp.stack([state_ssm_re[:, :, 0], state_ssm_im[:, :, 0], state_ssm_re[:, :, 1], state_ssm_im[:, :, 1]],
                       axis=0)
    h0_all = h0_all.reshape(4, bl, DEPTH, SSM_PAIRS, 2 * SSM_P).transpose(2, 3, 0, 1, 4)

    xp, xs = x_prompt, x_sample
    moe_c = moe_l = None
    modp_c = modp_l = None
    ks, vs, fins = [], [], []
    for l in range(DEPTH):
        mod_l = mods[l, :bl]
        mod_c = mods[l, bl:bl + 1]
        outs = _qkvu(xp, moe_c, modp_c, mod_c, n1, w_qkvu, l, None, tc, F32)
        q_c, k_c, v_c, u_c = outs[:4]
        if moe_c is not None:
            xp = outs[4]
        ks.append(k_c)
        vs.append(v_c)
        attn_c = _ctx_attention(sink[l], q_c, k_c, v_c)
        ssm_c, fin = _ssm(u_c, ssm_w, l, None)
        fins.append(fin)
        x1c, h2c, affc = _merge(xp, mod_c, n1, n2, w_b, conv_w, w_co, attn_c, w_ao, ssm_c, w_gl, w_o, rwt, l, tc)
        outs = _qkvu(xs, moe_l, modp_l, mod_l, n1, w_qkvu, l, rope_tabs, 1024, BF16)
        q_l, k_l, v_l, u_l = outs[:4]
        if moe_l is not None:
            xs = outs[4]
        attn_l = _lat_attention(sink[l], q_l, k_l, v_l, cache_k[:, l].reshape(bl, -1, KV_W),
                                cache_v[:, l].reshape(bl, -1, KV_W))
        ssm_l, _ = _ssm(u_l, ssm_w, l, h0_all[l])
        x1l, h2l, affl = _merge(xs, mod_l, n1, n2, w_b, conv_w, w_co, attn_l, w_ao, ssm_l, w_gl, w_o, rwt, l, 512)
        moe_c, moe_l = _route_and_experts(h2c, affc, h2l, affl, wg, wu, wd, l)
        xp, xs = x1c, x1l
        modp_c, modp_l = mod_c, mod_l

    fn = final_norm.reshape(1, D_MODEL)
    y_prompt = _final(xp, moe_c, modp_c, fn, tc)
    y_sample = _final(xs, moe_l, modp_l, fn, 1024)
    new_k = jnp.stack(ks, axis=1).reshape(bc, DEPTH, tc, N_KV, HEAD_DIM)
    new_v = jnp.stack(vs, axis=1).reshape(bc, DEPTH, tc, N_KV, HEAD_DIM)
    fin = jnp.stack(fins, axis=0)
    fin = fin.reshape(DEPTH, SSM_PAIRS, 2, 2, bc, 2, SSM_P).transpose(3, 4, 0, 2, 1, 5, 6)
    fin = fin.reshape(2, bc, DEPTH, 2, SSM_G, SSM_P)
    return (y_prompt, y_sample, new_k, new_v, fin[0], fin[1])
```

```python
import functools
import math

import jax
import jax.numpy as jnp
from jax import lax
from jax.experimental import pallas as pl
from jax.experimental.pallas import tpu as pltpu

D_MODEL = 1024
DEPTH = 4
GRID_W = 64
CONV_W = 512
N_HEADS = 8
N_KV = 2
HEAD_DIM = 64
GROUP = N_HEADS // N_KV
Q_W = N_HEADS * HEAD_DIM
KV_W = N_KV * HEAD_DIM
BLOCK = 128
ROPE_BASE = 10000.0
NEG_INF = -1e30
SSM_W = 512
SSM_GC = 16
SSM_G = SSM_W // SSM_GC
SSM_P = 64
N_BRANCH = 3
N_EXPERTS = 16
EXPERT_FF = 1024
CAPACITY = 2
EPS = 1e-6

F32 = jnp.float32
BF16 = jnp.bfloat16
HIGHEST = lax.Precision.HIGHEST

LANES = 128
SUBLANES = 8
SSM_CHUNK = 16
SSM_PAIRS = SSM_G // 2
SUPER = 4096
GATHER_TOKENS = 512
MIB = 1024 * 1024


def _params(sem, vmem_mib):
    return pltpu.CompilerParams(dimension_semantics=sem, vmem_limit_bytes=vmem_mib * MIB)


def _const_spec(shape):
    nd = len(shape)
    return pl.BlockSpec(shape, lambda *_: (0,) * nd, pipeline_mode=pl.Buffered(1))


def _layer_spec(arr):
    nd = arr.ndim
    return pl.BlockSpec((None,) + arr.shape[1:], lambda *idx: (idx[-1][0],) + (0,) * (nd - 1),
                        pipeline_mode=pl.Buffered(1))


def _layer_index(layer):
    return jnp.full((1,), layer, jnp.int32)


def _sigmoid(x):
    return 0.5 * jnp.tanh(0.5 * x) + 0.5


def _rms_mod(x, g, sc, sh):
    y = x * lax.rsqrt(jnp.mean(x * x, axis=-1, keepdims=True) + EPS)
    return (y * g) * (1.0 + sc) + sh


def _mod_kernel(c_ref, w_ref, b_ref, o_ref):
    cv = c_ref[...]
    s = cv * _sigmoid(cv)
    o_ref[...] = jnp.dot(s, w_ref[...], precision=HIGHEST, preferred_element_type=F32) + b_ref[...]


def _modulation(cvecs, ada_w, ada_b):
    nt = 1536
    return pl.pallas_call(
        _mod_kernel,
        grid=(DEPTH, 6 * D_MODEL // nt),
        in_specs=[pl.BlockSpec((SUBLANES, D_MODEL), lambda l, j: (0, 0)),
                  pl.BlockSpec((None, D_MODEL, nt), lambda l, j: (l, 0, j)),
                  pl.BlockSpec((None, 1, nt), lambda l, j: (l, 0, j))],
        out_specs=pl.BlockSpec((None, SUBLANES, nt), lambda l, j: (l, 0, j)),
        out_shape=jax.ShapeDtypeStruct((DEPTH, SUBLANES, 6 * D_MODEL), F32),
        compiler_params=_params(("parallel", "parallel"), 32),
        name="modulation",
    )(cvecs, ada_w, ada_b.reshape(DEPTH, 1, 6 * D_MODEL))


def _qkvu_kernel(*refs, rope, fuse_res):
    it = iter(refs[1:])
    x_ref = next(it)
    if fuse_res:
        moe_ref = next(it)
        modp_ref = next(it)
    mod_ref = next(it)
    n1_ref = next(it)
    w_ref = next(it)
    if rope:
        cos_ref = next(it)
        sin_ref = next(it)
    q_ref, k_ref, v_ref, u_ref = next(it), next(it), next(it), next(it)
    x = x_ref[...]
    if fuse_res:
        xo_ref = next(it)
        x = x + modp_ref[5:6, :] * moe_ref[...]
        xo_ref[...] = x
    h = _rms_mod(x, n1_ref[...], mod_ref[1:2, :], mod_ref[0:1, :]).astype(BF16)
    ncol = w_ref.shape[1] if rope else Q_W + 2 * KV_W + SSM_W
    p = jnp.dot(h, w_ref[:, 0:ncol], preferred_element_type=F32)
    q = p[:, 0:Q_W]
    k = p[:, Q_W:Q_W + KV_W]
    if rope:
        cos = cos_ref[...]
        sin = sin_ref[...]
        o = Q_W + 2 * KV_W + SSM_W
        q = jnp.concatenate(
            [q[:, m * LANES:(m + 1) * LANES] * cos + p[:, o + m * LANES:o + (m + 1) * LANES] * sin
             for m in range(Q_W // LANES)], axis=1)
        k = k * cos + p[:, o + Q_W:o + Q_W + KV_W] * sin
    q_ref[...] = (q * HEAD_DIM ** -0.5).astype(q_ref.dtype)
    k_ref[...] = k.astype(k_ref.dtype)
    v_ref[...] = p[:, Q_W + KV_W:Q_W + 2 * KV_W].astype(v_ref.dtype)
    u_ref[...] = p[:, Q_W + 2 * KV_W:Q_W + 2 * KV_W + SSM_W]


def _mod_spec(mod):
    if mod.shape[0] == 1:
        return pl.BlockSpec((None, 6, D_MODEL), lambda b, i, *_: (0, 0, 0))
    return pl.BlockSpec((None, 6, D_MODEL), lambda b, i, *_: (b, 0, 0))


def _qkvu(x, moe, modp, mod, n1, w, layer, rope_tabs, tm, kv_dtype):
    bsz, t, _ = x.shape
    rope = rope_tabs is not None
    fuse_res = moe is not None
    tok = lambda wd: pl.BlockSpec((None, tm, wd), lambda b, i, *_: (b, i, 0))
    in_specs, args = [tok(D_MODEL)], [x]
    if fuse_res:
        in_specs += [tok(D_MODEL), _mod_spec(modp)]
        args += [moe, modp]
    in_specs += [_mod_spec(mod), _layer_spec(n1), _layer_spec(w)]
    args += [mod, n1, w]
    if rope:
        in_specs += [pl.BlockSpec((tm, LANES), lambda b, i, *_: (i, 0))] * 2
        args += list(rope_tabs)
    out_specs = [tok(Q_W), tok(KV_W), tok(KV_W), tok(SSM_W)]
    out_shape = [jax.ShapeDtypeStruct((bsz, t, Q_W), BF16),
                 jax.ShapeDtypeStruct((bsz, t, KV_W), kv_dtype),
                 jax.ShapeDtypeStruct((bsz, t, KV_W), kv_dtype),
                 jax.ShapeDtypeStruct((bsz, t, SSM_W), F32)]
    if fuse_res:
        out_specs.append(tok(D_MODEL))
        out_shape.append(jax.ShapeDtypeStruct((bsz, t, D_MODEL), F32))
    grid_spec = pltpu.PrefetchScalarGridSpec(num_scalar_prefetch=1, grid=(bsz, t // tm), in_specs=in_specs,
                                             out_specs=out_specs)
    return pl.pallas_call(
        functools.partial(_qkvu_kernel, rope=rope, fuse_res=fuse_res), grid_spec=grid_spec, out_shape=out_shape,
        compiler_params=_params(("parallel", "parallel"), 56),
        name="qkvu",
    )(_layer_index(layer), *args)


def _stack_heads(q_ref):
    lo = lax.broadcasted_iota(jnp.int32, (1, LANES), 1) < HEAD_DIM
    keep_lo = jnp.where(lo, 1.0, 0.0).astype(BF16)
    keep_hi = jnp.where(lo, 0.0, 1.0).astype(BF16)
    parts = []
    for m in range(GROUP):
        qm = q_ref[:, m * LANES:(m + 1) * LANES]
        parts.append(qm * keep_lo)
        parts.append(qm * keep_hi)
    return jnp.concatenate(parts, axis=0), lo


def _attend(s, bias, sink_ref, v, tq, lo, o_ref):
    ps, dens = [], []
    for h in range(N_HEADS):
        sh = s[h * tq:(h + 1) * tq]
        if bias is not None:
            sh = sh + bias
        sink = sink_ref[h]
        mx = jnp.maximum(jnp.max(sh, axis=-1, keepdims=True), sink)
        p = jnp.exp(sh - mx)
        dens.append(jnp.sum(p, axis=-1, keepdims=True) + jnp.exp(sink - mx))
        ps.append(p.astype(BF16))
    o = jnp.dot(jnp.concatenate(ps, axis=0), v, preferred_element_type=F32)
    for m in range(GROUP):
        o0 = o[(2 * m) * tq:(2 * m + 1) * tq] / dens[2 * m]
        o1 = o[(2 * m + 1) * tq:(2 * m + 2) * tq] / dens[2 * m + 1]
        o_ref[:, m * LANES:(m + 1) * LANES] = jnp.where(lo, o0, o1).astype(o_ref.dtype)


def _ctx_attn_kernel(sink_ref, q_ref, k_ref, v_ref, o_ref):
    tq = q_ref.shape[0]
    qx, lo = _stack_heads(q_ref)
    s = lax.dot_general(qx, k_ref[...].astype(BF16), (((1,), (1,)), ((), ())), preferred_element_type=F32)
    _attend(s, None, sink_ref, v_ref[...].astype(BF16), tq, lo, o_ref)


def _ctx_attention(sink, q, k, v):
    bsz, t, _ = q.shape
    tok = lambda wd: pl.BlockSpec((None, t, wd), lambda b: (b, 0, 0))
    return pl.pallas_call(
        _ctx_attn_kernel, grid=(bsz,),
        in_specs=[pl.BlockSpec(memory_space=pltpu.SMEM), tok(Q_W), tok(KV_W), tok(KV_W)],
        out_specs=tok(Q_W), out_shape=jax.ShapeDtypeStruct((bsz, t, Q_W), BF16),
        compiler_params=_params(("parallel",), 32),
        name="ctx_attention",
    )(sink, q, k, v)


def _lat_attn_kernel(sink_ref, q_ref, kc_ref, vc_ref, kp_ref, k0_ref, kn_ref, vp_ref, v0_ref, vn_ref, o_ref):
    tq = q_ref.shape[0]
    i = pl.program_id(1)
    nb = pl.num_programs(1)
    qx, lo = _stack_heads(q_ref)
    kall = jnp.concatenate([kc_ref[...].astype(BF16), kp_ref[...], k0_ref[...], kn_ref[...]], axis=0)
    vall = jnp.concatenate([vc_ref[...].astype(BF16), vp_ref[...], v0_ref[...], vn_ref[...]], axis=0)
    s = lax.dot_general(qx, kall, (((1,), (1,)), ((), ())), preferred_element_type=F32)
    past = kc_ref.shape[0]
    r = lax.broadcasted_iota(jnp.int32, (tq, BLOCK), 0)
    c = lax.broadcasted_iota(jnp.int32, (tq, BLOCK), 1)
    m_prev = jnp.where((c >= r) & (i > 0), 0.0, NEG_INF)
    m_next = jnp.where((c <= r) & (i < nb - 1), 0.0, NEG_INF)
    bias = jnp.concatenate([jnp.zeros((tq, past), F32), m_prev, jnp.zeros((tq, BLOCK), F32), m_next], axis=1)
    _attend(s, bias, sink_ref, vall, tq, lo, o_ref)


def _lat_attention(sink, q, k, v, kc, vc):
    bsz, t, _ = q.shape
    nb = t // BLOCK
    past = kc.shape[1]
    tok = lambda wd: pl.BlockSpec((None, BLOCK, wd), lambda b, i: (b, i, 0))
    prev = pl.BlockSpec((None, BLOCK, KV_W), lambda b, i: (b, jnp.maximum(i - 1, 0), 0))
    nxt = pl.BlockSpec((None, BLOCK, KV_W), lambda b, i: (b, jnp.minimum(i + 1, nb - 1), 0))
    ctx = pl.BlockSpec((None, past, KV_W), lambda b, i: (b, 0, 0))
    return pl.pallas_call(
        _lat_attn_kernel, grid=(bsz, nb),
        in_specs=[pl.BlockSpec(memory_space=pltpu.SMEM), tok(Q_W), ctx, ctx,
                  prev, tok(KV_W), nxt, prev, tok(KV_W), nxt],
        out_specs=tok(Q_W), out_shape=jax.ShapeDtypeStruct((bsz, t, Q_W), BF16),
        compiler_params=_params(("parallel", "parallel"), 32),
        name="lat_attention",
    )(sink, q, kc, vc, k, k, k, v, v, v)


SSM_QUAD = LANES // (2 * SSM_GC)
GRANULES = LANES // SSM_GC


def _ssm_kernel(l_ref, u_ref, strip_ref, wst_ref, w2_ref, a_ref, h0_ref, d_ref, y_ref, fin_ref, pk_ref, z_ref,
                ent_ref, toep_ref, *, bsz, t):
    del l_ref
    nk = t // SSM_CHUNK
    gran = lax.broadcasted_iota(jnp.int32, (1, LANES), 1) // SSM_GC
    half = SSM_CHUNK // GRANULES

    ck = min(nk, 4 * SUBLANES)

    def transpose_granules(xs):
        xs = list(xs)
        for d in (4, 2, 1):
            low = (gran & d) == 0
            for i in range(GRANULES):
                if not i & d:
                    a, b = xs[i], xs[i + d]
                    xs[i] = jnp.where(low, a, pltpu.roll(b, SSM_GC * d, axis=1))
                    xs[i + d] = jnp.where(low, pltpu.roll(a, LANES - SSM_GC * d, axis=1), b)
        return xs

    def pack(it, carry):
        b, k0 = it // (nk // ck), (it % (nk // ck)) * ck
        for sh in range(half):
            xs = [u_ref[pl.ds(b * t + k0 * SSM_CHUNK + sh * GRANULES + s8, ck, stride=SSM_CHUNK), :]
                  for s8 in range(GRANULES)]
            for g8, x in enumerate(transpose_granules(xs)):
                pk_ref[g8 // 2, (g8 % 2) * half + sh, pl.ds(k0 * bsz + b, ck, stride=bsz), :] = x
        return carry

    lax.fori_loop(0, bsz * (nk // ck), pack, 0)

    rows = max(bsz, SUBLANES)
    steps = rows // bsz
    niter = nk // steps
    ns = SSM_W

    def step(sr, si, dr, di, ar, ai):
        return ar * sr - ai * si + dr, ar * si + ai * sr + di

    for p in range(SSM_QUAD):
        u = jnp.concatenate([pk_ref[p, col] for col in range(2 * half)], axis=1)
        ub = u.astype(BF16)
        gw = SSM_CHUNK * SSM_GC
        for gl in range(2):
            strip = strip_ref[p, gl]
            for s in range(SSM_CHUNK):
                off = (SSM_CHUNK - 1 - s) * SSM_GC
                blk = pltpu.roll(strip, 2 * gw - off, axis=1) if off else strip
                toep_ref[s * SSM_GC:(s + 1) * SSM_GC, :] = blk[:, 0:gw].astype(BF16)
            z_ref[:, gl * gw:(gl + 1) * gw] = jnp.dot(ub[:, gl * gw:(gl + 1) * gw], toep_ref[...],
                                                      preferred_element_type=F32)
        z_ref[:, ns:ns + 4 * LANES] = jnp.dot(ub, wst_ref[p], preferred_element_type=F32)
        afr, afi, abr, abi = a_ref[p, 0:1, :], a_ref[p, 1:2, :], a_ref[p, 2:3, :], a_ref[p, 3:4, :]

        def body(it, carry):
            sr, si, gr, gi = carry
            rf = pl.multiple_of(it * rows, SUBLANES)
            rb = pl.multiple_of((niter - 1 - it) * rows, SUBLANES)
            dfr, dfi = z_ref[pl.ds(rf, rows), ns:ns + LANES], z_ref[pl.ds(rf, rows), ns + LANES:ns + 2 * LANES]
            dbr = z_ref[pl.ds(rb, rows), ns + 2 * LANES:ns + 3 * LANES]
            dbi = z_ref[pl.ds(rb, rows), ns + 3 * LANES:ns + 4 * LANES]
            efr, efi, ebr, ebi = [], [], [None] * steps, [None] * steps
            for j in range(steps):
                efr.append(sr)
                efi.append(si)
                sr, si = step(sr, si, dfr[j * bsz:(j + 1) * bsz], dfi[j * bsz:(j + 1) * bsz], afr, afi)
            for j in reversed(range(steps)):
                ebr[j] = gr
                ebi[j] = gi
                gr, gi = step(gr, gi, dbr[j * bsz:(j + 1) * bsz], dbi[j * bsz:(j + 1) * bsz], abr, abi)
            cat = lambda xs: xs[0] if len(xs) == 1 else jnp.concatenate(xs, axis=0)
            ent_ref[pl.ds(rf, rows), 0:LANES] = cat(efr)
            ent_ref[pl.ds(rf, rows), LANES:2 * LANES] = cat(efi)
            ent_ref[pl.ds(rb, rows), 2 * LANES:3 * LANES] = cat(ebr)
            ent_ref[pl.ds(rb, rows), 3 * LANES:4 * LANES] = cat(ebi)
            return sr, si, gr, gi

        init = (h0_ref[p, 0], h0_ref[p, 1], h0_ref[p, 2], h0_ref[p, 3])
        sr, si, gr, gi = lax.fori_loop(0, niter, body, init)
        fin_ref[p, 0] = sr
        fin_ref[p, 1] = si
        fin_ref[p, 2] = gr
        fin_ref[p, 3] = gi
        y = (z_ref[:, 0:SSM_W] + jnp.dot(ent_ref[...].astype(BF16), w2_ref[p], preferred_element_type=F32)
             + d_ref[p] * u)
        for col in range(2 * half):
            pk_ref[p, col] = y[:, col * LANES:(col + 1) * LANES]

    def unpack(it, carry):
        b, k0 = it // (nk // ck), (it % (nk // ck)) * ck
        for sh in range(half):
            xs = [pk_ref[g8 // 2, (g8 % 2) * half + sh, pl.ds(k0 * bsz + b, ck, stride=bsz), :] for g8 in range(GRANULES)]
            for s8, x in enumerate(transpose_granules(xs)):
                y_ref[pl.ds(b * t + k0 * SSM_CHUNK + sh * GRANULES + s8, ck, stride=SSM_CHUNK), :] = x
        return carry

    lax.fori_loop(0, bsz * (nk // ck), unpack, 0)


def _ssm(u, ssm_w, layer, h0):
    bsz, t, _ = u.shape
    strip, wst, w2, a16, dflat = ssm_w
    rows = (t // SSM_CHUNK) * bsz
    if h0 is None:
        h0 = jnp.zeros((SSM_PAIRS, 4, bsz, LANES), F32)
    gw = SSM_CHUNK * SSM_GC
    quad = lambda *tail: pl.BlockSpec((SSM_QUAD,) + tail, lambda q, lref: (q,) + (0,) * len(tail))
    lquad = lambda *tail: pl.BlockSpec((None, SSM_QUAD) + tail, lambda q, lref: (lref[0], q) + (0,) * len(tail))
    tok = pl.BlockSpec((bsz * t, LANES), lambda q, lref: (0, q), pipeline_mode=pl.Buffered(1))
    tok_in = pl.BlockSpec((bsz * t, LANES), lambda q, lref: (0, q))
    grid_spec = pltpu.PrefetchScalarGridSpec(
        num_scalar_prefetch=1, grid=(SSM_PAIRS // SSM_QUAD,),
        in_specs=[tok_in, lquad(2, SSM_GC, 2 * gw), lquad(SSM_W, 4 * LANES), lquad(4 * LANES, SSM_W), lquad(4, LANES),
                  quad(4, bsz, LANES), lquad(1, SSM_W)],
        out_specs=[tok_in, quad(4, bsz, LANES)],
        scratch_shapes=[pltpu.VMEM((SSM_QUAD, SSM_W // LANES, rows, LANES), F32),
                        pltpu.VMEM((rows, SSM_W + 4 * LANES), F32), pltpu.VMEM((rows, 4 * LANES), F32),
                        pltpu.VMEM((gw, gw), BF16)])
    y, fin = pl.pallas_call(
        functools.partial(_ssm_kernel, bsz=bsz, t=t), grid_spec=grid_spec,
        out_shape=[jax.ShapeDtypeStruct((bsz * t, SSM_W), F32),
                   jax.ShapeDtypeStruct((SSM_PAIRS, 4, bsz, LANES), F32)],
        compiler_params=_params(("parallel",), 60),
        name="ssm",
    )(_layer_index(layer), u.reshape(bsz * t, SSM_W), strip, wst, w2, a16, h0, dflat)
    return y.reshape(bsz, t, SSM_W), fin


def _ssm_weights(a_re, a_im, log_dt, b_re, b_im, c_re, c_im, ssm_d):
    L = SSM_CHUNK
    nl = a_re.shape[0]
    dt = jnp.exp(log_dt)[..., None, None]
    steps = jnp.arange(L + 1, dtype=F32)
    mag = jnp.exp(a_re[..., None] * dt * steps)
    ang = a_im[..., None] * dt * steps
    pr, pi = mag * jnp.cos(ang), mag * jnp.sin(ang)
    nr, ni = pr[..., 1] - 1.0, pi[..., 1]
    den = a_re * a_re + a_im * a_im
    qr, qi = (nr * a_re + ni * a_im) / den, (ni * a_re - nr * a_im) / den
    swap = lambda x: x.transpose(0, 1, 2, 4, 3)
    bbr = swap(qr[..., None] * b_re - qi[..., None] * b_im)
    bbi = swap(qr[..., None] * b_im + qi[..., None] * b_re)
    ctr, cti = swap(c_re), swap(c_im)
    cpr = ctr[..., None, :] * pr[..., :L, None] - cti[..., None, :] * pi[..., :L, None]
    cpi = ctr[..., None, :] * pi[..., :L, None] + cti[..., None, :] * pr[..., :L, None]
    flat = lambda x: x.reshape(nl, 2, SSM_G, SSM_P, L * SSM_GC)
    kern = (jnp.einsum('ldgkp,ldgpn->ldgkn', bbr, flat(cpr), precision=HIGHEST)
            - jnp.einsum('ldgkp,ldgpn->ldgkn', bbi, flat(cpi), precision=HIGHEST))
    kern = kern.reshape(nl, 2, SSM_G, SSM_GC, L, SSM_GC)
    kf, kb = kern[:, 0], kern[:, 1]
    lagk = jnp.concatenate([kb[..., :0:-1, :], kf[..., 0:1, :] + kb[..., 0:1, :], kf[..., 1:, :]], axis=-2)
    strip = jnp.pad(lagk.reshape(nl, SSM_G, SSM_GC, (2 * L - 1) * SSM_GC), ((0, 0), (0, 0), (0, 0), (0, SSM_GC)))
    strip = strip.reshape(nl, SSM_PAIRS, 2, SSM_GC, 2 * L * SSM_GC)
    idx = jnp.arange(L)

    def pair(w):
        w = w.reshape((nl, SSM_PAIRS, 2) + w.shape[2:])
        even, odd = w[:, :, 0], w[:, :, 1]
        zero = jnp.zeros_like(even)
        return jnp.stack([jnp.concatenate([even, zero], axis=-1), jnp.concatenate([zero, odd], axis=-1)], axis=2)

    def inject(kpow, d):
        ar = pr[:, d].transpose(0, 1, 3, 2)[:, :, kpow][:, :, :, None, :]
        ai = pi[:, d].transpose(0, 1, 3, 2)[:, :, kpow][:, :, :, None, :]
        br, bi = bbr[:, d][:, :, None], bbi[:, d][:, :, None]
        return [pair(w).reshape(nl, SSM_PAIRS, 2 * L * SSM_GC, 2 * SSM_P) for w in (ar * br - ai * bi, ar * bi + ai * br)]

    wst = jnp.concatenate(inject(L - 1 - idx, 0) + inject(idx, 1), axis=-1)

    def readout(kpow, d):
        ar, ai = pr[:, d][..., kpow][..., None], pi[:, d][..., kpow][..., None]
        cr, ci = ctr[:, d][:, :, :, None, :], cti[:, d][:, :, :, None, :]
        planes = (cr * ar - ci * ai, -(cr * ai + ci * ar))
        return [pair(w.reshape(nl, SSM_G, SSM_P, L * SSM_GC)).reshape(nl, SSM_PAIRS, 2 * SSM_P, 2 * L * SSM_GC)
                for w in planes]

    w2 = jnp.concatenate(readout(1 + idx, 0) + readout(L - idx, 1), axis=2)
    plane = lambda x: x.reshape(nl, SSM_PAIRS, 2 * SSM_P)
    a16 = jnp.stack([plane(pr[:, 0, :, :, L]), plane(pi[:, 0, :, :, L]), plane(pr[:, 1, :, :, L]),
                     plane(pi[:, 1, :, :, L])], axis=2)
    dflat = jnp.broadcast_to(ssm_d.reshape(nl, SSM_PAIRS, 2, 1, SSM_GC), (nl, SSM_PAIRS, 2, L, SSM_GC))
    return strip, wst.astype(BF16), w2.astype(BF16), a16, dflat.reshape(nl, SSM_PAIRS, 1, 2 * L * SSM_GC)


def _merge_kernel(l_ref, x_ref, xp_ref, xn_ref, mod_ref, n1_ref, n2_ref, wcg_ref, cw_ref, wco_ref, attn_ref, wao_ref,
                  ssm_ref, wglu_ref, wout_ref, rw_ref, x1_ref, h2_ref, aff_ref):
    del l_ref
    tm = x_ref.shape[0]
    i = pl.program_id(1)
    nt = pl.num_programs(1)
    x = x_ref[...]
    n1 = n1_ref[...]
    sh1, sc1, g1 = mod_ref[0:1, :], mod_ref[1:2, :], mod_ref[2:3, :]
    sh2, sc2 = mod_ref[3:4, :], mod_ref[4:5, :]
    cw = CONV_W

    def conv_in(h):
        xin = jnp.dot(h, wcg_ref[:, 0:cw], preferred_element_type=F32)
        cg = jnp.dot(h, wcg_ref[:, 2 * cw:3 * cw], preferred_element_type=F32)
        return cg * xin

    x_ext = jnp.concatenate([x, xp_ref[...], xn_ref[...]], axis=0)
    h_ext = _rms_mod(x_ext, n1, sc1, sh1).astype(BF16)
    h = h_ext[0:tm]
    z_ext = conv_in(h_ext)
    z = z_ext[0:tm]
    z_before = jnp.where(i > 0, z_ext[tm + SUBLANES - 1:tm + SUBLANES, :], 0.0)
    z_after = jnp.where(i < nt - 1, z_ext[tm + SUBLANES:tm + SUBLANES + 1, :], 0.0)
    rows = lax.broadcasted_iota(jnp.int32, (tm, 1), 0)
    zl = jnp.where(rows == 0, z_before, pltpu.roll(z, 1, axis=0))
    zr = jnp.where(rows == tm - 1, z_after, pltpu.roll(z, tm - 1, axis=0))
    y = zl * cw_ref[0:1, :] + z * cw_ref[1:2, :] + zr * cw_ref[2:3, :]
    bg = jnp.dot(h, wcg_ref[:, cw:2 * cw], preferred_element_type=F32)
    conv_y = jnp.dot((bg * y).astype(BF16), wco_ref[...], preferred_element_type=F32)
    o = wcg_ref.shape[1] - N_BRANCH * D_MODEL
    ga = _sigmoid(jnp.dot(h, wcg_ref[:, o:o + D_MODEL], preferred_element_type=F32))
    mixed = ga * conv_y
    attn_y = jnp.dot(attn_ref[...], wao_ref[...], preferred_element_type=F32)
    gb = _sigmoid(jnp.dot(h, wcg_ref[:, o + D_MODEL:o + 2 * D_MODEL], preferred_element_type=F32))
    mixed = mixed + gb * attn_y
    s = ssm_ref[...]
    gelu = s * (0.5 * (1.0 + jnp.tanh(math.sqrt(2.0 / math.pi) * (s + 0.044715 * (s * s * s)))))
    zab = jnp.dot(gelu.astype(BF16), wglu_ref[...], preferred_element_type=F32)
    ssm_y = zab[:, 0:D_MODEL] * _sigmoid(zab[:, D_MODEL:2 * D_MODEL])
    gc = _sigmoid(jnp.dot(h, wcg_ref[:, o + 2 * D_MODEL:o + 3 * D_MODEL], preferred_element_type=F32))
    mixed = mixed + gc * ssm_y
    x1 = x + g1 * jnp.dot(mixed.astype(BF16), wout_ref[...], preferred_element_type=F32)
    x1_ref[...] = x1
    h2 = _rms_mod(x1, n2_ref[...], sc2, sh2)
    h2_ref[...] = h2.astype(BF16)
    h_hi = h2.astype(BF16)
    h_lo = (h2 - h_hi.astype(F32)).astype(BF16)
    rw = rw_ref[...]
    r_hi = rw.astype(BF16)
    r_lo = (rw - r_hi.astype(F32)).astype(BF16)
    nt_dims = (((1,), (1,)), ((), ()))
    a = lax.dot_general(jnp.concatenate([r_hi, r_lo], axis=0), h_hi, nt_dims, preferred_element_type=F32)
    b = lax.dot_general(r_hi, h_lo, nt_dims, preferred_element_type=F32)
    logits = a[0:N_EXPERTS] + a[N_EXPERTS:2 * N_EXPERTS] + b
    e = jnp.exp(logits - jnp.max(logits, axis=0, keepdims=True))
    aff_ref[...] = e / jnp.sum(e, axis=0, keepdims=True)


def _merge(x, mod, n1, n2, wcg, cw, wco, attn, wao, ssm, wglu, wout, rwt, layer, tm):
    bsz, t, _ = x.shape
    nt = t // tm
    per = tm // SUBLANES
    tok = lambda wd: pl.BlockSpec((None, tm, wd), lambda b, i, *_: (b, i, 0))
    prev = pl.BlockSpec((None, SUBLANES, D_MODEL), lambda b, i, *_: (b, jnp.maximum(i * per - 1, 0), 0))
    nxt = pl.BlockSpec((None, SUBLANES, D_MODEL),
                       lambda b, i, *_: (b, jnp.minimum((i + 1) * per, t // SUBLANES - 1), 0))
    lay = _layer_spec
    grid_spec = pltpu.PrefetchScalarGridSpec(
        num_scalar_prefetch=1, grid=(bsz, nt),
        in_specs=[tok(D_MODEL), prev, nxt, _mod_spec(mod), lay(n1), lay(n2), lay(wcg), lay(cw), lay(wco), tok(Q_W),
                  lay(wao), tok(SSM_W), lay(wglu), lay(wout), lay(rwt)],
        out_specs=[tok(D_MODEL), tok(D_MODEL), pl.BlockSpec((None, N_EXPERTS, tm), lambda b, i, *_: (b, 0, i))])
    return pl.pallas_call(
        _merge_kernel, grid_spec=grid_spec,
        out_shape=[jax.ShapeDtypeStruct((bsz, t, D_MODEL), F32), jax.ShapeDtypeStruct((bsz, t, D_MODEL), BF16),
                   jax.ShapeDtypeStruct((bsz, N_EXPERTS, t), F32)],
        compiler_params=_params(("parallel", "parallel"), 56),
        name="merge",
    )(_layer_index(layer), x, x, x, mod, n1, n2, wcg, cw, wco, attn, wao, ssm, wglu, wout, rwt)


def _route_kernel(aff_ref, slot_ref, cnt_ref, *, cap, slot_stride):
    nrow, n = aff_ref.shape
    bits = pltpu.bitcast(aff_ref[...], jnp.int32)
    thr = jnp.zeros((nrow, 1), jnp.int32)
    for b in range(30, -1, -1):
        cand = thr | (1 << b)
        cnt = jnp.sum(jnp.where(bits >= cand, 1.0, 0.0), axis=1, keepdims=True)
        thr = jnp.where(cnt >= cap, cand, thr)
    need = cap - jnp.sum(jnp.where(bits > thr, 1.0, 0.0), axis=1, keepdims=True)
    ri = lax.broadcasted_iota(jnp.int32, (LANES, LANES), 0)
    ci = lax.broadcasted_iota(jnp.int32, (LANES, LANES), 1)
    tri = jnp.where(ri <= ci, 1.0, 0.0).astype(BF16)
    lane = lax.broadcasted_iota(jnp.int32, (nrow, LANES), 1)
    base = (lax.broadcasted_iota(jnp.int32, (nrow, 1), 0) // N_EXPERTS) * slot_stride
    off_eq = jnp.zeros((nrow, 1), F32)
    off = jnp.zeros((nrow, 1), F32)
    cnts = jnp.zeros((nrow, LANES), jnp.int32)
    for j in range(n // LANES):
        sl = slice(j * LANES, (j + 1) * LANES)
        bj = bits[:, sl]
        eqf = jnp.where(bj == thr, 1.0, 0.0)
        ceq = jnp.dot(eqf.astype(BF16), tri, preferred_element_type=F32) + off_eq
        off_eq = ceq[:, LANES - 1:LANES]
        self_ = jnp.where(bj > thr, 1.0, jnp.where(ceq <= need, eqf, 0.0))
        csel = jnp.dot(self_.astype(BF16), tri, preferred_element_type=F32) + off
        cnts = jnp.where(lane == j, off.astype(jnp.int32) + base, cnts)
        off = csel[:, LANES - 1:LANES]
        slot_ref[:, sl] = jnp.where(self_ > 0.0, csel.astype(jnp.int32) - 1 + base, -1)
    cnt_ref[...] = jnp.where(lane >= n // LANES, off.astype(jnp.int32) + base, cnts)


def _route(aff, cap, slot_stride):
    sets, _, n = aff.shape
    nrow = sets * N_EXPERTS
    slot, cnt = pl.pallas_call(
        functools.partial(_route_kernel, cap=cap, slot_stride=slot_stride), grid=(1,),
        in_specs=[pl.BlockSpec((nrow, n), lambda s: (0, 0))],
        out_specs=[pl.BlockSpec((nrow, n), lambda s: (0, 0)), pl.BlockSpec((nrow, LANES), lambda s: (0, 0))],
        out_shape=[jax.ShapeDtypeStruct((nrow, n), jnp.int32), jax.ShapeDtypeStruct((nrow, LANES), jnp.int32)],
        compiler_params=_params(("arbitrary",), 32),
        name="route",
    )(aff.reshape(nrow, n))
    return slot.reshape(sets, N_EXPERTS, n), cnt.reshape(sets, N_EXPERTS, LANES)


def _expert_kernel(cnt_ref, l_ref, h2_ref, slot_ref, gate_ref, wg_ref, wu_ref, wd_ref, o_ref, xs_ref, ys_ref, gs_ref,
                   *, nslot):
    del l_ref
    s = pl.program_id(0)
    e = pl.program_id(1)
    ntb = SUPER // BLOCK
    nsb = nslot // BLOCK
    per = GATHER_TOKENS // BLOCK
    cbase = (s * N_EXPERTS + e) * (ntb + 1)

    @pl.when(e == 0)
    def _():
        o_ref[...] = jnp.zeros_like(o_ref)

    xs_ref[...] = jnp.zeros_like(xs_ref)
    gs_ref[...] = jnp.zeros_like(gs_ref)
    ys_ref[nslot:nslot + BLOCK, :] = jnp.zeros((BLOCK, D_MODEL), BF16)

    srow = lax.broadcasted_iota(jnp.int32, (BLOCK, GATHER_TOKENS), 0)
    for g in range(SUPER // GATHER_TOKENS):
        slots = slot_ref[g:g + 1, :]
        gates = gate_ref[g:g + 1, :]
        for j in range(nsb):
            @pl.when((cnt_ref[cbase + g * per] < (j + 1) * BLOCK) & (cnt_ref[cbase + (g + 1) * per] > j * BLOCK))
            def _():
                pick = slots == srow + j * BLOCK
                x = jnp.dot(jnp.where(pick, 1.0, 0.0).astype(BF16), h2_ref[g * GATHER_TOKENS:(g + 1) * GATHER_TOKENS, :],
                            preferred_element_type=F32)
                xs_ref[j * BLOCK:(j + 1) * BLOCK, :] += x.astype(BF16)
                gs_ref[j * BLOCK:(j + 1) * BLOCK, :] += jnp.sum(jnp.where(pick, gates, 0.0), axis=1, keepdims=True)

    xs = xs_ref[...]
    hg = jnp.dot(xs, wg_ref[...], preferred_element_type=F32)
    hu = jnp.dot(xs, wu_ref[...], preferred_element_type=F32)
    act = (hg * _sigmoid(hg) * hu).astype(BF16)
    y = jnp.dot(act, wd_ref[...], preferred_element_type=F32) * gs_ref[...]
    ys_ref[0:nslot, :] = y.astype(BF16)

    wrow = lax.broadcasted_iota(jnp.int32, (2 * BLOCK, BLOCK), 0)
    for i in range(ntb):
        w0 = pl.multiple_of(jnp.minimum(cnt_ref[cbase + i] // BLOCK, nsb - 1) * BLOCK, BLOCK)
        slots = slot_ref[i // per:i // per + 1, (i % per) * BLOCK:(i % per + 1) * BLOCK]
        pick = jnp.where(slots - w0 == wrow, 1.0, 0.0).astype(BF16)
        o_ref[i * BLOCK:(i + 1) * BLOCK, :] += lax.dot_general(
            pick, ys_ref[pl.ds(w0, 2 * BLOCK), :], (((0,), (0,)), ((), ())), preferred_element_type=F32)


def _experts(cnt, h2, slot, gate, wg, wu, wd, layer):
    nsup = h2.shape[0]
    nslot = CAPACITY * SUPER // N_EXPERTS
    ngt = SUPER // GATHER_TOKENS
    grid_spec = pltpu.PrefetchScalarGridSpec(
        num_scalar_prefetch=2, grid=(nsup, N_EXPERTS),
        in_specs=[pl.BlockSpec((None, SUPER, D_MODEL), lambda s, e, c, l: (s, 0, 0)),
                  pl.BlockSpec((None, None, ngt, GATHER_TOKENS), lambda s, e, c, l: (s, e, 0, 0)),
                  pl.BlockSpec((None, None, ngt, GATHER_TOKENS), lambda s, e, c, l: (s, e, 0, 0)),
                  pl.BlockSpec((None, None, D_MODEL, EXPERT_FF), lambda s, e, c, l: (l[0], e, 0, 0)),
                  pl.BlockSpec((None, None, D_MODEL, EXPERT_FF), lambda s, e, c, l: (l[0], e, 0, 0)),
                  pl.BlockSpec((None, None, EXPERT_FF, D_MODEL), lambda s, e, c, l: (l[0], e, 0, 0))],
        out_specs=pl.BlockSpec((None, SUPER, D_MODEL), lambda s, e, c, l: (s, 0, 0), pipeline_mode=pl.Buffered(1)),
        scratch_shapes=[pltpu.VMEM((nslot, D_MODEL), BF16), pltpu.VMEM((nslot + BLOCK, D_MODEL), BF16),
                        pltpu.VMEM((nslot, 1), F32)])
    return pl.pallas_call(
        functools.partial(_expert_kernel, nslot=nslot), grid_spec=grid_spec,
        out_shape=jax.ShapeDtypeStruct((nsup, SUPER, D_MODEL), F32),
        compiler_params=_params(("arbitrary", "arbitrary"), 56),
        name="experts",
    )(cnt, _layer_index(layer), h2, slot, gate, wg, wu, wd)


def _final_kernel(x_ref, moe_ref, mod_ref, g_ref, o_ref):
    x = x_ref[...] + mod_ref[5:6, :] * moe_ref[...]
    o_ref[...] = x * lax.rsqrt(jnp.mean(x * x, axis=-1, keepdims=True) + EPS) * g_ref[...]


def _final(x1, moe, mod, g, tm):
    bsz, t, _ = x1.shape
    tok = pl.BlockSpec((None, tm, D_MODEL), lambda b, i: (b, i, 0))
    return pl.pallas_call(
        _final_kernel, grid=(bsz, t // tm),
        in_specs=[tok, tok, _mod_spec(mod), _const_spec((1, D_MODEL))],
        out_specs=tok, out_shape=jax.ShapeDtypeStruct((bsz, t, D_MODEL), F32),
        compiler_params=_params(("parallel", "parallel"), 32),
        name="final_norm",
    )(x1, moe, mod, g)


def _head_perm():
    cols = []
    for m in range(GROUP):
        for kv in range(N_KV):
            head = kv * GROUP + m
            cols.extend(range(head * HEAD_DIM, (head + 1) * HEAD_DIM))
    return jnp.array(cols, jnp.int32)


def _rope_swap(width):
    idx = jnp.arange(width)
    nf = HEAD_DIM // 4
    return jnp.where((idx % (2 * nf)) < nf, idx + nf, idx - nf)


def _rope_tables(t):
    pos = jnp.arange(t)
    row = (pos // GRID_W).astype(F32)
    col = (pos % GRID_W).astype(F32)
    nf = HEAD_DIM // 4
    inv = ROPE_BASE ** (-jnp.arange(nf, dtype=F32) / nf)

    def tabs(p):
        ang = p[:, None] * inv[None, :]
        cos, sin = jnp.cos(ang), jnp.sin(ang)
        return jnp.concatenate([cos, cos], axis=1), jnp.concatenate([-sin, sin], axis=1)

    cr, sr = tabs(row)
    cc, sc = tabs(col)
    cos = jnp.concatenate([cr, cc], axis=1)
    sin = jnp.concatenate([sr, sc], axis=1)
    return jnp.tile(cos, (1, LANES // HEAD_DIM)), jnp.tile(sin, (1, LANES // HEAD_DIM))


def _route_and_experts(h2c, affc, h2l, affl, wg, wu, wd, layer):
    bc, tc, _ = h2c.shape
    bl, tl, _ = h2l.shape
    ntb = SUPER // BLOCK
    capc = CAPACITY * tc // N_EXPERTS
    capl = CAPACITY * tl // N_EXPERTS
    slot_c, cnt_c = _route(affc, capc, capc)
    slot_l, cnt_l = _route(affl, capl, 0)
    tbc = tc // BLOCK
    rows = (SUPER // GATHER_TOKENS, GATHER_TOKENS)
    slot_c = slot_c.transpose(1, 0, 2).reshape(1, N_EXPERTS, *rows)
    gate_c = affc.transpose(1, 0, 2).reshape(1, N_EXPERTS, *rows)
    cnt_c = jnp.concatenate([cnt_c[:, :, :tbc].transpose(1, 0, 2).reshape(N_EXPERTS, ntb),
                             cnt_c[bc - 1, :, tbc:tbc + 1]], axis=1)
    moe_c = _experts(cnt_c.reshape(-1), h2c.reshape(1, SUPER, D_MODEL), slot_c, gate_c, wg, wu, wd, layer)
    moe_l = _experts(cnt_l[:, :, :ntb + 1].reshape(-1), h2l, slot_l.reshape(bl, N_EXPERTS, *rows),
                     affl.reshape(bl, N_EXPERTS, *rows), wg, wu, wd, layer)
    return moe_c.reshape(bc, tc, D_MODEL), moe_l


def kernel(x_prompt, x_sample, cache_k, cache_v, state_ssm_re, state_ssm_im, c, c_ctx, ada_w, ada_b, norm1, norm2,
           final_norm, w_in, conv_w, w_conv_out, attn_sink, w_attn_out, ssm_a_re, ssm_a_im, ssm_log_dt, ssm_b_re,
           ssm_b_im, ssm_c_re, ssm_c_im, ssm_d, w_glu, w_out, router_w, w_gate, w_up, w_down):
    bc, tc, _ = x_prompt.shape
    bl, tl, _ = x_sample.shape
    assert bc * tc == SUPER and tl == SUPER and bl + 1 <= SUBLANES
    hp = _head_perm()
    cvecs = jnp.zeros((SUBLANES, D_MODEL), F32).at[:bl].set(c).at[bl].set(c_ctx)
    mods = _modulation(cvecs, ada_w, ada_b).reshape(DEPTH, SUBLANES, 6, D_MODEL)

    o_q, o_k, o_v, o_u, o_g = 3 * CONV_W, 3 * CONV_W + Q_W, 3 * CONV_W + Q_W + KV_W, 3 * CONV_W + Q_W + 2 * KV_W, \
        3 * CONV_W + Q_W + 2 * KV_W + SSM_W
    w_b = w_in.astype(BF16)
    wq = w_b[:, :, o_q:o_k][:, :, hp]
    wk = w_b[:, :, o_k:o_v]
    w_qkvu = jnp.concatenate([wq, w_b[:, :, o_k:o_g], wq[:, :, _rope_swap(Q_W)], wk[:, :, _rope_swap(KV_W)]], axis=-1)
    w_co = w_conv_out.astype(BF16)
    w_ao = w_attn_out[:, hp, :].astype(BF16)
    w_gl = w_glu.astype(BF16)
    w_o = w_out.astype(BF16)
    rwt = router_w.transpose(0, 2, 1)
    wg, wu, wd = w_gate.astype(BF16), w_up.astype(BF16), w_down.astype(BF16)
    n1 = norm1.reshape(DEPTH, 1, D_MODEL)
    n2 = norm2.reshape(DEPTH, 1, D_MODEL)
    sink = attn_sink.reshape(DEPTH, N_KV, GROUP).transpose(0, 2, 1).reshape(DEPTH, N_HEADS)
    rope_tabs = _rope_tables(tl)
    ssm_w = _ssm_weights(ssm_a_re, ssm_a_im, ssm_log_dt, ssm_b_re, ssm_b_im, ssm_c_re, ssm_c_im, ssm_d)
    h0_all = jnp.stack([state_ssm_re[:, :, 0], state_ssm_im[:, :, 0], state_ssm_re[:, :, 1], state_ssm_im[:, :, 1]],
                       axis=0)
    h0_all = h0_all.reshape(4, bl, DEPTH, SSM_PAIRS, 2 * SSM_P).transpose(2, 3, 0, 1, 4)

    xp, xs = x_prompt, x_sample
    moe_c = moe_l = None
    modp_c = modp_l = None
    ks, vs, fins = [], [], []
    for l in range(DEPTH):
        mod_l = mods[l, :bl]
        mod_c = mods[l, bl:bl + 1]
        outs = _qkvu(xp, moe_c, modp_c, mod_c, n1, w_qkvu, l, None, tc, F32)
        q_c, k_c, v_c, u_c = outs[:4]
        if moe_c is not None:
            xp = outs[4]
        ks.append(k_c)
        vs.append(v_c)
        attn_c = _ctx_attention(sink[l], q_c, k_c, v_c)
        ssm_c, fin = _ssm(u_c, ssm_w, l, None)
        fins.append(fin)
        x1c, h2c, affc = _merge(xp, mod_c, n1, n2, w_b, conv_w, w_co, attn_c, w_ao, ssm_c, w_gl, w_o, rwt, l, tc)
        outs = _qkvu(xs, moe_l, modp_l, mod_l, n1, w_qkvu, l, rope_tabs, 1024, BF16)
        q_l, k_l, v_l, u_l = outs[:4]
        if moe_l is not None:
            xs = outs[4]
        attn_l = _lat_attention(sink[l], q_l, k_l, v_l, cache_k[:, l].reshape(bl, -1, KV_W),
                                cache_v[:, l].reshape(bl, -1, KV_W))
        ssm_l, _ = _ssm(u_l, ssm_w, l, h0_all[l])
        x1l, h2l, affl = _merge(xs, mod_l, n1, n2, w_b, conv_w, w_co, attn_l, w_ao, ssm_l, w_gl, w_o, rwt, l, 512)
        moe_c, moe_l = _route_and_experts(h2c, affc, h2l, affl, wg, wu, wd, l)
        xp, xs = x1c, x1l
        modp_c, modp_l = mod_c, mod_l

    fn = final_norm.reshape(1, D_MODEL)
    y_prompt = _final(xp, moe_c, modp_c, fn, tc)
    y_sample = _final(xs, moe_l, modp_l, fn, 1024)
    new_k = jnp.stack(ks, axis=1).reshape(bc, DEPTH, tc, N_KV, HEAD_DIM)
    new_v = jnp.stack(vs, axis=1).reshape(bc, DEPTH, tc, N_KV, HEAD_DIM)
    fin = jnp.stack(fins, axis=0)
    fin = fin.reshape(DEPTH, SSM_PAIRS, 2, 2, bc, 2, SSM_P).transpose(3, 4, 0, 2, 1, 5, 6)
    fin = fin.reshape(2, bc, DEPTH, 2, SSM_G, SSM_P)
    return (y_prompt, y_sample, new_k, new_v, fin[0], fin[1])
```
